```python
import jax, jax.numpy as jnp
from jax import lax
import numpy as np

D_MODEL = 1024
BATCH = 8
SEQ = 4096
DEPTH = 4

CHUNK = 64
N_MEM = 256
EPS = 1e-6

BRANCH_WIDTH = D_MODEL // 2
N_BRANCH = 3

A_BLOCK = 128
A_GROUP_DIM = 128
A_WIDTH = BRANCH_WIDTH
A_GROUPS = A_WIDTH // A_GROUP_DIM

B_HEAD_DIM = 64
B_WIDTH = BRANCH_WIDTH
B_HEADS = B_WIDTH // B_HEAD_DIM
Q_BLOCK = 128

C_HEADS = 4
C_WIDTH = BRANCH_WIDTH
C_HEAD_DIM = C_WIDTH // C_HEADS
C_CONV = 4

MEM_HEADS = 4
MEM_HEAD_DIM = D_MODEL // MEM_HEADS

D_FF = 2816
FFN_CONV = 3

A_COLS = 2 * A_WIDTH
B_COLS = 3 * B_WIDTH + B_HEADS
C_COLS = 3 * C_WIDTH + 2 * C_HEADS + C_WIDTH
G_COLS = N_BRANCH * D_MODEL
IN_COLS = A_COLS + B_COLS + C_COLS + G_COLS

kernel_name = "hybrid_gmlp_fox_mlstm_encoder"


def rmsnorm(x, g):
    xf = x.astype(jnp.float32)
    y = xf * lax.rsqrt(jnp.mean(xf * xf, axis=-1, keepdims=True) + EPS)
    return (y * g.astype(jnp.float32)).astype(x.dtype)


def causal_dwconv(x, w):
    K = w.shape[0]
    S = x.shape[1]
    xp = jnp.pad(x, ((0, 0), (K - 1, 0), (0, 0)))
    return sum(w[k] * xp[:, k:k + S] for k in range(K))


def spatial_gating(u, v, g_norm, w_s, b_s):
    B, S, _ = u.shape
    nb = S // A_BLOCK
    v = rmsnorm(v, g_norm)
    idx = jnp.arange(A_BLOCK)
    mask = (idx[None, :] // CHUNK) <= (idx[:, None] // CHUNK)
    ws = jnp.where(mask[None], w_s, 0)
    vb = v.reshape(B, nb, A_BLOCK, A_GROUPS, A_GROUP_DIM)
    mixed = jnp.einsum('gts,bnsgc->bntgc', ws, vb) + b_s.T[None, None, :, :, None]
    return u * mixed.reshape(B, S, A_WIDTH)


def forgetting_attention(q, k, v, f_logit):
    B, S, H, Dh = q.shape
    nq = S // Q_BLOCK
    logf = jax.nn.log_sigmoid(f_logit.astype(jnp.float32))
    c = jnp.cumsum(logf, axis=1).transpose(0, 2, 1)
    qb = q.reshape(B, nq, Q_BLOCK, H, Dh).transpose(1, 0, 3, 2, 4)
    cq = c.reshape(B, H, nq, Q_BLOCK).transpose(2, 0, 1, 3)
    kpos = jnp.arange(S)
    scale = Dh ** -0.5

    def block(args):
        qi, ci, i = args
        s = jnp.einsum('bhqd,bkhd->bhqk', qi, k).astype(jnp.float32) * scale
        s = s + ci[..., None] - c[:, :, None, :]
        qpos = i * Q_BLOCK + jnp.arange(Q_BLOCK)
        s = jnp.where(kpos[None, :] <= qpos[:, None], s, -jnp.inf)
        p = jax.nn.softmax(s, axis=-1).astype(v.dtype)
        return jnp.einsum('bhqk,bkhd->bqhd', p, v)

    o = lax.map(block, (qb, cq, jnp.arange(nq)))
    return o.transpose(1, 0, 2, 3, 4).reshape(B, S, H * Dh)


def mlstm_chunkwise(q, k, v, i_raw, f_raw):
    out_dtype = q.dtype
    B, S, H, Dh = q.shape
    L = CHUNK
    NC = S // L
    f32 = jnp.float32

    def to_chunks(t):
        return t.astype(f32).reshape(B, NC, L, H, -1).transpose(0, 3, 1, 2, 4)

    qc = to_chunks(q)
    kc = to_chunks(k) * (Dh ** -0.5)
    vc = to_chunks(v)
    ig = i_raw.astype(f32).reshape(B, NC, L, H).transpose(0, 3, 1, 2)
    logf = jax.nn.log_sigmoid(f_raw.astype(f32)).reshape(B, NC, L, H).transpose(0, 3, 1, 2)
    b = jnp.cumsum(logf, axis=-1)
    g = b[..., -1]

    a = g[..., None] - b + ig
    m_loc = jnp.max(a, axis=-1)
    w_loc = jnp.exp(a - m_loc[..., None])
    C_loc = jnp.einsum('bhnl,bhnld,bhnle->bhnde', w_loc, kc, vc)
    n_loc = jnp.einsum('bhnl,bhnld->bhnd', w_loc, kc)

    def step(carry, inp):
        C, n, m = carry
        Cl, nl, ml, gl = inp
        m_new = jnp.maximum(gl + m, ml)
        a_old = jnp.exp(gl + m - m_new)
        a_new = jnp.exp(ml - m_new)
        C_new = a_old[..., None, None] * C + a_new[..., None, None] * Cl
        n_new = a_old[..., None] * n + a_new[..., None] * nl
        return (C_new, n_new, m_new), (C, n, m)

    init = (jnp.zeros((B, H, Dh, Dh), f32), jnp.zeros((B, H, Dh), f32), jnp.zeros((B, H), f32))
    xs = (jnp.moveaxis(C_loc, 2, 0), jnp.moveaxis(n_loc, 2, 0),
          jnp.moveaxis(m_loc, 2, 0), jnp.moveaxis(g, 2, 0))
    _, (C_in, n_in, m_in) = lax.scan(step, init, xs)
    C_in = jnp.moveaxis(C_in, 0, 2)
    n_in = jnp.moveaxis(n_in, 0, 2)
    m_in = jnp.moveaxis(m_in, 0, 2)

    tri = jnp.tril(jnp.ones((L, L), dtype=bool))
    dlog = jnp.where(tri, b[..., :, None] - b[..., None, :] + ig[..., None, :], -jnp.inf)
    inter = b + m_in[..., None]
    m_t = jnp.maximum(jnp.max(dlog, axis=-1), inter)
    sm = jnp.einsum('bhnld,bhnsd->bhnls', qc, kc) * jnp.exp(dlog - m_t[..., None])
    w_int = jnp.exp(inter - m_t)
    num = jnp.einsum('bhnls,bhnse->bhnle', sm, vc) \
        + w_int[..., None] * jnp.einsum('bhnld,bhnde->bhnle', qc, C_in)
    den = jnp.sum(sm, axis=-1) + w_int * jnp.einsum('bhnld,bhnd->bhnl', qc, n_in)
    h = num / jnp.maximum(jnp.abs(den), jnp.exp(-m_t))[..., None]
    return h.transpose(0, 2, 3, 1, 4).reshape(B, S, H * Dh).astype(out_dtype)


def memory_attention(h, mem_n, w_q, w_kv, w_o):
    B, S, _ = h.shape
    q = (h @ w_q).reshape(B, S, MEM_HEADS, MEM_HEAD_DIM)
    k, v = jnp.split(mem_n @ w_kv, 2, axis=-1)
    k = k.reshape(B, N_MEM, MEM_HEADS, MEM_HEAD_DIM)
    v = v.reshape(B, N_MEM, MEM_HEADS, MEM_HEAD_DIM)
    s = jnp.einsum('bshd,bmhd->bhsm', q, k).astype(jnp.float32) * (MEM_HEAD_DIM ** -0.5)
    p = jax.nn.softmax(s, axis=-1).astype(v.dtype)
    o = jnp.einsum('bhsm,bmhd->bshd', p, v).reshape(B, S, D_MODEL)
    return o @ w_o


def conv_glu_ffn(h, w_up, w_conv, w_down):
    up = causal_dwconv(h @ w_up, w_conv)
    a, b = jnp.split(up, 2, axis=-1)
    return (jax.nn.silu(a) * b) @ w_down


def setup_inputs(seed: int = 0) -> dict:
    key = jax.random.key(seed)
    ks = jax.random.split(key, 26)
    f32 = jnp.float32
    L = DEPTH

    def nrm(k, shape, scale):
        return jax.random.normal(k, shape, f32) * scale

    def gain(k, shape):
        return 1.0 + 0.02 * jax.random.normal(k, shape, f32)

    return {
        'x': nrm(ks[0], (BATCH, SEQ, D_MODEL), 1.0),
        'mem': nrm(ks[1], (BATCH, N_MEM, D_MODEL), 1.0),
        'g_mix': gain(ks[2], (L, D_MODEL)),
        'w_in': nrm(ks[3], (L, D_MODEL, IN_COLS), D_MODEL ** -0.5),
        'g_sgu': gain(ks[4], (L, A_WIDTH)),
        'w_s': nrm(ks[5], (L, A_GROUPS, A_BLOCK, A_BLOCK), A_BLOCK ** -0.5),
        'b_s': 1.0 + 0.1 * jax.random.normal(ks[6], (L, A_GROUPS, A_BLOCK), f32),
        'b_fox_f': jax.random.uniform(ks[7], (L, B_HEADS), f32, 1.0, 5.0),
        'w_conv_c': nrm(ks[8], (L, C_CONV, 2 * C_WIDTH), C_CONV ** -0.5),
        'b_mlstm_i': nrm(ks[9], (L, C_HEADS), 0.5),
        'b_mlstm_f': jax.random.uniform(ks[10], (L, C_HEADS), f32, 3.0, 6.0),
        'g_mh': gain(ks[11], (L, C_WIDTH)),
        'w_branch': nrm(ks[12], (L, N_BRANCH, BRANCH_WIDTH, D_MODEL), BRANCH_WIDTH ** -0.5),
        'w_out': nrm(ks[13], (L, D_MODEL, D_MODEL), D_MODEL ** -0.5),
        'g_mem_q': gain(ks[14], (L, D_MODEL)),
        'g_mem_kv': gain(ks[15], (L, D_MODEL)),
        'w_mq': nrm(ks[16], (L, D_MODEL, D_MODEL), D_MODEL ** -0.5),
        'w_mkv': nrm(ks[17], (L, D_MODEL, 2 * D_MODEL), D_MODEL ** -0.5),
        'w_mo': nrm(ks[18], (L, D_MODEL, D_MODEL), D_MODEL ** -0.5),
        'g_ffn': gain(ks[19], (L, D_MODEL)),
        'w_up': nrm(ks[20], (L, D_MODEL, 2 * D_FF), D_MODEL ** -0.5),
        'w_ffn_conv': nrm(ks[21], (L, FFN_CONV, 2 * D_FF), FFN_CONV ** -0.5),
        'w_down': nrm(ks[22], (L, D_FF, D_MODEL), D_FF ** -0.5),
        'g_final': gain(ks[23], (D_MODEL,)),
    }


def reference(x, mem, g_mix, w_in, g_sgu, w_s, b_s, b_fox_f, w_conv_c, b_mlstm_i, b_mlstm_f,
              g_mh, w_branch, w_out, g_mem_q, g_mem_kv, w_mq, w_mkv, w_mo, g_ffn, w_up,
              w_ffn_conv, w_down, g_final):
    B, S, _ = x.shape
    splits = [A_COLS, A_COLS + B_COLS, A_COLS + B_COLS + C_COLS]
    for i in range(DEPTH):
        h = rmsnorm(x, g_mix[i])
        pa, pb, pc, pg = jnp.split(h @ w_in[i], splits, axis=-1)

        u, v = jnp.split(jax.nn.gelu(pa), 2, axis=-1)
        ya = spatial_gating(u, v, g_sgu[i], w_s[i], b_s[i])

        qb = pb[..., :B_WIDTH].reshape(B, S, B_HEADS, B_HEAD_DIM)
        kb = pb[..., B_WIDTH:2 * B_WIDTH].reshape(B, S, B_HEADS, B_HEAD_DIM)
        vb = pb[..., 2 * B_WIDTH:3 * B_WIDTH].reshape(B, S, B_HEADS, B_HEAD_DIM)
        fb = pb[..., 3 * B_WIDTH:] + b_fox_f[i]
        yb = forgetting_attention(qb, kb, vb, fb)

        qk = jax.nn.silu(causal_dwconv(pc[..., :2 * C_WIDTH], w_conv_c[i]))
        qc = qk[..., :C_WIDTH].reshape(B, S, C_HEADS, C_HEAD_DIM)
        kc = qk[..., C_WIDTH:].reshape(B, S, C_HEADS, C_HEAD_DIM)
        vc = pc[..., 2 * C_WIDTH:3 * C_WIDTH].reshape(B, S, C_HEADS, C_HEAD_DIM)
        ic = pc[..., 3 * C_WIDTH:3 * C_WIDTH + C_HEADS] + b_mlstm_i[i]
        fc = pc[..., 3 * C_WIDTH + C_HEADS:3 * C_WIDTH + 2 * C_HEADS] + b_mlstm_f[i]
        oc = pc[..., 3 * C_WIDTH + 2 * C_HEADS:]
        hc = mlstm_chunkwise(qc, kc, vc, ic, fc)
        hc = rmsnorm(hc.reshape(B, S, C_HEADS, C_HEAD_DIM),
                     g_mh[i].reshape(C_HEADS, C_HEAD_DIM)).reshape(B, S, C_WIDTH)
        yc = jax.nn.sigmoid(oc) * hc

        ys = jnp.stack([ya, yb, yc], axis=2)
        branches = jnp.einsum('bsrc,rcd->bsrd', ys, w_branch[i])
        gates = jax.nn.sigmoid(pg).reshape(B, S, N_BRANCH, D_MODEL)
        x = x + jnp.sum(gates * branches, axis=2) @ w_out[i]

        x = x + memory_attention(rmsnorm(x, g_mem_q[i]), rmsnorm(mem, g_mem_kv[i]),
                                 w_mq[i], w_mkv[i], w_mo[i])

        x = x + conv_glu_ffn(rmsnorm(x, g_ffn[i]), w_up[i], w_ffn_conv[i], w_down[i])
    return rmsnorm(x, g_final)
```

```python
import functools

import jax
import jax.numpy as jnp
from jax import lax
from jax.experimental import pallas as pl
from jax.experimental.pallas import tpu as pltpu

F32 = jnp.float32
BF16 = jnp.bfloat16

D_MODEL = 1024
EPS = 1e-6
LANES = 128
CARRY_ROWS = 8

BRANCH_WIDTH = 512
A_BLOCK = 128
A_GROUPS = 4
CHUNK = 64
B_HEADS = 8
B_HEAD_DIM = 64
C_HEADS = 4
C_HEAD_DIM = 128
C_CONV = 4
MLSTM_CHUNK = 128
N_MEM = 256
MEM_HEADS = 4
MEM_HEAD_DIM = 256
D_FF = 2816
FFN_CONV = 3

PROJ_TN = 512
COL_G = 0
COL_A = 3072
COL_BQ = 4096
COL_BK = 4608
COL_BV = 5120
COL_CQ = 5632
COL_CK = 6144
COL_CV = 6656
COL_CO = 7168
PROJ_COLS = 7680
LANE_BF = 0
LANE_CI = 8
LANE_CF = 12
GATE_ROWS = 16

NEG = -1e30
VMEM_LIMIT = 56 * 1024 * 1024


def _cparams(sem):
    return pltpu.CompilerParams(dimension_semantics=sem, vmem_limit_bytes=VMEM_LIMIT)


def _rms(xf, g):
    return xf * lax.rsqrt(jnp.mean(xf * xf, axis=-1, keepdims=True) + EPS) * g


def _sigmoid(x):
    return 1.0 / (1.0 + jnp.exp(-x))


def _gelu_tanh(x):
    return 0.5 * x * (1.0 + jnp.tanh(0.7978845608028654 * (x + 0.044715 * (x * x * x))))


def _dot(a, b):
    return jnp.dot(a, b, preferred_element_type=F32)


def _dot_nt(a, b):
    return lax.dot_general(a, b, (((1,), (1,)), ((), ())), preferred_element_type=F32)


def _const_spec(shape):
    nd = len(shape)
    return pl.BlockSpec(shape, lambda *_: (0,) * nd, pipeline_mode=pl.Buffered(1))


def _proj_kernel(x_ref, g_ref, w_ref, ws_ref, wc_ref, p_ref, gs_ref, h_ref, ext_ref, carry_ref,
                 *, tm, seq):
    i = pl.program_id(0)
    j = pl.program_id(1)

    @pl.when(j == 0)
    def _():
        h = _rms(x_ref[...], g_ref[...]).astype(BF16)
        h_ref[...] = h
        gs_ref[...] = _dot(h, ws_ref[...])

    acc = _dot(h_ref[...], w_ref[...])

    is_sig = (j < COL_A // PROJ_TN) | (j == COL_CO // PROJ_TN)
    is_gelu = (j >= COL_A // PROJ_TN) & (j < COL_BQ // PROJ_TN)
    is_plain = ((j >= COL_BQ // PROJ_TN) & (j < COL_CQ // PROJ_TN)) | (j == COL_CV // PROJ_TN)

    @pl.when(is_sig)
    def _():
        p_ref[...] = _sigmoid(acc).astype(BF16)

    @pl.when(is_gelu)
    def _():
        p_ref[...] = _gelu_tanh(acc).astype(BF16)

    @pl.when(is_plain)
    def _():
        p_ref[...] = acc.astype(BF16)

    def conv_silu(slot):
        seq_start = (i * tm) % seq == 0
        ext_ref[pl.ds(CARRY_ROWS, tm), :] = acc
        ext_ref[pl.ds(0, CARRY_ROWS), :] = jnp.where(seq_start, 0.0, carry_ref[slot])
        carry_ref[slot] = acc[tm - CARRY_ROWS:, :]
        wc = wc_ref[...]
        y = wc[C_CONV - 1:C_CONV, :] * acc
        for d in range(1, C_CONV):
            y = y + wc[C_CONV - 1 - d:C_CONV - d, :] * ext_ref[pl.ds(CARRY_ROWS - d, tm), :]
        p_ref[...] = (y * _sigmoid(y)).astype(BF16)

    @pl.when(j == COL_CQ // PROJ_TN)
    def _():
        conv_silu(0)

    @pl.when(j == COL_CK // PROJ_TN)
    def _():
        conv_silu(1)


def _proj_call(x2d, g, w_main, w_small, w_conv, seq, tm):
    m = x2d.shape[0]
    nj = PROJ_COLS // PROJ_TN
    cq = COL_CQ // PROJ_TN
    return pl.pallas_call(
        functools.partial(_proj_kernel, tm=tm, seq=seq),
        grid=(m // tm, nj),
        in_specs=[
            pl.BlockSpec((tm, D_MODEL), lambda i, j: (i, 0)),
            pl.BlockSpec((1, D_MODEL), lambda i, j: (0, 0)),
            pl.BlockSpec((D_MODEL, PROJ_TN), lambda i, j: (0, j)),
            pl.BlockSpec((D_MODEL, LANES), lambda i, j: (0, 0)),
            pl.BlockSpec((C_CONV, PROJ_TN), lambda i, j: (0, jnp.clip(j - cq, 0, 1))),
        ],
        out_specs=[
            pl.BlockSpec((tm, PROJ_TN), lambda i, j: (i, j)),
            pl.BlockSpec((tm, LANES), lambda i, j: (i, 0)),
        ],
        out_shape=[
            jax.ShapeDtypeStruct((m, PROJ_COLS), BF16),
            jax.ShapeDtypeStruct((m, LANES), F32),
        ],
        scratch_shapes=[
            pltpu.VMEM((tm, D_MODEL), BF16),
            pltpu.VMEM((tm + CARRY_ROWS, PROJ_TN), F32),
            pltpu.VMEM((2, CARRY_ROWS, PROJ_TN), F32),
        ],
        compiler_params=_cparams(("arbitrary", "arbitrary")),
        name="proj",
    )(x2d, g, w_main, w_small, w_conv)


def _gates_kernel(gs_ref, bias_ref, gc_ref, gr_ref, *, seq):
    blk = MLSTM_CHUNK
    row = lax.broadcasted_iota(jnp.int32, (blk, blk), 0)
    col = lax.broadcasted_iota(jnp.int32, (blk, blk), 1)
    tri = (col <= row).astype(F32)
    lane = lax.broadcasted_iota(jnp.int32, (1, LANES), 1)

    def body(r, carry):
        r0 = pl.multiple_of(r * blk, blk)
        raw = gs_ref[pl.ds(r0, blk), :] + bias_ref[...]
        logsig = jnp.minimum(raw, 0.0) - jnp.log1p(jnp.exp(-jnp.abs(raw)))
        local = jnp.dot(tri, logsig, precision=lax.Precision.HIGHEST, preferred_element_type=F32)
        glob = local + carry
        out = jnp.where(lane < LANE_CI, glob, jnp.where(lane < LANE_CF, raw, local))
        gc_ref[pl.ds(r0, blk), :] = out
        gr_ref[:, pl.ds(r0, blk)] = out.T[0:GATE_ROWS, :]
        return glob[blk - 1:blk, :]

    lax.fori_loop(0, seq // blk, body, jnp.zeros((1, LANES), F32))


def _gates_call(gs3d, bias):
    b, seq, _ = gs3d.shape
    return pl.pallas_call(
        functools.partial(_gates_kernel, seq=seq),
        grid=(b,),
        in_specs=[
            pl.BlockSpec((None, seq, LANES), lambda bi: (bi, 0, 0)),
            pl.BlockSpec((1, LANES), lambda bi: (0, 0)),
        ],
        out_specs=[
            pl.BlockSpec((None, seq, LANES), lambda bi: (bi, 0, 0)),
            pl.BlockSpec((None, GATE_ROWS, seq), lambda bi: (bi, 0, 0)),
        ],
        out_shape=[
            jax.ShapeDtypeStruct((b, seq, LANES), F32),
            jax.ShapeDtypeStruct((b, GATE_ROWS, seq), F32),
        ],
        compiler_params=_cparams(("arbitrary",)),
        name="gates",
    )(gs3d, bias)


def _fox_kernel(q_ref, k_ref, v_ref, gc_ref, gr_ref, o_ref, *, t):
    hp = pl.program_id(1)
    qi = pl.program_id(2)
    lane = lax.broadcasted_iota(jnp.int32, (1, LANES), 1)
    low = lane < B_HEAD_DIM
    q = q_ref[...] * jnp.asarray(B_HEAD_DIM ** -0.5, BF16)
    zero = jnp.zeros_like(q)
    qh = (jnp.where(low, q, zero), jnp.where(low, zero, q))
    gc = gc_ref[...]
    cq = tuple(jnp.sum(jnp.where(lane == 2 * hp + h, gc, 0.0), axis=-1, keepdims=True) for h in range(2))
    row = lax.broadcasted_iota(jnp.int32, (t, t), 0)
    col = lax.broadcasted_iota(jnp.int32, (t, t), 1)
    causal = col <= row

    def step(j, carry, masked):
        k0 = pl.multiple_of(j * t, t)
        kb = k_ref[pl.ds(k0, t), :]
        vb = v_ref[pl.ds(k0, t), :]
        new = []
        for h in range(2):
            m, l, acc = carry[h]
            ck = gr_ref[pl.ds(LANE_BF + 2 * hp + h, 1), pl.ds(k0, t)]
            s = _dot_nt(qh[h], kb) + (cq[h] - ck)
            if masked:
                s = jnp.where(causal, s, NEG)
            m_new = jnp.maximum(m, jnp.max(s, axis=-1, keepdims=True))
            p = jnp.exp(s - m_new)
            alpha = jnp.exp(m - m_new)
            l_new = alpha * l + jnp.sum(p, axis=-1, keepdims=True)
            acc_new = alpha * acc + _dot(p.astype(BF16), vb)
            new.append((m_new, l_new, acc_new))
        return tuple(new)

    init = tuple((jnp.full((t, 1), NEG, F32), jnp.zeros((t, 1), F32), jnp.zeros((t, LANES), F32))
                 for _ in range(2))
    carry = lax.fori_loop(0, qi, lambda j, c: step(j, c, False), init)
    (_, l0, a0), (_, l1, a1) = step(qi, carry, True)
    o_ref[...] = jnp.where(low, a0 / l0, a1 / l1).astype(BF16)


def _fox_call(p3d, gc, gr, t):
    b, seq, _ = p3d.shape
    nhp = B_HEADS // 2
    return pl.pallas_call(
        functools.partial(_fox_kernel, t=t),
        grid=(b, nhp, seq // t),
        in_specs=[
            pl.BlockSpec((None, t, LANES), lambda bi, hp, qi: (bi, qi, COL_BQ // LANES + hp)),
            pl.BlockSpec((None, seq, LANES), lambda bi, hp, qi: (bi, 0, COL_BK // LANES + hp)),
            pl.BlockSpec((None, seq, LANES), lambda bi, hp, qi: (bi, 0, COL_BV // LANES + hp)),
            pl.BlockSpec((None, t, LANES), lambda bi, hp, qi: (bi, qi, 0)),
            pl.BlockSpec((None, GATE_ROWS, seq), lambda bi, hp, qi: (bi, 0, 0)),
        ],
        out_specs=pl.BlockSpec((None, t, LANES), lambda bi, hp, qi: (bi, qi, hp)),
        out_shape=jax.ShapeDtypeStruct((b, seq, BRANCH_WIDTH), BF16),
        compiler_params=_cparams(("arbitrary", "arbitrary", "arbitrary")),
        name="fox",
    )(p3d, p3d, p3d, gc, gr)


def _mlstm_kernel(q_ref, k_ref, v_ref, o_ref, gc_ref, gr_ref, gmh_ref, y_ref, c_ref, m_ref, *, ts):
    L = MLSTM_CHUNK
    dh = C_HEAD_DIM
    scale = dh ** -0.5
    si = pl.program_id(1)

    @pl.when(si == 0)
    def _():
        c_ref[...] = jnp.zeros_like(c_ref)
        m_ref[...] = jnp.zeros_like(m_ref)

    row = lax.broadcasted_iota(jnp.int32, (L, L), 0)
    col = lax.broadcasted_iota(jnp.int32, (L, L), 1)
    tri = col <= row
    lane = lax.broadcasted_iota(jnp.int32, (L, LANES), 1)
    ones_col = jnp.where(lane == 0, 1.0, 0.0).astype(BF16)

    def chunk(c, _):
        r0 = pl.multiple_of(c * L, L)
        gcb = gc_ref[pl.ds(r0, L), :]
        for h in range(C_HEADS):
            hs = slice(h * dh, (h + 1) * dh)
            q = q_ref[pl.ds(r0, L), hs]
            k = k_ref[pl.ds(r0, L), hs]
            v = v_ref[pl.ds(r0, L), hs]
            og = o_ref[pl.ds(r0, L), hs].astype(F32)
            ig_col = gcb[:, LANE_CI + h:LANE_CI + h + 1]
            b_col = gcb[:, LANE_CF + h:LANE_CF + h + 1]
            ig_row = gr_ref[pl.ds(LANE_CI + h, 1), pl.ds(r0, L)]
            b_row = gr_ref[pl.ds(LANE_CF + h, 1), pl.ds(r0, L)]
            g = b_row[:, L - 1:L]
            m_in = m_ref[h][0:1, 0:1]
            c_in = c_ref[h]
            v_ext = jnp.concatenate([v, ones_col], axis=1)

            dlog = jnp.where(tri, b_col - (b_row - ig_row), NEG)
            inter = b_col + m_in
            m_t = jnp.maximum(jnp.max(dlog, axis=-1, keepdims=True), inter)
            sm = (_dot_nt(q, k) * scale) * jnp.exp(dlog - m_t)
            w_int = jnp.exp(inter - m_t)
            ext = _dot(sm.astype(BF16), v_ext) + w_int * _dot(q, c_in.astype(BF16))
            num = ext[:, :dh]
            den = ext[:, dh:dh + 1]
            hh = num / jnp.maximum(jnp.abs(den), jnp.exp(-m_t))
            y_ref[pl.ds(r0, L), hs] = (og * _rms(hh, gmh_ref[:, hs])).astype(BF16)

            a_col = g - b_col + ig_col
            m_new = jnp.maximum(g + m_in, jnp.max(a_col, axis=0, keepdims=True))
            a_old = jnp.exp(g + m_in - m_new)
            kw = k.astype(F32) * (jnp.exp(a_col - m_new) * scale)
            c_ref[h] = a_old * c_in + _dot(kw.T.astype(BF16), v_ext)
            m_ref[h] = jnp.broadcast_to(m_new, m_ref.shape[1:])
        return 0

    lax.fori_loop(0, ts // L, chunk, 0)


def _mlstm_call(p3d, gc, gr, g_mh, ts):
    b, seq, _ = p3d.shape
    w = BRANCH_WIDTH

    def pspec(col):
        return pl.BlockSpec((None, ts, w), lambda bi, si: (bi, si, col // w))

    return pl.pallas_call(
        functools.partial(_mlstm_kernel, ts=ts),
        grid=(b, seq // ts),
        in_specs=[
            pspec(COL_CQ), pspec(COL_CK), pspec(COL_CV), pspec(COL_CO),
            pl.BlockSpec((None, ts, LANES), lambda bi, si: (bi, si, 0)),
            pl.BlockSpec((None, GATE_ROWS, ts), lambda bi, si: (bi, 0, si)),
            pl.BlockSpec((1, w), lambda bi, si: (0, 0)),
        ],
        out_specs=pl.BlockSpec((None, ts, w), lambda bi, si: (bi, si, 0)),
        out_shape=jax.ShapeDtypeStruct((b, seq, w), BF16),
        scratch_shapes=[
            pltpu.VMEM((C_HEADS, C_HEAD_DIM, 2 * C_HEAD_DIM), F32),
            pltpu.VMEM((C_HEADS, CARRY_ROWS, LANES), F32),
        ],
        compiler_params=_cparams(("arbitrary", "arbitrary")),
        name="mlstm",
    )(p3d, p3d, p3d, p3d, gc, gr, g_mh)


def _merge_kernel(gates_ref, uv_ref, yb_ref, yc_ref, x_ref, gsgu_ref, ws_ref, bs_ref, wb_ref, wo_ref,
                  out_ref, ya_ref, *, tm):
    w = BRANCH_WIDTH
    u = uv_ref[:, :w].astype(F32)
    v = uv_ref[:, w:].astype(F32)
    vn = _rms(v, gsgu_ref[...]).astype(BF16)
    gd = w // A_GROUPS
    for nb in range(tm // A_BLOCK):
        rs = slice(nb * A_BLOCK, (nb + 1) * A_BLOCK)
        mixed = jnp.concatenate(
            [_dot(ws_ref[g], vn[rs, g * gd:(g + 1) * gd]) for g in range(A_GROUPS)], axis=1)
        ya_ref[rs, :] = (u[rs, :] * (mixed + bs_ref[...])).astype(BF16)
    merged = gates_ref[:, 0:D_MODEL].astype(F32) * _dot(ya_ref[...], wb_ref[0])
    merged += gates_ref[:, D_MODEL:2 * D_MODEL].astype(F32) * _dot(yb_ref[...], wb_ref[1])
    merged += gates_ref[:, 2 * D_MODEL:3 * D_MODEL].astype(F32) * _dot(yc_ref[...], wb_ref[2])
    out_ref[...] = x_ref[...] + _dot(merged.astype(BF16), wo_ref[...])


def _merge_call(p2d, yb, yc, x2d, g_sgu, ws_masked, bs_full, w_branch, w_out, tm):
    m = x2d.shape[0]
    w = BRANCH_WIDTH
    return pl.pallas_call(
        functools.partial(_merge_kernel, tm=tm),
        grid=(m // tm,),
        in_specs=[
            pl.BlockSpec((tm, 3 * D_MODEL), lambda i: (i, COL_G // (3 * D_MODEL))),
            pl.BlockSpec((tm, 2 * w), lambda i: (i, COL_A // (2 * w))),
            pl.BlockSpec((tm, w), lambda i: (i, 0)),
            pl.BlockSpec((tm, w), lambda i: (i, 0)),
            pl.BlockSpec((tm, D_MODEL), lambda i: (i, 0)),
            _const_spec((1, w)),
            _const_spec((A_GROUPS, A_BLOCK, A_BLOCK)),
            _const_spec((A_BLOCK, w)),
            _const_spec((3, w, D_MODEL)),
            _const_spec((D_MODEL, D_MODEL)),
        ],
        out_specs=pl.BlockSpec((tm, D_MODEL), lambda i: (i, 0)),
        out_shape=jax.ShapeDtypeStruct((m, D_MODEL), F32),
        scratch_shapes=[pltpu.VMEM((tm, w), BF16)],
        compiler_params=_cparams(("arbitrary",)),
        name="merge",
    )(p2d, p2d, yb, yc, x2d, g_sgu, ws_masked, bs_full, w_branch, w_out)


def _memkv_kernel(mem_ref, g_ref, w_ref, kv_ref):
    kv_ref[...] = _dot(_rms(mem_ref[...], g_ref[...]).astype(BF16), w_ref[...]).astype(BF16)


def _memkv_call(mem2d, g, w_mkv):
    m = mem2d.shape[0]
    return pl.pallas_call(
        _memkv_kernel,
        grid=(m // N_MEM,),
        in_specs=[
            pl.BlockSpec((N_MEM, D_MODEL), lambda i: (i, 0)),
            _const_spec((1, D_MODEL)),
            _const_spec((D_MODEL, 2 * D_MODEL)),
        ],
        out_specs=pl.BlockSpec((N_MEM, 2 * D_MODEL), lambda i: (i, 0)),
        out_shape=jax.ShapeDtypeStruct((m, 2 * D_MODEL), BF16),
        compiler_params=_cparams(("arbitrary",)),
        name="memkv",
    )(mem2d, g, w_mkv)


def _memattn_kernel(x_ref, g_ref, wq_ref, kv_ref, wo_ref, out_ref, o_ref):
    x = x_ref[...]
    h = _rms(x, g_ref[...]).astype(BF16)
    q = (_dot(h, wq_ref[...]) * (MEM_HEAD_DIM ** -0.5)).astype(BF16)
    dh = MEM_HEAD_DIM
    for hd in range(MEM_HEADS):
        hs = slice(hd * dh, (hd + 1) * dh)
        s = _dot_nt(q[:, hs], kv_ref[:, hs])
        p = jnp.exp(s - jnp.max(s, axis=-1, keepdims=True))
        o = _dot(p.astype(BF16), kv_ref[:, D_MODEL + hd * dh:D_MODEL + (hd + 1) * dh])
        o_ref[:, hs] = (o / jnp.sum(p, axis=-1, keepdims=True)).astype(BF16)
    out_ref[...] = x + _dot(o_ref[...], wo_ref[...])


def _memattn_call(x2d, g, w_mq, kv, w_mo, seq, tm):
    m = x2d.shape[0]
    return pl.pallas_call(
        _memattn_kernel,
        grid=(m // tm,),
        in_specs=[
            pl.BlockSpec((tm, D_MODEL), lambda i: (i, 0)),
            _const_spec((1, D_MODEL)),
            _const_spec((D_MODEL, D_MODEL)),
            pl.BlockSpec((N_MEM, 2 * D_MODEL), lambda i: ((i * tm) // seq, 0)),
            _const_spec((D_MODEL, D_MODEL)),
        ],
        out_specs=pl.BlockSpec((tm, D_MODEL), lambda i: (i, 0)),
        out_shape=jax.ShapeDtypeStruct((m, D_MODEL), F32),
        scratch_shapes=[pltpu.VMEM((tm, D_MODEL), BF16)],
        compiler_params=_cparams(("arbitrary",)),
        name="memattn",
    )(x2d, g, w_mq, kv, w_mo)


def _ffn_kernel(x_ref, g_ref, wup_ref, wconv_ref, wdown_ref, gfin_ref, out_ref,
                act_ref, ext_ref, carry_ref, *, tm, tf, seq, final_norm):
    i = pl.program_id(0)
    x = x_ref[...]
    h = _rms(x, g_ref[...]).astype(BF16)
    seq_start = (i * tm) % seq == 0

    def conv(slot, half, cs):
        up = _dot(h, wup_ref[:, cs])
        ext_ref[half, pl.ds(CARRY_ROWS, tm), :] = up
        ext_ref[half, pl.ds(0, CARRY_ROWS), :] = jnp.where(seq_start, 0.0, carry_ref[slot])
        carry_ref[slot] = up[tm - CARRY_ROWS:, :]
        wc = wconv_ref[:, cs]
        y = wc[FFN_CONV - 1:FFN_CONV, :] * up
        for d in range(1, FFN_CONV):
            y = y + wc[FFN_CONV - 1 - d:FFN_CONV - d, :] * ext_ref[half, pl.ds(CARRY_ROWS - d, tm), :]
        return y

    nchunk = D_FF // tf
    for c in range(nchunk):
        a = conv(c, 0, slice(c * tf, (c + 1) * tf))
        b = conv(nchunk + c, 1, slice(D_FF + c * tf, D_FF + (c + 1) * tf))
        act_ref[:, c * tf:(c + 1) * tf] = (a * _sigmoid(a) * b).astype(BF16)
    y = x + _dot(act_ref[...], wdown_ref[...])
    if final_norm:
        y = _rms(y, gfin_ref[...])
    out_ref[...] = y


def _ffn_call(x2d, g, w_up, w_conv, w_down, g_final, seq, tm, tf, final_norm):
    m = x2d.shape[0]
    return pl.pallas_call(
        functools.partial(_ffn_kernel, tm=tm, tf=tf, seq=seq, final_norm=final_norm),
        grid=(m // tm,),
        in_specs=[
            pl.BlockSpec((tm, D_MODEL), lambda i: (i, 0)),
            _const_spec((1, D_MODEL)),
            _const_spec((D_MODEL, 2 * D_FF)),
            _const_spec((FFN_CONV, 2 * D_FF)),
            _const_spec((D_FF, D_MODEL)),
            _const_spec((1, D_MODEL)),
        ],
        out_specs=pl.BlockSpec((tm, D_MODEL), lambda i: (i, 0)),
        out_shape=jax.ShapeDtypeStruct((m, D_MODEL), F32),
        scratch_shapes=[
            pltpu.VMEM((tm, D_FF), BF16),
            pltpu.VMEM((2, tm + CARRY_ROWS, tf), F32),
            pltpu.VMEM((2 * (D_FF // tf), CARRY_ROWS, tf), F32),
        ],
        compiler_params=_cparams(("arbitrary",)),
        name="ffn",
    )(x2d, g, w_up, w_conv, w_down, g_final)


def _rearrange_w_in(w):
    bw = BRANCH_WIDTH
    a0 = 0
    b0 = 2 * bw
    c0 = b0 + 3 * bw + B_HEADS
    g0 = c0 + 3 * bw + 2 * C_HEADS + bw
    co = c0 + 3 * bw + 2 * C_HEADS
    main = jnp.concatenate([
        w[:, g0:g0 + 3 * D_MODEL],
        w[:, a0:a0 + 2 * bw],
        w[:, b0:b0 + 3 * bw],
        w[:, c0:c0 + 3 * bw],
        w[:, co:co + bw],
    ], axis=1).astype(BF16)
    small = jnp.concatenate([
        w[:, b0 + 3 * bw:b0 + 3 * bw + B_HEADS],
        w[:, c0 + 3 * bw:c0 + 3 * bw + 2 * C_HEADS],
    ], axis=1)
    small = jnp.pad(small, ((0, 0), (0, LANES - small.shape[1]))).astype(BF16)
    return main, small


def kernel(x, mem, g_mix, w_in, g_sgu, w_s, b_s, b_fox_f, w_conv_c, b_mlstm_i, b_mlstm_f, g_mh,
           w_branch, w_out, g_mem_q, g_mem_kv, w_mq, w_mkv, w_mo, g_ffn, w_up, w_ffn_conv, w_down,
           g_final):
    bsz, seq, _ = x.shape
    depth = w_in.shape[0]
    m = bsz * seq
    tm_proj = min(1024, seq)
    tm = min(512, seq)
    t_fox = min(256, seq)
    ts_mlstm = min(1024, seq)

    idx = jnp.arange(A_BLOCK)
    chunk_causal = (idx[None, :] // CHUNK) <= (idx[:, None] // CHUNK)

    x2d = x.reshape(m, D_MODEL)
    mem2d = mem.reshape(bsz * N_MEM, D_MODEL)
    row = lambda a: a.reshape(1, -1)

    for i in range(depth):
        w_main, w_small = _rearrange_w_in(w_in[i])
        gate_bias = jnp.pad(jnp.concatenate([b_fox_f[i], b_mlstm_i[i], b_mlstm_f[i]]),
                            (0, LANES - GATE_ROWS)).reshape(1, LANES)
        ws_masked = jnp.where(chunk_causal[None], w_s[i], 0).astype(BF16)
        bs_full = jnp.repeat(b_s[i].T, BRANCH_WIDTH // A_GROUPS, axis=1)

        p2d, gs = _proj_call(x2d, row(g_mix[i]), w_main, w_small, w_conv_c[i], seq, tm_proj)
        p3d = p2d.reshape(bsz, seq, PROJ_COLS)
        gc, gr = _gates_call(gs.reshape(bsz, seq, LANES), gate_bias)
        yb = _fox_call(p3d, gc, gr, t_fox).reshape(m, BRANCH_WIDTH)
        yc = _mlstm_call(p3d, gc, gr, row(g_mh[i]), ts_mlstm).reshape(m, BRANCH_WIDTH)
        x2d = _merge_call(p2d, yb, yc, x2d, row(g_sgu[i]), ws_masked, bs_full,
                          w_branch[i].astype(BF16), w_out[i].astype(BF16), tm)

        kv = _memkv_call(mem2d, row(g_mem_kv[i]), w_mkv[i].astype(BF16))
        x2d = _memattn_call(x2d, row(g_mem_q[i]), w_mq[i].astype(BF16), kv, w_mo[i].astype(BF16), seq, tm)

        x2d = _ffn_call(x2d, row(g_ffn[i]), w_up[i].astype(BF16), w_ffn_conv[i], w_down[i].astype(BF16),
                        row(g_final), seq, tm, 256, i == depth - 1)
    return x2d.reshape(bsz, seq, D_MODEL)
```

```python
import functools

import jax
import jax.numpy as jnp
from jax import lax
from jax.experimental import pallas as pl
from jax.experimental.pallas import tpu as pltpu

F32 = jnp.float32
BF16 = jnp.bfloat16

D_MODEL = 1024
EPS = 1e-6
LANES = 128
CARRY_ROWS = 8

BRANCH_WIDTH = 512
A_BLOCK = 128
A_GROUPS = 4
CHUNK = 64
B_HEADS = 8
B_HEAD_DIM = 64
C_HEADS = 4
C_HEAD_DIM = 128
C_CONV = 4
MLSTM_CHUNK = 128
N_MEM = 256
MEM_HEADS = 4
MEM_HEAD_DIM = 256
D_FF = 2816
FFN_CONV = 3

PROJ_TN = 512
COL_G = 0
COL_A = 3072
COL_BQ = 4096
COL_BK = 4608
COL_BV = 5120
COL_CQ = 5632
COL_CK = 6144
COL_CV = 6656
COL_CO = 7168
PROJ_COLS = 7680
LANE_BF = 0
LANE_CI = 8
LANE_CF = 12
GATE_ROWS = 16

NEG = -1e30
VMEM_LIMIT = 56 * 1024 * 1024


def _cparams(sem):
    return pltpu.CompilerParams(dimension_semantics=sem, vmem_limit_bytes=VMEM_LIMIT)


def _rms(xf, g):
    return xf * lax.rsqrt(jnp.mean(xf * xf, axis=-1, keepdims=True) + EPS) * g


def _sigmoid(x):
    return 1.0 / (1.0 + jnp.exp(-x))


def _gelu_tanh(x):
    return 0.5 * x * (1.0 + jnp.tanh(0.7978845608028654 * (x + 0.044715 * (x * x * x))))


def _dot(a, b):
    return jnp.dot(a, b, preferred_element_type=F32)


def _dot_nt(a, b):
    return lax.dot_general(a, b, (((1,), (1,)), ((), ())), preferred_element_type=F32)


def _const_spec(shape):
    nd = len(shape)
    return pl.BlockSpec(shape, lambda *_: (0,) * nd, pipeline_mode=pl.Buffered(1))


def _proj_kernel(x_ref, g_ref, w_ref, ws_ref, wc_ref, p_ref, gs_ref, h_ref, ext_ref, carry_ref,
                 *, tm, seq):
    i = pl.program_id(0)
    j = pl.program_id(1)

    @pl.when(j == 0)
    def _():
        h = _rms(x_ref[...], g_ref[...]).astype(BF16)
        h_ref[...] = h
        gs_ref[...] = _dot(h, ws_ref[...])

    acc = _dot(h_ref[...], w_ref[...])

    is_sig = (j < COL_A // PROJ_TN) | (j == COL_CO // PROJ_TN)
    is_gelu = (j >= COL_A // PROJ_TN) & (j < COL_BQ // PROJ_TN)
    is_plain = ((j >= COL_BQ // PROJ_TN) & (j < COL_CQ // PROJ_TN)) | (j == COL_CV // PROJ_TN)

    @pl.when(is_sig)
    def _():
        p_ref[...] = _sigmoid(acc).astype(BF16)

    @pl.when(is_gelu)
    def _():
        p_ref[...] = _gelu_tanh(acc).astype(BF16)

    @pl.when(is_plain)
    def _():
        p_ref[...] = acc.astype(BF16)

    def conv_silu(slot):
        seq_start = (i * tm) % seq == 0
        ext_ref[pl.ds(CARRY_ROWS, tm), :] = acc
        ext_ref[pl.ds(0, CARRY_ROWS), :] = jnp.where(seq_start, 0.0, carry_ref[slot])
        carry_ref[slot] = acc[tm - CARRY_ROWS:, :]
        wc = wc_ref[...]
        y = wc[C_CONV - 1:C_CONV, :] * acc
        for d in range(1, C_CONV):
            y = y + wc[C_CONV - 1 - d:C_CONV - d, :] * ext_ref[pl.ds(CARRY_ROWS - d, tm), :]
        p_ref[...] = (y * _sigmoid(y)).astype(BF16)

    @pl.when(j == COL_CQ // PROJ_TN)
    def _():
        conv_silu(0)

    @pl.when(j == COL_CK // PROJ_TN)
    def _():
        conv_silu(1)


def _proj_call(x2d, g, w_main, w_small, w_conv, seq, tm):
    m = x2d.shape[0]
    nj = PROJ_COLS // PROJ_TN
    cq = COL_CQ // PROJ_TN
    return pl.pallas_call(
        functools.partial(_proj_kernel, tm=tm, seq=seq),
        grid=(m // tm, nj),
        in_specs=[
            pl.BlockSpec((tm, D_MODEL), lambda i, j: (i, 0)),
            pl.BlockSpec((1, D_MODEL), lambda i, j: (0, 0)),
            pl.BlockSpec((D_MODEL, PROJ_TN), lambda i, j: (0, j)),
            pl.BlockSpec((D_MODEL, LANES), lambda i, j: (0, 0)),
            pl.BlockSpec((C_CONV, PROJ_TN), lambda i, j: (0, jnp.clip(j - cq, 0, 1))),
        ],
        out_specs=[
            pl.BlockSpec((tm, PROJ_TN), lambda i, j: (i, j)),
            pl.BlockSpec((tm, LANES), lambda i, j: (i, 0)),
        ],
        out_shape=[
            jax.ShapeDtypeStruct((m, PROJ_COLS), BF16),
            jax.ShapeDtypeStruct((m, LANES), F32),
        ],
        scratch_shapes=[
            pltpu.VMEM((tm, D_MODEL), BF16),
            pltpu.VMEM((tm + CARRY_ROWS, PROJ_TN), F32),
            pltpu.VMEM((2, CARRY_ROWS, PROJ_TN), F32),
        ],
        compiler_params=_cparams(("arbitrary", "arbitrary")),
        name="proj",
    )(x2d, g, w_main, w_small, w_conv)


def _gates_kernel(gs_ref, bias_ref, gc_ref, gr_ref, *, seq):
    blk = MLSTM_CHUNK
    row = lax.broadcasted_iota(jnp.int32, (blk, blk), 0)
    col = lax.broadcasted_iota(jnp.int32, (blk, blk), 1)
    tri = (col <= row).astype(F32)
    lane = lax.broadcasted_iota(jnp.int32, (1, LANES), 1)

    def body(r, carry):
        r0 = pl.multiple_of(r * blk, blk)
        raw = gs_ref[pl.ds(r0, blk), :] + bias_ref[...]
        logsig = jnp.minimum(raw, 0.0) - jnp.log1p(jnp.exp(-jnp.abs(raw)))
        local = jnp.dot(tri, logsig, precision=lax.Precision.HIGHEST, preferred_element_type=F32)
        glob = local + carry
        out = jnp.where(lane < LANE_CI, glob, jnp.where(lane < LANE_CF, raw, local))
        gc_ref[pl.ds(r0, blk), :] = out
        gr_ref[:, pl.ds(r0, blk)] = out.T[0:GATE_ROWS, :]
        return glob[blk - 1:blk, :]

    lax.fori_loop(0, seq // blk, body, jnp.zeros((1, LANES), F32))


def _gates_call(gs3d, bias):
    b, seq, _ = gs3d.shape
    return pl.pallas_call(
        functools.partial(_gates_kernel, seq=seq),
        grid=(b,),
        in_specs=[
            pl.BlockSpec((None, seq, LANES), lambda bi: (bi, 0, 0)),
            pl.BlockSpec((1, LANES), lambda bi: (0, 0)),
        ],
        out_specs=[
            pl.BlockSpec((None, seq, LANES), lambda bi: (bi, 0, 0)),
            pl.BlockSpec((None, GATE_ROWS, seq), lambda bi: (bi, 0, 0)),
        ],
        out_shape=[
            jax.ShapeDtypeStruct((b, seq, LANES), F32),
            jax.ShapeDtypeStruct((b, GATE_ROWS, seq), F32),
        ],
        compiler_params=_cparams(("arbitrary",)),
        name="gates",
    )(gs3d, bias)


FOX_VROWS = 80
FOX_HEADS = 8


def _fox_kernel(q_ref, k_ref, v_ref, gc_ref, o_ref, kaug_ref, vt_ref, *, t, seq):
    hg = pl.program_id(1)
    qi = pl.program_id(2)
    hd = B_HEAD_DIM
    nh = FOX_HEADS
    lane = lax.broadcasted_iota(jnp.int32, (1, LANES), 1)
    own = (lane < hd, lane >= hd)
    aug0 = (hd, 0)

    @pl.when(qi == 0)
    def _():
        sub = lax.broadcasted_iota(jnp.int32, (FOX_VROWS - hd, seq), 0)
        tail = jnp.where(sub == 0, 1.0, 0.0).astype(BF16)
        for hh in range(nh):
            vt_ref[hh, hd:FOX_VROWS, :] = tail

        def body(r, _):
            r0 = pl.multiple_of(r * LANES, LANES)
            g = gc_ref[pl.ds(r0, LANES), :]
            for pp in range(nh // 2):
                ls = slice(pp * LANES, (pp + 1) * LANES)
                kb = k_ref[pl.ds(r0, LANES), ls].astype(F32)
                vt = v_ref[pl.ds(r0, LANES), ls].astype(F32).T
                for h in range(2):
                    hh = 2 * pp + h
                    neg = -jnp.sum(jnp.where(lane == LANE_BF + nh * hg + hh, g, 0.0), axis=-1, keepdims=True)
                    hi = neg.astype(BF16).astype(F32)
                    mid = (neg - hi).astype(BF16).astype(F32)
                    lo = (neg - hi) - mid
                    a = aug0[h]
                    extra = jnp.where(lane == a, hi, jnp.where(lane == a + 1, mid, jnp.where(lane == a + 2, lo, 0.0)))
                    kaug_ref[hh, pl.ds(r0, LANES), :] = jnp.where(own[h], kb, extra).astype(BF16)
                    vt_ref[hh, 0:hd, pl.ds(r0, LANES)] = vt[h * hd:(h + 1) * hd, :].astype(BF16)
            return 0

        lax.fori_loop(0, seq // LANES, body, 0)

    qa = []
    for pp in range(nh // 2):
        q = q_ref[:, pp * LANES:(pp + 1) * LANES].astype(F32) * (hd ** -0.5)
        for h in range(2):
            a = aug0[h]
            ones3 = jnp.where((lane >= a) & (lane < a + 3), 1.0, 0.0)
            qa.append(jnp.where(own[h], q, ones3).astype(BF16))
    row = lax.broadcasted_iota(jnp.int32, (t, t), 0)
    col = lax.broadcasted_iota(jnp.int32, (t, t), 1)
    causal = row <= col

    def scores(j):
        k0 = pl.multiple_of(j * t, t)
        return tuple(_dot_nt(kaug_ref[hh, pl.ds(k0, t), :], qa[hh]) for hh in range(nh))

    def update(j, st, state, masked):
        k0 = pl.multiple_of(j * t, t)
        ms, ps, alphas = [], [], []
        for hh in range(nh):
            m = state[hh][0]
            s = jnp.where(causal, st[hh], NEG) if masked else st[hh]
            m_new = jnp.maximum(m, jnp.max(s, axis=0, keepdims=True))
            ps.append(jnp.exp(s - m_new).astype(BF16))
            alphas.append(jnp.exp(m - m_new))
            ms.append(m_new)
        pvs = [_dot(vt_ref[hh, :, pl.ds(k0, t)], ps[hh]) for hh in range(nh)]
        return tuple((ms[hh], alphas[hh] * state[hh][1] + pvs[hh]) for hh in range(nh))

    init = tuple((jnp.full((1, t), NEG, F32), jnp.zeros((FOX_VROWS, t), F32)) for _ in range(nh))
    state = lax.fori_loop(0, qi, lambda j, state: update(j, scores(j), state, False), init)
    state = update(qi, scores(qi), state, True)
    ot = jnp.concatenate([acc[0:hd] / acc[hd:hd + 1] for _, acc in state], axis=0)
    o_ref[...] = ot.T.astype(BF16)


def _fox_call(p3d, gc, t):
    b, seq, _ = p3d.shape
    w = FOX_HEADS * B_HEAD_DIM
    return pl.pallas_call(
        functools.partial(_fox_kernel, t=t, seq=seq),
        grid=(b, B_HEADS // FOX_HEADS, seq // t),
        in_specs=[
            pl.BlockSpec((None, t, w), lambda bi, hg, qi: (bi, qi, COL_BQ // w + hg)),
            pl.BlockSpec((None, seq, w), lambda bi, hg, qi: (bi, 0, COL_BK // w + hg)),
            pl.BlockSpec((None, seq, w), lambda bi, hg, qi: (bi, 0, COL_BV // w + hg)),
            pl.BlockSpec((None, seq, LANES), lambda bi, hg, qi: (bi, 0, 0)),
        ],
        out_specs=pl.BlockSpec((None, t, w), lambda bi, hg, qi: (bi, qi, hg)),
        out_shape=jax.ShapeDtypeStruct((b, seq, BRANCH_WIDTH), BF16),
        scratch_shapes=[
            pltpu.VMEM((FOX_HEADS, seq, LANES), BF16),
            pltpu.VMEM((FOX_HEADS, FOX_VROWS, seq), BF16),
        ],
        compiler_params=_cparams(("arbitrary", "arbitrary", "arbitrary")),
        name="fox",
    )(p3d, p3d, p3d, gc)


def _mlstm_kernel(q_ref, k_ref, v_ref, o_ref, gc_ref, gr_ref, gmh_ref, y_ref, c_ref, m_ref, *, ts):
    L = MLSTM_CHUNK
    dh = C_HEAD_DIM
    scale = dh ** -0.5
    si = pl.program_id(1)

    @pl.when(si == 0)
    def _():
        c_ref[...] = jnp.zeros_like(c_ref)
        m_ref[...] = jnp.zeros_like(m_ref)

    row = lax.broadcasted_iota(jnp.int32, (L, L), 0)
    col = lax.broadcasted_iota(jnp.int32, (L, L), 1)
    tri = col <= row
    lane = lax.broadcasted_iota(jnp.int32, (L, LANES), 1)
    ones_col = jnp.where(lane == 0, 1.0, 0.0).astype(BF16)

    def chunk(c, _):
        r0 = pl.multiple_of(c * L, L)
        gcb = gc_ref[pl.ds(r0, L), :]
        for h in range(C_HEADS):
            hs = slice(h * dh, (h + 1) * dh)
            q = q_ref[pl.ds(r0, L), hs]
            k = k_ref[pl.ds(r0, L), hs]
            v = v_ref[pl.ds(r0, L), hs]
            og = o_ref[pl.ds(r0, L), hs].astype(F32)
            ig_col = gcb[:, LANE_CI + h:LANE_CI + h + 1]
            b_col = gcb[:, LANE_CF + h:LANE_CF + h + 1]
            ig_row = gr_ref[pl.ds(LANE_CI + h, 1), pl.ds(r0, L)]
            b_row = gr_ref[pl.ds(LANE_CF + h, 1), pl.ds(r0, L)]
            g = b_row[:, L - 1:L]
            m_in = m_ref[h][0:1, 0:1]
            c_in = c_ref[h]
            v_ext = jnp.concatenate([v, ones_col], axis=1)

            dlog = jnp.where(tri, b_col - (b_row - ig_row), NEG)
            inter = b_col + m_in
            m_t = jnp.maximum(jnp.max(dlog, axis=-1, keepdims=True), inter)
            sm = (_dot_nt(q, k) * scale) * jnp.exp(dlog - m_t)
            w_int = jnp.exp(inter - m_t)
            ext = _dot(sm.astype(BF16), v_ext) + w_int * _dot(q, c_in.astype(BF16))
            num = ext[:, :dh]
            den = ext[:, dh:dh + 1]
            hh = num / jnp.maximum(jnp.abs(den), jnp.exp(-m_t))
            y_ref[pl.ds(r0, L), hs] = (og * _rms(hh, gmh_ref[:, hs])).astype(BF16)

            a_col = g - b_col + ig_col
            m_new = jnp.maximum(g + m_in, jnp.max(a_col, axis=0, keepdims=True))
            a_old = jnp.exp(g + m_in - m_new)
            kw = k.astype(F32) * (jnp.exp(a_col - m_new) * scale)
            c_ref[h] = a_old * c_in + _dot(kw.T.astype(BF16), v_ext)
            m_ref[h] = jnp.broadcast_to(m_new, m_ref.shape[1:])
        return 0

    lax.fori_loop(0, ts // L, chunk, 0)


def _mlstm_call(p3d, gc, gr, g_mh, ts):
    b, seq, _ = p3d.shape
    w = BRANCH_WIDTH

    def pspec(col):
        return pl.BlockSpec((None, ts, w), lambda bi, si: (bi, si, col // w))

    return pl.pallas_call(
        functools.partial(_mlstm_kernel, ts=ts),
        grid=(b, seq // ts),
        in_specs=[
            pspec(COL_CQ), pspec(COL_CK), pspec(COL_CV), pspec(COL_CO),
            pl.BlockSpec((None, ts, LANES), lambda bi, si: (bi, si, 0)),
            pl.BlockSpec((None, GATE_ROWS, ts), lambda bi, si: (bi, 0, si)),
            pl.BlockSpec((1, w), lambda bi, si: (0, 0)),
        ],
        out_specs=pl.BlockSpec((None, ts, w), lambda bi, si: (bi, si, 0)),
        out_shape=jax.ShapeDtypeStruct((b, seq, w), BF16),
        scratch_shapes=[
            pltpu.VMEM((C_HEADS, C_HEAD_DIM, 2 * C_HEAD_DIM), F32),
            pltpu.VMEM((C_HEADS, CARRY_ROWS, LANES), F32),
        ],
        compiler_params=_cparams(("arbitrary", "arbitrary")),
        name="mlstm",
    )(p3d, p3d, p3d, p3d, gc, gr, g_mh)


def _merge_kernel(gates_ref, uv_ref, yb_ref, yc_ref, x_ref, gsgu_ref, ws_ref, bs_ref, wb_ref, wo_ref,
                  out_ref, ya_ref, *, tm):
    w = BRANCH_WIDTH
    u = uv_ref[:, :w].astype(F32)
    v = uv_ref[:, w:].astype(F32)
    vn = _rms(v, gsgu_ref[...]).astype(BF16)
    gd = w // A_GROUPS
    for nb in range(tm // A_BLOCK):
        rs = slice(nb * A_BLOCK, (nb + 1) * A_BLOCK)
        mixed = jnp.concatenate(
            [_dot(ws_ref[g], vn[rs, g * gd:(g + 1) * gd]) for g in range(A_GROUPS)], axis=1)
        ya_ref[rs, :] = (u[rs, :] * (mixed + bs_ref[...])).astype(BF16)
    merged = gates_ref[:, 0:D_MODEL].astype(F32) * _dot(ya_ref[...], wb_ref[0])
    merged += gates_ref[:, D_MODEL:2 * D_MODEL].astype(F32) * _dot(yb_ref[...], wb_ref[1])
    merged += gates_ref[:, 2 * D_MODEL:3 * D_MODEL].astype(F32) * _dot(yc_ref[...], wb_ref[2])
    out_ref[...] = x_ref[...] + _dot(merged.astype(BF16), wo_ref[...])


def _merge_call(p2d, yb, yc, x2d, g_sgu, ws_masked, bs_full, w_branch, w_out, tm):
    m = x2d.shape[0]
    w = BRANCH_WIDTH
    return pl.pallas_call(
        functools.partial(_merge_kernel, tm=tm),
        grid=(m // tm,),
        in_specs=[
            pl.BlockSpec((tm, 3 * D_MODEL), lambda i: (i, COL_G // (3 * D_MODEL))),
            pl.BlockSpec((tm, 2 * w), lambda i: (i, COL_A // (2 * w))),
            pl.BlockSpec((tm, w), lambda i: (i, 0)),
            pl.BlockSpec((tm, w), lambda i: (i, 0)),
            pl.BlockSpec((tm, D_MODEL), lambda i: (i, 0)),
            _const_spec((1, w)),
            _const_spec((A_GROUPS, A_BLOCK, A_BLOCK)),
            _const_spec((A_BLOCK, w)),
            _const_spec((3, w, D_MODEL)),
            _const_spec((D_MODEL, D_MODEL)),
        ],
        out_specs=pl.BlockSpec((tm, D_MODEL), lambda i: (i, 0)),
        out_shape=jax.ShapeDtypeStruct((m, D_MODEL), F32),
        scratch_shapes=[pltpu.VMEM((tm, w), BF16)],
        compiler_params=_cparams(("arbitrary",)),
        name="merge",
    )(p2d, p2d, yb, yc, x2d, g_sgu, ws_masked, bs_full, w_branch, w_out)


def _memkv_kernel(mem_ref, g_ref, w_ref, kv_ref):
    kv_ref[...] = _dot(_rms(mem_ref[...], g_ref[...]).astype(BF16), w_ref[...]).astype(BF16)


def _memkv_call(mem2d, g, w_mkv):
    m = mem2d.shape[0]
    return pl.pallas_call(
        _memkv_kernel,
        grid=(m // N_MEM,),
        in_specs=[
            pl.BlockSpec((N_MEM, D_MODEL), lambda i: (i, 0)),
            _const_spec((1, D_MODEL)),
            _const_spec((D_MODEL, 2 * D_MODEL)),
        ],
        out_specs=pl.BlockSpec((N_MEM, 2 * D_MODEL), lambda i: (i, 0)),
        out_shape=jax.ShapeDtypeStruct((m, 2 * D_MODEL), BF16),
        compiler_params=_cparams(("arbitrary",)),
        name="memkv",
    )(mem2d, g, w_mkv)


def _memattn_kernel(x_ref, g_ref, wq_ref, kv_ref, wo_ref, out_ref, o_ref):
    x = x_ref[...]
    h = _rms(x, g_ref[...]).astype(BF16)
    q = (_dot(h, wq_ref[...]) * (MEM_HEAD_DIM ** -0.5)).astype(BF16)
    dh = MEM_HEAD_DIM
    for hd in range(MEM_HEADS):
        hs = slice(hd * dh, (hd + 1) * dh)
        s = _dot_nt(q[:, hs], kv_ref[:, hs])
        p = jnp.exp(s - jnp.max(s, axis=-1, keepdims=True))
        o = _dot(p.astype(BF16), kv_ref[:, D_MODEL + hd * dh:D_MODEL + (hd + 1) * dh])
        o_ref[:, hs] = (o / jnp.sum(p, axis=-1, keepdims=True)).astype(BF16)
    out_ref[...] = x + _dot(o_ref[...], wo_ref[...])


def _memattn_call(x2d, g, w_mq, kv, w_mo, seq, tm):
    m = x2d.shape[0]
    return pl.pallas_call(
        _memattn_kernel,
        grid=(m // tm,),
        in_specs=[
            pl.BlockSpec((tm, D_MODEL), lambda i: (i, 0)),
            _const_spec((1, D_MODEL)),
            _const_spec((D_MODEL, D_MODEL)),
            pl.BlockSpec((N_MEM, 2 * D_MODEL), lambda i: ((i * tm) // seq, 0)),
            _const_spec((D_MODEL, D_MODEL)),
        ],
        out_specs=pl.BlockSpec((tm, D_MODEL), lambda i: (i, 0)),
        out_shape=jax.ShapeDtypeStruct((m, D_MODEL), F32),
        scratch_shapes=[pltpu.VMEM((tm, D_MODEL), BF16)],
        compiler_params=_cparams(("arbitrary",)),
        name="memattn",
    )(x2d, g, w_mq, kv, w_mo)


def _ffn_kernel(x_ref, g_ref, wup_ref, wconv_ref, wdown_ref, gfin_ref, out_ref,
                act_ref, ext_ref, carry_ref, *, tm, tf, seq, final_norm):
    i = pl.program_id(0)
    x = x_ref[...]
    h = _rms(x, g_ref[...]).astype(BF16)
    seq_start = (i * tm) % seq == 0

    def conv(slot, half, cs):
        up = _dot(h, wup_ref[:, cs])
        ext_ref[half, pl.ds(CARRY_ROWS, tm), :] = up
        ext_ref[half, pl.ds(0, CARRY_ROWS), :] = jnp.where(seq_start, 0.0, carry_ref[slot])
        carry_ref[slot] = up[tm - CARRY_ROWS:, :]
        wc = wconv_ref[:, cs]
        y = wc[FFN_CONV - 1:FFN_CONV, :] * up
        for d in range(1, FFN_CONV):
            y = y + wc[FFN_CONV - 1 - d:FFN_CONV - d, :] * ext_ref[half, pl.ds(CARRY_ROWS - d, tm), :]
        return y

    nchunk = D_FF // tf
    for c in range(nchunk):
        a = conv(c, 0, slice(c * tf, (c + 1) * tf))
        b = conv(nchunk + c, 1, slice(D_FF + c * tf, D_FF + (c + 1) * tf))
        act_ref[:, c * tf:(c + 1) * tf] = (a * _sigmoid(a) * b).astype(BF16)
    y = x + _dot(act_ref[...], wdown_ref[...])
    if final_norm:
        y = _rms(y, gfin_ref[...])
    out_ref[...] = y


def _ffn_call(x2d, g, w_up, w_conv, w_down, g_final, seq, tm, tf, final_norm):
    m = x2d.shape[0]
    return pl.pallas_call(
        functools.partial(_ffn_kernel, tm=tm, tf=tf, seq=seq, final_norm=final_norm),
        grid=(m // tm,),
        in_specs=[
            pl.BlockSpec((tm, D_MODEL), lambda i: (i, 0)),
            _const_spec((1, D_MODEL)),
            _const_spec((D_MODEL, 2 * D_FF)),
            _const_spec((FFN_CONV, 2 * D_FF)),
            _const_spec((D_FF, D_MODEL)),
            _const_spec((1, D_MODEL)),
        ],
        out_specs=pl.BlockSpec((tm, D_MODEL), lambda i: (i, 0)),
        out_shape=jax.ShapeDtypeStruct((m, D_MODEL), F32),
        scratch_shapes=[
            pltpu.VMEM((tm, D_FF), BF16),
            pltpu.VMEM((2, tm + CARRY_ROWS, tf), F32),
            pltpu.VMEM((2 * (D_FF // tf), CARRY_ROWS, tf), F32),
        ],
        compiler_params=_cparams(("arbitrary",)),
        name="ffn",
    )(x2d, g, w_up, w_conv, w_down, g_final)


def _rearrange_w_in(w):
    bw = BRANCH_WIDTH
    a0 = 0
    b0 = 2 * bw
    c0 = b0 + 3 * bw + B_HEADS
    g0 = c0 + 3 * bw + 2 * C_HEADS + bw
    co = c0 + 3 * bw + 2 * C_HEADS
    main = jnp.concatenate([
        w[:, g0:g0 + 3 * D_MODEL],
        w[:, a0:a0 + 2 * bw],
        w[:, b0:b0 + 3 * bw],
        w[:, c0:c0 + 3 * bw],
        w[:, co:co + bw],
    ], axis=1).astype(BF16)
    small = jnp.concatenate([
        w[:, b0 + 3 * bw:b0 + 3 * bw + B_HEADS],
        w[:, c0 + 3 * bw:c0 + 3 * bw + 2 * C_HEADS],
    ], axis=1)
    small = jnp.pad(small, ((0, 0), (0, LANES - small.shape[1]))).astype(BF16)
    return main, small


def kernel(x, mem, g_mix, w_in, g_sgu, w_s, b_s, b_fox_f, w_conv_c, b_mlstm_i, b_mlstm_f, g_mh,
           w_branch, w_out, g_mem_q, g_mem_kv, w_mq, w_mkv, w_mo, g_ffn, w_up, w_ffn_conv, w_down,
           g_final):
    bsz, seq, _ = x.shape
    depth = w_in.shape[0]
    m = bsz * seq
    tm_proj = min(1024, seq)
    tm = min(512, seq)
    t_fox = min(256, seq)
    ts_mlstm = min(1024, seq)

    idx = jnp.arange(A_BLOCK)
    chunk_causal = (idx[None, :] // CHUNK) <= (idx[:, None] // CHUNK)

    x2d = x.reshape(m, D_MODEL)
    mem2d = mem.reshape(bsz * N_MEM, D_MODEL)
    row = lambda a: a.reshape(1, -1)

    for i in range(depth):
        w_main, w_small = _rearrange_w_in(w_in[i])
        gate_bias = jnp.pad(jnp.concatenate([b_fox_f[i], b_mlstm_i[i], b_mlstm_f[i]]),
                            (0, LANES - GATE_ROWS)).reshape(1, LANES)
        ws_masked = jnp.where(chunk_causal[None], w_s[i], 0).astype(BF16)
        bs_full = jnp.repeat(b_s[i].T, BRANCH_WIDTH // A_GROUPS, axis=1)

        p2d, gs = _proj_call(x2d, row(g_mix[i]), w_main, w_small, w_conv_c[i], seq, tm_proj)
        p3d = p2d.reshape(bsz, seq, PROJ_COLS)
        gc, gr = _gates_call(gs.reshape(bsz, seq, LANES), gate_bias)
        yb = _fox_call(p3d, gc, t_fox).reshape(m, BRANCH_WIDTH)
        yc = _mlstm_call(p3d, gc, gr, row(g_mh[i]), ts_mlstm).reshape(m, BRANCH_WIDTH)
        x2d = _merge_call(p2d, yb, yc, x2d, row(g_sgu[i]), ws_masked, bs_full,
                          w_branch[i].astype(BF16), w_out[i].astype(BF16), tm)

        kv = _memkv_call(mem2d, row(g_mem_kv[i]), w_mkv[i].astype(BF16))
        x2d = _memattn_call(x2d, row(g_mem_q[i]), w_mq[i].astype(BF16), kv, w_mo[i].astype(BF16), seq, tm)

        x2d = _ffn_call(x2d, row(g_ffn[i]), w_up[i].astype(BF16), w_ffn_conv[i], w_down[i].astype(BF16),
                        row(g_final), seq, tm, 256, i == depth - 1)
    return x2d.reshape(bsz, seq, D_MODEL)
```

```python
import functools

import jax
import jax.numpy as jnp
from jax import lax
from jax.experimental import pallas as pl
from jax.experimental.pallas import tpu as pltpu

F32 = jnp.float32
BF16 = jnp.bfloat16

D_MODEL = 1024
EPS = 1e-6
LANES = 128
CARRY_ROWS = 8

BRANCH_WIDTH = 512
A_BLOCK = 128
A_GROUPS = 4
CHUNK = 64
B_HEADS = 8
B_HEAD_DIM = 64
C_HEADS = 4
C_HEAD_DIM = 128
C_CONV = 4
MLSTM_CHUNK = 128
N_MEM = 256
MEM_HEADS = 4
MEM_HEAD_DIM = 256
D_FF = 2816
FFN_CONV = 3

PROJ_TN = 512
COL_G = 0
COL_A = 3072
COL_BQ = 4096
COL_BK = 4608
COL_BV = 5120
COL_CQ = 5632
COL_CK = 6144
COL_CV = 6656
COL_CO = 7168
PROJ_COLS = 7680
LANE_BF = 0
LANE_CI = 8
LANE_CF = 12
GATE_ROWS = 16

NEG = -1e30
VMEM_LIMIT = 56 * 1024 * 1024


def _cparams(sem):
    return pltpu.CompilerParams(dimension_semantics=sem, vmem_limit_bytes=VMEM_LIMIT)


def _rms(xf, g):
    return xf * lax.rsqrt(jnp.mean(xf * xf, axis=-1, keepdims=True) + EPS) * g


def _sigmoid(x):
    return 0.5 * jnp.tanh(0.5 * x) + 0.5


def _gelu_tanh(x):
    return 0.5 * x * (1.0 + jnp.tanh(0.7978845608028654 * (x + 0.044715 * (x * x * x))))


def _dot(a, b):
    return jnp.dot(a, b, preferred_element_type=F32)


def _dot_nt(a, b):
    return lax.dot_general(a, b, (((1,), (1,)), ((), ())), preferred_element_type=F32)


def _const_spec(shape):
    nd = len(shape)
    return pl.BlockSpec(shape, lambda *_: (0,) * nd, pipeline_mode=pl.Buffered(1))


def _proj_kernel(x_ref, g_ref, w_ref, ws_ref, wc_ref, p_ref, gs_ref, ext_ref, carry_ref, *, tm, seq):
    i = pl.program_id(0)
    h = _rms(x_ref[...], g_ref[...]).astype(BF16)
    gs_ref[...] = _dot(h, ws_ref[...])
    seq_start = (i * tm) % seq == 0

    def conv_silu(acc, slot):
        ext_ref[slot, pl.ds(CARRY_ROWS, tm), :] = acc
        ext_ref[slot, pl.ds(0, CARRY_ROWS), :] = jnp.where(seq_start, 0.0, carry_ref[slot])
        carry_ref[slot] = acc[tm - CARRY_ROWS:, :]
        wc = wc_ref[:, slot * PROJ_TN:(slot + 1) * PROJ_TN]
        y = wc[C_CONV - 1:C_CONV, :] * acc
        for d in range(1, C_CONV):
            y = y + wc[C_CONV - 1 - d:C_CONV - d, :] * ext_ref[slot, pl.ds(CARRY_ROWS - d, tm), :]
        return y * _sigmoid(y)

    for j in range(PROJ_COLS // PROJ_TN):
        c0 = j * PROJ_TN
        cs = slice(c0, c0 + PROJ_TN)
        acc = _dot(h, w_ref[:, cs])
        if c0 < COL_A or c0 == COL_CO:
            out = _sigmoid(acc)
        elif c0 < COL_BQ:
            out = _gelu_tanh(acc)
        elif c0 == COL_CQ:
            out = conv_silu(acc, 0)
        elif c0 == COL_CK:
            out = conv_silu(acc, 1)
        else:
            out = acc
        p_ref[:, cs] = out.astype(BF16)


def _proj_call(x2d, g, w_main, w_small, w_conv, seq, tm):
    m = x2d.shape[0]
    return pl.pallas_call(
        functools.partial(_proj_kernel, tm=tm, seq=seq),
        grid=(m // tm,),
        in_specs=[
            pl.BlockSpec((tm, D_MODEL), lambda i: (i, 0)),
            _const_spec((1, D_MODEL)),
            _const_spec((D_MODEL, PROJ_COLS)),
            _const_spec((D_MODEL, LANES)),
            _const_spec((C_CONV, 2 * PROJ_TN)),
        ],
        out_specs=[
            pl.BlockSpec((tm, PROJ_COLS), lambda i: (i, 0)),
            pl.BlockSpec((tm, LANES), lambda i: (i, 0)),
        ],
        out_shape=[
            jax.ShapeDtypeStruct((m, PROJ_COLS), BF16),
            jax.ShapeDtypeStruct((m, LANES), F32),
        ],
        scratch_shapes=[
            pltpu.VMEM((2, tm + CARRY_ROWS, PROJ_TN), F32),
            pltpu.VMEM((2, CARRY_ROWS, PROJ_TN), F32),
        ],
        compiler_params=_cparams(("arbitrary",)),
        name="proj",
    )(x2d, g, w_main, w_small, w_conv)


def _gates_kernel(gs_ref, bias_ref, gc_ref, gr_ref, *, seq):
    blk = MLSTM_CHUNK
    row = lax.broadcasted_iota(jnp.int32, (blk, blk), 0)
    col = lax.broadcasted_iota(jnp.int32, (blk, blk), 1)
    tri = (col <= row).astype(F32)
    lane = lax.broadcasted_iota(jnp.int32, (1, LANES), 1)

    def body(r, carry):
        r0 = pl.multiple_of(r * blk, blk)
        raw = gs_ref[pl.ds(r0, blk), :] + bias_ref[...]
        logsig = jnp.minimum(raw, 0.0) - jnp.log1p(jnp.exp(-jnp.abs(raw)))
        local = jnp.dot(tri, logsig, precision=lax.Precision.HIGHEST, preferred_element_type=F32)
        glob = local + carry
        out = jnp.where(lane < LANE_CI, glob, jnp.where(lane < LANE_CF, raw, local))
        gc_ref[pl.ds(r0, blk), :] = out
        gr_ref[:, pl.ds(r0, blk)] = out.T[0:GATE_ROWS, :]
        return glob[blk - 1:blk, :]

    lax.fori_loop(0, seq // blk, body, jnp.zeros((1, LANES), F32))


def _gates_call(gs3d, bias):
    b, seq, _ = gs3d.shape
    return pl.pallas_call(
        functools.partial(_gates_kernel, seq=seq),
        grid=(b,),
        in_specs=[
            pl.BlockSpec((None, seq, LANES), lambda bi: (bi, 0, 0)),
            pl.BlockSpec((1, LANES), lambda bi: (0, 0)),
        ],
        out_specs=[
            pl.BlockSpec((None, seq, LANES), lambda bi: (bi, 0, 0)),
            pl.BlockSpec((None, GATE_ROWS, seq), lambda bi: (bi, 0, 0)),
        ],
        out_shape=[
            jax.ShapeDtypeStruct((b, seq, LANES), F32),
            jax.ShapeDtypeStruct((b, GATE_ROWS, seq), F32),
        ],
        compiler_params=_cparams(("arbitrary",)),
        name="gates",
    )(gs3d, bias)


FOX_VROWS = 80
FOX_HEADS = 8


def _fox_kernel(q_ref, k_ref, v_ref, gc_ref, o_ref, kaug_ref, vt_ref, *, t, seq):
    hg = pl.program_id(1)
    qi = pl.program_id(2)
    hd = B_HEAD_DIM
    nh = FOX_HEADS
    lane = lax.broadcasted_iota(jnp.int32, (1, LANES), 1)
    own = (lane < hd, lane >= hd)
    aug0 = (hd, 0)

    @pl.when(qi == 0)
    def _():
        sub = lax.broadcasted_iota(jnp.int32, (FOX_VROWS - hd, seq), 0)
        tail = jnp.where(sub == 0, 1.0, 0.0).astype(BF16)
        for hh in range(nh):
            vt_ref[hh, hd:FOX_VROWS, :] = tail

        def body(r, _):
            r0 = pl.multiple_of(r * LANES, LANES)
            g = gc_ref[pl.ds(r0, LANES), :]
            for pp in range(nh // 2):
                ls = slice(pp * LANES, (pp + 1) * LANES)
                kb = k_ref[pl.ds(r0, LANES), ls].astype(F32)
                vt = v_ref[pl.ds(r0, LANES), ls].astype(F32).T
                for h in range(2):
                    hh = 2 * pp + h
                    neg = -jnp.sum(jnp.where(lane == LANE_BF + nh * hg + hh, g, 0.0), axis=-1, keepdims=True)
                    hi = neg.astype(BF16).astype(F32)
                    mid = (neg - hi).astype(BF16).astype(F32)
                    lo = (neg - hi) - mid
                    a = aug0[h]
                    extra = jnp.where(lane == a, hi, jnp.where(lane == a + 1, mid, jnp.where(lane == a + 2, lo, 0.0)))
                    kaug_ref[hh, pl.ds(r0, LANES), :] = jnp.where(own[h], kb, extra).astype(BF16)
                    vt_ref[hh, 0:hd, pl.ds(r0, LANES)] = vt[h * hd:(h + 1) * hd, :].astype(BF16)
            return 0

        lax.fori_loop(0, seq // LANES, body, 0)

    qa = []
    for pp in range(nh // 2):
        q = q_ref[:, pp * LANES:(pp + 1) * LANES].astype(F32) * (hd ** -0.5)
        for h in range(2):
            a = aug0[h]
            ones3 = jnp.where((lane >= a) & (lane < a + 3), 1.0, 0.0)
            qa.append(jnp.where(own[h], q, ones3).astype(BF16))
    row = lax.broadcasted_iota(jnp.int32, (t, t), 0)
    col = lax.broadcasted_iota(jnp.int32, (t, t), 1)
    causal = row <= col

    def scores(j):
        k0 = pl.multiple_of(j * t, t)
        return tuple(_dot_nt(kaug_ref[hh, pl.ds(k0, t), :], qa[hh]) for hh in range(nh))

    def update(j, st, state, masked):
        k0 = pl.multiple_of(j * t, t)
        ms, ps, alphas = [], [], []
        for hh in range(nh):
            m = state[hh][0]
            s = jnp.where(causal, st[hh], NEG) if masked else st[hh]
            m_new = jnp.maximum(m, jnp.max(s, axis=0, keepdims=True))
            ps.append(jnp.exp(s - m_new).astype(BF16))
            alphas.append(jnp.exp(m - m_new))
            ms.append(m_new)
        pvs = [_dot(vt_ref[hh, :, pl.ds(k0, t)], ps[hh]) for hh in range(nh)]
        return tuple((ms[hh], alphas[hh] * state[hh][1] + pvs[hh]) for hh in range(nh))

    init = tuple((jnp.full((1, t), NEG, F32), jnp.zeros((FOX_VROWS, t), F32)) for _ in range(nh))
    state = lax.fori_loop(0, qi, lambda j, state: update(j, scores(j), state, False), init)
    state = update(qi, scores(qi), state, True)
    ot = jnp.concatenate([acc[0:hd] / acc[hd:hd + 1] for _, acc in state], axis=0)
    o_ref[...] = ot.T.astype(BF16)


def _fox_call(p3d, gc, t):
    b, seq, _ = p3d.shape
    w = FOX_HEADS * B_HEAD_DIM
    return pl.pallas_call(
        functools.partial(_fox_kernel, t=t, seq=seq),
        grid=(b, B_HEADS // FOX_HEADS, seq // t),
        in_specs=[
            pl.BlockSpec((None, t, w), lambda bi, hg, qi: (bi, qi, COL_BQ // w + hg)),
            pl.BlockSpec((None, seq, w), lambda bi, hg, qi: (bi, 0, COL_BK // w + hg)),
            pl.BlockSpec((None, seq, w), lambda bi, hg, qi: (bi, 0, COL_BV // w + hg)),
            pl.BlockSpec((None, seq, LANES), lambda bi, hg, qi: (bi, 0, 0)),
        ],
        out_specs=pl.BlockSpec((None, t, w), lambda bi, hg, qi: (bi, qi, hg)),
        out_shape=jax.ShapeDtypeStruct((b, seq, BRANCH_WIDTH), BF16),
        scratch_shapes=[
            pltpu.VMEM((FOX_HEADS, seq, LANES), BF16),
            pltpu.VMEM((FOX_HEADS, FOX_VROWS, seq), BF16),
        ],
        compiler_params=_cparams(("arbitrary", "arbitrary", "arbitrary")),
        name="fox",
    )(p3d, p3d, p3d, gc)


def _mlstm_kernel(q_ref, k_ref, v_ref, o_ref, gc_ref, gr_ref, gmh_ref, y_ref, c_ref, m_ref, *, ts):
    L = MLSTM_CHUNK
    dh = C_HEAD_DIM
    scale = dh ** -0.5
    si = pl.program_id(1)

    @pl.when(si == 0)
    def _():
        c_ref[...] = jnp.zeros_like(c_ref)
        m_ref[...] = jnp.zeros_like(m_ref)

    row = lax.broadcasted_iota(jnp.int32, (L, L), 0)
    col = lax.broadcasted_iota(jnp.int32, (L, L), 1)
    tri = col <= row
    lane = lax.broadcasted_iota(jnp.int32, (L, LANES), 1)
    ones_col = jnp.where(lane == 0, 1.0, 0.0).astype(BF16)

    def chunk(c, _):
        r0 = pl.multiple_of(c * L, L)
        gcb = gc_ref[pl.ds(r0, L), :]
        for h in range(C_HEADS):
            hs = slice(h * dh, (h + 1) * dh)
            q = q_ref[pl.ds(r0, L), hs]
            k = k_ref[pl.ds(r0, L), hs]
            v = v_ref[pl.ds(r0, L), hs]
            og = o_ref[pl.ds(r0, L), hs].astype(F32)
            ig_col = gcb[:, LANE_CI + h:LANE_CI + h + 1]
            b_col = gcb[:, LANE_CF + h:LANE_CF + h + 1]
            ig_row = gr_ref[pl.ds(LANE_CI + h, 1), pl.ds(r0, L)]
            b_row = gr_ref[pl.ds(LANE_CF + h, 1), pl.ds(r0, L)]
            g = b_row[:, L - 1:L]
            m_in = m_ref[h][0:1, 0:1]
            c_in = c_ref[h]
            v_ext = jnp.concatenate([v, ones_col], axis=1)

            dlog = jnp.where(tri, b_col - (b_row - ig_row), NEG)
            inter = b_col + m_in
            m_t = jnp.maximum(jnp.max(dlog, axis=-1, keepdims=True), inter)
            sm = (_dot_nt(q, k) * scale) * jnp.exp(dlog - m_t)
            w_int = jnp.exp(inter - m_t)
            ext = _dot(sm.astype(BF16), v_ext) + w_int * _dot(q, c_in.astype(BF16))
            num = ext[:, :dh]
            den = ext[:, dh:dh + 1]
            hh = num / jnp.maximum(jnp.abs(den), jnp.exp(-m_t))
            y_ref[pl.ds(r0, L), hs] = (og * _rms(hh, gmh_ref[:, hs])).astype(BF16)

            a_col = g - b_col + ig_col
            m_new = jnp.maximum(g + m_in, jnp.max(a_col, axis=0, keepdims=True))
            a_old = jnp.exp(g + m_in - m_new)
            kw = k.astype(F32) * (jnp.exp(a_col - m_new) * scale)
            c_ref[h] = a_old * c_in + _dot(kw.T.astype(BF16), v_ext)
            m_ref[h] = jnp.broadcast_to(m_new, m_ref.shape[1:])
        return 0

    lax.fori_loop(0, ts // L, chunk, 0)


def _mlstm_call(p3d, gc, gr, g_mh, ts):
    b, seq, _ = p3d.shape
    w = BRANCH_WIDTH

    def pspec(col):
        return pl.BlockSpec((None, ts, w), lambda bi, si: (bi, si, col // w))

    return pl.pallas_call(
        functools.partial(_mlstm_kernel, ts=ts),
        grid=(b, seq // ts),
        in_specs=[
            pspec(COL_CQ), pspec(COL_CK), pspec(COL_CV), pspec(COL_CO),
            pl.BlockSpec((None, ts, LANES), lambda bi, si: (bi, si, 0)),
            pl.BlockSpec((None, GATE_ROWS, ts), lambda bi, si: (bi, 0, si)),
            pl.BlockSpec((1, w), lambda bi, si: (0, 0)),
        ],
        out_specs=pl.BlockSpec((None, ts, w), lambda bi, si: (bi, si, 0)),
        out_shape=jax.ShapeDtypeStruct((b, seq, w), BF16),
        scratch_shapes=[
            pltpu.VMEM((C_HEADS, C_HEAD_DIM, 2 * C_HEAD_DIM), F32),
            pltpu.VMEM((C_HEADS, CARRY_ROWS, LANES), F32),
        ],
        compiler_params=_cparams(("arbitrary", "arbitrary")),
        name="mlstm",
    )(p3d, p3d, p3d, p3d, gc, gr, g_mh)


def _merge_kernel(gates_ref, uv_ref, yb_ref, yc_ref, x_ref, gsgu_ref, ws_ref, bs_ref, wb_ref, wo_ref,
                  out_ref, ya_ref, *, tm):
    w = BRANCH_WIDTH
    u = uv_ref[:, :w].astype(F32)
    v = uv_ref[:, w:].astype(F32)
    vn = _rms(v, gsgu_ref[...]).astype(BF16)
    gd = w // A_GROUPS
    for nb in range(tm // A_BLOCK):
        rs = slice(nb * A_BLOCK, (nb + 1) * A_BLOCK)
        mixed = jnp.concatenate(
            [_dot(ws_ref[g], vn[rs, g * gd:(g + 1) * gd]) for g in range(A_GROUPS)], axis=1)
        ya_ref[rs, :] = (u[rs, :] * (mixed + bs_ref[...])).astype(BF16)
    merged = gates_ref[:, 0:D_MODEL].astype(F32) * _dot(ya_ref[...], wb_ref[0])
    merged += gates_ref[:, D_MODEL:2 * D_MODEL].astype(F32) * _dot(yb_ref[...], wb_ref[1])
    merged += gates_ref[:, 2 * D_MODEL:3 * D_MODEL].astype(F32) * _dot(yc_ref[...], wb_ref[2])
    out_ref[...] = x_ref[...] + _dot(merged.astype(BF16), wo_ref[...])


def _merge_call(p2d, yb, yc, x2d, g_sgu, ws_masked, bs_full, w_branch, w_out, tm):
    m = x2d.shape[0]
    w = BRANCH_WIDTH
    return pl.pallas_call(
        functools.partial(_merge_kernel, tm=tm),
        grid=(m // tm,),
        in_specs=[
            pl.BlockSpec((tm, 3 * D_MODEL), lambda i: (i, COL_G // (3 * D_MODEL))),
            pl.BlockSpec((tm, 2 * w), lambda i: (i, COL_A // (2 * w))),
            pl.BlockSpec((tm, w), lambda i: (i, 0)),
            pl.BlockSpec((tm, w), lambda i: (i, 0)),
            pl.BlockSpec((tm, D_MODEL), lambda i: (i, 0)),
            _const_spec((1, w)),
            _const_spec((A_GROUPS, A_BLOCK, A_BLOCK)),
            _const_spec((A_BLOCK, w)),
            _const_spec((3, w, D_MODEL)),
            _const_spec((D_MODEL, D_MODEL)),
        ],
        out_specs=pl.BlockSpec((tm, D_MODEL), lambda i: (i, 0)),
        out_shape=jax.ShapeDtypeStruct((m, D_MODEL), F32),
        scratch_shapes=[pltpu.VMEM((tm, w), BF16)],
        compiler_params=_cparams(("arbitrary",)),
        name="merge",
    )(p2d, p2d, yb, yc, x2d, g_sgu, ws_masked, bs_full, w_branch, w_out)


def _memkv_kernel(mem_ref, g_ref, w_ref, kv_ref):
    kv_ref[...] = _dot(_rms(mem_ref[...], g_ref[...]).astype(BF16), w_ref[...]).astype(BF16)


def _memkv_call(mem2d, g, w_mkv):
    m = mem2d.shape[0]
    return pl.pallas_call(
        _memkv_kernel,
        grid=(m // N_MEM,),
        in_specs=[
            pl.BlockSpec((N_MEM, D_MODEL), lambda i: (i, 0)),
            _const_spec((1, D_MODEL)),
            _const_spec((D_MODEL, 2 * D_MODEL)),
        ],
        out_specs=pl.BlockSpec((N_MEM, 2 * D_MODEL), lambda i: (i, 0)),
        out_shape=jax.ShapeDtypeStruct((m, 2 * D_MODEL), BF16),
        compiler_params=_cparams(("arbitrary",)),
        name="memkv",
    )(mem2d, g, w_mkv)


def _memattn_kernel(x_ref, g_ref, wq_ref, kv_ref, wo_ref, out_ref, o_ref):
    x = x_ref[...]
    h = _rms(x, g_ref[...]).astype(BF16)
    q = (_dot(h, wq_ref[...]) * (MEM_HEAD_DIM ** -0.5)).astype(BF16)
    dh = MEM_HEAD_DIM
    for hd in range(MEM_HEADS):
        hs = slice(hd * dh, (hd + 1) * dh)
        s = _dot_nt(q[:, hs], kv_ref[:, hs])
        p = jnp.exp(s - jnp.max(s, axis=-1, keepdims=True))
        o = _dot(p.astype(BF16), kv_ref[:, D_MODEL + hd * dh:D_MODEL + (hd + 1) * dh])
        o_ref[:, hs] = (o / jnp.sum(p, axis=-1, keepdims=True)).astype(BF16)
    out_ref[...] = x + _dot(o_ref[...], wo_ref[...])


def _memattn_call(x2d, g, w_mq, kv, w_mo, seq, tm):
    m = x2d.shape[0]
    return pl.pallas_call(
        _memattn_kernel,
        grid=(m // tm,),
        in_specs=[
            pl.BlockSpec((tm, D_MODEL), lambda i: (i, 0)),
            _const_spec((1, D_MODEL)),
            _const_spec((D_MODEL, D_MODEL)),
            pl.BlockSpec((N_MEM, 2 * D_MODEL), lambda i: ((i * tm) // seq, 0)),
            _const_spec((D_MODEL, D_MODEL)),
        ],
        out_specs=pl.BlockSpec((tm, D_MODEL), lambda i: (i, 0)),
        out_shape=jax.ShapeDtypeStruct((m, D_MODEL), F32),
        scratch_shapes=[pltpu.VMEM((tm, D_MODEL), BF16)],
        compiler_params=_cparams(("arbitrary",)),
        name="memattn",
    )(x2d, g, w_mq, kv, w_mo)


def _ffn_kernel(x_ref, g_ref, wup_ref, wconv_ref, wdown_ref, gfin_ref, out_ref,
                act_ref, ext_ref, carry_ref, *, tm, tf, seq, final_norm):
    i = pl.program_id(0)
    x = x_ref[...]
    h = _rms(x, g_ref[...]).astype(BF16)
    seq_start = (i * tm) % seq == 0

    def conv(slot, half, cs):
        up = _dot(h, wup_ref[:, cs])
        ext_ref[half, pl.ds(CARRY_ROWS, tm), :] = up
        ext_ref[half, pl.ds(0, CARRY_ROWS), :] = jnp.where(seq_start, 0.0, carry_ref[slot])
        carry_ref[slot] = up[tm - CARRY_ROWS:, :]
        wc = wconv_ref[:, cs]
        y = wc[FFN_CONV - 1:FFN_CONV, :] * up
        for d in range(1, FFN_CONV):
            y = y + wc[FFN_CONV - 1 - d:FFN_CONV - d, :] * ext_ref[half, pl.ds(CARRY_ROWS - d, tm), :]
        return y

    nchunk = D_FF // tf
    for c in range(nchunk):
        a = conv(c, 0, slice(c * tf, (c + 1) * tf))
        b = conv(nchunk + c, 1, slice(D_FF + c * tf, D_FF + (c + 1) * tf))
        act_ref[:, c * tf:(c + 1) * tf] = (a * _sigmoid(a) * b).astype(BF16)
    y = x + _dot(act_ref[...], wdown_ref[...])
    if final_norm:
        y = _rms(y, gfin_ref[...])
    out_ref[...] = y


def _ffn_call(x2d, g, w_up, w_conv, w_down, g_final, seq, tm, tf, final_norm):
    m = x2d.shape[0]
    return pl.pallas_call(
        functools.partial(_ffn_kernel, tm=tm, tf=tf, seq=seq, final_norm=final_norm),
        grid=(m // tm,),
        in_specs=[
            pl.BlockSpec((tm, D_MODEL), lambda i: (i, 0)),
            _const_spec((1, D_MODEL)),
            _const_spec((D_MODEL, 2 * D_FF)),
            _const_spec((FFN_CONV, 2 * D_FF)),
            _const_spec((D_FF, D_MODEL)),
            _const_spec((1, D_MODEL)),
        ],
        out_specs=pl.BlockSpec((tm, D_MODEL), lambda i: (i, 0)),
        out_shape=jax.ShapeDtypeStruct((m, D_MODEL), F32),
        scratch_shapes=[
            pltpu.VMEM((tm, D_FF), BF16),
            pltpu.VMEM((2, tm + CARRY_ROWS, tf), F32),
            pltpu.VMEM((2 * (D_FF // tf), CARRY_ROWS, tf), F32),
        ],
        compiler_params=_cparams(("arbitrary",)),
        name="ffn",
    )(x2d, g, w_up, w_conv, w_down, g_final)


def _rearrange_w_in(w):
    bw = BRANCH_WIDTH
    a0 = 0
    b0 = 2 * bw
    c0 = b0 + 3 * bw + B_HEADS
    g0 = c0 + 3 * bw + 2 * C_HEADS + bw
    co = c0 + 3 * bw + 2 * C_HEADS
    main = jnp.concatenate([
        w[:, g0:g0 + 3 * D_MODEL],
        w[:, a0:a0 + 2 * bw],
        w[:, b0:b0 + 3 * bw],
        w[:, c0:c0 + 3 * bw],
        w[:, co:co + bw],
    ], axis=1).astype(BF16)
    small = jnp.concatenate([
        w[:, b0 + 3 * bw:b0 + 3 * bw + B_HEADS],
        w[:, c0 + 3 * bw:c0 + 3 * bw + 2 * C_HEADS],
    ], axis=1)
    small = jnp.pad(small, ((0, 0), (0, LANES - small.shape[1]))).astype(BF16)
    return main, small


def kernel(x, mem, g_mix, w_in, g_sgu, w_s, b_s, b_fox_f, w_conv_c, b_mlstm_i, b_mlstm_f, g_mh,
           w_branch, w_out, g_mem_q, g_mem_kv, w_mq, w_mkv, w_mo, g_ffn, w_up, w_ffn_conv, w_down,
           g_final):
    bsz, seq, _ = x.shape
    depth = w_in.shape[0]
    m = bsz * seq
    tm = min(512, seq)
    tm_proj = tm
    t_fox = min(256, seq)
    ts_mlstm = min(1024, seq)

    idx = jnp.arange(A_BLOCK)
    chunk_causal = (idx[None, :] // CHUNK) <= (idx[:, None] // CHUNK)

    x2d = x.reshape(m, D_MODEL)
    mem2d = mem.reshape(bsz * N_MEM, D_MODEL)
    row = lambda a: a.reshape(1, -1)

    for i in range(depth):
        w_main, w_small = _rearrange_w_in(w_in[i])
        gate_bias = jnp.pad(jnp.concatenate([b_fox_f[i], b_mlstm_i[i], b_mlstm_f[i]]),
                            (0, LANES - GATE_ROWS)).reshape(1, LANES)
        ws_masked = jnp.where(chunk_causal[None], w_s[i], 0).astype(BF16)
        bs_full = jnp.repeat(b_s[i].T, BRANCH_WIDTH // A_GROUPS, axis=1)

        p2d, gs = _proj_call(x2d, row(g_mix[i]), w_main, w_small, w_conv_c[i], seq, tm_proj)
        p3d = p2d.reshape(bsz, seq, PROJ_COLS)
        gc, gr = _gates_call(gs.reshape(bsz, seq, LANES), gate_bias)
        yb = _fox_call(p3d, gc, t_fox).reshape(m, BRANCH_WIDTH)
        yc = _mlstm_call(p3d, gc, gr, row(g_mh[i]), ts_mlstm).reshape(m, BRANCH_WIDTH)
        x2d = _merge_call(p2d, yb, yc, x2d, row(g_sgu[i]), ws_masked, bs_full,
                          w_branch[i].astype(BF16), w_out[i].astype(BF16), tm)

        kv = _memkv_call(mem2d, row(g_mem_kv[i]), w_mkv[i].astype(BF16))
        x2d = _memattn_call(x2d, row(g_mem_q[i]), w_mq[i].astype(BF16), kv, w_mo[i].astype(BF16), seq, tm)

        x2d = _ffn_call(x2d, row(g_ffn[i]), w_up[i].astype(BF16), w_ffn_conv[i], w_down[i].astype(BF16),
                        row(g_final), seq, tm, 256, i == depth - 1)
    return x2d.reshape(bsz, seq, D_MODEL)
```

```python
import functools

import jax
import jax.numpy as jnp
from jax import lax
from jax.experimental import pallas as pl
from jax.experimental.pallas import tpu as pltpu

F32 = jnp.float32
BF16 = jnp.bfloat16

D_MODEL = 1024
EPS = 1e-6
LANES = 128
CARRY_ROWS = 8

BRANCH_WIDTH = 512
A_BLOCK = 128
A_GROUPS = 4
CHUNK = 64
B_HEADS = 8
B_HEAD_DIM = 64
C_HEADS = 4
C_HEAD_DIM = 128
C_CONV = 4
MLSTM_CHUNK = 128
N_MEM = 256
MEM_HEADS = 4
MEM_HEAD_DIM = 256
D_FF = 2816
FFN_CONV = 3

PROJ_TN = 512
COL_G = 0
COL_A = 3072
COL_BQ = 4096
COL_BK = 4608
COL_BV = 5120
COL_CQ = 5632
COL_CK = 6144
COL_CV = 6656
COL_CO = 7168
PROJ_COLS = 7680
LANE_BF = 0
LANE_CI = 8
LANE_CF = 12
GATE_ROWS = 16

NEG = -1e30
VMEM_LIMIT = 56 * 1024 * 1024


def _cparams(sem):
    return pltpu.CompilerParams(dimension_semantics=sem, vmem_limit_bytes=VMEM_LIMIT)


def _rms(xf, g):
    return xf * lax.rsqrt(jnp.mean(xf * xf, axis=-1, keepdims=True) + EPS) * g


def _sigmoid(x):
    return 0.5 * jnp.tanh(0.5 * x) + 0.5


def _gelu_tanh(x):
    return 0.5 * x * (1.0 + jnp.tanh(0.7978845608028654 * (x + 0.044715 * (x * x * x))))


def _dot(a, b):
    return jnp.dot(a, b, preferred_element_type=F32)


def _dot_nt(a, b):
    return lax.dot_general(a, b, (((1,), (1,)), ((), ())), preferred_element_type=F32)


def _const_spec(shape):
    nd = len(shape)
    return pl.BlockSpec(shape, lambda *_: (0,) * nd, pipeline_mode=pl.Buffered(1))


def _proj_kernel(x_ref, g_ref, w_ref, ws_ref, wc_ref, p_ref, gs_ref, ext_ref, carry_ref, *, tm, seq):
    i = pl.program_id(0)
    h = _rms(x_ref[...], g_ref[...]).astype(BF16)
    gs_ref[...] = _dot(h, ws_ref[...])
    seq_start = (i * tm) % seq == 0

    def conv_silu(acc, slot):
        ext_ref[slot, pl.ds(CARRY_ROWS, tm), :] = acc
        ext_ref[slot, pl.ds(0, CARRY_ROWS), :] = jnp.where(seq_start, 0.0, carry_ref[slot])
        carry_ref[slot] = acc[tm - CARRY_ROWS:, :]
        wc = wc_ref[:, slot * PROJ_TN:(slot + 1) * PROJ_TN]
        y = wc[C_CONV - 1:C_CONV, :] * acc
        for d in range(1, C_CONV):
            y = y + wc[C_CONV - 1 - d:C_CONV - d, :] * ext_ref[slot, pl.ds(CARRY_ROWS - d, tm), :]
        return y * _sigmoid(y)

    for j in range(PROJ_COLS // PROJ_TN):
        c0 = j * PROJ_TN
        cs = slice(c0, c0 + PROJ_TN)
        acc = _dot(h, w_ref[:, cs])
        if c0 < COL_A or c0 == COL_CO:
            out = _sigmoid(acc)
        elif c0 < COL_BQ:
            out = _gelu_tanh(acc)
        elif c0 == COL_CQ:
            out = conv_silu(acc, 0)
        elif c0 == COL_CK:
            out = conv_silu(acc, 1)
        else:
            out = acc
        p_ref[:, cs] = out.astype(BF16)


def _proj_call(x2d, g, w_main, w_small, w_conv, seq, tm):
    m = x2d.shape[0]
    return pl.pallas_call(
        functools.partial(_proj_kernel, tm=tm, seq=seq),
        grid=(m // tm,),
        in_specs=[
            pl.BlockSpec((tm, D_MODEL), lambda i: (i, 0)),
            _const_spec((1, D_MODEL)),
            _const_spec((D_MODEL, PROJ_COLS)),
            _const_spec((D_MODEL, LANES)),
            _const_spec((C_CONV, 2 * PROJ_TN)),
        ],
        out_specs=[
            pl.BlockSpec((tm, PROJ_COLS), lambda i: (i, 0)),
            pl.BlockSpec((tm, LANES), lambda i: (i, 0)),
        ],
        out_shape=[
            jax.ShapeDtypeStruct((m, PROJ_COLS), BF16),
            jax.ShapeDtypeStruct((m, LANES), F32),
        ],
        scratch_shapes=[
            pltpu.VMEM((2, tm + CARRY_ROWS, PROJ_TN), F32),
            pltpu.VMEM((2, CARRY_ROWS, PROJ_TN), F32),
        ],
        compiler_params=_cparams(("arbitrary",)),
        name="proj",
    )(x2d, g, w_main, w_small, w_conv)


def _gates_kernel(gs_ref, bias_ref, sel_ref, gc_ref, gr_ref, u_ref, gsum_ref, *, seq):
    blk = MLSTM_CHUNK
    row = lax.broadcasted_iota(jnp.int32, (blk, blk), 0)
    col = lax.broadcasted_iota(jnp.int32, (blk, blk), 1)
    tri = (col <= row).astype(F32)
    lane = lax.broadcasted_iota(jnp.int32, (1, LANES), 1)
    hw = C_HEADS * LANES

    def body(r, carry):
        r0 = pl.multiple_of(r * blk, blk)
        raw = gs_ref[pl.ds(r0, blk), :] + bias_ref[...]
        logsig = jnp.minimum(raw, 0.0) - jnp.log1p(jnp.exp(-jnp.abs(raw)))
        local = jnp.dot(tri, logsig, precision=lax.Precision.HIGHEST, preferred_element_type=F32)
        glob = local + carry
        out = jnp.where(lane < LANE_CI, glob, jnp.where(lane < LANE_CF, raw, local))
        gc_ref[pl.ds(r0, blk), :] = out
        gr_ref[:, pl.ds(r0, blk)] = out.T[0:GATE_ROWS, :]
        rep = jnp.dot(out, sel_ref[...], precision=lax.Precision.HIGHEST, preferred_element_type=F32)
        g_rows, mloc_rows = [], []
        for h in range(C_HEADS):
            u = rep[:, h * LANES:(h + 1) * LANES]
            u_ref[h, pl.ds(r0, blk), :] = u
            g = rep[blk - 1:blk, hw + h * LANES:hw + (h + 1) * LANES]
            g_rows.append(g)
            mloc_rows.append(g + jnp.max(u, axis=0, keepdims=True))
        gsum_ref[r] = jnp.concatenate(g_rows + mloc_rows, axis=0)
        return glob[blk - 1:blk, :]

    lax.fori_loop(0, seq // blk, body, jnp.zeros((1, LANES), F32))


def _gates_call(gs3d, bias, sel):
    b, seq, _ = gs3d.shape
    nchunk = seq // MLSTM_CHUNK
    return pl.pallas_call(
        functools.partial(_gates_kernel, seq=seq),
        grid=(b,),
        in_specs=[
            pl.BlockSpec((None, seq, LANES), lambda bi: (bi, 0, 0)),
            pl.BlockSpec((1, LANES), lambda bi: (0, 0)),
            pl.BlockSpec((LANES, 2 * C_HEADS * LANES), lambda bi: (0, 0)),
        ],
        out_specs=[
            pl.BlockSpec((None, seq, LANES), lambda bi: (bi, 0, 0)),
            pl.BlockSpec((None, GATE_ROWS, seq), lambda bi: (bi, 0, 0)),
            pl.BlockSpec((None, C_HEADS, seq, LANES), lambda bi: (bi, 0, 0, 0)),
            pl.BlockSpec((None, nchunk, 2 * C_HEADS, LANES), lambda bi: (bi, 0, 0, 0)),
        ],
        out_shape=[
            jax.ShapeDtypeStruct((b, seq, LANES), F32),
            jax.ShapeDtypeStruct((b, GATE_ROWS, seq), F32),
            jax.ShapeDtypeStruct((b, C_HEADS, seq, LANES), F32),
            jax.ShapeDtypeStruct((b, nchunk, 2 * C_HEADS, LANES), F32),
        ],
        compiler_params=_cparams(("arbitrary",)),
        name="gates",
    )(gs3d, bias, sel)


def _gate_select_matrix():
    lane = jnp.arange(LANES)[:, None]
    blocks = [jnp.broadcast_to((lane == LANE_CI + h).astype(F32) - (lane == LANE_CF + h).astype(F32),
                               (LANES, LANES)) for h in range(C_HEADS)]
    blocks += [jnp.broadcast_to((lane == LANE_CF + h).astype(F32), (LANES, LANES)) for h in range(C_HEADS)]
    return jnp.concatenate(blocks, axis=1)


FOX_VROWS = 80
FOX_HEADS = 8


def _fox_kernel(q_ref, k_ref, v_ref, gc_ref, o_ref, kaug_ref, vt_ref, *, t, seq):
    hg = pl.program_id(1)
    qi = pl.program_id(2)
    hd = B_HEAD_DIM
    nh = FOX_HEADS
    lane = lax.broadcasted_iota(jnp.int32, (1, LANES), 1)
    own = (lane < hd, lane >= hd)
    aug0 = (hd, 0)

    @pl.when(qi == 0)
    def _():
        sub = lax.broadcasted_iota(jnp.int32, (FOX_VROWS - hd, seq), 0)
        tail = jnp.where(sub == 0, 1.0, 0.0).astype(BF16)
        for hh in range(nh):
            vt_ref[hh, hd:FOX_VROWS, :] = tail

        def body(r, _):
            r0 = pl.multiple_of(r * LANES, LANES)
            g = gc_ref[pl.ds(r0, LANES), :]
            for pp in range(nh // 2):
                ls = slice(pp * LANES, (pp + 1) * LANES)
                kb = k_ref[pl.ds(r0, LANES), ls].astype(F32)
                vt = v_ref[pl.ds(r0, LANES), ls].astype(F32).T
                for h in range(2):
                    hh = 2 * pp + h
                    neg = -jnp.sum(jnp.where(lane == LANE_BF + nh * hg + hh, g, 0.0), axis=-1, keepdims=True)
                    hi = neg.astype(BF16).astype(F32)
                    mid = (neg - hi).astype(BF16).astype(F32)
                    lo = (neg - hi) - mid
                    a = aug0[h]
                    extra = jnp.where(lane == a, hi, jnp.where(lane == a + 1, mid, jnp.where(lane == a + 2, lo, 0.0)))
                    kaug_ref[hh, pl.ds(r0, LANES), :] = jnp.where(own[h], kb, extra).astype(BF16)
                    vt_ref[hh, 0:hd, pl.ds(r0, LANES)] = vt[h * hd:(h + 1) * hd, :].astype(BF16)
            return 0

        lax.fori_loop(0, seq // LANES, body, 0)

    qa = []
    for pp in range(nh // 2):
        q = q_ref[:, pp * LANES:(pp + 1) * LANES].astype(F32) * (hd ** -0.5)
        for h in range(2):
            a = aug0[h]
            ones3 = jnp.where((lane >= a) & (lane < a + 3), 1.0, 0.0)
            qa.append(jnp.where(own[h], q, ones3).astype(BF16))
    row = lax.broadcasted_iota(jnp.int32, (t, t), 0)
    col = lax.broadcasted_iota(jnp.int32, (t, t), 1)
    causal = row <= col

    def scores(j):
        k0 = pl.multiple_of(j * t, t)
        return tuple(_dot_nt(kaug_ref[hh, pl.ds(k0, t), :], qa[hh]) for hh in range(nh))

    def update(j, st, state, masked):
        k0 = pl.multiple_of(j * t, t)
        ms, ps, alphas = [], [], []
        for hh in range(nh):
            m = state[hh][0]
            s = jnp.where(causal, st[hh], NEG) if masked else st[hh]
            m_new = jnp.maximum(m, jnp.max(s, axis=0, keepdims=True))
            ps.append(jnp.exp(s - m_new).astype(BF16))
            alphas.append(jnp.exp(m - m_new))
            ms.append(m_new)
        pvs = [_dot(vt_ref[hh, :, pl.ds(k0, t)], ps[hh]) for hh in range(nh)]
        return tuple((ms[hh], alphas[hh] * state[hh][1] + pvs[hh]) for hh in range(nh))

    init = tuple((jnp.full((1, t), NEG, F32), jnp.zeros((FOX_VROWS, t), F32)) for _ in range(nh))
    state = lax.fori_loop(0, qi, lambda j, state: update(j, scores(j), state, False), init)
    state = update(qi, scores(qi), state, True)
    ot = jnp.concatenate([acc[0:hd] / acc[hd:hd + 1] for _, acc in state], axis=0)
    o_ref[...] = ot.T.astype(BF16)


def _fox_call(p3d, gc, t):
    b, seq, _ = p3d.shape
    w = FOX_HEADS * B_HEAD_DIM
    return pl.pallas_call(
        functools.partial(_fox_kernel, t=t, seq=seq),
        grid=(b, B_HEADS // FOX_HEADS, seq // t),
        in_specs=[
            pl.BlockSpec((None, t, w), lambda bi, hg, qi: (bi, qi, COL_BQ // w + hg)),
            pl.BlockSpec((None, seq, w), lambda bi, hg, qi: (bi, 0, COL_BK // w + hg)),
            pl.BlockSpec((None, seq, w), lambda bi, hg, qi: (bi, 0, COL_BV // w + hg)),
            pl.BlockSpec((None, seq, LANES), lambda bi, hg, qi: (bi, 0, 0)),
        ],
        out_specs=pl.BlockSpec((None, t, w), lambda bi, hg, qi: (bi, qi, hg)),
        out_shape=jax.ShapeDtypeStruct((b, seq, BRANCH_WIDTH), BF16),
        scratch_shapes=[
            pltpu.VMEM((FOX_HEADS, seq, LANES), BF16),
            pltpu.VMEM((FOX_HEADS, FOX_VROWS, seq), BF16),
        ],
        compiler_params=_cparams(("arbitrary", "arbitrary", "arbitrary")),
        name="fox",
    )(p3d, p3d, p3d, gc)


MLSTM_ROWS = 144


def _mlstm_kernel(q_ref, k_ref, v_ref, o_ref, u_ref, gr_ref, gsum_ref, gmh_ref, y_ref, c_ref, m_ref,
                  *, ts):
    L = MLSTM_CHUNK
    dh = C_HEAD_DIM
    scale = dh ** -0.5
    si = pl.program_id(1)

    @pl.when(si == 0)
    def _():
        c_ref[...] = jnp.zeros_like(c_ref)
        m_ref[...] = jnp.zeros_like(m_ref)

    row = lax.broadcasted_iota(jnp.int32, (L, L), 0)
    col = lax.broadcasted_iota(jnp.int32, (L, L), 1)
    causal = row <= col
    sub = lax.broadcasted_iota(jnp.int32, (MLSTM_ROWS - dh, L), 0)
    tail = jnp.where(sub == 0, 1.0, 0.0)

    def chunk(c, _):
        r0 = pl.multiple_of(c * L, L)
        gsum = gsum_ref[c]
        for h in range(C_HEADS):
            hs = slice(h * dh, (h + 1) * dh)
            q = q_ref[pl.ds(r0, L), hs]
            k = k_ref[pl.ds(r0, L), hs]
            vt = jnp.concatenate([v_ref[pl.ds(r0, L), hs].astype(F32).T, tail], axis=0)
            ig = gr_ref[pl.ds(LANE_CI + h, 1), pl.ds(r0, L)]
            b = gr_ref[pl.ds(LANE_CF + h, 1), pl.ds(r0, L)]
            g = gsum[h:h + 1, :]
            m_loc = gsum[C_HEADS + h:C_HEADS + h + 1, :]
            m_in = m_ref[h]
            c_in = c_ref[h]

            dlog = jnp.where(causal, u_ref[h, pl.ds(r0, L), :] + b, NEG)
            inter = b + m_in
            m_t = jnp.maximum(jnp.max(dlog, axis=0, keepdims=True), inter)
            sm = (_dot_nt(k, q) * scale) * jnp.exp(dlog - m_t)
            w_int = jnp.exp(inter - m_t)
            ext = _dot(vt.astype(BF16), sm.astype(BF16)) + w_int * _dot_nt(c_in.astype(BF16), q)
            hh = ext[0:dh] / jnp.maximum(jnp.abs(ext[dh:dh + 1]), jnp.exp(-m_t))
            hn = hh * lax.rsqrt(jnp.mean(hh * hh, axis=0, keepdims=True) + EPS) * gmh_ref[h]
            y_ref[pl.ds(r0, L), hs] = (hn.T * o_ref[pl.ds(r0, L), hs].astype(F32)).astype(BF16)

            m_new = jnp.maximum(g + m_in, m_loc)
            w = jnp.exp(g + (ig - b) - m_new) * scale
            c_ref[h] = jnp.exp(g + m_in - m_new) * c_in + _dot((vt * w).astype(BF16), k)
            m_ref[h] = m_new
        return 0

    lax.fori_loop(0, ts // L, chunk, 0, unroll=2)


def _mlstm_call(p3d, u, gr, gsum, gmh_rep, ts):
    b, seq, _ = p3d.shape
    w = BRANCH_WIDTH
    nc = ts // MLSTM_CHUNK

    def pspec(col):
        return pl.BlockSpec((None, ts, w), lambda bi, si: (bi, si, col // w))

    return pl.pallas_call(
        functools.partial(_mlstm_kernel, ts=ts),
        grid=(b, seq // ts),
        in_specs=[
            pspec(COL_CQ), pspec(COL_CK), pspec(COL_CV), pspec(COL_CO),
            pl.BlockSpec((None, C_HEADS, ts, LANES), lambda bi, si: (bi, 0, si, 0)),
            pl.BlockSpec((None, GATE_ROWS, ts), lambda bi, si: (bi, 0, si)),
            pl.BlockSpec((None, nc, 2 * C_HEADS, LANES), lambda bi, si: (bi, si, 0, 0)),
            _const_spec((C_HEADS, C_HEAD_DIM, LANES)),
        ],
        out_specs=pl.BlockSpec((None, ts, w), lambda bi, si: (bi, si, 0)),
        out_shape=jax.ShapeDtypeStruct((b, seq, w), BF16),
        scratch_shapes=[
            pltpu.VMEM((C_HEADS, MLSTM_ROWS, C_HEAD_DIM), F32),
            pltpu.VMEM((C_HEADS, 1, LANES), F32),
        ],
        compiler_params=_cparams(("arbitrary", "arbitrary")),
        name="mlstm",
    )(p3d, p3d, p3d, p3d, u, gr, gsum, gmh_rep)


def _merge_kernel(gates_ref, uv_ref, yb_ref, yc_ref, x_ref, gsgu_ref, ws_ref, bs_ref, wb_ref, wo_ref,
                  out_ref, ya_ref, *, tm):
    w = BRANCH_WIDTH
    u = uv_ref[:, :w].astype(F32)
    v = uv_ref[:, w:].astype(F32)
    vn = _rms(v, gsgu_ref[...]).astype(BF16)
    gd = w // A_GROUPS
    for nb in range(tm // A_BLOCK):
        rs = slice(nb * A_BLOCK, (nb + 1) * A_BLOCK)
        mixed = jnp.concatenate(
            [_dot(ws_ref[g], vn[rs, g * gd:(g + 1) * gd]) for g in range(A_GROUPS)], axis=1)
        ya_ref[rs, :] = (u[rs, :] * (mixed + bs_ref[...])).astype(BF16)
    merged = gates_ref[:, 0:D_MODEL].astype(F32) * _dot(ya_ref[...], wb_ref[0])
    merged += gates_ref[:, D_MODEL:2 * D_MODEL].astype(F32) * _dot(yb_ref[...], wb_ref[1])
    merged += gates_ref[:, 2 * D_MODEL:3 * D_MODEL].astype(F32) * _dot(yc_ref[...], wb_ref[2])
    out_ref[...] = x_ref[...] + _dot(merged.astype(BF16), wo_ref[...])


def _merge_call(p2d, yb, yc, x2d, g_sgu, ws_masked, bs_full, w_branch, w_out, tm):
    m = x2d.shape[0]
    w = BRANCH_WIDTH
    return pl.pallas_call(
        functools.partial(_merge_kernel, tm=tm),
        grid=(m // tm,),
        in_specs=[
            pl.BlockSpec((tm, 3 * D_MODEL), lambda i: (i, COL_G // (3 * D_MODEL))),
            pl.BlockSpec((tm, 2 * w), lambda i: (i, COL_A // (2 * w))),
            pl.BlockSpec((tm, w), lambda i: (i, 0)),
            pl.BlockSpec((tm, w), lambda i: (i, 0)),
            pl.BlockSpec((tm, D_MODEL), lambda i: (i, 0)),
            _const_spec((1, w)),
            _const_spec((A_GROUPS, A_BLOCK, A_BLOCK)),
            _const_spec((A_BLOCK, w)),
            _const_spec((3, w, D_MODEL)),
            _const_spec((D_MODEL, D_MODEL)),
        ],
        out_specs=pl.BlockSpec((tm, D_MODEL), lambda i: (i, 0)),
        out_shape=jax.ShapeDtypeStruct((m, D_MODEL), F32),
        scratch_shapes=[pltpu.VMEM((tm, w), BF16)],
        compiler_params=_cparams(("arbitrary",)),
        name="merge",
    )(p2d, p2d, yb, yc, x2d, g_sgu, ws_masked, bs_full, w_branch, w_out)


def _memkv_kernel(mem_ref, g_ref, w_ref, kv_ref):
    kv_ref[...] = _dot(_rms(mem_ref[...], g_ref[...]).astype(BF16), w_ref[...]).astype(BF16)


def _memkv_call(mem2d, g, w_mkv):
    m = mem2d.shape[0]
    return pl.pallas_call(
        _memkv_kernel,
        grid=(m // N_MEM,),
        in_specs=[
            pl.BlockSpec((N_MEM, D_MODEL), lambda i: (i, 0)),
            _const_spec((1, D_MODEL)),
            _const_spec((D_MODEL, 2 * D_MODEL)),
        ],
        out_specs=pl.BlockSpec((N_MEM, 2 * D_MODEL), lambda i: (i, 0)),
        out_shape=jax.ShapeDtypeStruct((m, 2 * D_MODEL), BF16),
        compiler_params=_cparams(("arbitrary",)),
        name="memkv",
    )(mem2d, g, w_mkv)


def _memattn_kernel(x_ref, g_ref, wq_ref, kv_ref, wo_ref, out_ref, o_ref):
    x = x_ref[...]
    h = _rms(x, g_ref[...]).astype(BF16)
    q = (_dot(h, wq_ref[...]) * (MEM_HEAD_DIM ** -0.5)).astype(BF16)
    dh = MEM_HEAD_DIM
    for hd in range(MEM_HEADS):
        hs = slice(hd * dh, (hd + 1) * dh)
        s = _dot_nt(q[:, hs], kv_ref[:, hs])
        p = jnp.exp(s - jnp.max(s, axis=-1, keepdims=True))
        o = _dot(p.astype(BF16), kv_ref[:, D_MODEL + hd * dh:D_MODEL + (hd + 1) * dh])
        o_ref[:, hs] = (o / jnp.sum(p, axis=-1, keepdims=True)).astype(BF16)
    out_ref[...] = x + _dot(o_ref[...], wo_ref[...])


def _memattn_call(x2d, g, w_mq, kv, w_mo, seq, tm):
    m = x2d.shape[0]
    return pl.pallas_call(
        _memattn_kernel,
        grid=(m // tm,),
        in_specs=[
            pl.BlockSpec((tm, D_MODEL), lambda i: (i, 0)),
            _const_spec((1, D_MODEL)),
            _const_spec((D_MODEL, D_MODEL)),
            pl.BlockSpec((N_MEM, 2 * D_MODEL), lambda i: ((i * tm) // seq, 0)),
            _const_spec((D_MODEL, D_MODEL)),
        ],
        out_specs=pl.BlockSpec((tm, D_MODEL), lambda i: (i, 0)),
        out_shape=jax.ShapeDtypeStruct((m, D_MODEL), F32),
        scratch_shapes=[pltpu.VMEM((tm, D_MODEL), BF16)],
        compiler_params=_cparams(("arbitrary",)),
        name="memattn",
    )(x2d, g, w_mq, kv, w_mo)


def _ffn_kernel(x_ref, g_ref, wup_ref, wconv_ref, wdown_ref, gfin_ref, out_ref,
                act_ref, ext_ref, carry_ref, *, tm, tf, seq, final_norm):
    i = pl.program_id(0)
    x = x_ref[...]
    h = _rms(x, g_ref[...]).astype(BF16)
    seq_start = (i * tm) % seq == 0

    def conv(slot, half, cs):
        up = _dot(h, wup_ref[:, cs])
        ext_ref[half, pl.ds(CARRY_ROWS, tm), :] = up
        ext_ref[half, pl.ds(0, CARRY_ROWS), :] = jnp.where(seq_start, 0.0, carry_ref[slot])
        carry_ref[slot] = up[tm - CARRY_ROWS:, :]
        wc = wconv_ref[:, cs]
        y = wc[FFN_CONV - 1:FFN_CONV, :] * up
        for d in range(1, FFN_CONV):
            y = y + wc[FFN_CONV - 1 - d:FFN_CONV - d, :] * ext_ref[half, pl.ds(CARRY_ROWS - d, tm), :]
        return y

    nchunk = D_FF // tf
    for c in range(nchunk):
        a = conv(c, 0, slice(c * tf, (c + 1) * tf))
        b = conv(nchunk + c, 1, slice(D_FF + c * tf, D_FF + (c + 1) * tf))
        act_ref[:, c * tf:(c + 1) * tf] = (a * _sigmoid(a) * b).astype(BF16)
    y = x + _dot(act_ref[...], wdown_ref[...])
    if final_norm:
        y = _rms(y, gfin_ref[...])
    out_ref[...] = y


def _ffn_call(x2d, g, w_up, w_conv, w_down, g_final, seq, tm, tf, final_norm):
    m = x2d.shape[0]
    return pl.pallas_call(
        functools.partial(_ffn_kernel, tm=tm, tf=tf, seq=seq, final_norm=final_norm),
        grid=(m // tm,),
        in_specs=[
            pl.BlockSpec((tm, D_MODEL), lambda i: (i, 0)),
            _const_spec((1, D_MODEL)),
            _const_spec((D_MODEL, 2 * D_FF)),
            _const_spec((FFN_CONV, 2 * D_FF)),
            _const_spec((D_FF, D_MODEL)),
            _const_spec((1, D_MODEL)),
        ],
        out_specs=pl.BlockSpec((tm, D_MODEL), lambda i: (i, 0)),
        out_shape=jax.ShapeDtypeStruct((m, D_MODEL), F32),
        scratch_shapes=[
            pltpu.VMEM((tm, D_FF), BF16),
            pltpu.VMEM((2, tm + CARRY_ROWS, tf), F32),
            pltpu.VMEM((2 * (D_FF // tf), CARRY_ROWS, tf), F32),
        ],
        compiler_params=_cparams(("arbitrary",)),
        name="ffn",
    )(x2d, g, w_up, w_conv, w_down, g_final)


def _rearrange_w_in(w):
    bw = BRANCH_WIDTH
    a0 = 0
    b0 = 2 * bw
    c0 = b0 + 3 * bw + B_HEADS
    g0 = c0 + 3 * bw + 2 * C_HEADS + bw
    co = c0 + 3 * bw + 2 * C_HEADS
    main = jnp.concatenate([
        w[:, g0:g0 + 3 * D_MODEL],
        w[:, a0:a0 + 2 * bw],
        w[:, b0:b0 + 3 * bw],
        w[:, c0:c0 + 3 * bw],
        w[:, co:co + bw],
    ], axis=1).astype(BF16)
    small = jnp.concatenate([
        w[:, b0 + 3 * bw:b0 + 3 * bw + B_HEADS],
        w[:, c0 + 3 * bw:c0 + 3 * bw + 2 * C_HEADS],
    ], axis=1)
    small = jnp.pad(small, ((0, 0), (0, LANES - small.shape[1]))).astype(BF16)
    return main, small


def kernel(x, mem, g_mix, w_in, g_sgu, w_s, b_s, b_fox_f, w_conv_c, b_mlstm_i, b_mlstm_f, g_mh,
           w_branch, w_out, g_mem_q, g_mem_kv, w_mq, w_mkv, w_mo, g_ffn, w_up, w_ffn_conv, w_down,
           g_final):
    bsz, seq, _ = x.shape
    depth = w_in.shape[0]
    m = bsz * seq
    tm = min(512, seq)
    tm_proj = tm
    t_fox = min(256, seq)
    ts_mlstm = min(1024, seq)

    idx = jnp.arange(A_BLOCK)
    chunk_causal = (idx[None, :] // CHUNK) <= (idx[:, None] // CHUNK)
    gate_sel = _gate_select_matrix()

    x2d = x.reshape(m, D_MODEL)
    mem2d = mem.reshape(bsz * N_MEM, D_MODEL)
    row = lambda a: a.reshape(1, -1)

    for i in range(depth):
        w_main, w_small = _rearrange_w_in(w_in[i])
        gate_bias = jnp.pad(jnp.concatenate([b_fox_f[i], b_mlstm_i[i], b_mlstm_f[i]]),
                            (0, LANES - GATE_ROWS)).reshape(1, LANES)
        ws_masked = jnp.where(chunk_causal[None], w_s[i], 0).astype(BF16)
        bs_full = jnp.repeat(b_s[i].T, BRANCH_WIDTH // A_GROUPS, axis=1)

        p2d, gs = _proj_call(x2d, row(g_mix[i]), w_main, w_small, w_conv_c[i], seq, tm_proj)
        p3d = p2d.reshape(bsz, seq, PROJ_COLS)
        gc, gr, u, gsum = _gates_call(gs.reshape(bsz, seq, LANES), gate_bias, gate_sel)
        yb = _fox_call(p3d, gc, t_fox).reshape(m, BRANCH_WIDTH)
        gmh_rep = jnp.broadcast_to(g_mh[i].reshape(C_HEADS, C_HEAD_DIM, 1), (C_HEADS, C_HEAD_DIM, LANES))
        yc = _mlstm_call(p3d, u, gr, gsum, gmh_rep, ts_mlstm).reshape(m, BRANCH_WIDTH)
        x2d = _merge_call(p2d, yb, yc, x2d, row(g_sgu[i]), ws_masked, bs_full,
                          w_branch[i].astype(BF16), w_out[i].astype(BF16), tm)

        kv = _memkv_call(mem2d, row(g_mem_kv[i]), w_mkv[i].astype(BF16))
        x2d = _memattn_call(x2d, row(g_mem_q[i]), w_mq[i].astype(BF16), kv, w_mo[i].astype(BF16), seq, tm)

        x2d = _ffn_call(x2d, row(g_ffn[i]), w_up[i].astype(BF16), w_ffn_conv[i], w_down[i].astype(BF16),
                        row(g_final), seq, tm, 256, i == depth - 1)
    return x2d.reshape(bsz, seq, D_MODEL)
```

```python
import functools

import jax
import jax.numpy as jnp
from jax import lax
from jax.experimental import pallas as pl
from jax.experimental.pallas import tpu as pltpu

F32 = jnp.float32
BF16 = jnp.bfloat16

D_MODEL = 1024
EPS = 1e-6
LANES = 128
CARRY_ROWS = 8

BRANCH_WIDTH = 512
A_BLOCK = 128
A_GROUPS = 4
CHUNK = 64
B_HEADS = 8
B_HEAD_DIM = 64
C_HEADS = 4
C_HEAD_DIM = 128
C_CONV = 4
MLSTM_CHUNK = 128
N_MEM = 256
MEM_HEADS = 4
MEM_HEAD_DIM = 256
D_FF = 2816
FFN_CONV = 3

PROJ_TN = 512
COL_G = 0
COL_A = 3072
COL_BQ = 4096
COL_BK = 4608
COL_BV = 5120
COL_CQ = 5632
COL_CK = 6144
COL_CV = 6656
COL_CO = 7168
PROJ_COLS = 7680
LANE_BF = 0
LANE_CI = 8
LANE_CF = 12
GATE_ROWS = 16

LOG2E = 1.4426950408889634
FOX_Q_SCALE = B_HEAD_DIM ** -0.5 * LOG2E
NEG = -1e30
VMEM_LIMIT = 56 * 1024 * 1024


def _cparams(sem):
    return pltpu.CompilerParams(dimension_semantics=sem, vmem_limit_bytes=VMEM_LIMIT)


def _rms(xf, g):
    return xf * lax.rsqrt(jnp.mean(xf * xf, axis=-1, keepdims=True) + EPS) * g


def _sigmoid(x):
    return 0.5 * jnp.tanh(0.5 * x) + 0.5


def _gelu_tanh(x):
    return 0.5 * x * (1.0 + jnp.tanh(0.7978845608028654 * (x + 0.044715 * (x * x * x))))


def _dot(a, b):
    return jnp.dot(a, b, preferred_element_type=F32)


def _dot_nt(a, b):
    return lax.dot_general(a, b, (((1,), (1,)), ((), ())), preferred_element_type=F32)


def _const_spec(shape):
    nd = len(shape)
    return pl.BlockSpec(shape, lambda *_: (0,) * nd, pipeline_mode=pl.Buffered(1))


def _proj_kernel(x_ref, g_ref, w_ref, ws_ref, wc_ref, p_ref, gs_ref, ext_ref, carry_ref, *, tm, seq):
    i = pl.program_id(0)
    h = _rms(x_ref[...], g_ref[...]).astype(BF16)
    gs_ref[...] = _dot(h, ws_ref[...])
    seq_start = (i * tm) % seq == 0

    def conv_silu(acc, slot):
        ext_ref[slot, pl.ds(CARRY_ROWS, tm), :] = acc
        ext_ref[slot, pl.ds(0, CARRY_ROWS), :] = jnp.where(seq_start, 0.0, carry_ref[slot])
        carry_ref[slot] = acc[tm - CARRY_ROWS:, :]
        wc = wc_ref[:, slot * PROJ_TN:(slot + 1) * PROJ_TN]
        y = wc[C_CONV - 1:C_CONV, :] * acc
        for d in range(1, C_CONV):
            y = y + wc[C_CONV - 1 - d:C_CONV - d, :] * ext_ref[slot, pl.ds(CARRY_ROWS - d, tm), :]
        return y * _sigmoid(y)

    for j in range(PROJ_COLS // PROJ_TN):
        c0 = j * PROJ_TN
        cs = slice(c0, c0 + PROJ_TN)
        acc = _dot(h, w_ref[:, cs])
        if c0 < COL_A or c0 == COL_CO:
            out = _sigmoid(acc)
        elif c0 < COL_BQ:
            out = _gelu_tanh(acc)
        elif c0 == COL_BQ:
            out = acc * FOX_Q_SCALE
        elif c0 == COL_CQ:
            out = conv_silu(acc, 0)
        elif c0 == COL_CK:
            out = conv_silu(acc, 1)
        else:
            out = acc
        p_ref[:, cs] = out.astype(BF16)


def _proj_call(x2d, g, w_main, w_small, w_conv, seq, tm):
    m = x2d.shape[0]
    return pl.pallas_call(
        functools.partial(_proj_kernel, tm=tm, seq=seq),
        grid=(m // tm,),
        in_specs=[
            pl.BlockSpec((tm, D_MODEL), lambda i: (i, 0)),
            _const_spec((1, D_MODEL)),
            _const_spec((D_MODEL, PROJ_COLS)),
            _const_spec((D_MODEL, LANES)),
            _const_spec((C_CONV, 2 * PROJ_TN)),
        ],
        out_specs=[
            pl.BlockSpec((tm, PROJ_COLS), lambda i: (i, 0)),
            pl.BlockSpec((tm, LANES), lambda i: (i, 0)),
        ],
        out_shape=[
            jax.ShapeDtypeStruct((m, PROJ_COLS), BF16),
            jax.ShapeDtypeStruct((m, LANES), F32),
        ],
        scratch_shapes=[
            pltpu.VMEM((2, tm + CARRY_ROWS, PROJ_TN), F32),
            pltpu.VMEM((2, CARRY_ROWS, PROJ_TN), F32),
        ],
        compiler_params=_cparams(("arbitrary",)),
        name="proj",
    )(x2d, g, w_main, w_small, w_conv)


def _gates_kernel(gs_ref, bias_ref, gc_ref, gr_ref, u_ref, gsum_ref, *, seq):
    blk = MLSTM_CHUNK
    row = lax.broadcasted_iota(jnp.int32, (blk, blk), 0)
    col = lax.broadcasted_iota(jnp.int32, (blk, blk), 1)
    tri = (col <= row).astype(F32)
    lane = lax.broadcasted_iota(jnp.int32, (1, LANES), 1)

    def body(r, carry):
        r0 = pl.multiple_of(r * blk, blk)
        raw = gs_ref[pl.ds(r0, blk), :] + bias_ref[...]
        logsig = jnp.minimum(raw, 0.0) - jnp.log1p(jnp.exp(-jnp.abs(raw)))
        local = jnp.dot(tri, logsig, precision=lax.Precision.HIGHEST, preferred_element_type=F32)
        glob = local + carry
        out = jnp.where(lane < LANE_CI, glob, jnp.where(lane < LANE_CF, raw, local))
        gc_ref[pl.ds(r0, blk), :] = out
        gr_ref[:, pl.ds(r0, blk)] = out.T[0:GATE_ROWS, :]
        g_rows, mloc_rows = [], []
        for h in range(C_HEADS):
            d = out[:, LANE_CI + h:LANE_CI + h + 1] - out[:, LANE_CF + h:LANE_CF + h + 1]
            u = jnp.broadcast_to(d, (blk, LANES))
            u_ref[h, pl.ds(r0, blk), :] = u
            g = jnp.broadcast_to(out[blk - 1:blk, LANE_CF + h:LANE_CF + h + 1], (1, LANES))
            g_rows.append(g)
            mloc_rows.append(g + jnp.max(u, axis=0, keepdims=True))
        gsum_ref[r] = jnp.concatenate(g_rows + mloc_rows, axis=0)
        return glob[blk - 1:blk, :]

    lax.fori_loop(0, seq // blk, body, jnp.zeros((1, LANES), F32))


def _gates_call(gs3d, bias):
    b, seq, _ = gs3d.shape
    nchunk = seq // MLSTM_CHUNK
    return pl.pallas_call(
        functools.partial(_gates_kernel, seq=seq),
        grid=(b,),
        in_specs=[
            pl.BlockSpec((None, seq, LANES), lambda bi: (bi, 0, 0)),
            pl.BlockSpec((1, LANES), lambda bi: (0, 0)),
        ],
        out_specs=[
            pl.BlockSpec((None, seq, LANES), lambda bi: (bi, 0, 0)),
            pl.BlockSpec((None, GATE_ROWS, seq), lambda bi: (bi, 0, 0)),
            pl.BlockSpec((None, C_HEADS, seq, LANES), lambda bi: (bi, 0, 0, 0)),
            pl.BlockSpec((None, nchunk, 2 * C_HEADS, LANES), lambda bi: (bi, 0, 0, 0)),
        ],
        out_shape=[
            jax.ShapeDtypeStruct((b, seq, LANES), F32),
            jax.ShapeDtypeStruct((b, GATE_ROWS, seq), F32),
            jax.ShapeDtypeStruct((b, C_HEADS, seq, LANES), F32),
            jax.ShapeDtypeStruct((b, nchunk, 2 * C_HEADS, LANES), F32),
        ],
        compiler_params=_cparams(("arbitrary",)),
        name="gates",
    )(gs3d, bias)


FOX_VROWS = 80
FOX_HEADS = 8


def _fox_kernel(q_ref, k_ref, v_ref, gc_ref, o_ref, kaug_ref, vt_ref, *, t, seq):
    hg = pl.program_id(1)
    qi = pl.program_id(2)
    hd = B_HEAD_DIM
    nh = FOX_HEADS
    lane = lax.broadcasted_iota(jnp.int32, (1, LANES), 1)
    own = (lane < hd, lane >= hd)
    aug0 = (hd, 0)

    @pl.when(qi == 0)
    def _():
        sub = lax.broadcasted_iota(jnp.int32, (FOX_VROWS - hd, seq), 0)
        tail = jnp.where(sub == 0, 1.0, 0.0).astype(BF16)
        for hh in range(nh):
            vt_ref[hh, hd:FOX_VROWS, :] = tail

        def body(r, _):
            r0 = pl.multiple_of(r * LANES, LANES)
            g = gc_ref[pl.ds(r0, LANES), :]
            for pp in range(nh // 2):
                ls = slice(pp * LANES, (pp + 1) * LANES)
                kb = k_ref[pl.ds(r0, LANES), ls].astype(F32)
                vt = v_ref[pl.ds(r0, LANES), ls].astype(F32).T
                for h in range(2):
                    hh = 2 * pp + h
                    neg = -LOG2E * jnp.sum(jnp.where(lane == LANE_BF + nh * hg + hh, g, 0.0), axis=-1,
                                           keepdims=True)
                    hi = neg.astype(BF16).astype(F32)
                    mid = (neg - hi).astype(BF16).astype(F32)
                    lo = (neg - hi) - mid
                    a = aug0[h]
                    extra = jnp.where(lane == a, hi, jnp.where(lane == a + 1, mid, jnp.where(lane == a + 2, lo, 0.0)))
                    kaug_ref[hh, pl.ds(r0, LANES), :] = jnp.where(own[h], kb, extra).astype(BF16)
                    vt_ref[hh, 0:hd, pl.ds(r0, LANES)] = vt[h * hd:(h + 1) * hd, :].astype(BF16)
            return 0

        lax.fori_loop(0, seq // LANES, body, 0)

    qa = []
    for pp in range(nh // 2):
        q = q_ref[:, pp * LANES:(pp + 1) * LANES].astype(F32)
        for h in range(2):
            a = aug0[h]
            ones3 = jnp.where((lane >= a) & (lane < a + 3), 1.0, 0.0)
            qa.append(jnp.where(own[h], q, ones3).astype(BF16))
    row = lax.broadcasted_iota(jnp.int32, (t, t), 0)
    col = lax.broadcasted_iota(jnp.int32, (t, t), 1)
    causal = row <= col

    def qk(hh, k0, tk):
        return _dot_nt(kaug_ref[hh, pl.ds(k0, tk), :], qa[hh])

    def update(k0, tk, state, masked):
        k0 = pl.multiple_of(k0, t)
        sts = [qk(hh, k0, tk) for hh in range(nh)]
        ms, ps, alphas = [], [], []
        for hh in range(nh):
            m = state[hh][0]
            s = jnp.where(causal, sts[hh], NEG) if masked else sts[hh]
            m_new = jnp.maximum(m, jnp.max(s, axis=0, keepdims=True))
            ps.append(jnp.exp2((s - m_new).astype(BF16)))
            alphas.append(jnp.exp2(m - m_new))
            ms.append(m_new)
        pvs = [_dot(vt_ref[hh, :, pl.ds(k0, tk)], ps[hh]) for hh in range(nh)]
        return tuple((ms[hh], alphas[hh] * state[hh][1] + pvs[hh]) for hh in range(nh))

    init = tuple((jnp.full((1, t), NEG, F32), jnp.zeros((FOX_VROWS, t), F32)) for _ in range(nh))
    state = lax.fori_loop(0, qi // 2, lambda j, state: update(j * 2 * t, 2 * t, state, False), init)
    state = lax.cond(qi % 2 == 1, lambda st: update((qi - 1) * t, t, st, False), lambda st: st, state)
    state = update(qi * t, t, state, True)
    ot = jnp.concatenate([acc[0:hd] / acc[hd:hd + 1] for _, acc in state], axis=0)
    o_ref[...] = ot.T.astype(BF16)


def _fox_call(p3d, gc, t):
    b, seq, _ = p3d.shape
    w = FOX_HEADS * B_HEAD_DIM
    return pl.pallas_call(
        functools.partial(_fox_kernel, t=t, seq=seq),
        grid=(b, B_HEADS // FOX_HEADS, seq // t),
        in_specs=[
            pl.BlockSpec((None, t, w), lambda bi, hg, qi: (bi, qi, COL_BQ // w + hg)),
            pl.BlockSpec((None, seq, w), lambda bi, hg, qi: (bi, 0, COL_BK // w + hg)),
            pl.BlockSpec((None, seq, w), lambda bi, hg, qi: (bi, 0, COL_BV // w + hg)),
            pl.BlockSpec((None, seq, LANES), lambda bi, hg, qi: (bi, 0, 0)),
        ],
        out_specs=pl.BlockSpec((None, t, w), lambda bi, hg, qi: (bi, qi, hg)),
        out_shape=jax.ShapeDtypeStruct((b, seq, BRANCH_WIDTH), BF16),
        scratch_shapes=[
            pltpu.VMEM((FOX_HEADS, seq, LANES), BF16),
            pltpu.VMEM((FOX_HEADS, FOX_VROWS, seq), BF16),
        ],
        compiler_params=_cparams(("arbitrary", "arbitrary", "arbitrary")),
        name="fox",
    )(p3d, p3d, p3d, gc)


MLSTM_ROWS = 144


def _mlstm_kernel(q_ref, k_ref, v_ref, o_ref, u_ref, gr_ref, gsum_ref, gmh_ref, y_ref, c_ref, m_ref,
                  *, ts):
    L = MLSTM_CHUNK
    dh = C_HEAD_DIM
    scale = dh ** -0.5
    si = pl.program_id(1)

    @pl.when(si == 0)
    def _():
        c_ref[...] = jnp.zeros_like(c_ref)
        m_ref[...] = jnp.zeros_like(m_ref)

    row = lax.broadcasted_iota(jnp.int32, (L, L), 0)
    col = lax.broadcasted_iota(jnp.int32, (L, L), 1)
    causal = row <= col
    sub = lax.broadcasted_iota(jnp.int32, (MLSTM_ROWS - dh, L), 0)
    tail = jnp.where(sub == 0, 1.0, 0.0)

    def chunk(c, _):
        r0 = pl.multiple_of(c * L, L)
        gsum = gsum_ref[c]
        for h in range(C_HEADS):
            hs = slice(h * dh, (h + 1) * dh)
            q = q_ref[pl.ds(r0, L), hs]
            k = k_ref[pl.ds(r0, L), hs]
            vt = jnp.concatenate([v_ref[pl.ds(r0, L), hs].astype(F32).T, tail], axis=0)
            ig = gr_ref[pl.ds(LANE_CI + h, 1), pl.ds(r0, L)]
            b = gr_ref[pl.ds(LANE_CF + h, 1), pl.ds(r0, L)]
            g = gsum[h:h + 1, :]
            m_loc = gsum[C_HEADS + h:C_HEADS + h + 1, :]
            m_in = m_ref[h]
            c_in = c_ref[h]

            dlog = jnp.where(causal, u_ref[h, pl.ds(r0, L), :] + b, NEG)
            inter = b + m_in
            m_t = jnp.maximum(jnp.max(dlog, axis=0, keepdims=True), inter)
            sm = (_dot_nt(k, q) * scale) * jnp.exp(dlog - m_t)
            w_int = jnp.exp(inter - m_t)
            ext = _dot(vt.astype(BF16), sm.astype(BF16)) + w_int * _dot_nt(c_in.astype(BF16), q)
            hh = ext[0:dh] / jnp.maximum(jnp.abs(ext[dh:dh + 1]), jnp.exp(-m_t))
            hn = hh * lax.rsqrt(jnp.mean(hh * hh, axis=0, keepdims=True) + EPS) * gmh_ref[h]
            y_ref[pl.ds(r0, L), hs] = (hn.T * o_ref[pl.ds(r0, L), hs].astype(F32)).astype(BF16)

            m_new = jnp.maximum(g + m_in, m_loc)
            w = jnp.exp(g + (ig - b) - m_new) * scale
            c_ref[h] = jnp.exp(g + m_in - m_new) * c_in + _dot((vt * w).astype(BF16), k)
            m_ref[h] = m_new
        return 0

    lax.fori_loop(0, ts // L, chunk, 0, unroll=2)


def _mlstm_call(p3d, u, gr, gsum, gmh_rep, ts):
    b, seq, _ = p3d.shape
    w = BRANCH_WIDTH
    nc = ts // MLSTM_CHUNK

    def pspec(col):
        return pl.BlockSpec((None, ts, w), lambda bi, si: (bi, si, col // w))

    return pl.pallas_call(
        functools.partial(_mlstm_kernel, ts=ts),
        grid=(b, seq // ts),
        in_specs=[
            pspec(COL_CQ), pspec(COL_CK), pspec(COL_CV), pspec(COL_CO),
            pl.BlockSpec((None, C_HEADS, ts, LANES), lambda bi, si: (bi, 0, si, 0)),
            pl.BlockSpec((None, GATE_ROWS, ts), lambda bi, si: (bi, 0, si)),
            pl.BlockSpec((None, nc, 2 * C_HEADS, LANES), lambda bi, si: (bi, si, 0, 0)),
            _const_spec((C_HEADS, C_HEAD_DIM, LANES)),
        ],
        out_specs=pl.BlockSpec((None, ts, w), lambda bi, si: (bi, si, 0)),
        out_shape=jax.ShapeDtypeStruct((b, seq, w), BF16),
        scratch_shapes=[
            pltpu.VMEM((C_HEADS, MLSTM_ROWS, C_HEAD_DIM), F32),
            pltpu.VMEM((C_HEADS, 1, LANES), F32),
        ],
        compiler_params=_cparams(("arbitrary", "arbitrary")),
        name="mlstm",
    )(p3d, p3d, p3d, p3d, u, gr, gsum, gmh_rep)


def _merge_kernel(gates_ref, uv_ref, yb_ref, yc_ref, x_ref, gsgu_ref, ws_ref, bs_ref, wb_ref, wo_ref,
                  out_ref, ya_ref, *, tm):
    w = BRANCH_WIDTH
    u = uv_ref[:, :w].astype(F32)
    v = uv_ref[:, w:].astype(F32)
    vn = _rms(v, gsgu_ref[...]).astype(BF16)
    gd = w // A_GROUPS
    for nb in range(tm // A_BLOCK):
        rs = slice(nb * A_BLOCK, (nb + 1) * A_BLOCK)
        mixed = jnp.concatenate(
            [_dot(ws_ref[g], vn[rs, g * gd:(g + 1) * gd]) for g in range(A_GROUPS)], axis=1)
        ya_ref[rs, :] = (u[rs, :] * (mixed + bs_ref[...])).astype(BF16)
    merged = gates_ref[:, 0:D_MODEL].astype(F32) * _dot(ya_ref[...], wb_ref[0])
    merged += gates_ref[:, D_MODEL:2 * D_MODEL].astype(F32) * _dot(yb_ref[...], wb_ref[1])
    merged += gates_ref[:, 2 * D_MODEL:3 * D_MODEL].astype(F32) * _dot(yc_ref[...], wb_ref[2])
    out_ref[...] = x_ref[...] + _dot(merged.astype(BF16), wo_ref[...])


def _merge_call(p2d, yb, yc, x2d, g_sgu, ws_masked, bs_full, w_branch, w_out, tm):
    m = x2d.shape[0]
    w = BRANCH_WIDTH
    return pl.pallas_call(
        functools.partial(_merge_kernel, tm=tm),
        grid=(m // tm,),
        in_specs=[
            pl.BlockSpec((tm, 3 * D_MODEL), lambda i: (i, COL_G // (3 * D_MODEL))),
            pl.BlockSpec((tm, 2 * w), lambda i: (i, COL_A // (2 * w))),
            pl.BlockSpec((tm, w), lambda i: (i, 0)),
            pl.BlockSpec((tm, w), lambda i: (i, 0)),
            pl.BlockSpec((tm, D_MODEL), lambda i: (i, 0)),
            _const_spec((1, w)),
            _const_spec((A_GROUPS, A_BLOCK, A_BLOCK)),
            _const_spec((A_BLOCK, w)),
            _const_spec((3, w, D_MODEL)),
            _const_spec((D_MODEL, D_MODEL)),
        ],
        out_specs=pl.BlockSpec((tm, D_MODEL), lambda i: (i, 0)),
        out_shape=jax.ShapeDtypeStruct((m, D_MODEL), F32),
        scratch_shapes=[pltpu.VMEM((tm, w), BF16)],
        compiler_params=_cparams(("arbitrary",)),
        name="merge",
    )(p2d, p2d, yb, yc, x2d, g_sgu, ws_masked, bs_full, w_branch, w_out)


def _memkv_kernel(mem_ref, g_ref, w_ref, kv_ref):
    kv_ref[...] = _dot(_rms(mem_ref[...], g_ref[...]).astype(BF16), w_ref[...]).astype(BF16)


def _memkv_call(mem2d, g, w_mkv):
    m = mem2d.shape[0]
    return pl.pallas_call(
        _memkv_kernel,
        grid=(m // N_MEM,),
        in_specs=[
            pl.BlockSpec((N_MEM, D_MODEL), lambda i: (i, 0)),
            _const_spec((1, D_MODEL)),
            _const_spec((D_MODEL, 2 * D_MODEL)),
        ],
        out_specs=pl.BlockSpec((N_MEM, 2 * D_MODEL), lambda i: (i, 0)),
        out_shape=jax.ShapeDtypeStruct((m, 2 * D_MODEL), BF16),
        compiler_params=_cparams(("arbitrary",)),
        name="memkv",
    )(mem2d, g, w_mkv)


def _memattn_kernel(x_ref, g_ref, wq_ref, kv_ref, wo_ref, out_ref, o_ref):
    x = x_ref[...]
    h = _rms(x, g_ref[...]).astype(BF16)
    q = (_dot(h, wq_ref[...]) * (MEM_HEAD_DIM ** -0.5)).astype(BF16)
    dh = MEM_HEAD_DIM
    for hd in range(MEM_HEADS):
        hs = slice(hd * dh, (hd + 1) * dh)
        s = _dot_nt(q[:, hs], kv_ref[:, hs])
        p = jnp.exp(s - jnp.max(s, axis=-1, keepdims=True))
        o = _dot(p.astype(BF16), kv_ref[:, D_MODEL + hd * dh:D_MODEL + (hd + 1) * dh])
        o_ref[:, hs] = (o / jnp.sum(p, axis=-1, keepdims=True)).astype(BF16)
    out_ref[...] = x + _dot(o_ref[...], wo_ref[...])


def _memattn_call(x2d, g, w_mq, kv, w_mo, seq, tm):
    m = x2d.shape[0]
    return pl.pallas_call(
        _memattn_kernel,
        grid=(m // tm,),
        in_specs=[
            pl.BlockSpec((tm, D_MODEL), lambda i: (i, 0)),
            _const_spec((1, D_MODEL)),
            _const_spec((D_MODEL, D_MODEL)),
            pl.BlockSpec((N_MEM, 2 * D_MODEL), lambda i: ((i * tm) // seq, 0)),
            _const_spec((D_MODEL, D_MODEL)),
        ],
        out_specs=pl.BlockSpec((tm, D_MODEL), lambda i: (i, 0)),
        out_shape=jax.ShapeDtypeStruct((m, D_MODEL), F32),
        scratch_shapes=[pltpu.VMEM((tm, D_MODEL), BF16)],
        compiler_params=_cparams(("arbitrary",)),
        name="memattn",
    )(x2d, g, w_mq, kv, w_mo)


def _ffn_kernel(x_ref, g_ref, wup_ref, wconv_ref, wdown_ref, gfin_ref, out_ref,
                act_ref, ext_ref, carry_ref, *, tm, tf, seq, final_norm):
    i = pl.program_id(0)
    x = x_ref[...]
    h = _rms(x, g_ref[...]).astype(BF16)
    seq_start = (i * tm) % seq == 0

    def conv(slot, half, cs):
        up = _dot(h, wup_ref[:, cs])
        ext_ref[half, pl.ds(CARRY_ROWS, tm), :] = up
        ext_ref[half, pl.ds(0, CARRY_ROWS), :] = jnp.where(seq_start, 0.0, carry_ref[slot])
        carry_ref[slot] = up[tm - CARRY_ROWS:, :]
        wc = wconv_ref[:, cs]
        y = wc[FFN_CONV - 1:FFN_CONV, :] * up
        for d in range(1, FFN_CONV):
            y = y + wc[FFN_CONV - 1 - d:FFN_CONV - d, :] * ext_ref[half, pl.ds(CARRY_ROWS - d, tm), :]
        return y

    nchunk = D_FF // tf
    for c in range(nchunk):
        a = conv(c, 0, slice(c * tf, (c + 1) * tf))
        b = conv(nchunk + c, 1, slice(D_FF + c * tf, D_FF + (c + 1) * tf))
        act_ref[:, c * tf:(c + 1) * tf] = (a * _sigmoid(a) * b).astype(BF16)
    y = x + _dot(act_ref[...], wdown_ref[...])
    if final_norm:
        y = _rms(y, gfin_ref[...])
    out_ref[...] = y


def _ffn_call(x2d, g, w_up, w_conv, w_down, g_final, seq, tm, tf, final_norm):
    m = x2d.shape[0]
    return pl.pallas_call(
        functools.partial(_ffn_kernel, tm=tm, tf=tf, seq=seq, final_norm=final_norm),
        grid=(m // tm,),
        in_specs=[
            pl.BlockSpec((tm, D_MODEL), lambda i: (i, 0)),
            _const_spec((1, D_MODEL)),
            _const_spec((D_MODEL, 2 * D_FF)),
            _const_spec((FFN_CONV, 2 * D_FF)),
            _const_spec((D_FF, D_MODEL)),
            _const_spec((1, D_MODEL)),
        ],
        out_specs=pl.BlockSpec((tm, D_MODEL), lambda i: (i, 0)),
        out_shape=jax.ShapeDtypeStruct((m, D_MODEL), F32),
        scratch_shapes=[
            pltpu.VMEM((tm, D_FF), BF16),
            pltpu.VMEM((2, tm + CARRY_ROWS, tf), F32),
            pltpu.VMEM((2 * (D_FF // tf), CARRY_ROWS, tf), F32),
        ],
        compiler_params=_cparams(("arbitrary",)),
        name="ffn",
    )(x2d, g, w_up, w_conv, w_down, g_final)


def _rearrange_w_in(w):
    bw = BRANCH_WIDTH
    a0 = 0
    b0 = 2 * bw
    c0 = b0 + 3 * bw + B_HEADS
    g0 = c0 + 3 * bw + 2 * C_HEADS + bw
    co = c0 + 3 * bw + 2 * C_HEADS
    main = jnp.concatenate([
        w[:, g0:g0 + 3 * D_MODEL],
        w[:, a0:a0 + 2 * bw],
        w[:, b0:b0 + 3 * bw],
        w[:, c0:c0 + 3 * bw],
        w[:, co:co + bw],
    ], axis=1).astype(BF16)
    small = jnp.concatenate([
        w[:, b0 + 3 * bw:b0 + 3 * bw + B_HEADS],
        w[:, c0 + 3 * bw:c0 + 3 * bw + 2 * C_HEADS],
    ], axis=1)
    small = jnp.pad(small, ((0, 0), (0, LANES - small.shape[1]))).astype(BF16)
    return main, small


def kernel(x, mem, g_mix, w_in, g_sgu, w_s, b_s, b_fox_f, w_conv_c, b_mlstm_i, b_mlstm_f, g_mh,
           w_branch, w_out, g_mem_q, g_mem_kv, w_mq, w_mkv, w_mo, g_ffn, w_up, w_ffn_conv, w_down,
           g_final):
    bsz, seq, _ = x.shape
    depth = w_in.shape[0]
    m = bsz * seq
    tm = min(512, seq)
    tm_proj = tm
    t_fox = min(256, seq)
    ts_mlstm = min(1024, seq)

    idx = jnp.arange(A_BLOCK)
    chunk_causal = (idx[None, :] // CHUNK) <= (idx[:, None] // CHUNK)

    x2d = x.reshape(m, D_MODEL)
    mem2d = mem.reshape(bsz * N_MEM, D_MODEL)
    row = lambda a: a.reshape(1, -1)

    for i in range(depth):
        w_main, w_small = _rearrange_w_in(w_in[i])
        gate_bias = jnp.pad(jnp.concatenate([b_fox_f[i], b_mlstm_i[i], b_mlstm_f[i]]),
                            (0, LANES - GATE_ROWS)).reshape(1, LANES)
        ws_masked = jnp.where(chunk_causal[None], w_s[i], 0).astype(BF16)
        bs_full = jnp.repeat(b_s[i].T, BRANCH_WIDTH // A_GROUPS, axis=1)

        p2d, gs = _proj_call(x2d, row(g_mix[i]), w_main, w_small, w_conv_c[i], seq, tm_proj)
        p3d = p2d.reshape(bsz, seq, PROJ_COLS)
        gc, gr, u, gsum = _gates_call(gs.reshape(bsz, seq, LANES), gate_bias)
        yb = _fox_call(p3d, gc, t_fox).reshape(m, BRANCH_WIDTH)
        gmh_rep = jnp.broadcast_to(g_mh[i].reshape(C_HEADS, C_HEAD_DIM, 1), (C_HEADS, C_HEAD_DIM, LANES))
        yc = _mlstm_call(p3d, u, gr, gsum, gmh_rep, ts_mlstm).reshape(m, BRANCH_WIDTH)
        x2d = _merge_call(p2d, yb, yc, x2d, row(g_sgu[i]), ws_masked, bs_full,
                          w_branch[i].astype(BF16), w_out[i].astype(BF16), tm)

        kv = _memkv_call(mem2d, row(g_mem_kv[i]), w_mkv[i].astype(BF16))
        x2d = _memattn_call(x2d, row(g_mem_q[i]), w_mq[i].astype(BF16), kv, w_mo[i].astype(BF16), seq, tm)

        x2d = _ffn_call(x2d, row(g_ffn[i]), w_up[i].astype(BF16), w_ffn_conv[i], w_down[i].astype(BF16),
                        row(g_final), seq, tm, 256, i == depth - 1)
    return x2d.reshape(bsz, seq, D_MODEL)
```

```python
import functools

import jax
import jax.numpy as jnp
from jax import lax
from jax.experimental import pallas as pl
from jax.experimental.pallas import tpu as pltpu

F32 = jnp.float32
BF16 = jnp.bfloat16

D_MODEL = 1024
EPS = 1e-6
LANES = 128
CARRY_ROWS = 8

BRANCH_WIDTH = 512
A_BLOCK = 128
A_GROUPS = 4
CHUNK = 64
B_HEADS = 8
B_HEAD_DIM = 64
C_HEADS = 4
C_HEAD_DIM = 128
C_CONV = 4
MLSTM_CHUNK = 128
N_MEM = 256
MEM_HEADS = 4
MEM_HEAD_DIM = 256
D_FF = 2816
FFN_CONV = 3

PROJ_TN = 512
COL_G = 0
COL_A = 3072
COL_BQ = 4096
COL_BK = 4608
COL_BV = 5120
COL_CQ = 5632
COL_CK = 6144
COL_CV = 6656
COL_CO = 7168
PROJ_COLS = 7680
LANE_BF = 0
LANE_CI = 8
LANE_CF = 12
GATE_ROWS = 16

LOG2E = 1.4426950408889634
FOX_Q_SCALE = B_HEAD_DIM ** -0.5 * LOG2E
NEG = -1e30
VMEM_LIMIT = 56 * 1024 * 1024


def _cparams(sem):
    return pltpu.CompilerParams(dimension_semantics=sem, vmem_limit_bytes=VMEM_LIMIT)


def _rms(xf, g):
    return xf * lax.rsqrt(jnp.mean(xf * xf, axis=-1, keepdims=True) + EPS) * g


def _sigmoid(x):
    return 0.5 * jnp.tanh(0.5 * x) + 0.5


def _gelu_tanh(x):
    return 0.5 * x * (1.0 + jnp.tanh(0.7978845608028654 * (x + 0.044715 * (x * x * x))))


def _dot(a, b):
    return jnp.dot(a, b, preferred_element_type=F32)


def _dot_nt(a, b):
    return lax.dot_general(a, b, (((1,), (1,)), ((), ())), preferred_element_type=F32)


def _const_spec(shape):
    nd = len(shape)
    return pl.BlockSpec(shape, lambda *_: (0,) * nd, pipeline_mode=pl.Buffered(1))


def _proj_kernel(x_ref, g_ref, w_ref, ws_ref, wc_ref, p_ref, gs_ref, ext_ref, carry_ref, *, tm, seq):
    i = pl.program_id(0)
    h = _rms(x_ref[...], g_ref[...]).astype(BF16)
    gs_ref[...] = _dot(h, ws_ref[...])
    seq_start = (i * tm) % seq == 0

    def conv_silu(acc, slot):
        ext_ref[slot, pl.ds(CARRY_ROWS, tm), :] = acc
        ext_ref[slot, pl.ds(0, CARRY_ROWS), :] = jnp.where(seq_start, 0.0, carry_ref[slot])
        carry_ref[slot] = ext_ref[slot, pl.ds(tm, CARRY_ROWS), :]
        wc = 0.5 * wc_ref[:, slot * PROJ_TN:(slot + 1) * PROJ_TN]
        half = wc[C_CONV - 1:C_CONV, :] * ext_ref[slot, pl.ds(CARRY_ROWS, tm), :]
        for d in range(1, C_CONV):
            half = half + wc[C_CONV - 1 - d:C_CONV - d, :] * ext_ref[slot, pl.ds(CARRY_ROWS - d, tm), :]
        return half * (jnp.tanh(half) + 1.0)

    order = [COL_CQ, COL_CK] + [c for c in range(0, PROJ_COLS, PROJ_TN) if c not in (COL_CQ, COL_CK)]
    for c0 in order:
        cs = slice(c0, c0 + PROJ_TN)
        acc = _dot(h, w_ref[:, cs])
        if c0 < COL_A or c0 == COL_CO:
            out = _sigmoid(acc)
        elif c0 < COL_BQ:
            out = _gelu_tanh(acc)
        elif c0 == COL_BQ:
            out = acc * FOX_Q_SCALE
        elif c0 == COL_CQ:
            out = conv_silu(acc, 0)
        elif c0 == COL_CK:
            out = conv_silu(acc, 1)
        else:
            out = acc
        p_ref[:, cs] = out.astype(BF16)


def _proj_call(x2d, g, w_main, w_small, w_conv, seq, tm):
    m = x2d.shape[0]
    return pl.pallas_call(
        functools.partial(_proj_kernel, tm=tm, seq=seq),
        grid=(m // tm,),
        in_specs=[
            pl.BlockSpec((tm, D_MODEL), lambda i: (i, 0)),
            _const_spec((1, D_MODEL)),
            _const_spec((D_MODEL, PROJ_COLS)),
            _const_spec((D_MODEL, LANES)),
            _const_spec((C_CONV, 2 * PROJ_TN)),
        ],
        out_specs=[
            pl.BlockSpec((tm, PROJ_COLS), lambda i: (i, 0)),
            pl.BlockSpec((tm, LANES), lambda i: (i, 0)),
        ],
        out_shape=[
            jax.ShapeDtypeStruct((m, PROJ_COLS), BF16),
            jax.ShapeDtypeStruct((m, LANES), F32),
        ],
        scratch_shapes=[
            pltpu.VMEM((2, tm + CARRY_ROWS, PROJ_TN), F32),
            pltpu.VMEM((2, CARRY_ROWS, PROJ_TN), F32),
        ],
        compiler_params=_cparams(("arbitrary",)),
        name="proj",
    )(x2d, g, w_main, w_small, w_conv)


def _gates_kernel(gs_ref, bias_ref, gc_ref, gr_ref, u_ref, gsum_ref, *, seq):
    blk = MLSTM_CHUNK
    row = lax.broadcasted_iota(jnp.int32, (blk, blk), 0)
    col = lax.broadcasted_iota(jnp.int32, (blk, blk), 1)
    tri = (col <= row).astype(F32)
    lane = lax.broadcasted_iota(jnp.int32, (1, LANES), 1)

    def body(r, carry):
        r0 = pl.multiple_of(r * blk, blk)
        raw = gs_ref[pl.ds(r0, blk), :] + bias_ref[...]
        logsig = jnp.minimum(raw, 0.0) - jnp.log1p(jnp.exp(-jnp.abs(raw)))
        local = jnp.dot(tri, logsig, precision=lax.Precision.HIGHEST, preferred_element_type=F32)
        glob = local + carry
        out = jnp.where(lane < LANE_CI, glob, jnp.where(lane < LANE_CF, raw, local))
        gc_ref[pl.ds(r0, blk), :] = out
        gr_ref[:, pl.ds(r0, blk)] = out.T[0:GATE_ROWS, :]
        g_rows, mloc_rows = [], []
        for h in range(C_HEADS):
            d = out[:, LANE_CI + h:LANE_CI + h + 1] - out[:, LANE_CF + h:LANE_CF + h + 1]
            u = jnp.broadcast_to(d, (blk, LANES))
            u_ref[h, pl.ds(r0, blk), :] = u
            g = jnp.broadcast_to(out[blk - 1:blk, LANE_CF + h:LANE_CF + h + 1], (1, LANES))
            g_rows.append(g)
            mloc_rows.append(g + jnp.max(u, axis=0, keepdims=True))
        gsum_ref[r] = jnp.concatenate(g_rows + mloc_rows, axis=0)
        return glob[blk - 1:blk, :]

    lax.fori_loop(0, seq // blk, body, jnp.zeros((1, LANES), F32))


def _gates_call(gs3d, bias):
    b, seq, _ = gs3d.shape
    nchunk = seq // MLSTM_CHUNK
    return pl.pallas_call(
        functools.partial(_gates_kernel, seq=seq),
        grid=(b,),
        in_specs=[
            pl.BlockSpec((None, seq, LANES), lambda bi: (bi, 0, 0)),
            pl.BlockSpec((1, LANES), lambda bi: (0, 0)),
        ],
        out_specs=[
            pl.BlockSpec((None, seq, LANES), lambda bi: (bi, 0, 0)),
            pl.BlockSpec((None, GATE_ROWS, seq), lambda bi: (bi, 0, 0)),
            pl.BlockSpec((None, C_HEADS, seq, LANES), lambda bi: (bi, 0, 0, 0)),
            pl.BlockSpec((None, nchunk, 2 * C_HEADS, LANES), lambda bi: (bi, 0, 0, 0)),
        ],
        out_shape=[
            jax.ShapeDtypeStruct((b, seq, LANES), F32),
            jax.ShapeDtypeStruct((b, GATE_ROWS, seq), F32),
            jax.ShapeDtypeStruct((b, C_HEADS, seq, LANES), F32),
            jax.ShapeDtypeStruct((b, nchunk, 2 * C_HEADS, LANES), F32),
        ],
        compiler_params=_cparams(("arbitrary",)),
        name="gates",
    )(gs3d, bias)


FOX_VROWS = 80
FOX_KMULT = 2
FOX_HEADS = 8


def _fox_kernel(q_ref, k_ref, v_ref, gc_ref, o_ref, kaug_ref, vt_ref, *, t, seq):
    hg = pl.program_id(1)
    qi = pl.program_id(2)
    hd = B_HEAD_DIM
    nh = FOX_HEADS
    lane = lax.broadcasted_iota(jnp.int32, (1, LANES), 1)
    own = (lane < hd, lane >= hd)
    aug0 = (hd, 0)

    @pl.when(qi == 0)
    def _():
        sub = lax.broadcasted_iota(jnp.int32, (FOX_VROWS - hd, seq), 0)
        tail = jnp.where(sub == 0, 1.0, 0.0).astype(BF16)
        for hh in range(nh):
            vt_ref[hh, hd:FOX_VROWS, :] = tail

        def body(r, _):
            r0 = pl.multiple_of(r * LANES, LANES)
            g = gc_ref[pl.ds(r0, LANES), :]
            for pp in range(nh // 2):
                ls = slice(pp * LANES, (pp + 1) * LANES)
                kb = k_ref[pl.ds(r0, LANES), ls].astype(F32)
                vt = v_ref[pl.ds(r0, LANES), ls].astype(F32).T
                for h in range(2):
                    hh = 2 * pp + h
                    neg = -LOG2E * jnp.sum(jnp.where(lane == LANE_BF + nh * hg + hh, g, 0.0), axis=-1,
                                           keepdims=True)
                    hi = neg.astype(BF16).astype(F32)
                    mid = (neg - hi).astype(BF16).astype(F32)
                    lo = (neg - hi) - mid
                    a = aug0[h]
                    extra = jnp.where(lane == a, hi, jnp.where(lane == a + 1, mid, jnp.where(lane == a + 2, lo, 0.0)))
                    kaug_ref[hh, pl.ds(r0, LANES), :] = jnp.where(own[h], kb, extra).astype(BF16)
                    vt_ref[hh, 0:hd, pl.ds(r0, LANES)] = vt[h * hd:(h + 1) * hd, :].astype(BF16)
            return 0

        lax.fori_loop(0, seq // LANES, body, 0)

    qa = []
    for pp in range(nh // 2):
        q = q_ref[:, pp * LANES:(pp + 1) * LANES].astype(F32)
        for h in range(2):
            a = aug0[h]
            ones3 = jnp.where((lane >= a) & (lane < a + 3), 1.0, 0.0)
            qa.append(jnp.where(own[h], q, ones3).astype(BF16))
    row = lax.broadcasted_iota(jnp.int32, (t, t), 0)
    col = lax.broadcasted_iota(jnp.int32, (t, t), 1)
    causal = row <= col

    def qk(hh, k0, tk):
        return _dot_nt(kaug_ref[hh, pl.ds(k0, tk), :], qa[hh])

    def update(k0, tk, state, masked):
        k0 = pl.multiple_of(k0, t)
        sts = [qk(hh, k0, tk) for hh in range(nh)]
        ms, ps, alphas = [], [], []
        for hh in range(nh):
            m = state[hh][0]
            s = jnp.where(causal, sts[hh], NEG) if masked else sts[hh]
            m_new = jnp.maximum(m, jnp.max(s, axis=0, keepdims=True))
            ps.append(jnp.exp2((s - m_new).astype(BF16)))
            alphas.append(jnp.exp2(m - m_new))
            ms.append(m_new)
        pvs = [_dot(vt_ref[hh, :, pl.ds(k0, tk)], ps[hh]) for hh in range(nh)]
        return tuple((ms[hh], alphas[hh] * state[hh][1] + pvs[hh]) for hh in range(nh))

    km = FOX_KMULT
    init = tuple((jnp.full((1, t), NEG, F32), jnp.zeros((FOX_VROWS, t), F32)) for _ in range(nh))
    state = lax.fori_loop(0, qi // km, lambda j, state: update(j * km * t, km * t, state, False), init)
    state = lax.fori_loop((qi // km) * km, qi, lambda j, state: update(j * t, t, state, False), state)
    state = update(qi * t, t, state, True)
    ot = jnp.concatenate([acc[0:hd] / acc[hd:hd + 1] for _, acc in state], axis=0)
    o_ref[...] = ot.T.astype(BF16)


def _fox_call(p3d, gc, t):
    b, seq, _ = p3d.shape
    w = FOX_HEADS * B_HEAD_DIM
    return pl.pallas_call(
        functools.partial(_fox_kernel, t=t, seq=seq),
        grid=(b, B_HEADS // FOX_HEADS, seq // t),
        in_specs=[
            pl.BlockSpec((None, t, w), lambda bi, hg, qi: (bi, qi, COL_BQ // w + hg)),
            pl.BlockSpec((None, seq, w), lambda bi, hg, qi: (bi, 0, COL_BK // w + hg)),
            pl.BlockSpec((None, seq, w), lambda bi, hg, qi: (bi, 0, COL_BV // w + hg)),
            pl.BlockSpec((None, seq, LANES), lambda bi, hg, qi: (bi, 0, 0)),
        ],
        out_specs=pl.BlockSpec((None, t, w), lambda bi, hg, qi: (bi, qi, hg)),
        out_shape=jax.ShapeDtypeStruct((b, seq, BRANCH_WIDTH), BF16),
        scratch_shapes=[
            pltpu.VMEM((FOX_HEADS, seq, LANES), BF16),
            pltpu.VMEM((FOX_HEADS, FOX_VROWS, seq), BF16),
        ],
        compiler_params=_cparams(("arbitrary", "arbitrary", "arbitrary")),
        name="fox",
    )(p3d, p3d, p3d, gc)


MLSTM_ROWS = 144


def _mlstm_kernel(q_ref, k_ref, v_ref, o_ref, u_ref, gr_ref, gsum_ref, gmh_ref, y_ref, c_ref, m_ref,
                  *, ts):
    L = MLSTM_CHUNK
    dh = C_HEAD_DIM
    scale = dh ** -0.5
    si = pl.program_id(1)

    @pl.when(si == 0)
    def _():
        c_ref[...] = jnp.zeros_like(c_ref)
        m_ref[...] = jnp.zeros_like(m_ref)

    row = lax.broadcasted_iota(jnp.int32, (L, L), 0)
    col = lax.broadcasted_iota(jnp.int32, (L, L), 1)
    causal = row <= col
    sub = lax.broadcasted_iota(jnp.int32, (MLSTM_ROWS - dh, L), 0)
    tail = jnp.where(sub == 0, 1.0, 0.0)

    def chunk(c, _):
        r0 = pl.multiple_of(c * L, L)
        gsum = gsum_ref[c]
        for h in range(C_HEADS):
            hs = slice(h * dh, (h + 1) * dh)
            q = q_ref[pl.ds(r0, L), hs]
            k = k_ref[pl.ds(r0, L), hs]
            vt = jnp.concatenate([v_ref[pl.ds(r0, L), hs].astype(F32).T, tail], axis=0)
            ig = gr_ref[pl.ds(LANE_CI + h, 1), pl.ds(r0, L)]
            b = gr_ref[pl.ds(LANE_CF + h, 1), pl.ds(r0, L)]
            g = gsum[h:h + 1, :]
            m_loc = gsum[C_HEADS + h:C_HEADS + h + 1, :]
            m_in = m_ref[h]
            c_in = c_ref[h]

            dlog = jnp.where(causal, u_ref[h, pl.ds(r0, L), :] + b, NEG)
            inter = b + m_in
            m_t = jnp.maximum(jnp.max(dlog, axis=0, keepdims=True), inter)
            sm = (_dot_nt(k, q) * scale) * jnp.exp(dlog - m_t)
            w_int = jnp.exp(inter - m_t)
            ext = _dot(vt.astype(BF16), sm.astype(BF16)) + w_int * _dot_nt(c_in.astype(BF16), q)
            hh = ext[0:dh] / jnp.maximum(jnp.abs(ext[dh:dh + 1]), jnp.exp(-m_t))
            hn = hh * lax.rsqrt(jnp.mean(hh * hh, axis=0, keepdims=True) + EPS) * gmh_ref[h]
            y_ref[pl.ds(r0, L), hs] = (hn.T * o_ref[pl.ds(r0, L), hs].astype(F32)).astype(BF16)

            m_new = jnp.maximum(g + m_in, m_loc)
            w = jnp.exp(g + (ig - b) - m_new) * scale
            c_ref[h] = jnp.exp(g + m_in - m_new) * c_in + _dot((vt * w).astype(BF16), k)
            m_ref[h] = m_new
        return 0

    lax.fori_loop(0, ts // L, chunk, 0, unroll=4)


def _mlstm_call(p3d, u, gr, gsum, gmh_rep, ts):
    b, seq, _ = p3d.shape
    w = BRANCH_WIDTH
    nc = ts // MLSTM_CHUNK

    def pspec(col):
        return pl.BlockSpec((None, ts, w), lambda bi, si: (bi, si, col // w))

    return pl.pallas_call(
        functools.partial(_mlstm_kernel, ts=ts),
        grid=(b, seq // ts),
        in_specs=[
            pspec(COL_CQ), pspec(COL_CK), pspec(COL_CV), pspec(COL_CO),
            pl.BlockSpec((None, C_HEADS, ts, LANES), lambda bi, si: (bi, 0, si, 0)),
            pl.BlockSpec((None, GATE_ROWS, ts), lambda bi, si: (bi, 0, si)),
            pl.BlockSpec((None, nc, 2 * C_HEADS, LANES), lambda bi, si: (bi, si, 0, 0)),
            _const_spec((C_HEADS, C_HEAD_DIM, LANES)),
        ],
        out_specs=pl.BlockSpec((None, ts, w), lambda bi, si: (bi, si, 0)),
        out_shape=jax.ShapeDtypeStruct((b, seq, w), BF16),
        scratch_shapes=[
            pltpu.VMEM((C_HEADS, MLSTM_ROWS, C_HEAD_DIM), F32),
            pltpu.VMEM((C_HEADS, 1, LANES), F32),
        ],
        compiler_params=_cparams(("arbitrary", "arbitrary")),
        name="mlstm",
    )(p3d, p3d, p3d, p3d, u, gr, gsum, gmh_rep)


def _merge_kernel(gates_ref, uv_ref, yb_ref, yc_ref, x_ref, gsgu_ref, ws_ref, bs_ref, wb_ref, wo_ref,
                  out_ref, ya_ref, *, tm):
    w = BRANCH_WIDTH
    u = uv_ref[:, :w].astype(F32)
    v = uv_ref[:, w:].astype(F32)
    vn = _rms(v, gsgu_ref[...]).astype(BF16)
    gd = w // A_GROUPS
    for nb in range(tm // A_BLOCK):
        rs = slice(nb * A_BLOCK, (nb + 1) * A_BLOCK)
        mixed = jnp.concatenate(
            [_dot(ws_ref[g], vn[rs, g * gd:(g + 1) * gd]) for g in range(A_GROUPS)], axis=1)
        ya_ref[rs, :] = (u[rs, :] * (mixed + bs_ref[...])).astype(BF16)
    merged = gates_ref[:, 0:D_MODEL].astype(F32) * _dot(ya_ref[...], wb_ref[0])
    merged += gates_ref[:, D_MODEL:2 * D_MODEL].astype(F32) * _dot(yb_ref[...], wb_ref[1])
    merged += gates_ref[:, 2 * D_MODEL:3 * D_MODEL].astype(F32) * _dot(yc_ref[...], wb_ref[2])
    out_ref[...] = x_ref[...] + _dot(merged.astype(BF16), wo_ref[...])


def _merge_call(p2d, yb, yc, x2d, g_sgu, ws_masked, bs_full, w_branch, w_out, tm):
    m = x2d.shape[0]
    w = BRANCH_WIDTH
    return pl.pallas_call(
        functools.partial(_merge_kernel, tm=tm),
        grid=(m // tm,),
        in_specs=[
            pl.BlockSpec((tm, 3 * D_MODEL), lambda i: (i, COL_G // (3 * D_MODEL))),
            pl.BlockSpec((tm, 2 * w), lambda i: (i, COL_A // (2 * w))),
            pl.BlockSpec((tm, w), lambda i: (i, 0)),
            pl.BlockSpec((tm, w), lambda i: (i, 0)),
            pl.BlockSpec((tm, D_MODEL), lambda i: (i, 0)),
            _const_spec((1, w)),
            _const_spec((A_GROUPS, A_BLOCK, A_BLOCK)),
            _const_spec((A_BLOCK, w)),
            _const_spec((3, w, D_MODEL)),
            _const_spec((D_MODEL, D_MODEL)),
        ],
        out_specs=pl.BlockSpec((tm, D_MODEL), lambda i: (i, 0)),
        out_shape=jax.ShapeDtypeStruct((m, D_MODEL), F32),
        scratch_shapes=[pltpu.VMEM((tm, w), BF16)],
        compiler_params=_cparams(("arbitrary",)),
        name="merge",
    )(p2d, p2d, yb, yc, x2d, g_sgu, ws_masked, bs_full, w_branch, w_out)


def _memkv_kernel(mem_ref, g_ref, w_ref, kv_ref):
    kv_ref[...] = _dot(_rms(mem_ref[...], g_ref[...]).astype(BF16), w_ref[...]).astype(BF16)


def _memkv_call(mem2d, g, w_mkv):
    m = mem2d.shape[0]
    return pl.pallas_call(
        _memkv_kernel,
        grid=(m // N_MEM,),
        in_specs=[
            pl.BlockSpec((N_MEM, D_MODEL), lambda i: (i, 0)),
            _const_spec((1, D_MODEL)),
            _const_spec((D_MODEL, 2 * D_MODEL)),
        ],
        out_specs=pl.BlockSpec((N_MEM, 2 * D_MODEL), lambda i: (i, 0)),
        out_shape=jax.ShapeDtypeStruct((m, 2 * D_MODEL), BF16),
        compiler_params=_cparams(("arbitrary",)),
        name="memkv",
    )(mem2d, g, w_mkv)


def _memattn_kernel(x_ref, g_ref, wq_ref, kv_ref, wo_ref, out_ref, o_ref):
    x = x_ref[...]
    h = _rms(x, g_ref[...]).astype(BF16)
    q = (_dot(h, wq_ref[...]) * (MEM_HEAD_DIM ** -0.5)).astype(BF16)
    dh = MEM_HEAD_DIM
    for hd in range(MEM_HEADS):
        hs = slice(hd * dh, (hd + 1) * dh)
        s = _dot_nt(q[:, hs], kv_ref[:, hs])
        p = jnp.exp(s - jnp.max(s, axis=-1, keepdims=True))
        o = _dot(p.astype(BF16), kv_ref[:, D_MODEL + hd * dh:D_MODEL + (hd + 1) * dh])
        o_ref[:, hs] = (o / jnp.sum(p, axis=-1, keepdims=True)).astype(BF16)
    out_ref[...] = x + _dot(o_ref[...], wo_ref[...])


def _memattn_call(x2d, g, w_mq, kv, w_mo, seq, tm):
    m = x2d.shape[0]
    return pl.pallas_call(
        _memattn_kernel,
        grid=(m // tm,),
        in_specs=[
            pl.BlockSpec((tm, D_MODEL), lambda i: (i, 0)),
            _const_spec((1, D_MODEL)),
            _const_spec((D_MODEL, D_MODEL)),
            pl.BlockSpec((N_MEM, 2 * D_MODEL), lambda i: ((i * tm) // seq, 0)),
            _const_spec((D_MODEL, D_MODEL)),
        ],
        out_specs=pl.BlockSpec((tm, D_MODEL), lambda i: (i, 0)),
        out_shape=jax.ShapeDtypeStruct((m, D_MODEL), F32),
        scratch_shapes=[pltpu.VMEM((tm, D_MODEL), BF16)],
        compiler_params=_cparams(("arbitrary",)),
        name="memattn",
    )(x2d, g, w_mq, kv, w_mo)


def _ffn_kernel(x_ref, g_ref, wup_ref, wconv_ref, wdown_ref, gfin_ref, out_ref,
                act_ref, ext_ref, carry_ref, *, tm, tf, seq, final_norm):
    i = pl.program_id(0)
    x = x_ref[...]
    h = _rms(x, g_ref[...]).astype(BF16)
    seq_start = (i * tm) % seq == 0

    def conv(slot, half, cs):
        up = _dot(h, wup_ref[:, cs])
        ext_ref[half, pl.ds(CARRY_ROWS, tm), :] = up
        ext_ref[half, pl.ds(0, CARRY_ROWS), :] = jnp.where(seq_start, 0.0, carry_ref[slot])
        carry_ref[slot] = up[tm - CARRY_ROWS:, :]
        wc = wconv_ref[:, cs]
        y = wc[FFN_CONV - 1:FFN_CONV, :] * up
        for d in range(1, FFN_CONV):
            y = y + wc[FFN_CONV - 1 - d:FFN_CONV - d, :] * ext_ref[half, pl.ds(CARRY_ROWS - d, tm), :]
        return y

    nchunk = D_FF // tf
    for c in range(nchunk):
        a = conv(c, 0, slice(c * tf, (c + 1) * tf))
        b = conv(nchunk + c, 1, slice(D_FF + c * tf, D_FF + (c + 1) * tf))
        act_ref[:, c * tf:(c + 1) * tf] = (a * _sigmoid(a) * b).astype(BF16)
    y = x + _dot(act_ref[...], wdown_ref[...])
    if final_norm:
        y = _rms(y, gfin_ref[...])
    out_ref[...] = y


def _ffn_call(x2d, g, w_up, w_conv, w_down, g_final, seq, tm, tf, final_norm):
    m = x2d.shape[0]
    return pl.pallas_call(
        functools.partial(_ffn_kernel, tm=tm, tf=tf, seq=seq, final_norm=final_norm),
        grid=(m // tm,),
        in_specs=[
            pl.BlockSpec((tm, D_MODEL), lambda i: (i, 0)),
            _const_spec((1, D_MODEL)),
            _const_spec((D_MODEL, 2 * D_FF)),
            _const_spec((FFN_CONV, 2 * D_FF)),
            _const_spec((D_FF, D_MODEL)),
            _const_spec((1, D_MODEL)),
        ],
        out_specs=pl.BlockSpec((tm, D_MODEL), lambda i: (i, 0)),
        out_shape=jax.ShapeDtypeStruct((m, D_MODEL), F32),
        scratch_shapes=[
            pltpu.VMEM((tm, D_FF), BF16),
            pltpu.VMEM((2, tm + CARRY_ROWS, tf), F32),
            pltpu.VMEM((2 * (D_FF // tf), CARRY_ROWS, tf), F32),
        ],
        compiler_params=_cparams(("arbitrary",)),
        name="ffn",
    )(x2d, g, w_up, w_conv, w_down, g_final)


def _rearrange_w_in(w):
    bw = BRANCH_WIDTH
    a0 = 0
    b0 = 2 * bw
    c0 = b0 + 3 * bw + B_HEADS
    g0 = c0 + 3 * bw + 2 * C_HEADS + bw
    co = c0 + 3 * bw + 2 * C_HEADS
    main = jnp.concatenate([
        w[:, g0:g0 + 3 * D_MODEL],
        w[:, a0:a0 + 2 * bw],
        w[:, b0:b0 + 3 * bw],
        w[:, c0:c0 + 3 * bw],
        w[:, co:co + bw],
    ], axis=1).astype(BF16)
    small = jnp.concatenate([
        w[:, b0 + 3 * bw:b0 + 3 * bw + B_HEADS],
        w[:, c0 + 3 * bw:c0 + 3 * bw + 2 * C_HEADS],
    ], axis=1)
    small = jnp.pad(small, ((0, 0), (0, LANES - small.shape[1]))).astype(BF16)
    return main, small


def kernel(x, mem, g_mix, w_in, g_sgu, w_s, b_s, b_fox_f, w_conv_c, b_mlstm_i, b_mlstm_f, g_mh,
           w_branch, w_out, g_mem_q, g_mem_kv, w_mq, w_mkv, w_mo, g_ffn, w_up, w_ffn_conv, w_down,
           g_final):
    bsz, seq, _ = x.shape
    depth = w_in.shape[0]
    m = bsz * seq
    tm = min(512, seq)
    tm_proj = tm
    t_fox = min(256, seq)
    ts_mlstm = min(1024, seq)

    idx = jnp.arange(A_BLOCK)
    chunk_causal = (idx[None, :] // CHUNK) <= (idx[:, None] // CHUNK)

    x2d = x.reshape(m, D_MODEL)
    mem2d = mem.reshape(bsz * N_MEM, D_MODEL)
    row = lambda a: a.reshape(1, -1)

    for i in range(depth):
        w_main, w_small = _rearrange_w_in(w_in[i])
        gate_bias = jnp.pad(jnp.concatenate([b_fox_f[i], b_mlstm_i[i], b_mlstm_f[i]]),
                            (0, LANES - GATE_ROWS)).reshape(1, LANES)
        ws_masked = jnp.where(chunk_causal[None], w_s[i], 0).astype(BF16)
        bs_full = jnp.repeat(b_s[i].T, BRANCH_WIDTH // A_GROUPS, axis=1)

        p2d, gs = _proj_call(x2d, row(g_mix[i]), w_main, w_small, w_conv_c[i], seq, tm_proj)
        p3d = p2d.reshape(bsz, seq, PROJ_COLS)
        gc, gr, u, gsum = _gates_call(gs.reshape(bsz, seq, LANES), gate_bias)
        yb = _fox_call(p3d, gc, t_fox).reshape(m, BRANCH_WIDTH)
        gmh_rep = jnp.broadcast_to(g_mh[i].reshape(C_HEADS, C_HEAD_DIM, 1), (C_HEADS, C_HEAD_DIM, LANES))
        yc = _mlstm_call(p3d, u, gr, gsum, gmh_rep, ts_mlstm).reshape(m, BRANCH_WIDTH)
        x2d = _merge_call(p2d, yb, yc, x2d, row(g_sgu[i]), ws_masked, bs_full,
                          w_branch[i].astype(BF16), w_out[i].astype(BF16), tm)

        kv = _memkv_call(mem2d, row(g_mem_kv[i]), w_mkv[i].astype(BF16))
        x2d = _memattn_call(x2d, row(g_mem_q[i]), w_mq[i].astype(BF16), kv, w_mo[i].astype(BF16), seq, tm)

        x2d = _ffn_call(x2d, row(g_ffn[i]), w_up[i].astype(BF16), w_ffn_conv[i], w_down[i].astype(BF16),
                        row(g_final), seq, tm, 256, i == depth - 1)
    return x2d.reshape(bsz, seq, D_MODEL)
```

```python
import functools

import jax
import jax.numpy as jnp
from jax import lax
from jax.experimental import pallas as pl
from jax.experimental.pallas import tpu as pltpu

F32 = jnp.float32
BF16 = jnp.bfloat16

D_MODEL = 1024
EPS = 1e-6
LANES = 128
CARRY_ROWS = 8

BRANCH_WIDTH = 512
A_BLOCK = 128
A_GROUPS = 4
CHUNK = 64
B_HEADS = 8
B_HEAD_DIM = 64
C_HEADS = 4
C_HEAD_DIM = 128
C_CONV = 4
MLSTM_CHUNK = 128
N_MEM = 256
MEM_HEADS = 4
MEM_HEAD_DIM = 256
D_FF = 2816
FFN_CONV = 3

PROJ_TN = 512
COL_G = 0
COL_A = 3072
COL_BQ = 4096
COL_BK = 4608
COL_BV = 5120
COL_CQ = 5632
COL_CK = 6144
COL_CV = 6656
COL_CO = 7168
PROJ_COLS = 7680
LANE_BF = 0
LANE_CI = 8
LANE_CF = 12
GATE_ROWS = 16

LOG2E = 1.4426950408889634
FOX_Q_SCALE = B_HEAD_DIM ** -0.5 * LOG2E
NEG = -1e30
VMEM_LIMIT = 56 * 1024 * 1024


def _cparams(sem):
    return pltpu.CompilerParams(dimension_semantics=sem, vmem_limit_bytes=VMEM_LIMIT)


def _rms(xf, g):
    return xf * lax.rsqrt(jnp.mean(xf * xf, axis=-1, keepdims=True) + EPS) * g


def _sigmoid(x):
    return 0.5 * jnp.tanh(0.5 * x) + 0.5


def _gelu_tanh(x):
    return 0.5 * x * (1.0 + jnp.tanh(0.7978845608028654 * (x + 0.044715 * (x * x * x))))


def _dot(a, b):
    return jnp.dot(a, b, preferred_element_type=F32)


def _dot_nt(a, b):
    return lax.dot_general(a, b, (((1,), (1,)), ((), ())), preferred_element_type=F32)


def _const_spec(shape):
    nd = len(shape)
    return pl.BlockSpec(shape, lambda *_: (0,) * nd, pipeline_mode=pl.Buffered(1))


def _proj_kernel(x_ref, g_ref, w_ref, ws_ref, wc_ref, p_ref, gs_ref, ext_ref, carry_ref, *, tm, seq):
    i = pl.program_id(0)
    h = _rms(x_ref[...], g_ref[...]).astype(BF16)
    gs_ref[...] = _dot(h, ws_ref[...])
    seq_start = (i * tm) % seq == 0

    def conv_silu(acc, slot):
        ext_ref[slot, pl.ds(CARRY_ROWS, tm), :] = acc
        ext_ref[slot, pl.ds(0, CARRY_ROWS), :] = jnp.where(seq_start, 0.0, carry_ref[slot])
        carry_ref[slot] = ext_ref[slot, pl.ds(tm, CARRY_ROWS), :]
        wc = 0.5 * wc_ref[:, slot * PROJ_TN:(slot + 1) * PROJ_TN]
        half = wc[C_CONV - 1:C_CONV, :] * ext_ref[slot, pl.ds(CARRY_ROWS, tm), :]
        for d in range(1, C_CONV):
            half = half + wc[C_CONV - 1 - d:C_CONV - d, :] * ext_ref[slot, pl.ds(CARRY_ROWS - d, tm), :]
        return half * (jnp.tanh(half) + 1.0)

    order = [COL_CQ, COL_CK] + [c for c in range(0, PROJ_COLS, PROJ_TN) if c not in (COL_CQ, COL_CK)]
    for c0 in order:
        cs = slice(c0, c0 + PROJ_TN)
        acc = _dot(h, w_ref[:, cs])
        if c0 < COL_A or c0 == COL_CO:
            out = _sigmoid(acc)
        elif c0 < COL_BQ:
            out = _gelu_tanh(acc)
        elif c0 == COL_BQ:
            out = acc * FOX_Q_SCALE
        elif c0 == COL_CQ:
            out = conv_silu(acc, 0)
        elif c0 == COL_CK:
            out = conv_silu(acc, 1)
        else:
            out = acc
        p_ref[:, cs] = out.astype(BF16)


def _proj_call(x2d, g, w_main, w_small, w_conv, seq, tm):
    m = x2d.shape[0]
    return pl.pallas_call(
        functools.partial(_proj_kernel, tm=tm, seq=seq),
        grid=(m // tm,),
        in_specs=[
            pl.BlockSpec((tm, D_MODEL), lambda i: (i, 0)),
            _const_spec((1, D_MODEL)),
            _const_spec((D_MODEL, PROJ_COLS)),
            _const_spec((D_MODEL, LANES)),
            _const_spec((C_CONV, 2 * PROJ_TN)),
        ],
        out_specs=[
            pl.BlockSpec((tm, PROJ_COLS), lambda i: (i, 0)),
            pl.BlockSpec((tm, LANES), lambda i: (i, 0)),
        ],
        out_shape=[
            jax.ShapeDtypeStruct((m, PROJ_COLS), BF16),
            jax.ShapeDtypeStruct((m, LANES), F32),
        ],
        scratch_shapes=[
            pltpu.VMEM((2, tm + CARRY_ROWS, PROJ_TN), F32),
            pltpu.VMEM((2, CARRY_ROWS, PROJ_TN), F32),
        ],
        compiler_params=_cparams(("arbitrary",)),
        name="proj",
    )(x2d, g, w_main, w_small, w_conv)


def _gates_kernel(gs_ref, bias_ref, gc_ref, gr_ref, u_ref, gsum_ref, *, seq):
    blk = MLSTM_CHUNK
    row = lax.broadcasted_iota(jnp.int32, (blk, blk), 0)
    col = lax.broadcasted_iota(jnp.int32, (blk, blk), 1)
    tri = (col <= row).astype(F32)
    lane = lax.broadcasted_iota(jnp.int32, (1, LANES), 1)

    def body(r, carry):
        r0 = pl.multiple_of(r * blk, blk)
        raw = gs_ref[pl.ds(r0, blk), :] + bias_ref[...]
        logsig = jnp.minimum(raw, 0.0) - jnp.log1p(jnp.exp(-jnp.abs(raw)))
        local = jnp.dot(tri, logsig, precision=lax.Precision.HIGHEST, preferred_element_type=F32)
        glob = local + carry
        out = jnp.where(lane < LANE_CI, glob, jnp.where(lane < LANE_CF, raw, local))
        gc_ref[pl.ds(r0, blk), :] = out
        gr_ref[:, pl.ds(r0, blk)] = out.T[0:GATE_ROWS, :]
        g_rows, mloc_rows = [], []
        for h in range(C_HEADS):
            d = out[:, LANE_CI + h:LANE_CI + h + 1] - out[:, LANE_CF + h:LANE_CF + h + 1]
            u = jnp.broadcast_to(d, (blk, LANES))
            u_ref[h, pl.ds(r0, blk), :] = u
            g = jnp.broadcast_to(out[blk - 1:blk, LANE_CF + h:LANE_CF + h + 1], (1, LANES))
            g_rows.append(g)
            mloc_rows.append(g + jnp.max(u, axis=0, keepdims=True))
        gsum_ref[r] = jnp.concatenate(g_rows + mloc_rows, axis=0)
        return glob[blk - 1:blk, :]

    lax.fori_loop(0, seq // blk, body, jnp.zeros((1, LANES), F32), unroll=4)


def _gates_call(gs3d, bias):
    b, seq, _ = gs3d.shape
    nchunk = seq // MLSTM_CHUNK
    return pl.pallas_call(
        functools.partial(_gates_kernel, seq=seq),
        grid=(b,),
        in_specs=[
            pl.BlockSpec((None, seq, LANES), lambda bi: (bi, 0, 0)),
            pl.BlockSpec((1, LANES), lambda bi: (0, 0)),
        ],
        out_specs=[
            pl.BlockSpec((None, seq, LANES), lambda bi: (bi, 0, 0)),
            pl.BlockSpec((None, GATE_ROWS, seq), lambda bi: (bi, 0, 0)),
            pl.BlockSpec((None, C_HEADS, seq, LANES), lambda bi: (bi, 0, 0, 0)),
            pl.BlockSpec((None, nchunk, 2 * C_HEADS, LANES), lambda bi: (bi, 0, 0, 0)),
        ],
        out_shape=[
            jax.ShapeDtypeStruct((b, seq, LANES), F32),
            jax.ShapeDtypeStruct((b, GATE_ROWS, seq), F32),
            jax.ShapeDtypeStruct((b, C_HEADS, seq, LANES), F32),
            jax.ShapeDtypeStruct((b, nchunk, 2 * C_HEADS, LANES), F32),
        ],
        compiler_params=_cparams(("arbitrary",)),
        name="gates",
    )(gs3d, bias)


FOX_VROWS = 80
FOX_KMULT = 2
FOX_HEADS = 8


def _fox_kernel(q_ref, k_ref, v_ref, gc_ref, o_ref, kaug_ref, vt_ref, *, t, seq):
    hg = pl.program_id(1)
    qi = pl.program_id(2)
    hd = B_HEAD_DIM
    nh = FOX_HEADS
    lane = lax.broadcasted_iota(jnp.int32, (1, LANES), 1)
    own = (lane < hd, lane >= hd)
    aug0 = (hd, 0)

    @pl.when(qi == 0)
    def _():
        sub = lax.broadcasted_iota(jnp.int32, (FOX_VROWS - hd, seq), 0)
        tail = jnp.where(sub == 0, 1.0, 0.0).astype(BF16)
        for hh in range(nh):
            vt_ref[hh, hd:FOX_VROWS, :] = tail

        def body(r, _):
            r0 = pl.multiple_of(r * LANES, LANES)
            g = gc_ref[pl.ds(r0, LANES), :]
            for pp in range(nh // 2):
                ls = slice(pp * LANES, (pp + 1) * LANES)
                kb = k_ref[pl.ds(r0, LANES), ls].astype(F32)
                vt = v_ref[pl.ds(r0, LANES), ls].astype(F32).T
                for h in range(2):
                    hh = 2 * pp + h
                    neg = -LOG2E * jnp.sum(jnp.where(lane == LANE_BF + nh * hg + hh, g, 0.0), axis=-1,
                                           keepdims=True)
                    hi = neg.astype(BF16).astype(F32)
                    mid = (neg - hi).astype(BF16).astype(F32)
                    lo = (neg - hi) - mid
                    a = aug0[h]
                    extra = jnp.where(lane == a, hi, jnp.where(lane == a + 1, mid, jnp.where(lane == a + 2, lo, 0.0)))
                    kaug_ref[hh, pl.ds(r0, LANES), :] = jnp.where(own[h], kb, extra).astype(BF16)
                    vt_ref[hh, 0:hd, pl.ds(r0, LANES)] = vt[h * hd:(h + 1) * hd, :].astype(BF16)
            return 0

        lax.fori_loop(0, seq // LANES, body, 0)

    qa = []
    for pp in range(nh // 2):
        q = q_ref[:, pp * LANES:(pp + 1) * LANES].astype(F32)
        for h in range(2):
            a = aug0[h]
            ones3 = jnp.where((lane >= a) & (lane < a + 3), 1.0, 0.0)
            qa.append(jnp.where(own[h], q, ones3).astype(BF16))
    row = lax.broadcasted_iota(jnp.int32, (t, t), 0)
    col = lax.broadcasted_iota(jnp.int32, (t, t), 1)
    causal = row <= col

    def qk(hh, k0, tk):
        return _dot_nt(kaug_ref[hh, pl.ds(k0, tk), :], qa[hh])

    def update(k0, tk, state, masked):
        k0 = pl.multiple_of(k0, t)
        sts = [qk(hh, k0, tk) for hh in range(nh)]
        ms, ps, alphas = [], [], []
        for hh in range(nh):
            m = state[hh][0]
            s = jnp.where(causal, sts[hh], NEG) if masked else sts[hh]
            m_new = jnp.maximum(m, jnp.max(s, axis=0, keepdims=True))
            ps.append(jnp.exp2((s - m_new).astype(BF16)))
            alphas.append(jnp.exp2(m - m_new))
            ms.append(m_new)
        pvs = [_dot(vt_ref[hh, :, pl.ds(k0, tk)], ps[hh]) for hh in range(nh)]
        return tuple((ms[hh], alphas[hh] * state[hh][1] + pvs[hh]) for hh in range(nh))

    km = FOX_KMULT
    init = tuple((jnp.full((1, t), NEG, F32), jnp.zeros((FOX_VROWS, t), F32)) for _ in range(nh))
    state = lax.fori_loop(0, qi // km, lambda j, state: update(j * km * t, km * t, state, False), init)
    state = lax.fori_loop((qi // km) * km, qi, lambda j, state: update(j * t, t, state, False), state)
    state = update(qi * t, t, state, True)
    ot = jnp.concatenate([acc[0:hd] / acc[hd:hd + 1] for _, acc in state], axis=0)
    o_ref[...] = ot.T.astype(BF16)


def _fox_call(p3d, gc, t):
    b, seq, _ = p3d.shape
    w = FOX_HEADS * B_HEAD_DIM
    return pl.pallas_call(
        functools.partial(_fox_kernel, t=t, seq=seq),
        grid=(b, B_HEADS // FOX_HEADS, seq // t),
        in_specs=[
            pl.BlockSpec((None, t, w), lambda bi, hg, qi: (bi, qi, COL_BQ // w + hg)),
            pl.BlockSpec((None, seq, w), lambda bi, hg, qi: (bi, 0, COL_BK // w + hg)),
            pl.BlockSpec((None, seq, w), lambda bi, hg, qi: (bi, 0, COL_BV // w + hg)),
            pl.BlockSpec((None, seq, LANES), lambda bi, hg, qi: (bi, 0, 0)),
        ],
        out_specs=pl.BlockSpec((None, t, w), lambda bi, hg, qi: (bi, qi, hg)),
        out_shape=jax.ShapeDtypeStruct((b, seq, BRANCH_WIDTH), BF16),
        scratch_shapes=[
            pltpu.VMEM((FOX_HEADS, seq, LANES), BF16),
            pltpu.VMEM((FOX_HEADS, FOX_VROWS, seq), BF16),
        ],
        compiler_params=_cparams(("arbitrary", "arbitrary", "arbitrary")),
        name="fox",
    )(p3d, p3d, p3d, gc)


MLSTM_ROWS = 144


def _mlstm_kernel(q_ref, k_ref, v_ref, o_ref, u_ref, gr_ref, gsum_ref, gmh_ref, y_ref, c_ref, m_ref,
                  *, ts):
    L = MLSTM_CHUNK
    dh = C_HEAD_DIM
    scale = dh ** -0.5
    si = pl.program_id(1)

    @pl.when(si == 0)
    def _():
        c_ref[...] = jnp.zeros_like(c_ref)
        m_ref[...] = jnp.zeros_like(m_ref)

    row = lax.broadcasted_iota(jnp.int32, (L, L), 0)
    col = lax.broadcasted_iota(jnp.int32, (L, L), 1)
    causal = row <= col
    sub = lax.broadcasted_iota(jnp.int32, (MLSTM_ROWS - dh, L), 0)
    tail = jnp.where(sub == 0, 1.0, 0.0)

    def chunk(c, _):
        r0 = pl.multiple_of(c * L, L)
        gsum = gsum_ref[c]
        for h in range(C_HEADS):
            hs = slice(h * dh, (h + 1) * dh)
            q = q_ref[pl.ds(r0, L), hs]
            k = k_ref[pl.ds(r0, L), hs]
            vt = jnp.concatenate([v_ref[pl.ds(r0, L), hs].astype(F32).T, tail], axis=0)
            ig = gr_ref[pl.ds(LANE_CI + h, 1), pl.ds(r0, L)]
            b = gr_ref[pl.ds(LANE_CF + h, 1), pl.ds(r0, L)]
            g = gsum[h:h + 1, :]
            m_loc = gsum[C_HEADS + h:C_HEADS + h + 1, :]
            m_in = m_ref[h]
            c_in = c_ref[h]

            dlog = jnp.where(causal, u_ref[h, pl.ds(r0, L), :] + b, NEG)
            inter = b + m_in
            m_t = jnp.maximum(jnp.max(dlog, axis=0, keepdims=True), inter)
            sm = (_dot_nt(k, q) * scale) * jnp.exp(dlog - m_t)
            w_int = jnp.exp(inter - m_t)
            ext = _dot(vt.astype(BF16), sm.astype(BF16)) + w_int * _dot_nt(c_in.astype(BF16), q)
            hh = ext[0:dh] / jnp.maximum(jnp.abs(ext[dh:dh + 1]), jnp.exp(-m_t))
            hn = hh * lax.rsqrt(jnp.mean(hh * hh, axis=0, keepdims=True) + EPS) * gmh_ref[h]
            y_ref[pl.ds(r0, L), hs] = (hn.T * o_ref[pl.ds(r0, L), hs].astype(F32)).astype(BF16)

            m_new = jnp.maximum(g + m_in, m_loc)
            w = jnp.exp(g + (ig - b) - m_new) * scale
            c_ref[h] = jnp.exp(g + m_in - m_new) * c_in + _dot((vt * w).astype(BF16), k)
            m_ref[h] = m_new
        return 0

    lax.fori_loop(0, ts // L, chunk, 0, unroll=4)


def _mlstm_call(p3d, u, gr, gsum, gmh_rep, ts):
    b, seq, _ = p3d.shape
    w = BRANCH_WIDTH
    nc = ts // MLSTM_CHUNK

    def pspec(col):
        return pl.BlockSpec((None, ts, w), lambda bi, si: (bi, si, col // w))

    return pl.pallas_call(
        functools.partial(_mlstm_kernel, ts=ts),
        grid=(b, seq // ts),
        in_specs=[
            pspec(COL_CQ), pspec(COL_CK), pspec(COL_CV), pspec(COL_CO),
            pl.BlockSpec((None, C_HEADS, ts, LANES), lambda bi, si: (bi, 0, si, 0)),
            pl.BlockSpec((None, GATE_ROWS, ts), lambda bi, si: (bi, 0, si)),
            pl.BlockSpec((None, nc, 2 * C_HEADS, LANES), lambda bi, si: (bi, si, 0, 0)),
            _const_spec((C_HEADS, C_HEAD_DIM, LANES)),
        ],
        out_specs=pl.BlockSpec((None, ts, w), lambda bi, si: (bi, si, 0)),
        out_shape=jax.ShapeDtypeStruct((b, seq, w), BF16),
        scratch_shapes=[
            pltpu.VMEM((C_HEADS, MLSTM_ROWS, C_HEAD_DIM), F32),
            pltpu.VMEM((C_HEADS, 1, LANES), F32),
        ],
        compiler_params=_cparams(("arbitrary", "arbitrary")),
        name="mlstm",
    )(p3d, p3d, p3d, p3d, u, gr, gsum, gmh_rep)


def _merge_kernel(gates_ref, uv_ref, yb_ref, yc_ref, x_ref, gsgu_ref, ws_ref, bs_ref, wb_ref, wo_ref,
                  gq_ref, wq_ref, kv_ref, wmo_ref, out_ref, ya_ref, o_ref, *, tm):
    w = BRANCH_WIDTH
    u = uv_ref[:, :w].astype(F32)
    v = uv_ref[:, w:].astype(F32)
    vn = _rms(v, gsgu_ref[...]).astype(BF16)
    gd = w // A_GROUPS
    for nb in range(tm // A_BLOCK):
        rs = slice(nb * A_BLOCK, (nb + 1) * A_BLOCK)
        mixed = jnp.concatenate(
            [_dot(ws_ref[g], vn[rs, g * gd:(g + 1) * gd]) for g in range(A_GROUPS)], axis=1)
        ya_ref[rs, :] = (u[rs, :] * (mixed + bs_ref[...])).astype(BF16)
    merged = gates_ref[:, 0:D_MODEL].astype(F32) * _dot(ya_ref[...], wb_ref[0])
    merged += gates_ref[:, D_MODEL:2 * D_MODEL].astype(F32) * _dot(yb_ref[...], wb_ref[1])
    merged += gates_ref[:, 2 * D_MODEL:3 * D_MODEL].astype(F32) * _dot(yc_ref[...], wb_ref[2])
    x = x_ref[...] + _dot(merged.astype(BF16), wo_ref[...])

    h = _rms(x, gq_ref[...]).astype(BF16)
    q = (_dot(h, wq_ref[...]) * (MEM_HEAD_DIM ** -0.5)).astype(BF16)
    dh = MEM_HEAD_DIM
    for hd in range(MEM_HEADS):
        hs = slice(hd * dh, (hd + 1) * dh)
        s = _dot_nt(q[:, hs], kv_ref[:, hs])
        p = jnp.exp(s - jnp.max(s, axis=-1, keepdims=True))
        o = _dot(p.astype(BF16), kv_ref[:, D_MODEL + hd * dh:D_MODEL + (hd + 1) * dh])
        o_ref[:, hs] = (o / jnp.sum(p, axis=-1, keepdims=True)).astype(BF16)
    out_ref[...] = x + _dot(o_ref[...], wmo_ref[...])


def _merge_call(p2d, yb, yc, x2d, g_sgu, ws_masked, bs_full, w_branch, w_out, g_mq, w_mq, kv, w_mo,
                seq, tm):
    m = x2d.shape[0]
    w = BRANCH_WIDTH
    return pl.pallas_call(
        functools.partial(_merge_kernel, tm=tm),
        grid=(m // tm,),
        in_specs=[
            pl.BlockSpec((tm, 3 * D_MODEL), lambda i: (i, COL_G // (3 * D_MODEL))),
            pl.BlockSpec((tm, 2 * w), lambda i: (i, COL_A // (2 * w))),
            pl.BlockSpec((tm, w), lambda i: (i, 0)),
            pl.BlockSpec((tm, w), lambda i: (i, 0)),
            pl.BlockSpec((tm, D_MODEL), lambda i: (i, 0)),
            _const_spec((1, w)),
            _const_spec((A_GROUPS, A_BLOCK, A_BLOCK)),
            _const_spec((A_BLOCK, w)),
            _const_spec((3, w, D_MODEL)),
            _const_spec((D_MODEL, D_MODEL)),
            _const_spec((1, D_MODEL)),
            _const_spec((D_MODEL, D_MODEL)),
            pl.BlockSpec((N_MEM, 2 * D_MODEL), lambda i: ((i * tm) // seq, 0)),
            _const_spec((D_MODEL, D_MODEL)),
        ],
        out_specs=pl.BlockSpec((tm, D_MODEL), lambda i: (i, 0)),
        out_shape=jax.ShapeDtypeStruct((m, D_MODEL), F32),
        scratch_shapes=[pltpu.VMEM((tm, w), BF16), pltpu.VMEM((tm, D_MODEL), BF16)],
        compiler_params=_cparams(("arbitrary",)),
        name="merge",
    )(p2d, p2d, yb, yc, x2d, g_sgu, ws_masked, bs_full, w_branch, w_out, g_mq, w_mq, kv, w_mo)


def _memkv_kernel(mem_ref, g_ref, w_ref, kv_ref):
    kv_ref[...] = _dot(_rms(mem_ref[...], g_ref[...]).astype(BF16), w_ref[...]).astype(BF16)


def _memkv_call(mem2d, g, w_mkv):
    m = mem2d.shape[0]
    return pl.pallas_call(
        _memkv_kernel,
        grid=(m // N_MEM,),
        in_specs=[
            pl.BlockSpec((N_MEM, D_MODEL), lambda i: (i, 0)),
            _const_spec((1, D_MODEL)),
            _const_spec((D_MODEL, 2 * D_MODEL)),
        ],
        out_specs=pl.BlockSpec((N_MEM, 2 * D_MODEL), lambda i: (i, 0)),
        out_shape=jax.ShapeDtypeStruct((m, 2 * D_MODEL), BF16),
        compiler_params=_cparams(("arbitrary",)),
        name="memkv",
    )(mem2d, g, w_mkv)


def _ffn_kernel(x_ref, g_ref, wup_ref, wconv_ref, wdown_ref, gfin_ref, out_ref,
                act_ref, ext_ref, carry_ref, *, tm, tf, seq, final_norm):
    i = pl.program_id(0)
    x = x_ref[...]
    h = _rms(x, g_ref[...]).astype(BF16)
    seq_start = (i * tm) % seq == 0

    def conv(slot, half, cs):
        up = _dot(h, wup_ref[:, cs])
        ext_ref[half, pl.ds(CARRY_ROWS, tm), :] = up
        ext_ref[half, pl.ds(0, CARRY_ROWS), :] = jnp.where(seq_start, 0.0, carry_ref[slot])
        carry_ref[slot] = up[tm - CARRY_ROWS:, :]
        wc = wconv_ref[:, cs]
        y = wc[FFN_CONV - 1:FFN_CONV, :] * up
        for d in range(1, FFN_CONV):
            y = y + wc[FFN_CONV - 1 - d:FFN_CONV - d, :] * ext_ref[half, pl.ds(CARRY_ROWS - d, tm), :]
        return y

    nchunk = D_FF // tf
    for c in range(nchunk):
        a = conv(c, 0, slice(c * tf, (c + 1) * tf))
        b = conv(nchunk + c, 1, slice(D_FF + c * tf, D_FF + (c + 1) * tf))
        act_ref[:, c * tf:(c + 1) * tf] = (a * _sigmoid(a) * b).astype(BF16)
    y = x + _dot(act_ref[...], wdown_ref[...])
    if final_norm:
        y = _rms(y, gfin_ref[...])
    out_ref[...] = y


def _ffn_call(x2d, g, w_up, w_conv, w_down, g_final, seq, tm, tf, final_norm):
    m = x2d.shape[0]
    return pl.pallas_call(
        functools.partial(_ffn_kernel, tm=tm, tf=tf, seq=seq, final_norm=final_norm),
        grid=(m // tm,),
        in_specs=[
            pl.BlockSpec((tm, D_MODEL), lambda i: (i, 0)),
            _const_spec((1, D_MODEL)),
            _const_spec((D_MODEL, 2 * D_FF)),
            _const_spec((FFN_CONV, 2 * D_FF)),
            _const_spec((D_FF, D_MODEL)),
            _const_spec((1, D_MODEL)),
        ],
        out_specs=pl.BlockSpec((tm, D_MODEL), lambda i: (i, 0)),
        out_shape=jax.ShapeDtypeStruct((m, D_MODEL), F32),
        scratch_shapes=[
            pltpu.VMEM((tm, D_FF), BF16),
            pltpu.VMEM((2, tm + CARRY_ROWS, tf), F32),
            pltpu.VMEM((2 * (D_FF // tf), CARRY_ROWS, tf), F32),
        ],
        compiler_params=_cparams(("arbitrary",)),
        name="ffn",
    )(x2d, g, w_up, w_conv, w_down, g_final)


def _rearrange_w_in(w):
    bw = BRANCH_WIDTH
    a0 = 0
    b0 = 2 * bw
    c0 = b0 + 3 * bw + B_HEADS
    g0 = c0 + 3 * bw + 2 * C_HEADS + bw
    co = c0 + 3 * bw + 2 * C_HEADS
    main = jnp.concatenate([
        w[:, g0:g0 + 3 * D_MODEL],
        w[:, a0:a0 + 2 * bw],
        w[:, b0:b0 + 3 * bw],
        w[:, c0:c0 + 3 * bw],
        w[:, co:co + bw],
    ], axis=1).astype(BF16)
    small = jnp.concatenate([
        w[:, b0 + 3 * bw:b0 + 3 * bw + B_HEADS],
        w[:, c0 + 3 * bw:c0 + 3 * bw + 2 * C_HEADS],
    ], axis=1)
    small = jnp.pad(small, ((0, 0), (0, LANES - small.shape[1]))).astype(BF16)
    return main, small


def kernel(x, mem, g_mix, w_in, g_sgu, w_s, b_s, b_fox_f, w_conv_c, b_mlstm_i, b_mlstm_f, g_mh,
           w_branch, w_out, g_mem_q, g_mem_kv, w_mq, w_mkv, w_mo, g_ffn, w_up, w_ffn_conv, w_down,
           g_final):
    bsz, seq, _ = x.shape
    depth = w_in.shape[0]
    m = bsz * seq
    tm = min(512, seq)
    tm_proj = tm
    t_fox = min(256, seq)
    ts_mlstm = min(1024, seq)

    idx = jnp.arange(A_BLOCK)
    chunk_causal = (idx[None, :] // CHUNK) <= (idx[:, None] // CHUNK)

    x2d = x.reshape(m, D_MODEL)
    mem2d = mem.reshape(bsz * N_MEM, D_MODEL)
    row = lambda a: a.reshape(1, -1)

    for i in range(depth):
        w_main, w_small = _rearrange_w_in(w_in[i])
        gate_bias = jnp.pad(jnp.concatenate([b_fox_f[i], b_mlstm_i[i], b_mlstm_f[i]]),
                            (0, LANES - GATE_ROWS)).reshape(1, LANES)
        ws_masked = jnp.where(chunk_causal[None], w_s[i], 0).astype(BF16)
        bs_full = jnp.repeat(b_s[i].T, BRANCH_WIDTH // A_GROUPS, axis=1)

        p2d, gs = _proj_call(x2d, row(g_mix[i]), w_main, w_small, w_conv_c[i], seq, tm_proj)
        p3d = p2d.reshape(bsz, seq, PROJ_COLS)
        gc, gr, u, gsum = _gates_call(gs.reshape(bsz, seq, LANES), gate_bias)
        yb = _fox_call(p3d, gc, t_fox).reshape(m, BRANCH_WIDTH)
        gmh_rep = jnp.broadcast_to(g_mh[i].reshape(C_HEADS, C_HEAD_DIM, 1), (C_HEADS, C_HEAD_DIM, LANES))
        yc = _mlstm_call(p3d, u, gr, gsum, gmh_rep, ts_mlstm).reshape(m, BRANCH_WIDTH)
        kv = _memkv_call(mem2d, row(g_mem_kv[i]), w_mkv[i].astype(BF16))
        x2d = _merge_call(p2d, yb, yc, x2d, row(g_sgu[i]), ws_masked, bs_full,
                          w_branch[i].astype(BF16), w_out[i].astype(BF16),
                          row(g_mem_q[i]), w_mq[i].astype(BF16), kv, w_mo[i].astype(BF16), seq, tm)

        x2d = _ffn_call(x2d, row(g_ffn[i]), w_up[i].astype(BF16), w_ffn_conv[i], w_down[i].astype(BF16),
                        row(g_final), seq, tm, 256, i == depth - 1)
    return x2d.reshape(bsz, seq, D_MODEL)
```

```python
import functools

import jax
import jax.numpy as jnp
from jax import lax
from jax.experimental import pallas as pl
from jax.experimental.pallas import tpu as pltpu

F32 = jnp.float32
BF16 = jnp.bfloat16

D_MODEL = 1024
EPS = 1e-6
LANES = 128
CARRY_ROWS = 8

BRANCH_WIDTH = 512
A_BLOCK = 128
A_GROUPS = 4
CHUNK = 64
B_HEADS = 8
B_HEAD_DIM = 64
C_HEADS = 4
C_HEAD_DIM = 128
C_CONV = 4
MLSTM_CHUNK = 128
N_MEM = 256
MEM_HEADS = 4
MEM_HEAD_DIM = 256
D_FF = 2816
FFN_CONV = 3

PROJ_TN = 512
COL_G = 0
COL_A = 3072
COL_BQ = 4096
COL_BK = 4608
COL_BV = 5120
COL_CQ = 5632
COL_CK = 6144
COL_CV = 6656
COL_CO = 7168
PROJ_COLS = 7680
LANE_BF = 0
LANE_CI = 8
LANE_CF = 12
GATE_ROWS = 16

LOG2E = 1.4426950408889634
FOX_Q_SCALE = B_HEAD_DIM ** -0.5 * LOG2E
NEG = -1e30
VMEM_LIMIT = 56 * 1024 * 1024


def _cparams(sem):
    return pltpu.CompilerParams(dimension_semantics=sem, vmem_limit_bytes=VMEM_LIMIT)


def _rms(xf, g):
    return xf * lax.rsqrt(jnp.mean(xf * xf, axis=-1, keepdims=True) + EPS) * g


def _sigmoid(x):
    return 0.5 * jnp.tanh(0.5 * x) + 0.5


def _gelu_tanh(x):
    return 0.5 * x * (1.0 + jnp.tanh(0.7978845608028654 * (x + 0.044715 * (x * x * x))))


def _dot(a, b):
    return jnp.dot(a, b, preferred_element_type=F32)


def _dot_nt(a, b):
    return lax.dot_general(a, b, (((1,), (1,)), ((), ())), preferred_element_type=F32)


def _const_spec(shape, layer=None):
    nd = len(shape)
    if layer is None:
        return pl.BlockSpec(shape, lambda *_: (0,) * nd, pipeline_mode=pl.Buffered(1))
    return pl.BlockSpec((None,) + tuple(shape), lambda *_: (layer,) + (0,) * nd,
                        pipeline_mode=pl.Buffered(1))


def _proj_kernel(x_ref, g_ref, w_ref, ws_ref, wc_ref, p_ref, gs_ref, ext_ref, carry_ref, *, tm, seq):
    i = pl.program_id(0)
    h = _rms(x_ref[...], g_ref[...]).astype(BF16)
    gs_ref[...] = _dot(h, ws_ref[...])
    seq_start = (i * tm) % seq == 0

    def conv_silu(acc, slot):
        ext_ref[slot, pl.ds(CARRY_ROWS, tm), :] = acc
        ext_ref[slot, pl.ds(0, CARRY_ROWS), :] = jnp.where(seq_start, 0.0, carry_ref[slot])
        carry_ref[slot] = ext_ref[slot, pl.ds(tm, CARRY_ROWS), :]
        wc = 0.5 * wc_ref[:, slot * PROJ_TN:(slot + 1) * PROJ_TN]
        half = wc[C_CONV - 1:C_CONV, :] * ext_ref[slot, pl.ds(CARRY_ROWS, tm), :]
        for d in range(1, C_CONV):
            half = half + wc[C_CONV - 1 - d:C_CONV - d, :] * ext_ref[slot, pl.ds(CARRY_ROWS - d, tm), :]
        return half * (jnp.tanh(half) + 1.0)

    order = [COL_CQ, COL_CK] + [c for c in range(0, PROJ_COLS, PROJ_TN) if c not in (COL_CQ, COL_CK)]
    for c0 in order:
        cs = slice(c0, c0 + PROJ_TN)
        acc = _dot(h, w_ref[:, cs])
        if c0 < COL_A or c0 == COL_CO:
            out = _sigmoid(acc)
        elif c0 < COL_BQ:
            out = _gelu_tanh(acc)
        elif c0 == COL_BQ:
            out = acc * FOX_Q_SCALE
        elif c0 == COL_CQ:
            out = conv_silu(acc, 0)
        elif c0 == COL_CK:
            out = conv_silu(acc, 1)
        else:
            out = acc
        p_ref[:, cs] = out.astype(BF16)


def _proj_call(x2d, g, w_main, w_small, w_conv, layer, seq, tm):
    m = x2d.shape[0]
    return pl.pallas_call(
        functools.partial(_proj_kernel, tm=tm, seq=seq),
        grid=(m // tm,),
        in_specs=[
            pl.BlockSpec((tm, D_MODEL), lambda i: (i, 0)),
            _const_spec((1, D_MODEL), layer),
            _const_spec((D_MODEL, PROJ_COLS), layer),
            _const_spec((D_MODEL, LANES), layer),
            _const_spec((C_CONV, 2 * PROJ_TN), layer),
        ],
        out_specs=[
            pl.BlockSpec((tm, PROJ_COLS), lambda i: (i, 0)),
            pl.BlockSpec((tm, LANES), lambda i: (i, 0)),
        ],
        out_shape=[
            jax.ShapeDtypeStruct((m, PROJ_COLS), BF16),
            jax.ShapeDtypeStruct((m, LANES), F32),
        ],
        scratch_shapes=[
            pltpu.VMEM((2, tm + CARRY_ROWS, PROJ_TN), F32),
            pltpu.VMEM((2, CARRY_ROWS, PROJ_TN), F32),
        ],
        compiler_params=_cparams(("arbitrary",)),
        name="proj",
    )(x2d, g, w_main, w_small, w_conv)


def _gates_kernel(gs_ref, bias_ref, gc_ref, gr_ref, u_ref, gsum_ref, *, seq):
    blk = MLSTM_CHUNK
    row = lax.broadcasted_iota(jnp.int32, (blk, blk), 0)
    col = lax.broadcasted_iota(jnp.int32, (blk, blk), 1)
    tri = (col <= row).astype(F32)
    lane = lax.broadcasted_iota(jnp.int32, (1, LANES), 1)

    def body(r, carry):
        r0 = pl.multiple_of(r * blk, blk)
        raw = gs_ref[pl.ds(r0, blk), :] + bias_ref[...]
        logsig = jnp.minimum(raw, 0.0) - jnp.log1p(jnp.exp(-jnp.abs(raw)))
        local = jnp.dot(tri, logsig, precision=lax.Precision.HIGHEST, preferred_element_type=F32)
        glob = local + carry
        out = jnp.where(lane < LANE_CI, glob, jnp.where(lane < LANE_CF, raw, local))
        gc_ref[pl.ds(r0, blk), :] = out
        gr_ref[:, pl.ds(r0, blk)] = out.T[0:GATE_ROWS, :]
        g_rows, mloc_rows = [], []
        for h in range(C_HEADS):
            d = out[:, LANE_CI + h:LANE_CI + h + 1] - out[:, LANE_CF + h:LANE_CF + h + 1]
            u = jnp.broadcast_to(d, (blk, LANES))
            u_ref[h, pl.ds(r0, blk), :] = u
            g = jnp.broadcast_to(out[blk - 1:blk, LANE_CF + h:LANE_CF + h + 1], (1, LANES))
            g_rows.append(g)
            mloc_rows.append(g + jnp.max(u, axis=0, keepdims=True))
        gsum_ref[r] = jnp.concatenate(g_rows + mloc_rows, axis=0)
        return glob[blk - 1:blk, :]

    lax.fori_loop(0, seq // blk, body, jnp.zeros((1, LANES), F32), unroll=4)


def _gates_call(gs3d, bias, layer):
    b, seq, _ = gs3d.shape
    nchunk = seq // MLSTM_CHUNK
    return pl.pallas_call(
        functools.partial(_gates_kernel, seq=seq),
        grid=(b,),
        in_specs=[
            pl.BlockSpec((None, seq, LANES), lambda bi: (bi, 0, 0)),
            _const_spec((1, LANES), layer),
        ],
        out_specs=[
            pl.BlockSpec((None, seq, LANES), lambda bi: (bi, 0, 0)),
            pl.BlockSpec((None, GATE_ROWS, seq), lambda bi: (bi, 0, 0)),
            pl.BlockSpec((None, C_HEADS, seq, LANES), lambda bi: (bi, 0, 0, 0)),
            pl.BlockSpec((None, nchunk, 2 * C_HEADS, LANES), lambda bi: (bi, 0, 0, 0)),
        ],
        out_shape=[
            jax.ShapeDtypeStruct((b, seq, LANES), F32),
            jax.ShapeDtypeStruct((b, GATE_ROWS, seq), F32),
            jax.ShapeDtypeStruct((b, C_HEADS, seq, LANES), F32),
            jax.ShapeDtypeStruct((b, nchunk, 2 * C_HEADS, LANES), F32),
        ],
        compiler_params=_cparams(("arbitrary",)),
        name="gates",
    )(gs3d, bias)


FOX_VROWS = 80
FOX_KMULT = 2
FOX_HEADS = 8


def _fox_kernel(q_ref, k_ref, v_ref, gc_ref, o_ref, kaug_ref, vt_ref, *, t, seq):
    hg = pl.program_id(1)
    qi = pl.program_id(2)
    hd = B_HEAD_DIM
    nh = FOX_HEADS
    lane = lax.broadcasted_iota(jnp.int32, (1, LANES), 1)
    own = (lane < hd, lane >= hd)
    aug0 = (hd, 0)

    @pl.when(qi == 0)
    def _():
        sub = lax.broadcasted_iota(jnp.int32, (FOX_VROWS - hd, seq), 0)
        tail = jnp.where(sub == 0, 1.0, 0.0).astype(BF16)
        for hh in range(nh):
            vt_ref[hh, hd:FOX_VROWS, :] = tail

        def body(r, _):
            r0 = pl.multiple_of(r * LANES, LANES)
            g = gc_ref[pl.ds(r0, LANES), :]
            for pp in range(nh // 2):
                ls = slice(pp * LANES, (pp + 1) * LANES)
                kb = k_ref[pl.ds(r0, LANES), ls].astype(F32)
                vt = v_ref[pl.ds(r0, LANES), ls].astype(F32).T
                for h in range(2):
                    hh = 2 * pp + h
                    neg = -LOG2E * jnp.sum(jnp.where(lane == LANE_BF + nh * hg + hh, g, 0.0), axis=-1,
                                           keepdims=True)
                    hi = neg.astype(BF16).astype(F32)
                    mid = (neg - hi).astype(BF16).astype(F32)
                    lo = (neg - hi) - mid
                    a = aug0[h]
                    extra = jnp.where(lane == a, hi, jnp.where(lane == a + 1, mid, jnp.where(lane == a + 2, lo, 0.0)))
                    kaug_ref[hh, pl.ds(r0, LANES), :] = jnp.where(own[h], kb, extra).astype(BF16)
                    vt_ref[hh, 0:hd, pl.ds(r0, LANES)] = vt[h * hd:(h + 1) * hd, :].astype(BF16)
            return 0

        lax.fori_loop(0, seq // LANES, body, 0)

    qa = []
    for pp in range(nh // 2):
        q = q_ref[:, pp * LANES:(pp + 1) * LANES].astype(F32)
        for h in range(2):
            a = aug0[h]
            ones3 = jnp.where((lane >= a) & (lane < a + 3), 1.0, 0.0)
            qa.append(jnp.where(own[h], q, ones3).astype(BF16))
    row = lax.broadcasted_iota(jnp.int32, (t, t), 0)
    col = lax.broadcasted_iota(jnp.int32, (t, t), 1)
    causal = row <= col

    def qk(hh, k0, tk):
        return _dot_nt(kaug_ref[hh, pl.ds(k0, tk), :], qa[hh])

    def update(k0, tk, state, masked):
        k0 = pl.multiple_of(k0, t)
        sts = [qk(hh, k0, tk) for hh in range(nh)]
        ms, ps, alphas = [], [], []
        for hh in range(nh):
            m = state[hh][0]
            s = jnp.where(causal, sts[hh], NEG) if masked else sts[hh]
            m_new = jnp.maximum(m, jnp.max(s, axis=0, keepdims=True))
            ps.append(jnp.exp2((s - m_new).astype(BF16)))
            alphas.append(jnp.exp2(m - m_new))
            ms.append(m_new)
        pvs = [_dot(vt_ref[hh, :, pl.ds(k0, tk)], ps[hh]) for hh in range(nh)]
        return tuple((ms[hh], alphas[hh] * state[hh][1] + pvs[hh]) for hh in range(nh))

    km = FOX_KMULT
    init = tuple((jnp.full((1, t), NEG, F32), jnp.zeros((FOX_VROWS, t), F32)) for _ in range(nh))
    state = lax.fori_loop(0, qi // km, lambda j, state: update(j * km * t, km * t, state, False), init)
    state = lax.fori_loop((qi // km) * km, qi, lambda j, state: update(j * t, t, state, False), state)
    state = update(qi * t, t, state, True)
    ot = jnp.concatenate([acc[0:hd] / acc[hd:hd + 1] for _, acc in state], axis=0)
    o_ref[...] = ot.T.astype(BF16)


def _fox_call(p3d, gc, t):
    b, seq, _ = p3d.shape
    w = FOX_HEADS * B_HEAD_DIM
    return pl.pallas_call(
        functools.partial(_fox_kernel, t=t, seq=seq),
        grid=(b, B_HEADS // FOX_HEADS, seq // t),
        in_specs=[
            pl.BlockSpec((None, t, w), lambda bi, hg, qi: (bi, qi, COL_BQ // w + hg)),
            pl.BlockSpec((None, seq, w), lambda bi, hg, qi: (bi, 0, COL_BK // w + hg)),
            pl.BlockSpec((None, seq, w), lambda bi, hg, qi: (bi, 0, COL_BV // w + hg)),
            pl.BlockSpec((None, seq, LANES), lambda bi, hg, qi: (bi, 0, 0)),
        ],
        out_specs=pl.BlockSpec((None, t, w), lambda bi, hg, qi: (bi, qi, hg)),
        out_shape=jax.ShapeDtypeStruct((b, seq, BRANCH_WIDTH), BF16),
        scratch_shapes=[
            pltpu.VMEM((FOX_HEADS, seq, LANES), BF16),
            pltpu.VMEM((FOX_HEADS, FOX_VROWS, seq), BF16),
        ],
        compiler_params=_cparams(("arbitrary", "arbitrary", "arbitrary")),
        name="fox",
    )(p3d, p3d, p3d, gc)


MLSTM_ROWS = 144


def _mlstm_kernel(q_ref, k_ref, v_ref, o_ref, u_ref, gr_ref, gsum_ref, gmh_ref, y_ref, c_ref, m_ref,
                  *, ts):
    L = MLSTM_CHUNK
    dh = C_HEAD_DIM
    scale = dh ** -0.5
    si = pl.program_id(1)

    @pl.when(si == 0)
    def _():
        c_ref[...] = jnp.zeros_like(c_ref)
        m_ref[...] = jnp.zeros_like(m_ref)

    row = lax.broadcasted_iota(jnp.int32, (L, L), 0)
    col = lax.broadcasted_iota(jnp.int32, (L, L), 1)
    causal = row <= col
    sub = lax.broadcasted_iota(jnp.int32, (MLSTM_ROWS - dh, L), 0)
    tail = jnp.where(sub == 0, 1.0, 0.0)

    def chunk(c, _):
        r0 = pl.multiple_of(c * L, L)
        gsum = gsum_ref[c]
        for h in range(C_HEADS):
            hs = slice(h * dh, (h + 1) * dh)
            q = q_ref[pl.ds(r0, L), hs]
            k = k_ref[pl.ds(r0, L), hs]
            vt = jnp.concatenate([v_ref[pl.ds(r0, L), hs].astype(F32).T, tail], axis=0)
            ig = gr_ref[pl.ds(LANE_CI + h, 1), pl.ds(r0, L)]
            b = gr_ref[pl.ds(LANE_CF + h, 1), pl.ds(r0, L)]
            g = gsum[h:h + 1, :]
            m_loc = gsum[C_HEADS + h:C_HEADS + h + 1, :]
            m_in = m_ref[h]
            c_in = c_ref[h]

            dlog = jnp.where(causal, u_ref[h, pl.ds(r0, L), :] + b, NEG)
            inter = b + m_in
            m_t = jnp.maximum(jnp.max(dlog, axis=0, keepdims=True), inter)
            sm = (_dot_nt(k, q) * scale) * jnp.exp(dlog - m_t)
            w_int = jnp.exp(inter - m_t)
            ext = _dot(vt.astype(BF16), sm.astype(BF16)) + w_int * _dot_nt(c_in.astype(BF16), q)
            hh = ext[0:dh] / jnp.maximum(jnp.abs(ext[dh:dh + 1]), jnp.exp(-m_t))
            hn = hh * lax.rsqrt(jnp.mean(hh * hh, axis=0, keepdims=True) + EPS) * gmh_ref[h]
            y_ref[pl.ds(r0, L), hs] = (hn.T * o_ref[pl.ds(r0, L), hs].astype(F32)).astype(BF16)

            m_new = jnp.maximum(g + m_in, m_loc)
            w = jnp.exp(g + (ig - b) - m_new) * scale
            c_ref[h] = jnp.exp(g + m_in - m_new) * c_in + _dot((vt * w).astype(BF16), k)
            m_ref[h] = m_new
        return 0

    lax.fori_loop(0, ts // L, chunk, 0, unroll=4)


def _mlstm_call(p3d, u, gr, gsum, gmh_rep, layer, ts):
    b, seq, _ = p3d.shape
    w = BRANCH_WIDTH
    nc = ts // MLSTM_CHUNK

    def pspec(col):
        return pl.BlockSpec((None, ts, w), lambda bi, si: (bi, si, col // w))

    return pl.pallas_call(
        functools.partial(_mlstm_kernel, ts=ts),
        grid=(b, seq // ts),
        in_specs=[
            pspec(COL_CQ), pspec(COL_CK), pspec(COL_CV), pspec(COL_CO),
            pl.BlockSpec((None, C_HEADS, ts, LANES), lambda bi, si: (bi, 0, si, 0)),
            pl.BlockSpec((None, GATE_ROWS, ts), lambda bi, si: (bi, 0, si)),
            pl.BlockSpec((None, nc, 2 * C_HEADS, LANES), lambda bi, si: (bi, si, 0, 0)),
            _const_spec((C_HEADS, C_HEAD_DIM, LANES), layer),
        ],
        out_specs=pl.BlockSpec((None, ts, w), lambda bi, si: (bi, si, 0)),
        out_shape=jax.ShapeDtypeStruct((b, seq, w), BF16),
        scratch_shapes=[
            pltpu.VMEM((C_HEADS, MLSTM_ROWS, C_HEAD_DIM), F32),
            pltpu.VMEM((C_HEADS, 1, LANES), F32),
        ],
        compiler_params=_cparams(("arbitrary", "arbitrary")),
        name="mlstm",
    )(p3d, p3d, p3d, p3d, u, gr, gsum, gmh_rep)


def _merge_kernel(gates_ref, uv_ref, yb_ref, yc_ref, x_ref, gsgu_ref, ws_ref, bs_ref, wb_ref, wo_ref,
                  gq_ref, wq_ref, kv_ref, wmo_ref, out_ref, ya_ref, o_ref, *, tm):
    w = BRANCH_WIDTH
    u = uv_ref[:, :w].astype(F32)
    v = uv_ref[:, w:].astype(F32)
    vn = _rms(v, gsgu_ref[...]).astype(BF16)
    gd = w // A_GROUPS
    for nb in range(tm // A_BLOCK):
        rs = slice(nb * A_BLOCK, (nb + 1) * A_BLOCK)
        mixed = jnp.concatenate(
            [_dot(ws_ref[g], vn[rs, g * gd:(g + 1) * gd]) for g in range(A_GROUPS)], axis=1)
        ya_ref[rs, :] = (u[rs, :] * (mixed + bs_ref[...])).astype(BF16)
    merged = gates_ref[:, 0:D_MODEL].astype(F32) * _dot(ya_ref[...], wb_ref[0])
    merged += gates_ref[:, D_MODEL:2 * D_MODEL].astype(F32) * _dot(yb_ref[...], wb_ref[1])
    merged += gates_ref[:, 2 * D_MODEL:3 * D_MODEL].astype(F32) * _dot(yc_ref[...], wb_ref[2])
    x = x_ref[...] + _dot(merged.astype(BF16), wo_ref[...])

    h = _rms(x, gq_ref[...]).astype(BF16)
    q = (_dot(h, wq_ref[...]) * (MEM_HEAD_DIM ** -0.5)).astype(BF16)
    dh = MEM_HEAD_DIM
    for hd in range(MEM_HEADS):
        hs = slice(hd * dh, (hd + 1) * dh)
        s = _dot_nt(q[:, hs], kv_ref[:, hs])
        p = jnp.exp(s - jnp.max(s, axis=-1, keepdims=True))
        o = _dot(p.astype(BF16), kv_ref[:, D_MODEL + hd * dh:D_MODEL + (hd + 1) * dh])
        o_ref[:, hs] = (o / jnp.sum(p, axis=-1, keepdims=True)).astype(BF16)
    out_ref[...] = x + _dot(o_ref[...], wmo_ref[...])


def _merge_call(p2d, yb, yc, x2d, g_sgu, ws_masked, bs_full, w_branch, w_out, g_mq, w_mq, kv, w_mo,
                layer, seq, tm):
    m = x2d.shape[0]
    w = BRANCH_WIDTH
    return pl.pallas_call(
        functools.partial(_merge_kernel, tm=tm),
        grid=(m // tm,),
        in_specs=[
            pl.BlockSpec((tm, 3 * D_MODEL), lambda i: (i, COL_G // (3 * D_MODEL))),
            pl.BlockSpec((tm, 2 * w), lambda i: (i, COL_A // (2 * w))),
            pl.BlockSpec((tm, w), lambda i: (i, 0)),
            pl.BlockSpec((tm, w), lambda i: (i, 0)),
            pl.BlockSpec((tm, D_MODEL), lambda i: (i, 0)),
            _const_spec((1, w), layer),
            _const_spec((A_GROUPS, A_BLOCK, A_BLOCK), layer),
            _const_spec((A_BLOCK, w), layer),
            _const_spec((3, w, D_MODEL), layer),
            _const_spec((D_MODEL, D_MODEL), layer),
            _const_spec((1, D_MODEL), layer),
            _const_spec((D_MODEL, D_MODEL), layer),
            pl.BlockSpec((N_MEM, 2 * D_MODEL), lambda i: ((i * tm) // seq, 0)),
            _const_spec((D_MODEL, D_MODEL), layer),
        ],
        out_specs=pl.BlockSpec((tm, D_MODEL), lambda i: (i, 0)),
        out_shape=jax.ShapeDtypeStruct((m, D_MODEL), F32),
        scratch_shapes=[pltpu.VMEM((tm, w), BF16), pltpu.VMEM((tm, D_MODEL), BF16)],
        compiler_params=_cparams(("arbitrary",)),
        name="merge",
    )(p2d, p2d, yb, yc, x2d, g_sgu, ws_masked, bs_full, w_branch, w_out, g_mq, w_mq, kv, w_mo)


def _memkv_kernel(mem_ref, g_ref, w_ref, kv_ref):
    kv_ref[...] = _dot(_rms(mem_ref[...], g_ref[...]).astype(BF16), w_ref[...]).astype(BF16)


def _memkv_call(mem2d, g, w_mkv, layer):
    m = mem2d.shape[0]
    return pl.pallas_call(
        _memkv_kernel,
        grid=(m // N_MEM,),
        in_specs=[
            pl.BlockSpec((N_MEM, D_MODEL), lambda i: (i, 0)),
            _const_spec((1, D_MODEL), layer),
            _const_spec((D_MODEL, 2 * D_MODEL), layer),
        ],
        out_specs=pl.BlockSpec((N_MEM, 2 * D_MODEL), lambda i: (i, 0)),
        out_shape=jax.ShapeDtypeStruct((m, 2 * D_MODEL), BF16),
        compiler_params=_cparams(("arbitrary",)),
        name="memkv",
    )(mem2d, g, w_mkv)


def _ffn_kernel(x_ref, g_ref, wup_ref, wconv_ref, wdown_ref, gfin_ref, out_ref,
                act_ref, ext_ref, carry_ref, *, tm, tf, seq, final_norm):
    i = pl.program_id(0)
    x = x_ref[...]
    h = _rms(x, g_ref[...]).astype(BF16)
    seq_start = (i * tm) % seq == 0

    def conv(slot, half, cs):
        up = _dot(h, wup_ref[:, cs])
        ext_ref[half, pl.ds(CARRY_ROWS, tm), :] = up
        ext_ref[half, pl.ds(0, CARRY_ROWS), :] = jnp.where(seq_start, 0.0, carry_ref[slot])
        carry_ref[slot] = up[tm - CARRY_ROWS:, :]
        wc = wconv_ref[:, cs]
        y = wc[FFN_CONV - 1:FFN_CONV, :] * up
        for d in range(1, FFN_CONV):
            y = y + wc[FFN_CONV - 1 - d:FFN_CONV - d, :] * ext_ref[half, pl.ds(CARRY_ROWS - d, tm), :]
        return y

    nchunk = D_FF // tf
    for c in range(nchunk):
        a = conv(c, 0, slice(c * tf, (c + 1) * tf))
        b = conv(nchunk + c, 1, slice(D_FF + c * tf, D_FF + (c + 1) * tf))
        act_ref[:, c * tf:(c + 1) * tf] = (a * _sigmoid(a) * b).astype(BF16)
    y = x + _dot(act_ref[...], wdown_ref[...])
    if final_norm:
        y = _rms(y, gfin_ref[...])
    out_ref[...] = y


def _ffn_call(x2d, g, w_up, w_conv, w_down, g_final, layer, seq, tm, tf, final_norm):
    m = x2d.shape[0]
    return pl.pallas_call(
        functools.partial(_ffn_kernel, tm=tm, tf=tf, seq=seq, final_norm=final_norm),
        grid=(m // tm,),
        in_specs=[
            pl.BlockSpec((tm, D_MODEL), lambda i: (i, 0)),
            _const_spec((1, D_MODEL), layer),
            _const_spec((D_MODEL, 2 * D_FF), layer),
            _const_spec((FFN_CONV, 2 * D_FF), layer),
            _const_spec((D_FF, D_MODEL), layer),
            _const_spec((1, D_MODEL)),
        ],
        out_specs=pl.BlockSpec((tm, D_MODEL), lambda i: (i, 0)),
        out_shape=jax.ShapeDtypeStruct((m, D_MODEL), F32),
        scratch_shapes=[
            pltpu.VMEM((tm, D_FF), BF16),
            pltpu.VMEM((2, tm + CARRY_ROWS, tf), F32),
            pltpu.VMEM((2 * (D_FF // tf), CARRY_ROWS, tf), F32),
        ],
        compiler_params=_cparams(("arbitrary",)),
        name="ffn",
    )(x2d, g, w_up, w_conv, w_down, g_final)


def _rearrange_w_in(w):
    bw = BRANCH_WIDTH
    a0 = 0
    b0 = 2 * bw
    c0 = b0 + 3 * bw + B_HEADS
    g0 = c0 + 3 * bw + 2 * C_HEADS + bw
    co = c0 + 3 * bw + 2 * C_HEADS
    main = jnp.concatenate([
        w[..., g0:g0 + 3 * D_MODEL],
        w[..., a0:a0 + 2 * bw],
        w[..., b0:b0 + 3 * bw],
        w[..., c0:c0 + 3 * bw],
        w[..., co:co + bw],
    ], axis=-1).astype(BF16)
    small = jnp.concatenate([
        w[..., b0 + 3 * bw:b0 + 3 * bw + B_HEADS],
        w[..., c0 + 3 * bw:c0 + 3 * bw + 2 * C_HEADS],
        jnp.zeros(w.shape[:-1] + (LANES - GATE_ROWS,), w.dtype),
    ], axis=-1).astype(BF16)
    return main, small


def kernel(x, mem, g_mix, w_in, g_sgu, w_s, b_s, b_fox_f, w_conv_c, b_mlstm_i, b_mlstm_f, g_mh,
           w_branch, w_out, g_mem_q, g_mem_kv, w_mq, w_mkv, w_mo, g_ffn, w_up, w_ffn_conv, w_down,
           g_final):
    bsz, seq, _ = x.shape
    depth = w_in.shape[0]
    m = bsz * seq
    tm = min(512, seq)
    tm_proj = tm
    t_fox = min(256, seq)
    ts_mlstm = min(1024, seq)

    idx = jnp.arange(A_BLOCK)
    chunk_causal = (idx[None, :] // CHUNK) <= (idx[:, None] // CHUNK)

    x2d = x.reshape(m, D_MODEL)
    mem2d = mem.reshape(bsz * N_MEM, D_MODEL)
    rows = lambda a: a.reshape(depth, 1, -1)

    w_main, w_small = _rearrange_w_in(w_in)
    gate_bias = jnp.concatenate(
        [b_fox_f, b_mlstm_i, b_mlstm_f, jnp.zeros((depth, LANES - GATE_ROWS), F32)], axis=-1)
    ws_masked = jnp.where(chunk_causal, w_s, 0).astype(BF16)
    bs_full = jnp.repeat(jnp.swapaxes(b_s, 1, 2), BRANCH_WIDTH // A_GROUPS, axis=2)
    gmh_rep = jnp.broadcast_to(g_mh.reshape(depth, C_HEADS, C_HEAD_DIM, 1),
                               (depth, C_HEADS, C_HEAD_DIM, LANES))
    w_branch, w_out, w_mq, w_mkv, w_mo, w_up, w_down = (
        a.astype(BF16) for a in (w_branch, w_out, w_mq, w_mkv, w_mo, w_up, w_down))

    for i in range(depth):
        p2d, gs = _proj_call(x2d, rows(g_mix), w_main, w_small, w_conv_c, i, seq, tm_proj)
        p3d = p2d.reshape(bsz, seq, PROJ_COLS)
        gc, gr, u, gsum = _gates_call(gs.reshape(bsz, seq, LANES), rows(gate_bias), i)
        yb = _fox_call(p3d, gc, t_fox).reshape(m, BRANCH_WIDTH)
        yc = _mlstm_call(p3d, u, gr, gsum, gmh_rep, i, ts_mlstm).reshape(m, BRANCH_WIDTH)
        kv = _memkv_call(mem2d, rows(g_mem_kv), w_mkv, i)
        x2d = _merge_call(p2d, yb, yc, x2d, rows(g_sgu), ws_masked, bs_full, w_branch, w_out,
                          rows(g_mem_q), w_mq, kv, w_mo, i, seq, tm)
        x2d = _ffn_call(x2d, rows(g_ffn), w_up, w_ffn_conv, w_down, g_final.reshape(1, -1), i, seq, tm,
                        256, i == depth - 1)
    return x2d.reshape(bsz, seq, D_MODEL)
```

```python
import functools

import jax
import jax.numpy as jnp
from jax import lax
from jax.experimental import pallas as pl
from jax.experimental.pallas import tpu as pltpu

F32 = jnp.float32
BF16 = jnp.bfloat16

D_MODEL = 1024
EPS = 1e-6
LANES = 128
CARRY_ROWS = 8

BRANCH_WIDTH = 512
A_BLOCK = 128
A_GROUPS = 4
CHUNK = 64
B_HEADS = 8
B_HEAD_DIM = 64
C_HEADS = 4
C_HEAD_DIM = 128
C_CONV = 4
MLSTM_CHUNK = 128
N_MEM = 256
MEM_HEADS = 4
MEM_HEAD_DIM = 256
D_FF = 2816
FFN_CONV = 3

PROJ_TN = 512
COL_G = 0
COL_A = 3072
COL_BQ = 4096
COL_BK = 4608
COL_BV = 5120
COL_CQ = 5632
COL_CK = 6144
COL_CV = 6656
COL_CO = 7168
PROJ_COLS = 7680
LANE_BF = 0
LANE_CI = 8
LANE_CF = 12
GATE_ROWS = 16

LOG2E = 1.4426950408889634
FOX_Q_SCALE = B_HEAD_DIM ** -0.5 * LOG2E
NEG = -1e30
VMEM_LIMIT = 56 * 1024 * 1024


def _cparams(sem):
    return pltpu.CompilerParams(dimension_semantics=sem, vmem_limit_bytes=VMEM_LIMIT)


def _rms(xf, g):
    return xf * lax.rsqrt(jnp.mean(xf * xf, axis=-1, keepdims=True) + EPS) * g


def _sigmoid(x):
    return 0.5 * jnp.tanh(0.5 * x) + 0.5


def _gelu_tanh(x):
    return 0.5 * x * (1.0 + jnp.tanh(0.7978845608028654 * (x + 0.044715 * (x * x * x))))


def _dot(a, b):
    return jnp.dot(a, b, preferred_element_type=F32)


def _dot_nt(a, b):
    return lax.dot_general(a, b, (((1,), (1,)), ((), ())), preferred_element_type=F32)


def _const_spec(shape, layer=None):
    nd = len(shape)
    if layer is None:
        return pl.BlockSpec(shape, lambda *_: (0,) * nd, pipeline_mode=pl.Buffered(1))
    return pl.BlockSpec((None,) + tuple(shape), lambda *_: (layer,) + (0,) * nd,
                        pipeline_mode=pl.Buffered(1))


def _proj_kernel(x_ref, g_ref, w_ref, ws_ref, wc_ref, p_ref, gs_ref, ext_ref, carry_ref, *, tm, seq):
    i = pl.program_id(0)
    h = _rms(x_ref[...], g_ref[...]).astype(BF16)
    gs_ref[...] = _dot(h, ws_ref[...])
    seq_start = (i * tm) % seq == 0

    def conv_silu(acc, slot):
        ext_ref[slot, pl.ds(CARRY_ROWS, tm), :] = acc
        ext_ref[slot, pl.ds(0, CARRY_ROWS), :] = jnp.where(seq_start, 0.0, carry_ref[slot])
        carry_ref[slot] = ext_ref[slot, pl.ds(tm, CARRY_ROWS), :]
        wc = 0.5 * wc_ref[:, slot * PROJ_TN:(slot + 1) * PROJ_TN]
        half = wc[C_CONV - 1:C_CONV, :] * ext_ref[slot, pl.ds(CARRY_ROWS, tm), :]
        for d in range(1, C_CONV):
            half = half + wc[C_CONV - 1 - d:C_CONV - d, :] * ext_ref[slot, pl.ds(CARRY_ROWS - d, tm), :]
        return half * (jnp.tanh(half) + 1.0)

    order = [COL_CQ, COL_CK] + [c for c in range(0, PROJ_COLS, PROJ_TN) if c not in (COL_CQ, COL_CK)]
    for c0 in order:
        cs = slice(c0, c0 + PROJ_TN)
        acc = _dot(h, w_ref[:, cs])
        if c0 < COL_A or c0 == COL_CO:
            out = _sigmoid(acc)
        elif c0 < COL_BQ:
            out = _gelu_tanh(acc)
        elif c0 == COL_BQ:
            out = acc * FOX_Q_SCALE
        elif c0 == COL_CQ:
            out = conv_silu(acc, 0)
        elif c0 == COL_CK:
            out = conv_silu(acc, 1)
        else:
            out = acc
        p_ref[:, cs] = out.astype(BF16)


def _proj_call(x2d, g, w_main, w_small, w_conv, layer, seq, tm):
    m = x2d.shape[0]
    return pl.pallas_call(
        functools.partial(_proj_kernel, tm=tm, seq=seq),
        grid=(m // tm,),
        in_specs=[
            pl.BlockSpec((tm, D_MODEL), lambda i: (i, 0)),
            _const_spec((1, D_MODEL), layer),
            _const_spec((D_MODEL, PROJ_COLS), layer),
            _const_spec((D_MODEL, LANES), layer),
            _const_spec((C_CONV, 2 * PROJ_TN), layer),
        ],
        out_specs=[
            pl.BlockSpec((tm, PROJ_COLS), lambda i: (i, 0)),
            pl.BlockSpec((tm, LANES), lambda i: (i, 0)),
        ],
        out_shape=[
            jax.ShapeDtypeStruct((m, PROJ_COLS), BF16),
            jax.ShapeDtypeStruct((m, LANES), F32),
        ],
        scratch_shapes=[
            pltpu.VMEM((2, tm + CARRY_ROWS, PROJ_TN), F32),
            pltpu.VMEM((2, CARRY_ROWS, PROJ_TN), F32),
        ],
        compiler_params=_cparams(("arbitrary",)),
        name="proj",
    )(x2d, g, w_main, w_small, w_conv)


def _gates_kernel(gs_ref, bias_ref, gc_ref, gr_ref, u_ref, gsum_ref, *, seq):
    blk = MLSTM_CHUNK
    row = lax.broadcasted_iota(jnp.int32, (blk, blk), 0)
    col = lax.broadcasted_iota(jnp.int32, (blk, blk), 1)
    tri = (col <= row).astype(F32)
    lane = lax.broadcasted_iota(jnp.int32, (1, LANES), 1)

    def body(r, carry):
        r0 = pl.multiple_of(r * blk, blk)
        raw = gs_ref[pl.ds(r0, blk), :] + bias_ref[...]
        logsig = jnp.minimum(raw, 0.0) - jnp.log1p(jnp.exp(-jnp.abs(raw)))
        local = jnp.dot(tri, logsig, precision=lax.Precision.HIGHEST, preferred_element_type=F32)
        glob = local + carry
        out = jnp.where(lane < LANE_CI, glob, jnp.where(lane < LANE_CF, raw, local))
        gc_ref[pl.ds(r0, blk), :] = out
        gr_ref[:, pl.ds(r0, blk)] = out.T[0:GATE_ROWS, :]
        g_rows, mloc_rows = [], []
        for h in range(C_HEADS):
            d = out[:, LANE_CI + h:LANE_CI + h + 1] - out[:, LANE_CF + h:LANE_CF + h + 1]
            u = jnp.broadcast_to(d, (blk, LANES))
            u_ref[h, pl.ds(r0, blk), :] = u
            g = jnp.broadcast_to(out[blk - 1:blk, LANE_CF + h:LANE_CF + h + 1], (1, LANES))
            g_rows.append(g)
            mloc_rows.append(g + jnp.max(u, axis=0, keepdims=True))
        gsum_ref[r] = jnp.concatenate(g_rows + mloc_rows, axis=0)
        return glob[blk - 1:blk, :]

    lax.fori_loop(0, seq // blk, body, jnp.zeros((1, LANES), F32), unroll=4)


def _gates_call(gs3d, bias, layer):
    b, seq, _ = gs3d.shape
    nchunk = seq // MLSTM_CHUNK
    return pl.pallas_call(
        functools.partial(_gates_kernel, seq=seq),
        grid=(b,),
        in_specs=[
            pl.BlockSpec((None, seq, LANES), lambda bi: (bi, 0, 0)),
            _const_spec((1, LANES), layer),
        ],
        out_specs=[
            pl.BlockSpec((None, seq, LANES), lambda bi: (bi, 0, 0)),
            pl.BlockSpec((None, GATE_ROWS, seq), lambda bi: (bi, 0, 0)),
            pl.BlockSpec((None, C_HEADS, seq, LANES), lambda bi: (bi, 0, 0, 0)),
            pl.BlockSpec((None, nchunk, 2 * C_HEADS, LANES), lambda bi: (bi, 0, 0, 0)),
        ],
        out_shape=[
            jax.ShapeDtypeStruct((b, seq, LANES), F32),
            jax.ShapeDtypeStruct((b, GATE_ROWS, seq), F32),
            jax.ShapeDtypeStruct((b, C_HEADS, seq, LANES), F32),
            jax.ShapeDtypeStruct((b, nchunk, 2 * C_HEADS, LANES), F32),
        ],
        compiler_params=_cparams(("arbitrary",)),
        name="gates",
    )(gs3d, bias)


FOX_VROWS = 80
FOX_KMULT = 2
FOX_HEADS = 8


def _fox_kernel(q_ref, k_ref, v_ref, gc_ref, o_ref, kaug_ref, vt_ref, *, t, seq):
    hg = pl.program_id(1)
    qi = pl.program_id(2)
    hd = B_HEAD_DIM
    nh = FOX_HEADS
    lane = lax.broadcasted_iota(jnp.int32, (1, LANES), 1)
    own = (lane < hd, lane >= hd)
    aug0 = (hd, 0)

    @pl.when(qi == 0)
    def _():
        sub = lax.broadcasted_iota(jnp.int32, (FOX_VROWS - hd, seq), 0)
        tail = jnp.where(sub == 0, 1.0, 0.0).astype(BF16)
        for hh in range(nh):
            vt_ref[hh, hd:FOX_VROWS, :] = tail

        def body(r, _):
            r0 = pl.multiple_of(r * LANES, LANES)
            g = gc_ref[pl.ds(r0, LANES), :]
            for pp in range(nh // 2):
                ls = slice(pp * LANES, (pp + 1) * LANES)
                kb = k_ref[pl.ds(r0, LANES), ls].astype(F32)
                vt = v_ref[pl.ds(r0, LANES), ls].astype(F32).T
                for h in range(2):
                    hh = 2 * pp + h
                    neg = -LOG2E * jnp.sum(jnp.where(lane == LANE_BF + nh * hg + hh, g, 0.0), axis=-1,
                                           keepdims=True)
                    hi = neg.astype(BF16).astype(F32)
                    mid = (neg - hi).astype(BF16).astype(F32)
                    lo = (neg - hi) - mid
                    a = aug0[h]
                    extra = jnp.where(lane == a, hi, jnp.where(lane == a + 1, mid, jnp.where(lane == a + 2, lo, 0.0)))
                    kaug_ref[hh, pl.ds(r0, LANES), :] = jnp.where(own[h], kb, extra).astype(BF16)
                    vt_ref[hh, 0:hd, pl.ds(r0, LANES)] = vt[h * hd:(h + 1) * hd, :].astype(BF16)
            return 0

        lax.fori_loop(0, seq // LANES, body, 0)

    qa = []
    for pp in range(nh // 2):
        q = q_ref[:, pp * LANES:(pp + 1) * LANES].astype(F32)
        for h in range(2):
            a = aug0[h]
            ones3 = jnp.where((lane >= a) & (lane < a + 3), 1.0, 0.0)
            qa.append(jnp.where(own[h], q, ones3).astype(BF16))
    row = lax.broadcasted_iota(jnp.int32, (t, t), 0)
    col = lax.broadcasted_iota(jnp.int32, (t, t), 1)
    causal = row <= col

    def qk(hh, k0, tk):
        return _dot_nt(kaug_ref[hh, pl.ds(k0, tk), :], qa[hh])

    def update(k0, tk, state, masked):
        k0 = pl.multiple_of(k0, t)
        sts = [qk(hh, k0, tk) for hh in range(nh)]
        ms, ps, alphas = [], [], []
        for hh in range(nh):
            m = state[hh][0]
            s = jnp.where(causal, sts[hh], NEG) if masked else sts[hh]
            m_new = jnp.maximum(m, jnp.max(s, axis=0, keepdims=True))
            ps.append(jnp.exp2((s - m_new).astype(BF16)))
            alphas.append(jnp.exp2(m - m_new))
            ms.append(m_new)
        pvs = [_dot(vt_ref[hh, :, pl.ds(k0, tk)], ps[hh]) for hh in range(nh)]
        return tuple((ms[hh], alphas[hh] * state[hh][1] + pvs[hh]) for hh in range(nh))

    km = FOX_KMULT
    init = tuple((jnp.full((1, t), NEG, F32), jnp.zeros((FOX_VROWS, t), F32)) for _ in range(nh))
    def wide_pair(j, state):
        state = update(j * 2 * km * t, km * t, state, False)
        return update((2 * j + 1) * km * t, km * t, state, False)

    nw = qi // km
    state = lax.fori_loop(0, nw // 2, wide_pair, init)
    state = lax.fori_loop((nw // 2) * 2, nw, lambda j, state: update(j * km * t, km * t, state, False), state)
    state = lax.fori_loop(nw * km, qi, lambda j, state: update(j * t, t, state, False), state)
    state = update(qi * t, t, state, True)
    ot = jnp.concatenate([acc[0:hd] / acc[hd:hd + 1] for _, acc in state], axis=0)
    o_ref[...] = ot.T.astype(BF16)


def _fox_call(p3d, gc, t):
    b, seq, _ = p3d.shape
    w = FOX_HEADS * B_HEAD_DIM
    return pl.pallas_call(
        functools.partial(_fox_kernel, t=t, seq=seq),
        grid=(b, B_HEADS // FOX_HEADS, seq // t),
        in_specs=[
            pl.BlockSpec((None, t, w), lambda bi, hg, qi: (bi, qi, COL_BQ // w + hg)),
            pl.BlockSpec((None, seq, w), lambda bi, hg, qi: (bi, 0, COL_BK // w + hg)),
            pl.BlockSpec((None, seq, w), lambda bi, hg, qi: (bi, 0, COL_BV // w + hg)),
            pl.BlockSpec((None, seq, LANES), lambda bi, hg, qi: (bi, 0, 0)),
        ],
        out_specs=pl.BlockSpec((None, t, w), lambda bi, hg, qi: (bi, qi, hg)),
        out_shape=jax.ShapeDtypeStruct((b, seq, BRANCH_WIDTH), BF16),
        scratch_shapes=[
            pltpu.VMEM((FOX_HEADS, seq, LANES), BF16),
            pltpu.VMEM((FOX_HEADS, FOX_VROWS, seq), BF16),
        ],
        compiler_params=_cparams(("arbitrary", "arbitrary", "arbitrary")),
        name="fox",
    )(p3d, p3d, p3d, gc)


MLSTM_ROWS = 144


def _mlstm_kernel(q_ref, k_ref, v_ref, o_ref, u_ref, gr_ref, gsum_ref, gmh_ref, y_ref, c_ref, m_ref,
                  *, ts):
    L = MLSTM_CHUNK
    dh = C_HEAD_DIM
    scale = dh ** -0.5
    si = pl.program_id(1)

    @pl.when(si == 0)
    def _():
        c_ref[...] = jnp.zeros_like(c_ref)
        m_ref[...] = jnp.zeros_like(m_ref)

    row = lax.broadcasted_iota(jnp.int32, (L, L), 0)
    col = lax.broadcasted_iota(jnp.int32, (L, L), 1)
    causal = row <= col
    sub = lax.broadcasted_iota(jnp.int32, (MLSTM_ROWS - dh, L), 0)
    tail = jnp.where(sub == 0, 1.0, 0.0)

    def chunk(c, _):
        r0 = pl.multiple_of(c * L, L)
        gsum = gsum_ref[c]
        for h in range(C_HEADS):
            hs = slice(h * dh, (h + 1) * dh)
            q = q_ref[pl.ds(r0, L), hs]
            k = k_ref[pl.ds(r0, L), hs]
            vt = jnp.concatenate([v_ref[pl.ds(r0, L), hs].astype(F32).T, tail], axis=0)
            ig = gr_ref[pl.ds(LANE_CI + h, 1), pl.ds(r0, L)]
            b = gr_ref[pl.ds(LANE_CF + h, 1), pl.ds(r0, L)]
            g = gsum[h:h + 1, :]
            m_loc = gsum[C_HEADS + h:C_HEADS + h + 1, :]
            m_in = m_ref[h]
            c_in = c_ref[h]

            dlog = jnp.where(causal, u_ref[h, pl.ds(r0, L), :] + b, NEG)
            inter = b + m_in
            m_t = jnp.maximum(jnp.max(dlog, axis=0, keepdims=True), inter)
            sm = (_dot_nt(k, q) * scale) * jnp.exp(dlog - m_t)
            w_int = jnp.exp(inter - m_t)
            ext = _dot(vt.astype(BF16), sm.astype(BF16)) + w_int * _dot_nt(c_in.astype(BF16), q)
            hh = ext[0:dh] / jnp.maximum(jnp.abs(ext[dh:dh + 1]), jnp.exp(-m_t))
            hn = hh * lax.rsqrt(jnp.mean(hh * hh, axis=0, keepdims=True) + EPS) * gmh_ref[h]
            y_ref[pl.ds(r0, L), hs] = (hn.T * o_ref[pl.ds(r0, L), hs].astype(F32)).astype(BF16)

            m_new = jnp.maximum(g + m_in, m_loc)
            w = jnp.exp(g + (ig - b) - m_new) * scale
            c_ref[h] = jnp.exp(g + m_in - m_new) * c_in + _dot((vt * w).astype(BF16), k)
            m_ref[h] = m_new
        return 0

    lax.fori_loop(0, ts // L, chunk, 0, unroll=4)


def _mlstm_call(p3d, u, gr, gsum, gmh_rep, layer, ts):
    b, seq, _ = p3d.shape
    w = BRANCH_WIDTH
    nc = ts // MLSTM_CHUNK

    def pspec(col):
        return pl.BlockSpec((None, ts, w), lambda bi, si: (bi, si, col // w))

    return pl.pallas_call(
        functools.partial(_mlstm_kernel, ts=ts),
        grid=(b, seq // ts),
        in_specs=[
            pspec(COL_CQ), pspec(COL_CK), pspec(COL_CV), pspec(COL_CO),
            pl.BlockSpec((None, C_HEADS, ts, LANES), lambda bi, si: (bi, 0, si, 0)),
            pl.BlockSpec((None, GATE_ROWS, ts), lambda bi, si: (bi, 0, si)),
            pl.BlockSpec((None, nc, 2 * C_HEADS, LANES), lambda bi, si: (bi, si, 0, 0)),
            _const_spec((C_HEADS, C_HEAD_DIM, LANES), layer),
        ],
        out_specs=pl.BlockSpec((None, ts, w), lambda bi, si: (bi, si, 0)),
        out_shape=jax.ShapeDtypeStruct((b, seq, w), BF16),
        scratch_shapes=[
            pltpu.VMEM((C_HEADS, MLSTM_ROWS, C_HEAD_DIM), F32),
            pltpu.VMEM((C_HEADS, 1, LANES), F32),
        ],
        compiler_params=_cparams(("arbitrary", "arbitrary")),
        name="mlstm",
    )(p3d, p3d, p3d, p3d, u, gr, gsum, gmh_rep)


def _merge_kernel(gates_ref, uv_ref, yb_ref, yc_ref, x_ref, gsgu_ref, ws_ref, bs_ref, wb_ref, wo_ref,
                  gq_ref, wq_ref, kv_ref, wmo_ref, out_ref, ya_ref, o_ref, *, tm):
    w = BRANCH_WIDTH
    u = uv_ref[:, :w].astype(F32)
    v = uv_ref[:, w:].astype(F32)
    vn = _rms(v, gsgu_ref[...]).astype(BF16)
    gd = w // A_GROUPS
    for nb in range(tm // A_BLOCK):
        rs = slice(nb * A_BLOCK, (nb + 1) * A_BLOCK)
        mixed = jnp.concatenate(
            [_dot(ws_ref[g], vn[rs, g * gd:(g + 1) * gd]) for g in range(A_GROUPS)], axis=1)
        ya_ref[rs, :] = (u[rs, :] * (mixed + bs_ref[...])).astype(BF16)
    merged = gates_ref[:, 0:D_MODEL].astype(F32) * _dot(ya_ref[...], wb_ref[0])
    merged += gates_ref[:, D_MODEL:2 * D_MODEL].astype(F32) * _dot(yb_ref[...], wb_ref[1])
    merged += gates_ref[:, 2 * D_MODEL:3 * D_MODEL].astype(F32) * _dot(yc_ref[...], wb_ref[2])
    x = x_ref[...] + _dot(merged.astype(BF16), wo_ref[...])

    h = _rms(x, gq_ref[...]).astype(BF16)
    q = (_dot(h, wq_ref[...]) * (MEM_HEAD_DIM ** -0.5)).astype(BF16)
    dh = MEM_HEAD_DIM
    for hd in range(MEM_HEADS):
        hs = slice(hd * dh, (hd + 1) * dh)
        s = _dot_nt(q[:, hs], kv_ref[:, hs])
        p = jnp.exp(s - jnp.max(s, axis=-1, keepdims=True))
        o = _dot(p.astype(BF16), kv_ref[:, D_MODEL + hd * dh:D_MODEL + (hd + 1) * dh])
        o_ref[:, hs] = (o / jnp.sum(p, axis=-1, keepdims=True)).astype(BF16)
    out_ref[...] = x + _dot(o_ref[...], wmo_ref[...])


def _merge_call(p2d, yb, yc, x2d, g_sgu, ws_masked, bs_full, w_branch, w_out, g_mq, w_mq, kv, w_mo,
                layer, seq, tm):
    m = x2d.shape[0]
    w = BRANCH_WIDTH
    return pl.pallas_call(
        functools.partial(_merge_kernel, tm=tm),
        grid=(m // tm,),
        in_specs=[
            pl.BlockSpec((tm, 3 * D_MODEL), lambda i: (i, COL_G // (3 * D_MODEL))),
            pl.BlockSpec((tm, 2 * w), lambda i: (i, COL_A // (2 * w))),
            pl.BlockSpec((tm, w), lambda i: (i, 0)),
            pl.BlockSpec((tm, w), lambda i: (i, 0)),
            pl.BlockSpec((tm, D_MODEL), lambda i: (i, 0)),
            _const_spec((1, w), layer),
            _const_spec((A_GROUPS, A_BLOCK, A_BLOCK), layer),
            _const_spec((A_BLOCK, w), layer),
            _const_spec((3, w, D_MODEL), layer),
            _const_spec((D_MODEL, D_MODEL), layer),
            _const_spec((1, D_MODEL), layer),
            _const_spec((D_MODEL, D_MODEL), layer),
            pl.BlockSpec((N_MEM, 2 * D_MODEL), lambda i: ((i * tm) // seq, 0)),
            _const_spec((D_MODEL, D_MODEL), layer),
        ],
        out_specs=pl.BlockSpec((tm, D_MODEL), lambda i: (i, 0)),
        out_shape=jax.ShapeDtypeStruct((m, D_MODEL), F32),
        scratch_shapes=[pltpu.VMEM((tm, w), BF16), pltpu.VMEM((tm, D_MODEL), BF16)],
        compiler_params=_cparams(("arbitrary",)),
        name="merge",
    )(p2d, p2d, yb, yc, x2d, g_sgu, ws_masked, bs_full, w_branch, w_out, g_mq, w_mq, kv, w_mo)


def _memkv_kernel(mem_ref, g_ref, w_ref, kv_ref):
    kv_ref[...] = _dot(_rms(mem_ref[...], g_ref[...]).astype(BF16), w_ref[...]).astype(BF16)


def _memkv_call(mem2d, g, w_mkv, layer):
    m = mem2d.shape[0]
    return pl.pallas_call(
        _memkv_kernel,
        grid=(m // N_MEM,),
        in_specs=[
            pl.BlockSpec((N_MEM, D_MODEL), lambda i: (i, 0)),
            _const_spec((1, D_MODEL), layer),
            _const_spec((D_MODEL, 2 * D_MODEL), layer),
        ],
        out_specs=pl.BlockSpec((N_MEM, 2 * D_MODEL), lambda i: (i, 0)),
        out_shape=jax.ShapeDtypeStruct((m, 2 * D_MODEL), BF16),
        compiler_params=_cparams(("arbitrary",)),
        name="memkv",
    )(mem2d, g, w_mkv)


def _ffn_kernel(x_ref, g_ref, wup_ref, wconv_ref, wdown_ref, gfin_ref, out_ref,
                act_ref, ext_ref, carry_ref, *, tm, tf, seq, final_norm):
    i = pl.program_id(0)
    x = x_ref[...]
    h = _rms(x, g_ref[...]).astype(BF16)
    seq_start = (i * tm) % seq == 0

    def conv(slot, half, cs):
        up = _dot(h, wup_ref[:, cs])
        ext_ref[half, pl.ds(CARRY_ROWS, tm), :] = up
        ext_ref[half, pl.ds(0, CARRY_ROWS), :] = jnp.where(seq_start, 0.0, carry_ref[slot])
        carry_ref[slot] = up[tm - CARRY_ROWS:, :]
        wc = wconv_ref[:, cs]
        y = wc[FFN_CONV - 1:FFN_CONV, :] * up
        for d in range(1, FFN_CONV):
            y = y + wc[FFN_CONV - 1 - d:FFN_CONV - d, :] * ext_ref[half, pl.ds(CARRY_ROWS - d, tm), :]
        return y

    nchunk = D_FF // tf
    for c in range(nchunk):
        a = conv(c, 0, slice(c * tf, (c + 1) * tf))
        b = conv(nchunk + c, 1, slice(D_FF + c * tf, D_FF + (c + 1) * tf))
        act_ref[:, c * tf:(c + 1) * tf] = (a * _sigmoid(a) * b).astype(BF16)
    y = x + _dot(act_ref[...], wdown_ref[...])
    if final_norm:
        y = _rms(y, gfin_ref[...])
    out_ref[...] = y


def _ffn_call(x2d, g, w_up, w_conv, w_down, g_final, layer, seq, tm, tf, final_norm):
    m = x2d.shape[0]
    return pl.pallas_call(
        functools.partial(_ffn_kernel, tm=tm, tf=tf, seq=seq, final_norm=final_norm),
        grid=(m // tm,),
        in_specs=[
            pl.BlockSpec((tm, D_MODEL), lambda i: (i, 0)),
            _const_spec((1, D_MODEL), layer),
            _const_spec((D_MODEL, 2 * D_FF), layer),
            _const_spec((FFN_CONV, 2 * D_FF), layer),
            _const_spec((D_FF, D_MODEL), layer),
            _const_spec((1, D_MODEL)),
        ],
        out_specs=pl.BlockSpec((tm, D_MODEL), lambda i: (i, 0)),
        out_shape=jax.ShapeDtypeStruct((m, D_MODEL), F32),
        scratch_shapes=[
            pltpu.VMEM((tm, D_FF), BF16),
            pltpu.VMEM((2, tm + CARRY_ROWS, tf), F32),
            pltpu.VMEM((2 * (D_FF // tf), CARRY_ROWS, tf), F32),
        ],
        compiler_params=_cparams(("arbitrary",)),
        name="ffn",
    )(x2d, g, w_up, w_conv, w_down, g_final)


def _rearrange_w_in(w):
    bw = BRANCH_WIDTH
    a0 = 0
    b0 = 2 * bw
    c0 = b0 + 3 * bw + B_HEADS
    g0 = c0 + 3 * bw + 2 * C_HEADS + bw
    co = c0 + 3 * bw + 2 * C_HEADS
    main = jnp.concatenate([
        w[..., g0:g0 + 3 * D_MODEL],
        w[..., a0:a0 + 2 * bw],
        w[..., b0:b0 + 3 * bw],
        w[..., c0:c0 + 3 * bw],
        w[..., co:co + bw],
    ], axis=-1).astype(BF16)
    small = jnp.concatenate([
        w[..., b0 + 3 * bw:b0 + 3 * bw + B_HEADS],
        w[..., c0 + 3 * bw:c0 + 3 * bw + 2 * C_HEADS],
        jnp.zeros(w.shape[:-1] + (LANES - GATE_ROWS,), w.dtype),
    ], axis=-1).astype(BF16)
    return main, small


def kernel(x, mem, g_mix, w_in, g_sgu, w_s, b_s, b_fox_f, w_conv_c, b_mlstm_i, b_mlstm_f, g_mh,
           w_branch, w_out, g_mem_q, g_mem_kv, w_mq, w_mkv, w_mo, g_ffn, w_up, w_ffn_conv, w_down,
           g_final):
    bsz, seq, _ = x.shape
    depth = w_in.shape[0]
    m = bsz * seq
    tm = min(512, seq)
    tm_proj = tm
    tm_ffn = tm
    t_fox = min(256, seq)
    ts_mlstm = min(1024, seq)

    idx = jnp.arange(A_BLOCK)
    chunk_causal = (idx[None, :] // CHUNK) <= (idx[:, None] // CHUNK)

    x2d = x.reshape(m, D_MODEL)
    mem2d = mem.reshape(bsz * N_MEM, D_MODEL)
    rows = lambda a: a.reshape(depth, 1, -1)

    w_main, w_small = _rearrange_w_in(w_in)
    gate_bias = jnp.concatenate(
        [b_fox_f, b_mlstm_i, b_mlstm_f, jnp.zeros((depth, LANES - GATE_ROWS), F32)], axis=-1)
    ws_masked = jnp.where(chunk_causal, w_s, 0).astype(BF16)
    bs_full = jnp.repeat(jnp.swapaxes(b_s, 1, 2), BRANCH_WIDTH // A_GROUPS, axis=2)
    gmh_rep = jnp.broadcast_to(g_mh.reshape(depth, C_HEADS, C_HEAD_DIM, 1),
                               (depth, C_HEADS, C_HEAD_DIM, LANES))
    w_branch, w_out, w_mq, w_mkv, w_mo, w_up, w_down = (
        a.astype(BF16) for a in (w_branch, w_out, w_mq, w_mkv, w_mo, w_up, w_down))

    for i in range(depth):
        p2d, gs = _proj_call(x2d, rows(g_mix), w_main, w_small, w_conv_c, i, seq, tm_proj)
        p3d = p2d.reshape(bsz, seq, PROJ_COLS)
        gc, gr, u, gsum = _gates_call(gs.reshape(bsz, seq, LANES), rows(gate_bias), i)
        yb = _fox_call(p3d, gc, t_fox).reshape(m, BRANCH_WIDTH)
        yc = _mlstm_call(p3d, u, gr, gsum, gmh_rep, i, ts_mlstm).reshape(m, BRANCH_WIDTH)
        kv = _memkv_call(mem2d, rows(g_mem_kv), w_mkv, i)
        x2d = _merge_call(p2d, yb, yc, x2d, rows(g_sgu), ws_masked, bs_full, w_branch, w_out,
                          rows(g_mem_q), w_mq, kv, w_mo, i, seq, tm)
        x2d = _ffn_call(x2d, rows(g_ffn), w_up, w_ffn_conv, w_down, g_final.reshape(1, -1), i, seq, tm_ffn,
                        256, i == depth - 1)
    return x2d.reshape(bsz, seq, D_MODEL)
```

```python
import functools

import jax
import jax.numpy as jnp
from jax import lax
from jax.experimental import pallas as pl
from jax.experimental.pallas import tpu as pltpu

F32 = jnp.float32
BF16 = jnp.bfloat16

D_MODEL = 1024
EPS = 1e-6
LANES = 128
CARRY_ROWS = 8

BRANCH_WIDTH = 512
A_BLOCK = 128
A_GROUPS = 4
CHUNK = 64
B_HEADS = 8
B_HEAD_DIM = 64
C_HEADS = 4
C_HEAD_DIM = 128
C_CONV = 4
MLSTM_CHUNK = 128
N_MEM = 256
MEM_HEADS = 4
MEM_HEAD_DIM = 256
D_FF = 2816
FFN_CONV = 3

PROJ_TN = 512
COL_G = 0
COL_A = 3072
COL_BQ = 4096
COL_BK = 4608
COL_BV = 5120
COL_CQ = 5632
COL_CK = 6144
COL_CV = 6656
COL_CO = 7168
PROJ_COLS = 7680
LANE_BF = 0
LANE_CI = 8
LANE_CF = 12
GATE_ROWS = 16

LOG2E = 1.4426950408889634
FOX_Q_SCALE = B_HEAD_DIM ** -0.5 * LOG2E
NEG = -1e30
VMEM_LIMIT = 56 * 1024 * 1024


def _cparams(sem):
    return pltpu.CompilerParams(dimension_semantics=sem, vmem_limit_bytes=VMEM_LIMIT)


def _rms(xf, g):
    return xf * lax.rsqrt(jnp.mean(xf * xf, axis=-1, keepdims=True) + EPS) * g


def _sigmoid(x):
    return 0.5 * jnp.tanh(0.5 * x) + 0.5


def _gelu_tanh(x):
    return 0.5 * x * (1.0 + jnp.tanh(0.7978845608028654 * (x + 0.044715 * (x * x * x))))


def _dot(a, b):
    return jnp.dot(a, b, preferred_element_type=F32)


def _dot_nt(a, b):
    return lax.dot_general(a, b, (((1,), (1,)), ((), ())), preferred_element_type=F32)


def _const_spec(shape, layer=None):
    nd = len(shape)
    if layer is None:
        return pl.BlockSpec(shape, lambda *_: (0,) * nd, pipeline_mode=pl.Buffered(1))
    return pl.BlockSpec((None,) + tuple(shape), lambda *_: (layer,) + (0,) * nd,
                        pipeline_mode=pl.Buffered(1))


def _proj_kernel(x_ref, g_ref, w_ref, ws_ref, wc_ref, p_ref, gs_ref, ext_ref, carry_ref, *, tm, seq):
    i = pl.program_id(0)
    h = _rms(x_ref[...], g_ref[...]).astype(BF16)
    gs_ref[...] = _dot(h, ws_ref[...])
    seq_start = (i * tm) % seq == 0

    def conv_silu(acc, slot):
        ext_ref[slot, pl.ds(CARRY_ROWS, tm), :] = acc
        ext_ref[slot, pl.ds(0, CARRY_ROWS), :] = jnp.where(seq_start, 0.0, carry_ref[slot])
        carry_ref[slot] = ext_ref[slot, pl.ds(tm, CARRY_ROWS), :]
        wc = 0.5 * wc_ref[:, slot * PROJ_TN:(slot + 1) * PROJ_TN]
        half = wc[C_CONV - 1:C_CONV, :] * ext_ref[slot, pl.ds(CARRY_ROWS, tm), :]
        for d in range(1, C_CONV):
            half = half + wc[C_CONV - 1 - d:C_CONV - d, :] * ext_ref[slot, pl.ds(CARRY_ROWS - d, tm), :]
        return half * (jnp.tanh(half) + 1.0)

    order = [COL_CQ, COL_CK] + [c for c in range(0, PROJ_COLS, PROJ_TN) if c not in (COL_CQ, COL_CK)]
    for c0 in order:
        cs = slice(c0, c0 + PROJ_TN)
        acc = _dot(h, w_ref[:, cs])
        if c0 < COL_A or c0 == COL_CO:
            out = _sigmoid(acc)
        elif c0 < COL_BQ:
            out = _gelu_tanh(acc)
        elif c0 == COL_BQ:
            out = acc * FOX_Q_SCALE
        elif c0 == COL_CQ:
            out = conv_silu(acc, 0)
        elif c0 == COL_CK:
            out = conv_silu(acc, 1)
        else:
            out = acc
        p_ref[:, cs] = out.astype(BF16)


def _proj_call(x2d, g, w_main, w_small, w_conv, layer, seq, tm):
    m = x2d.shape[0]
    return pl.pallas_call(
        functools.partial(_proj_kernel, tm=tm, seq=seq),
        grid=(m // tm,),
        in_specs=[
            pl.BlockSpec((tm, D_MODEL), lambda i: (i, 0)),
            _const_spec((1, D_MODEL), layer),
            _const_spec((D_MODEL, PROJ_COLS), layer),
            _const_spec((D_MODEL, LANES), layer),
            _const_spec((C_CONV, 2 * PROJ_TN), layer),
        ],
        out_specs=[
            pl.BlockSpec((tm, PROJ_COLS), lambda i: (i, 0)),
            pl.BlockSpec((tm, LANES), lambda i: (i, 0)),
        ],
        out_shape=[
            jax.ShapeDtypeStruct((m, PROJ_COLS), BF16),
            jax.ShapeDtypeStruct((m, LANES), F32),
        ],
        scratch_shapes=[
            pltpu.VMEM((2, tm + CARRY_ROWS, PROJ_TN), F32),
            pltpu.VMEM((2, CARRY_ROWS, PROJ_TN), F32),
        ],
        compiler_params=_cparams(("arbitrary",)),
        name="proj",
    )(x2d, g, w_main, w_small, w_conv)


def _gates_kernel(gs_ref, bias_ref, gc_ref, gr_ref, u_ref, gsum_ref, *, seq):
    blk = MLSTM_CHUNK
    row = lax.broadcasted_iota(jnp.int32, (blk, blk), 0)
    col = lax.broadcasted_iota(jnp.int32, (blk, blk), 1)
    tri = (col <= row).astype(F32)
    lane = lax.broadcasted_iota(jnp.int32, (1, LANES), 1)

    def body(r, carry):
        r0 = pl.multiple_of(r * blk, blk)
        raw = gs_ref[pl.ds(r0, blk), :] + bias_ref[...]
        logsig = jnp.minimum(raw, 0.0) - jnp.log1p(jnp.exp(-jnp.abs(raw)))
        local = jnp.dot(tri, logsig, precision=lax.Precision.HIGHEST, preferred_element_type=F32)
        glob = local + carry
        out = jnp.where(lane < LANE_CI, glob, jnp.where(lane < LANE_CF, raw, local))
        gc_ref[pl.ds(r0, blk), :] = out
        gr_ref[:, pl.ds(r0, blk)] = out.T[0:GATE_ROWS, :]
        g_rows, mloc_rows = [], []
        for h in range(C_HEADS):
            d = out[:, LANE_CI + h:LANE_CI + h + 1] - out[:, LANE_CF + h:LANE_CF + h + 1]
            u = jnp.broadcast_to(d, (blk, LANES))
            u_ref[h, pl.ds(r0, blk), :] = u
            g = jnp.broadcast_to(out[blk - 1:blk, LANE_CF + h:LANE_CF + h + 1], (1, LANES))
            g_rows.append(g)
            mloc_rows.append(g + jnp.max(u, axis=0, keepdims=True))
        gsum_ref[r] = jnp.concatenate(g_rows + mloc_rows, axis=0)
        return glob[blk - 1:blk, :]

    lax.fori_loop(0, seq // blk, body, jnp.zeros((1, LANES), F32), unroll=4)


def _gates_call(gs3d, bias, layer):
    b, seq, _ = gs3d.shape
    nchunk = seq // MLSTM_CHUNK
    return pl.pallas_call(
        functools.partial(_gates_kernel, seq=seq),
        grid=(b,),
        in_specs=[
            pl.BlockSpec((None, seq, LANES), lambda bi: (bi, 0, 0)),
            _const_spec((1, LANES), layer),
        ],
        out_specs=[
            pl.BlockSpec((None, seq, LANES), lambda bi: (bi, 0, 0)),
            pl.BlockSpec((None, GATE_ROWS, seq), lambda bi: (bi, 0, 0)),
            pl.BlockSpec((None, C_HEADS, seq, LANES), lambda bi: (bi, 0, 0, 0)),
            pl.BlockSpec((None, nchunk, 2 * C_HEADS, LANES), lambda bi: (bi, 0, 0, 0)),
        ],
        out_shape=[
            jax.ShapeDtypeStruct((b, seq, LANES), F32),
            jax.ShapeDtypeStruct((b, GATE_ROWS, seq), F32),
            jax.ShapeDtypeStruct((b, C_HEADS, seq, LANES), F32),
            jax.ShapeDtypeStruct((b, nchunk, 2 * C_HEADS, LANES), F32),
        ],
        compiler_params=_cparams(("arbitrary",)),
        name="gates",
    )(gs3d, bias)


FOX_VROWS = 80
FOX_KMULT = 2
FOX_HEADS = 8


def _fox_kernel(q_ref, k_ref, v_ref, gc_ref, o_ref, kaug_ref, vt_ref, *, t, seq):
    hg = pl.program_id(1)
    qi = pl.program_id(2)
    hd = B_HEAD_DIM
    nh = FOX_HEADS
    lane = lax.broadcasted_iota(jnp.int32, (1, LANES), 1)
    own = (lane < hd, lane >= hd)
    aug0 = (hd, 0)

    @pl.when(qi == 0)
    def _():
        sub = lax.broadcasted_iota(jnp.int32, (FOX_VROWS - hd, seq), 0)
        tail = jnp.where(sub == 0, 1.0, 0.0).astype(BF16)
        for hh in range(nh):
            vt_ref[hh, hd:FOX_VROWS, :] = tail

        def body(r, _):
            r0 = pl.multiple_of(r * LANES, LANES)
            g = gc_ref[pl.ds(r0, LANES), :]
            for pp in range(nh // 2):
                ls = slice(pp * LANES, (pp + 1) * LANES)
                kb = k_ref[pl.ds(r0, LANES), ls].astype(F32)
                vt = v_ref[pl.ds(r0, LANES), ls].astype(F32).T
                for h in range(2):
                    hh = 2 * pp + h
                    neg = -LOG2E * jnp.sum(jnp.where(lane == LANE_BF + nh * hg + hh, g, 0.0), axis=-1,
                                           keepdims=True)
                    hi = neg.astype(BF16).astype(F32)
                    mid = (neg - hi).astype(BF16).astype(F32)
                    lo = (neg - hi) - mid
                    a = aug0[h]
                    extra = jnp.where(lane == a, hi, jnp.where(lane == a + 1, mid, jnp.where(lane == a + 2, lo, 0.0)))
                    kaug_ref[hh, pl.ds(r0, LANES), :] = jnp.where(own[h], kb, extra).astype(BF16)
                    vt_ref[hh, 0:hd, pl.ds(r0, LANES)] = vt[h * hd:(h + 1) * hd, :].astype(BF16)
            return 0

        lax.fori_loop(0, seq // LANES, body, 0)

    qa = []
    for pp in range(nh // 2):
        q = q_ref[:, pp * LANES:(pp + 1) * LANES].astype(F32)
        for h in range(2):
            a = aug0[h]
            ones3 = jnp.where((lane >= a) & (lane < a + 3), 1.0, 0.0)
            qa.append(jnp.where(own[h], q, ones3).astype(BF16))
    row = lax.broadcasted_iota(jnp.int32, (t, t), 0)
    col = lax.broadcasted_iota(jnp.int32, (t, t), 1)
    causal = row <= col

    def qk(hh, k0, tk):
        return _dot_nt(kaug_ref[hh, pl.ds(k0, tk), :], qa[hh])

    def update(k0, tk, state, masked):
        k0 = pl.multiple_of(k0, t)
        sts = [qk(hh, k0, tk) for hh in range(nh)]
        ms, ps, alphas = [], [], []
        for hh in range(nh):
            m = state[hh][0]
            s = jnp.where(causal, sts[hh], NEG) if masked else sts[hh]
            m_new = jnp.maximum(m, jnp.max(s, axis=0, keepdims=True))
            ps.append(jnp.exp2((s - m_new).astype(BF16)))
            alphas.append(jnp.exp2(m - m_new))
            ms.append(m_new)
        pvs = [_dot(vt_ref[hh, :, pl.ds(k0, tk)], ps[hh]) for hh in range(nh)]
        return tuple((ms[hh], alphas[hh] * state[hh][1] + pvs[hh]) for hh in range(nh))

    km = FOX_KMULT
    init = tuple((jnp.full((1, t), NEG, F32), jnp.zeros((FOX_VROWS, t), F32)) for _ in range(nh))
    def wide_pair(j, state):
        state = update(j * 2 * km * t, km * t, state, False)
        return update((2 * j + 1) * km * t, km * t, state, False)

    nw = qi // km
    state = lax.fori_loop(0, nw // 2, wide_pair, init)
    state = lax.fori_loop((nw // 2) * 2, nw, lambda j, state: update(j * km * t, km * t, state, False), state)
    state = lax.fori_loop(nw * km, qi, lambda j, state: update(j * t, t, state, False), state)
    state = update(qi * t, t, state, True)
    ot = jnp.concatenate([acc[0:hd] / acc[hd:hd + 1] for _, acc in state], axis=0)
    o_ref[...] = ot.T.astype(BF16)


def _fox_call(p3d, gc, t):
    b, seq, _ = p3d.shape
    w = FOX_HEADS * B_HEAD_DIM
    return pl.pallas_call(
        functools.partial(_fox_kernel, t=t, seq=seq),
        grid=(b, B_HEADS // FOX_HEADS, seq // t),
        in_specs=[
            pl.BlockSpec((None, t, w), lambda bi, hg, qi: (bi, qi, COL_BQ // w + hg)),
            pl.BlockSpec((None, seq, w), lambda bi, hg, qi: (bi, 0, COL_BK // w + hg)),
            pl.BlockSpec((None, seq, w), lambda bi, hg, qi: (bi, 0, COL_BV // w + hg)),
            pl.BlockSpec((None, seq, LANES), lambda bi, hg, qi: (bi, 0, 0)),
        ],
        out_specs=pl.BlockSpec((None, t, w), lambda bi, hg, qi: (bi, qi, hg)),
        out_shape=jax.ShapeDtypeStruct((b, seq, BRANCH_WIDTH), BF16),
        scratch_shapes=[
            pltpu.VMEM((FOX_HEADS, seq, LANES), BF16),
            pltpu.VMEM((FOX_HEADS, FOX_VROWS, seq), BF16),
        ],
        compiler_params=_cparams(("arbitrary", "arbitrary", "arbitrary")),
        name="fox",
    )(p3d, p3d, p3d, gc)


MLSTM_ROWS = 144


def _mlstm_kernel(q_ref, k_ref, v_ref, o_ref, u_ref, gr_ref, gsum_ref, gmh_ref, y_ref, c_ref, m_ref,
                  *, ts):
    L = MLSTM_CHUNK
    dh = C_HEAD_DIM
    scale = dh ** -0.5
    si = pl.program_id(1)

    @pl.when(si == 0)
    def _():
        c_ref[...] = jnp.zeros_like(c_ref)
        m_ref[...] = jnp.zeros_like(m_ref)

    row = lax.broadcasted_iota(jnp.int32, (L, L), 0)
    col = lax.broadcasted_iota(jnp.int32, (L, L), 1)
    causal = row <= col
    sub = lax.broadcasted_iota(jnp.int32, (MLSTM_ROWS - dh, L), 0)
    tail = jnp.where(sub == 0, 1.0, 0.0)

    def chunk(c, _):
        r0 = pl.multiple_of(c * L, L)
        gsum = gsum_ref[c]
        for h in range(C_HEADS):
            hs = slice(h * dh, (h + 1) * dh)
            q = q_ref[pl.ds(r0, L), hs]
            k = k_ref[pl.ds(r0, L), hs]
            vt = jnp.concatenate([v_ref[pl.ds(r0, L), hs].astype(F32).T, tail], axis=0)
            ig = gr_ref[pl.ds(LANE_CI + h, 1), pl.ds(r0, L)]
            b = gr_ref[pl.ds(LANE_CF + h, 1), pl.ds(r0, L)]
            g = gsum[h:h + 1, :]
            m_loc = gsum[C_HEADS + h:C_HEADS + h + 1, :]
            m_in = m_ref[h]
            c_in = c_ref[h]

            dlog = jnp.where(causal, u_ref[h, pl.ds(r0, L), :] + b, NEG)
            inter = b + m_in
            m_t = jnp.maximum(jnp.max(dlog, axis=0, keepdims=True), inter)
            sm = (_dot_nt(k, q) * scale) * jnp.exp(dlog - m_t)
            w_int = jnp.exp(inter - m_t)
            ext = _dot(vt.astype(BF16), sm.astype(BF16)) + w_int * _dot_nt(c_in.astype(BF16), q)
            hh = ext[0:dh] / jnp.maximum(jnp.abs(ext[dh:dh + 1]), jnp.exp(-m_t))
            hn = hh * lax.rsqrt(jnp.mean(hh * hh, axis=0, keepdims=True) + EPS) * gmh_ref[h]
            y_ref[pl.ds(r0, L), hs] = (hn.T * o_ref[pl.ds(r0, L), hs].astype(F32)).astype(BF16)

            m_new = jnp.maximum(g + m_in, m_loc)
            w = jnp.exp(g + (ig - b) - m_new) * scale
            c_ref[h] = jnp.exp(g + m_in - m_new) * c_in + _dot((vt * w).astype(BF16), k)
            m_ref[h] = m_new
        return 0

    lax.fori_loop(0, ts // L, chunk, 0, unroll=8)


def _mlstm_call(p3d, u, gr, gsum, gmh_rep, layer, ts):
    b, seq, _ = p3d.shape
    w = BRANCH_WIDTH
    nc = ts // MLSTM_CHUNK

    def pspec(col):
        return pl.BlockSpec((None, ts, w), lambda bi, si: (bi, si, col // w))

    return pl.pallas_call(
        functools.partial(_mlstm_kernel, ts=ts),
        grid=(b, seq // ts),
        in_specs=[
            pspec(COL_CQ), pspec(COL_CK), pspec(COL_CV), pspec(COL_CO),
            pl.BlockSpec((None, C_HEADS, ts, LANES), lambda bi, si: (bi, 0, si, 0)),
            pl.BlockSpec((None, GATE_ROWS, ts), lambda bi, si: (bi, 0, si)),
            pl.BlockSpec((None, nc, 2 * C_HEADS, LANES), lambda bi, si: (bi, si, 0, 0)),
            _const_spec((C_HEADS, C_HEAD_DIM, LANES), layer),
        ],
        out_specs=pl.BlockSpec((None, ts, w), lambda bi, si: (bi, si, 0)),
        out_shape=jax.ShapeDtypeStruct((b, seq, w), BF16),
        scratch_shapes=[
            pltpu.VMEM((C_HEADS, MLSTM_ROWS, C_HEAD_DIM), F32),
            pltpu.VMEM((C_HEADS, 1, LANES), F32),
        ],
        compiler_params=_cparams(("arbitrary", "arbitrary")),
        name="mlstm",
    )(p3d, p3d, p3d, p3d, u, gr, gsum, gmh_rep)


def _merge_kernel(gates_ref, uv_ref, yb_ref, yc_ref, x_ref, gsgu_ref, ws_ref, bs_ref, wb_ref, wo_ref,
                  gq_ref, wq_ref, kv_ref, wmo_ref, out_ref, ya_ref, o_ref, *, tm):
    w = BRANCH_WIDTH
    u = uv_ref[:, :w].astype(F32)
    v = uv_ref[:, w:].astype(F32)
    vn = _rms(v, gsgu_ref[...]).astype(BF16)
    gd = w // A_GROUPS
    for nb in range(tm // A_BLOCK):
        rs = slice(nb * A_BLOCK, (nb + 1) * A_BLOCK)
        mixed = jnp.concatenate(
            [_dot(ws_ref[g], vn[rs, g * gd:(g + 1) * gd]) for g in range(A_GROUPS)], axis=1)
        ya_ref[rs, :] = (u[rs, :] * (mixed + bs_ref[...])).astype(BF16)
    merged = gates_ref[:, 0:D_MODEL].astype(F32) * _dot(ya_ref[...], wb_ref[0])
    merged += gates_ref[:, D_MODEL:2 * D_MODEL].astype(F32) * _dot(yb_ref[...], wb_ref[1])
    merged += gates_ref[:, 2 * D_MODEL:3 * D_MODEL].astype(F32) * _dot(yc_ref[...], wb_ref[2])
    x = x_ref[...] + _dot(merged.astype(BF16), wo_ref[...])

    h = _rms(x, gq_ref[...]).astype(BF16)
    q = (_dot(h, wq_ref[...]) * (MEM_HEAD_DIM ** -0.5)).astype(BF16)
    dh = MEM_HEAD_DIM
    for hd in range(MEM_HEADS):
        hs = slice(hd * dh, (hd + 1) * dh)
        s = _dot_nt(q[:, hs], kv_ref[:, hs])
        p = jnp.exp(s - jnp.max(s, axis=-1, keepdims=True))
        o = _dot(p.astype(BF16), kv_ref[:, D_MODEL + hd * dh:D_MODEL + (hd + 1) * dh])
        o_ref[:, hs] = (o / jnp.sum(p, axis=-1, keepdims=True)).astype(BF16)
    out_ref[...] = x + _dot(o_ref[...], wmo_ref[...])


def _merge_call(p2d, yb, yc, x2d, g_sgu, ws_masked, bs_full, w_branch, w_out, g_mq, w_mq, kv, w_mo,
                layer, seq, tm):
    m = x2d.shape[0]
    w = BRANCH_WIDTH
    return pl.pallas_call(
        functools.partial(_merge_kernel, tm=tm),
        grid=(m // tm,),
        in_specs=[
            pl.BlockSpec((tm, 3 * D_MODEL), lambda i: (i, COL_G // (3 * D_MODEL))),
            pl.BlockSpec((tm, 2 * w), lambda i: (i, COL_A // (2 * w))),
            pl.BlockSpec((tm, w), lambda i: (i, 0)),
            pl.BlockSpec((tm, w), lambda i: (i, 0)),
            pl.BlockSpec((tm, D_MODEL), lambda i: (i, 0)),
            _const_spec((1, w), layer),
            _const_spec((A_GROUPS, A_BLOCK, A_BLOCK), layer),
            _const_spec((A_BLOCK, w), layer),
            _const_spec((3, w, D_MODEL), layer),
            _const_spec((D_MODEL, D_MODEL), layer),
            _const_spec((1, D_MODEL), layer),
            _const_spec((D_MODEL, D_MODEL), layer),
            pl.BlockSpec((N_MEM, 2 * D_MODEL), lambda i: ((i * tm) // seq, 0)),
            _const_spec((D_MODEL, D_MODEL), layer),
        ],
        out_specs=pl.BlockSpec((tm, D_MODEL), lambda i: (i, 0)),
        out_shape=jax.ShapeDtypeStruct((m, D_MODEL), F32),
        scratch_shapes=[pltpu.VMEM((tm, w), BF16), pltpu.VMEM((tm, D_MODEL), BF16)],
        compiler_params=_cparams(("arbitrary",)),
        name="merge",
    )(p2d, p2d, yb, yc, x2d, g_sgu, ws_masked, bs_full, w_branch, w_out, g_mq, w_mq, kv, w_mo)


def _memkv_kernel(mem_ref, g_ref, w_ref, kv_ref):
    kv_ref[...] = _dot(_rms(mem_ref[...], g_ref[...]).astype(BF16), w_ref[...]).astype(BF16)


def _memkv_call(mem2d, g, w_mkv, layer):
    m = mem2d.shape[0]
    return pl.pallas_call(
        _memkv_kernel,
        grid=(m // N_MEM,),
        in_specs=[
            pl.BlockSpec((N_MEM, D_MODEL), lambda i: (i, 0)),
            _const_spec((1, D_MODEL), layer),
            _const_spec((D_MODEL, 2 * D_MODEL), layer),
        ],
        out_specs=pl.BlockSpec((N_MEM, 2 * D_MODEL), lambda i: (i, 0)),
        out_shape=jax.ShapeDtypeStruct((m, 2 * D_MODEL), BF16),
        compiler_params=_cparams(("arbitrary",)),
        name="memkv",
    )(mem2d, g, w_mkv)


def _ffn_kernel(x_ref, g_ref, wup_ref, wconv_ref, wdown_ref, gfin_ref, out_ref,
                act_ref, ext_ref, carry_ref, *, tm, tf, seq, final_norm):
    i = pl.program_id(0)
    x = x_ref[...]
    h = _rms(x, g_ref[...]).astype(BF16)
    seq_start = (i * tm) % seq == 0

    def conv(slot, half, cs):
        up = _dot(h, wup_ref[:, cs])
        ext_ref[half, pl.ds(CARRY_ROWS, tm), :] = up
        ext_ref[half, pl.ds(0, CARRY_ROWS), :] = jnp.where(seq_start, 0.0, carry_ref[slot])
        carry_ref[slot] = up[tm - CARRY_ROWS:, :]
        wc = wconv_ref[:, cs]
        y = wc[FFN_CONV - 1:FFN_CONV, :] * up
        for d in range(1, FFN_CONV):
            y = y + wc[FFN_CONV - 1 - d:FFN_CONV - d, :] * ext_ref[half, pl.ds(CARRY_ROWS - d, tm), :]
        return y

    nchunk = D_FF // tf
    for c in range(nchunk):
        a = conv(c, 0, slice(c * tf, (c + 1) * tf))
        b = conv(nchunk + c, 1, slice(D_FF + c * tf, D_FF + (c + 1) * tf))
        act_ref[:, c * tf:(c + 1) * tf] = (a * _sigmoid(a) * b).astype(BF16)
    y = x + _dot(act_ref[...], wdown_ref[...])
    if final_norm:
        y = _rms(y, gfin_ref[...])
    out_ref[...] = y


def _ffn_call(x2d, g, w_up, w_conv, w_down, g_final, layer, seq, tm, tf, final_norm):
    m = x2d.shape[0]
    return pl.pallas_call(
        functools.partial(_ffn_kernel, tm=tm, tf=tf, seq=seq, final_norm=final_norm),
        grid=(m // tm,),
        in_specs=[
            pl.BlockSpec((tm, D_MODEL), lambda i: (i, 0)),
            _const_spec((1, D_MODEL), layer),
            _const_spec((D_MODEL, 2 * D_FF), layer),
            _const_spec((FFN_CONV, 2 * D_FF), layer),
            _const_spec((D_FF, D_MODEL), layer),
            _const_spec((1, D_MODEL)),
        ],
        out_specs=pl.BlockSpec((tm, D_MODEL), lambda i: (i, 0)),
        out_shape=jax.ShapeDtypeStruct((m, D_MODEL), F32),
        scratch_shapes=[
            pltpu.VMEM((tm, D_FF), BF16),
            pltpu.VMEM((2, tm + CARRY_ROWS, tf), F32),
            pltpu.VMEM((2 * (D_FF // tf), CARRY_ROWS, tf), F32),
        ],
        compiler_params=_cparams(("arbitrary",)),
        name="ffn",
    )(x2d, g, w_up, w_conv, w_down, g_final)


def _rearrange_w_in(w):
    bw = BRANCH_WIDTH
    a0 = 0
    b0 = 2 * bw
    c0 = b0 + 3 * bw + B_HEADS
    g0 = c0 + 3 * bw + 2 * C_HEADS + bw
    co = c0 + 3 * bw + 2 * C_HEADS
    w = w.astype(BF16)
    main = jnp.concatenate([
        w[..., g0:g0 + 3 * D_MODEL],
        w[..., a0:a0 + 2 * bw],
        w[..., b0:b0 + 3 * bw],
        w[..., c0:c0 + 3 * bw],
        w[..., co:co + bw],
    ], axis=-1)
    small = jnp.concatenate([
        w[..., b0 + 3 * bw:b0 + 3 * bw + B_HEADS],
        w[..., c0 + 3 * bw:c0 + 3 * bw + 2 * C_HEADS],
        jnp.zeros(w.shape[:-1] + (LANES - GATE_ROWS,), w.dtype),
    ], axis=-1)
    return main, small


def kernel(x, mem, g_mix, w_in, g_sgu, w_s, b_s, b_fox_f, w_conv_c, b_mlstm_i, b_mlstm_f, g_mh,
           w_branch, w_out, g_mem_q, g_mem_kv, w_mq, w_mkv, w_mo, g_ffn, w_up, w_ffn_conv, w_down,
           g_final):
    bsz, seq, _ = x.shape
    depth = w_in.shape[0]
    m = bsz * seq
    tm = min(512, seq)
    tm_proj = tm
    tm_ffn = tm
    t_fox = min(256, seq)
    ts_mlstm = min(1024, seq)

    idx = jnp.arange(A_BLOCK)
    chunk_causal = (idx[None, :] // CHUNK) <= (idx[:, None] // CHUNK)

    x2d = x.reshape(m, D_MODEL)
    mem2d = mem.reshape(bsz * N_MEM, D_MODEL)
    rows = lambda a: a.reshape(depth, 1, -1)

    w_main, w_small = _rearrange_w_in(w_in)
    gate_bias = jnp.concatenate(
        [b_fox_f, b_mlstm_i, b_mlstm_f, jnp.zeros((depth, LANES - GATE_ROWS), F32)], axis=-1)
    ws_masked = jnp.where(chunk_causal, w_s, 0).astype(BF16)
    bs_full = jnp.repeat(jnp.swapaxes(b_s, 1, 2), BRANCH_WIDTH // A_GROUPS, axis=2)
    gmh_rep = jnp.broadcast_to(g_mh.reshape(depth, C_HEADS, C_HEAD_DIM, 1),
                               (depth, C_HEADS, C_HEAD_DIM, LANES))
    w_branch, w_out, w_mq, w_mkv, w_mo, w_up, w_down = (
        a.astype(BF16) for a in (w_branch, w_out, w_mq, w_mkv, w_mo, w_up, w_down))

    for i in range(depth):
        p2d, gs = _proj_call(x2d, rows(g_mix), w_main, w_small, w_conv_c, i, seq, tm_proj)
        p3d = p2d.reshape(bsz, seq, PROJ_COLS)
        gc, gr, u, gsum = _gates_call(gs.reshape(bsz, seq, LANES), rows(gate_bias), i)
        yb = _fox_call(p3d, gc, t_fox).reshape(m, BRANCH_WIDTH)
        yc = _mlstm_call(p3d, u, gr, gsum, gmh_rep, i, ts_mlstm).reshape(m, BRANCH_WIDTH)
        kv = _memkv_call(mem2d, rows(g_mem_kv), w_mkv, i)
        x2d = _merge_call(p2d, yb, yc, x2d, rows(g_sgu), ws_masked, bs_full, w_branch, w_out,
                          rows(g_mem_q), w_mq, kv, w_mo, i, seq, tm)
        x2d = _ffn_call(x2d, rows(g_ffn), w_up, w_ffn_conv, w_down, g_final.reshape(1, -1), i, seq, tm_ffn,
                        256, i == depth - 1)
    return x2d.reshape(bsz, seq, D_MODEL)
```

```python
import functools

import jax
import jax.numpy as jnp
from jax import lax
from jax.experimental import pallas as pl
from jax.experimental.pallas import tpu as pltpu

F32 = jnp.float32
BF16 = jnp.bfloat16

D_MODEL = 1024
EPS = 1e-6
LANES = 128
CARRY_ROWS = 8

BRANCH_WIDTH = 512
A_BLOCK = 128
A_GROUPS = 4
CHUNK = 64
B_HEADS = 8
B_HEAD_DIM = 64
C_HEADS = 4
C_HEAD_DIM = 128
C_CONV = 4
MLSTM_CHUNK = 128
N_MEM = 256
MEM_HEADS = 4
MEM_HEAD_DIM = 256
D_FF = 2816
FFN_CONV = 3

PROJ_TN = 512
COL_G = 0
COL_A = 3072
COL_BQ = 4096
COL_BK = 4608
COL_BV = 5120
COL_CQ = 5632
COL_CK = 6144
COL_CV = 6656
COL_CO = 7168
PROJ_COLS = 7680
LANE_BF = 0
LANE_CI = 8
LANE_CF = 12
GATE_ROWS = 16

LOG2E = 1.4426950408889634
FOX_Q_SCALE = B_HEAD_DIM ** -0.5 * LOG2E
NEG = -1e30
VMEM_LIMIT = 56 * 1024 * 1024


def _cparams(sem):
    return pltpu.CompilerParams(dimension_semantics=sem, vmem_limit_bytes=VMEM_LIMIT)


def _rms(xf, g):
    return xf * lax.rsqrt(jnp.mean(xf * xf, axis=-1, keepdims=True) + EPS) * g


def _sigmoid(x):
    return 0.5 * jnp.tanh(0.5 * x) + 0.5


def _gelu_tanh(x):
    return 0.5 * x * (1.0 + jnp.tanh(0.7978845608028654 * (x + 0.044715 * (x * x * x))))


def _dot(a, b):
    return jnp.dot(a, b, preferred_element_type=F32)


def _dot_nt(a, b):
    return lax.dot_general(a, b, (((1,), (1,)), ((), ())), preferred_element_type=F32)


def _const_spec(shape, layer=None):
    nd = len(shape)
    if layer is None:
        return pl.BlockSpec(shape, lambda *_: (0,) * nd, pipeline_mode=pl.Buffered(1))
    return pl.BlockSpec((None,) + tuple(shape), lambda *_: (layer,) + (0,) * nd,
                        pipeline_mode=pl.Buffered(1))


def _proj_kernel(x_ref, g_ref, w_ref, ws_ref, wc_ref, p_ref, gs_ref, ext_ref, carry_ref, *, tm, seq):
    i = pl.program_id(0)
    h = _rms(x_ref[...], g_ref[...]).astype(BF16)
    gs_ref[...] = _dot(h, ws_ref[...])
    seq_start = (i * tm) % seq == 0

    def conv_silu(acc, slot):
        ext_ref[slot, pl.ds(CARRY_ROWS, tm), :] = acc
        ext_ref[slot, pl.ds(0, CARRY_ROWS), :] = jnp.where(seq_start, 0.0, carry_ref[slot])
        carry_ref[slot] = ext_ref[slot, pl.ds(tm, CARRY_ROWS), :]
        wc = 0.5 * wc_ref[:, slot * PROJ_TN:(slot + 1) * PROJ_TN]
        half = wc[C_CONV - 1:C_CONV, :] * ext_ref[slot, pl.ds(CARRY_ROWS, tm), :]
        for d in range(1, C_CONV):
            half = half + wc[C_CONV - 1 - d:C_CONV - d, :] * ext_ref[slot, pl.ds(CARRY_ROWS - d, tm), :]
        return half * (jnp.tanh(half) + 1.0)

    order = [COL_CQ, COL_CK] + [c for c in range(0, PROJ_COLS, PROJ_TN) if c not in (COL_CQ, COL_CK)]
    for c0 in order:
        cs = slice(c0, c0 + PROJ_TN)
        acc = _dot(h, w_ref[:, cs])
        if c0 < COL_A or c0 == COL_CO:
            out = _sigmoid(acc)
        elif c0 < COL_BQ:
            out = _gelu_tanh(acc)
        elif c0 == COL_BQ:
            out = acc * FOX_Q_SCALE
        elif c0 == COL_CQ:
            out = conv_silu(acc, 0)
        elif c0 == COL_CK:
            out = conv_silu(acc, 1)
        else:
            out = acc
        p_ref[:, cs] = out.astype(BF16)


def _proj_call(x2d, g, w_main, w_small, w_conv, layer, seq, tm):
    m = x2d.shape[0]
    return pl.pallas_call(
        functools.partial(_proj_kernel, tm=tm, seq=seq),
        grid=(m // tm,),
        in_specs=[
            pl.BlockSpec((tm, D_MODEL), lambda i: (i, 0)),
            _const_spec((1, D_MODEL), layer),
            _const_spec((D_MODEL, PROJ_COLS), layer),
            _const_spec((D_MODEL, LANES), layer),
            _const_spec((C_CONV, 2 * PROJ_TN), layer),
        ],
        out_specs=[
            pl.BlockSpec((tm, PROJ_COLS), lambda i: (i, 0)),
            pl.BlockSpec((tm, LANES), lambda i: (i, 0)),
        ],
        out_shape=[
            jax.ShapeDtypeStruct((m, PROJ_COLS), BF16),
            jax.ShapeDtypeStruct((m, LANES), F32),
        ],
        scratch_shapes=[
            pltpu.VMEM((2, tm + CARRY_ROWS, PROJ_TN), F32),
            pltpu.VMEM((2, CARRY_ROWS, PROJ_TN), F32),
        ],
        compiler_params=_cparams(("arbitrary",)),
        name="proj",
    )(x2d, g, w_main, w_small, w_conv)


def _gates_kernel(gs_ref, bias_ref, gc_ref, gr_ref, u_ref, gsum_ref, *, seq):
    blk = MLSTM_CHUNK
    row = lax.broadcasted_iota(jnp.int32, (blk, blk), 0)
    col = lax.broadcasted_iota(jnp.int32, (blk, blk), 1)
    tri = (col <= row).astype(F32)
    lane = lax.broadcasted_iota(jnp.int32, (1, LANES), 1)

    def body(r, carry):
        r0 = pl.multiple_of(r * blk, blk)
        raw = gs_ref[pl.ds(r0, blk), :] + bias_ref[...]
        logsig = jnp.minimum(raw, 0.0) - jnp.log1p(jnp.exp(-jnp.abs(raw)))
        local = jnp.dot(tri, logsig, precision=lax.Precision.HIGHEST, preferred_element_type=F32)
        glob = local + carry
        out = jnp.where(lane < LANE_CI, glob, jnp.where(lane < LANE_CF, raw, local))
        gc_ref[pl.ds(r0, blk), :] = out
        gr_ref[:, pl.ds(r0, blk)] = out.T[0:GATE_ROWS, :]
        g_rows, mloc_rows = [], []
        for h in range(C_HEADS):
            d = out[:, LANE_CI + h:LANE_CI + h + 1] - out[:, LANE_CF + h:LANE_CF + h + 1]
            u = jnp.broadcast_to(d, (blk, LANES))
            u_ref[h, pl.ds(r0, blk), :] = u
            g = jnp.broadcast_to(out[blk - 1:blk, LANE_CF + h:LANE_CF + h + 1], (1, LANES))
            g_rows.append(g)
            mloc_rows.append(g + jnp.max(u, axis=0, keepdims=True))
        gsum_ref[r] = jnp.concatenate(g_rows + mloc_rows, axis=0)
        return glob[blk - 1:blk, :]

    lax.fori_loop(0, seq // blk, body, jnp.zeros((1, LANES), F32), unroll=4)


def _gates_call(gs3d, bias, layer):
    b, seq, _ = gs3d.shape
    nchunk = seq // MLSTM_CHUNK
    return pl.pallas_call(
        functools.partial(_gates_kernel, seq=seq),
        grid=(b,),
        in_specs=[
            pl.BlockSpec((None, seq, LANES), lambda bi: (bi, 0, 0)),
            _const_spec((1, LANES), layer),
        ],
        out_specs=[
            pl.BlockSpec((None, seq, LANES), lambda bi: (bi, 0, 0)),
            pl.BlockSpec((None, GATE_ROWS, seq), lambda bi: (bi, 0, 0)),
            pl.BlockSpec((None, C_HEADS, seq, LANES), lambda bi: (bi, 0, 0, 0)),
            pl.BlockSpec((None, nchunk, 2 * C_HEADS, LANES), lambda bi: (bi, 0, 0, 0)),
        ],
        out_shape=[
            jax.ShapeDtypeStruct((b, seq, LANES), F32),
            jax.ShapeDtypeStruct((b, GATE_ROWS, seq), F32),
            jax.ShapeDtypeStruct((b, C_HEADS, seq, LANES), F32),
            jax.ShapeDtypeStruct((b, nchunk, 2 * C_HEADS, LANES), F32),
        ],
        compiler_params=_cparams(("arbitrary",)),
        name="gates",
    )(gs3d, bias)


FOX_VROWS = 80
FOX_KMULT = 2
FOX_HEADS = 8


def _fox_kernel(q_ref, k_ref, v_ref, gc_ref, place_ref, o_ref, kaug_ref, vt_ref, *, t, seq):
    qi = pl.program_id(2)
    hd = B_HEAD_DIM
    nh = FOX_HEADS
    lane = lax.broadcasted_iota(jnp.int32, (1, LANES), 1)
    own = (lane < hd, lane >= hd)
    aug0 = (hd, 0)

    @pl.when(qi == 0)
    def _():
        sub = lax.broadcasted_iota(jnp.int32, (FOX_VROWS - hd, seq), 0)
        tail = jnp.where(sub == 0, 1.0, 0.0).astype(BF16)
        for hh in range(nh):
            vt_ref[hh, hd:FOX_VROWS, :] = tail
        eye = (lax.broadcasted_iota(jnp.int32, (LANES, LANES), 0)
               == lax.broadcasted_iota(jnp.int32, (LANES, LANES), 1)).astype(F32).astype(BF16)

        def body(r, _):
            r0 = pl.multiple_of(r * LANES, LANES)
            neg = -LOG2E * gc_ref[pl.ds(r0, LANES), :]
            hi = neg.astype(BF16).astype(F32)
            mid = (neg - hi).astype(BF16).astype(F32)
            lo = (neg - hi) - mid
            pieces = jnp.where(lane < B_HEADS, hi,
                               jnp.where(lane < 2 * B_HEADS, pltpu.roll(mid, B_HEADS, 1),
                                         pltpu.roll(lo, 2 * B_HEADS, 1))).astype(BF16)
            extra = _dot(pieces, place_ref[...]).astype(BF16)
            for pp in range(nh // 2):
                ls = slice(pp * LANES, (pp + 1) * LANES)
                kb = k_ref[pl.ds(r0, LANES), ls]
                vt = _dot_nt(eye, v_ref[pl.ds(r0, LANES), ls])
                for h in range(2):
                    hh = 2 * pp + h
                    kaug_ref[hh, pl.ds(r0, LANES), :] = jnp.where(
                        own[h], kb, extra[:, hh * LANES:(hh + 1) * LANES])
                    vt_ref[hh, 0:hd, pl.ds(r0, LANES)] = vt[h * hd:(h + 1) * hd, :].astype(BF16)
            return 0

        lax.fori_loop(0, seq // LANES, body, 0, unroll=4)

    qa = []
    for pp in range(nh // 2):
        q = q_ref[:, pp * LANES:(pp + 1) * LANES].astype(F32)
        for h in range(2):
            a = aug0[h]
            ones3 = jnp.where((lane >= a) & (lane < a + 3), 1.0, 0.0)
            qa.append(jnp.where(own[h], q, ones3).astype(BF16))
    row = lax.broadcasted_iota(jnp.int32, (t, t), 0)
    col = lax.broadcasted_iota(jnp.int32, (t, t), 1)
    causal = row <= col

    def qk(hh, k0, tk):
        return _dot_nt(kaug_ref[hh, pl.ds(k0, tk), :], qa[hh])

    def update(k0, tk, state, masked):
        k0 = pl.multiple_of(k0, t)
        sts = [qk(hh, k0, tk) for hh in range(nh)]
        ms, ps, alphas = [], [], []
        for hh in range(nh):
            m = state[hh][0]
            s = jnp.where(causal, sts[hh], NEG) if masked else sts[hh]
            m_new = jnp.maximum(m, jnp.max(s, axis=0, keepdims=True))
            ps.append(jnp.exp2((s - m_new).astype(BF16)))
            alphas.append(jnp.exp2(m - m_new))
            ms.append(m_new)
        pvs = [_dot(vt_ref[hh, :, pl.ds(k0, tk)], ps[hh]) for hh in range(nh)]
        return tuple((ms[hh], alphas[hh] * state[hh][1] + pvs[hh]) for hh in range(nh))

    km = FOX_KMULT
    init = tuple((jnp.full((1, t), NEG, F32), jnp.zeros((FOX_VROWS, t), F32)) for _ in range(nh))
    def wide_pair(j, state):
        state = update(j * 2 * km * t, km * t, state, False)
        return update((2 * j + 1) * km * t, km * t, state, False)

    nw = qi // km
    state = lax.fori_loop(0, nw // 2, wide_pair, init)
    state = lax.fori_loop((nw // 2) * 2, nw, lambda j, state: update(j * km * t, km * t, state, False), state)
    state = lax.fori_loop(nw * km, qi, lambda j, state: update(j * t, t, state, False), state)
    state = update(qi * t, t, state, True)
    ot = jnp.concatenate([acc[0:hd] / acc[hd:hd + 1] for _, acc in state], axis=0)
    o_ref[...] = ot.T.astype(BF16)


def _fox_place_matrix():
    assert FOX_HEADS == B_HEADS and LANE_BF == 0
    r = jnp.arange(LANES)[:, None]
    c = jnp.arange(B_HEADS * LANES)[None, :]
    head = c // LANES
    aug0 = jnp.where(head % 2 == 0, B_HEAD_DIM, 0)
    piece = c % LANES - aug0
    return ((piece >= 0) & (piece < 3) & (r == piece * B_HEADS + head)).astype(BF16)


def _fox_call(p3d, gc, t):
    b, seq, _ = p3d.shape
    w = FOX_HEADS * B_HEAD_DIM
    return pl.pallas_call(
        functools.partial(_fox_kernel, t=t, seq=seq),
        grid=(b, B_HEADS // FOX_HEADS, seq // t),
        in_specs=[
            pl.BlockSpec((None, t, w), lambda bi, hg, qi: (bi, qi, COL_BQ // w + hg)),
            pl.BlockSpec((None, seq, w), lambda bi, hg, qi: (bi, 0, COL_BK // w + hg)),
            pl.BlockSpec((None, seq, w), lambda bi, hg, qi: (bi, 0, COL_BV // w + hg)),
            pl.BlockSpec((None, seq, LANES), lambda bi, hg, qi: (bi, 0, 0)),
            _const_spec((LANES, FOX_HEADS * LANES)),
        ],
        out_specs=pl.BlockSpec((None, t, w), lambda bi, hg, qi: (bi, qi, hg)),
        out_shape=jax.ShapeDtypeStruct((b, seq, BRANCH_WIDTH), BF16),
        scratch_shapes=[
            pltpu.VMEM((FOX_HEADS, seq, LANES), BF16),
            pltpu.VMEM((FOX_HEADS, FOX_VROWS, seq), BF16),
        ],
        compiler_params=_cparams(("arbitrary", "arbitrary", "arbitrary")),
        name="fox",
    )(p3d, p3d, p3d, gc, _fox_place_matrix())


MLSTM_ROWS = 144


def _mlstm_kernel(q_ref, k_ref, v_ref, o_ref, u_ref, gr_ref, gsum_ref, gmh_ref, y_ref, c_ref, m_ref,
                  *, ts):
    L = MLSTM_CHUNK
    dh = C_HEAD_DIM
    scale = dh ** -0.5
    si = pl.program_id(1)

    @pl.when(si == 0)
    def _():
        c_ref[...] = jnp.zeros_like(c_ref)
        m_ref[...] = jnp.zeros_like(m_ref)

    row = lax.broadcasted_iota(jnp.int32, (L, L), 0)
    col = lax.broadcasted_iota(jnp.int32, (L, L), 1)
    causal = row <= col
    sub = lax.broadcasted_iota(jnp.int32, (MLSTM_ROWS - dh, L), 0)
    tail = jnp.where(sub == 0, 1.0, 0.0)

    def chunk(c, _):
        r0 = pl.multiple_of(c * L, L)
        gsum = gsum_ref[c]
        for h in range(C_HEADS):
            hs = slice(h * dh, (h + 1) * dh)
            q = q_ref[pl.ds(r0, L), hs]
            k = k_ref[pl.ds(r0, L), hs]
            vt = jnp.concatenate([v_ref[pl.ds(r0, L), hs].astype(F32).T, tail], axis=0)
            ig = gr_ref[pl.ds(LANE_CI + h, 1), pl.ds(r0, L)]
            b = gr_ref[pl.ds(LANE_CF + h, 1), pl.ds(r0, L)]
            g = gsum[h:h + 1, :]
            m_loc = gsum[C_HEADS + h:C_HEADS + h + 1, :]
            m_in = m_ref[h]
            c_in = c_ref[h]

            dlog = jnp.where(causal, u_ref[h, pl.ds(r0, L), :] + b, NEG)
            inter = b + m_in
            m_t = jnp.maximum(jnp.max(dlog, axis=0, keepdims=True), inter)
            sm = (_dot_nt(k, q) * scale) * jnp.exp(dlog - m_t)
            w_int = jnp.exp(inter - m_t)
            ext = _dot(vt.astype(BF16), sm.astype(BF16)) + w_int * _dot_nt(c_in.astype(BF16), q)
            hh = ext[0:dh] / jnp.maximum(jnp.abs(ext[dh:dh + 1]), jnp.exp(-m_t))
            hn = hh * lax.rsqrt(jnp.mean(hh * hh, axis=0, keepdims=True) + EPS) * gmh_ref[h]
            y_ref[pl.ds(r0, L), hs] = (hn.T * o_ref[pl.ds(r0, L), hs].astype(F32)).astype(BF16)

            m_new = jnp.maximum(g + m_in, m_loc)
            w = jnp.exp(g + (ig - b) - m_new) * scale
            c_ref[h] = jnp.exp(g + m_in - m_new) * c_in + _dot((vt * w).astype(BF16), k)
            m_ref[h] = m_new
        return 0

    lax.fori_loop(0, ts // L, chunk, 0, unroll=8)


def _mlstm_call(p3d, u, gr, gsum, gmh_rep, layer, ts):
    b, seq, _ = p3d.shape
    w = BRANCH_WIDTH
    nc = ts // MLSTM_CHUNK

    def pspec(col):
        return pl.BlockSpec((None, ts, w), lambda bi, si: (bi, si, col // w))

    return pl.pallas_call(
        functools.partial(_mlstm_kernel, ts=ts),
        grid=(b, seq // ts),
        in_specs=[
            pspec(COL_CQ), pspec(COL_CK), pspec(COL_CV), pspec(COL_CO),
            pl.BlockSpec((None, C_HEADS, ts, LANES), lambda bi, si: (bi, 0, si, 0)),
            pl.BlockSpec((None, GATE_ROWS, ts), lambda bi, si: (bi, 0, si)),
            pl.BlockSpec((None, nc, 2 * C_HEADS, LANES), lambda bi, si: (bi, si, 0, 0)),
            _const_spec((C_HEADS, C_HEAD_DIM, LANES), layer),
        ],
        out_specs=pl.BlockSpec((None, ts, w), lambda bi, si: (bi, si, 0)),
        out_shape=jax.ShapeDtypeStruct((b, seq, w), BF16),
        scratch_shapes=[
            pltpu.VMEM((C_HEADS, MLSTM_ROWS, C_HEAD_DIM), F32),
            pltpu.VMEM((C_HEADS, 1, LANES), F32),
        ],
        compiler_params=_cparams(("arbitrary", "arbitrary")),
        name="mlstm",
    )(p3d, p3d, p3d, p3d, u, gr, gsum, gmh_rep)


def _merge_kernel(gates_ref, uv_ref, yb_ref, yc_ref, x_ref, gsgu_ref, ws_ref, bs_ref, wb_ref, wo_ref,
                  gq_ref, wq_ref, kv_ref, wmo_ref, out_ref, ya_ref, o_ref, *, tm):
    w = BRANCH_WIDTH
    u = uv_ref[:, :w].astype(F32)
    v = uv_ref[:, w:].astype(F32)
    vn = _rms(v, gsgu_ref[...]).astype(BF16)
    gd = w // A_GROUPS
    for nb in range(tm // A_BLOCK):
        rs = slice(nb * A_BLOCK, (nb + 1) * A_BLOCK)
        mixed = jnp.concatenate(
            [_dot(ws_ref[g], vn[rs, g * gd:(g + 1) * gd]) for g in range(A_GROUPS)], axis=1)
        ya_ref[rs, :] = (u[rs, :] * (mixed + bs_ref[...])).astype(BF16)
    merged = gates_ref[:, 0:D_MODEL].astype(F32) * _dot(ya_ref[...], wb_ref[0])
    merged += gates_ref[:, D_MODEL:2 * D_MODEL].astype(F32) * _dot(yb_ref[...], wb_ref[1])
    merged += gates_ref[:, 2 * D_MODEL:3 * D_MODEL].astype(F32) * _dot(yc_ref[...], wb_ref[2])
    x = x_ref[...] + _dot(merged.astype(BF16), wo_ref[...])

    h = _rms(x, gq_ref[...]).astype(BF16)
    q = (_dot(h, wq_ref[...]) * (MEM_HEAD_DIM ** -0.5)).astype(BF16)
    dh = MEM_HEAD_DIM
    for hd in range(MEM_HEADS):
        hs = slice(hd * dh, (hd + 1) * dh)
        s = _dot_nt(q[:, hs], kv_ref[:, hs])
        p = jnp.exp(s - jnp.max(s, axis=-1, keepdims=True))
        o = _dot(p.astype(BF16), kv_ref[:, D_MODEL + hd * dh:D_MODEL + (hd + 1) * dh])
        o_ref[:, hs] = (o / jnp.sum(p, axis=-1, keepdims=True)).astype(BF16)
    out_ref[...] = x + _dot(o_ref[...], wmo_ref[...])


def _merge_call(p2d, yb, yc, x2d, g_sgu, ws_masked, bs_full, w_branch, w_out, g_mq, w_mq, kv, w_mo,
                layer, seq, tm):
    m = x2d.shape[0]
    w = BRANCH_WIDTH
    return pl.pallas_call(
        functools.partial(_merge_kernel, tm=tm),
        grid=(m // tm,),
        in_specs=[
            pl.BlockSpec((tm, 3 * D_MODEL), lambda i: (i, COL_G // (3 * D_MODEL))),
            pl.BlockSpec((tm, 2 * w), lambda i: (i, COL_A // (2 * w))),
            pl.BlockSpec((tm, w), lambda i: (i, 0)),
            pl.BlockSpec((tm, w), lambda i: (i, 0)),
            pl.BlockSpec((tm, D_MODEL), lambda i: (i, 0)),
            _const_spec((1, w), layer),
            _const_spec((A_GROUPS, A_BLOCK, A_BLOCK), layer),
            _const_spec((A_BLOCK, w), layer),
            _const_spec((3, w, D_MODEL), layer),
            _const_spec((D_MODEL, D_MODEL), layer),
            _const_spec((1, D_MODEL), layer),
            _const_spec((D_MODEL, D_MODEL), layer),
            pl.BlockSpec((N_MEM, 2 * D_MODEL), lambda i: ((i * tm) // seq, 0)),
            _const_spec((D_MODEL, D_MODEL), layer),
        ],
        out_specs=pl.BlockSpec((tm, D_MODEL), lambda i: (i, 0)),
        out_shape=jax.ShapeDtypeStruct((m, D_MODEL), F32),
        scratch_shapes=[pltpu.VMEM((tm, w), BF16), pltpu.VMEM((tm, D_MODEL), BF16)],
        compiler_params=_cparams(("arbitrary",)),
        name="merge",
    )(p2d, p2d, yb, yc, x2d, g_sgu, ws_masked, bs_full, w_branch, w_out, g_mq, w_mq, kv, w_mo)


def _memkv_kernel(mem_ref, g_ref, w_ref, kv_ref):
    kv_ref[...] = _dot(_rms(mem_ref[...], g_ref[...]).astype(BF16), w_ref[...]).astype(BF16)


def _memkv_call(mem2d, g, w_mkv, layer):
    m = mem2d.shape[0]
    return pl.pallas_call(
        _memkv_kernel,
        grid=(m // N_MEM,),
        in_specs=[
            pl.BlockSpec((N_MEM, D_MODEL), lambda i: (i, 0)),
            _const_spec((1, D_MODEL), layer),
            _const_spec((D_MODEL, 2 * D_MODEL), layer),
        ],
        out_specs=pl.BlockSpec((N_MEM, 2 * D_MODEL), lambda i: (i, 0)),
        out_shape=jax.ShapeDtypeStruct((m, 2 * D_MODEL), BF16),
        compiler_params=_cparams(("arbitrary",)),
        name="memkv",
    )(mem2d, g, w_mkv)


def _ffn_kernel(x_ref, g_ref, wup_ref, wconv_ref, wdown_ref, gfin_ref, out_ref,
                act_ref, ext_ref, carry_ref, *, tm, tf, seq, final_norm):
    i = pl.program_id(0)
    x = x_ref[...]
    h = _rms(x, g_ref[...]).astype(BF16)
    seq_start = (i * tm) % seq == 0

    def conv(slot, half, cs):
        up = _dot(h, wup_ref[:, cs])
        ext_ref[half, pl.ds(CARRY_ROWS, tm), :] = up
        ext_ref[half, pl.ds(0, CARRY_ROWS), :] = jnp.where(seq_start, 0.0, carry_ref[slot])
        carry_ref[slot] = up[tm - CARRY_ROWS:, :]
        wc = wconv_ref[:, cs]
        y = wc[FFN_CONV - 1:FFN_CONV, :] * up
        for d in range(1, FFN_CONV):
            y = y + wc[FFN_CONV - 1 - d:FFN_CONV - d, :] * ext_ref[half, pl.ds(CARRY_ROWS - d, tm), :]
        return y

    nchunk = D_FF // tf
    for c in range(nchunk):
        a = conv(c, 0, slice(c * tf, (c + 1) * tf))
        b = conv(nchunk + c, 1, slice(D_FF + c * tf, D_FF + (c + 1) * tf))
        act_ref[:, c * tf:(c + 1) * tf] = (a * _sigmoid(a) * b).astype(BF16)
    y = x + _dot(act_ref[...], wdown_ref[...])
    if final_norm:
        y = _rms(y, gfin_ref[...])
    out_ref[...] = y


def _ffn_call(x2d, g, w_up, w_conv, w_down, g_final, layer, seq, tm, tf, final_norm):
    m = x2d.shape[0]
    return pl.pallas_call(
        functools.partial(_ffn_kernel, tm=tm, tf=tf, seq=seq, final_norm=final_norm),
        grid=(m // tm,),
        in_specs=[
            pl.BlockSpec((tm, D_MODEL), lambda i: (i, 0)),
            _const_spec((1, D_MODEL), layer),
            _const_spec((D_MODEL, 2 * D_FF), layer),
            _const_spec((FFN_CONV, 2 * D_FF), layer),
            _const_spec((D_FF, D_MODEL), layer),
            _const_spec((1, D_MODEL)),
        ],
        out_specs=pl.BlockSpec((tm, D_MODEL), lambda i: (i, 0)),
        out_shape=jax.ShapeDtypeStruct((m, D_MODEL), F32),
        scratch_shapes=[
            pltpu.VMEM((tm, D_FF), BF16),
            pltpu.VMEM((2, tm + CARRY_ROWS, tf), F32),
            pltpu.VMEM((2 * (D_FF // tf), CARRY_ROWS, tf), F32),
        ],
        compiler_params=_cparams(("arbitrary",)),
        name="ffn",
    )(x2d, g, w_up, w_conv, w_down, g_final)


def _rearrange_w_in(w):
    bw = BRANCH_WIDTH
    a0 = 0
    b0 = 2 * bw
    c0 = b0 + 3 * bw + B_HEADS
    g0 = c0 + 3 * bw + 2 * C_HEADS + bw
    co = c0 + 3 * bw + 2 * C_HEADS
    w = w.astype(BF16)
    main = jnp.concatenate([
        w[..., g0:g0 + 3 * D_MODEL],
        w[..., a0:a0 + 2 * bw],
        w[..., b0:b0 + 3 * bw],
        w[..., c0:c0 + 3 * bw],
        w[..., co:co + bw],
    ], axis=-1)
    small = jnp.concatenate([
        w[..., b0 + 3 * bw:b0 + 3 * bw + B_HEADS],
        w[..., c0 + 3 * bw:c0 + 3 * bw + 2 * C_HEADS],
        jnp.zeros(w.shape[:-1] + (LANES - GATE_ROWS,), w.dtype),
    ], axis=-1)
    return main, small


def kernel(x, mem, g_mix, w_in, g_sgu, w_s, b_s, b_fox_f, w_conv_c, b_mlstm_i, b_mlstm_f, g_mh,
           w_branch, w_out, g_mem_q, g_mem_kv, w_mq, w_mkv, w_mo, g_ffn, w_up, w_ffn_conv, w_down,
           g_final):
    bsz, seq, _ = x.shape
    depth = w_in.shape[0]
    m = bsz * seq
    tm = min(512, seq)
    tm_proj = tm
    tm_ffn = tm
    t_fox = min(256, seq)
    ts_mlstm = min(1024, seq)

    idx = jnp.arange(A_BLOCK)
    chunk_causal = (idx[None, :] // CHUNK) <= (idx[:, None] // CHUNK)

    x2d = x.reshape(m, D_MODEL)
    mem2d = mem.reshape(bsz * N_MEM, D_MODEL)
    rows = lambda a: a.reshape(depth, 1, -1)

    w_main, w_small = _rearrange_w_in(w_in)
    gate_bias = jnp.concatenate(
        [b_fox_f, b_mlstm_i, b_mlstm_f, jnp.zeros((depth, LANES - GATE_ROWS), F32)], axis=-1)
    ws_masked = jnp.where(chunk_causal, w_s, 0).astype(BF16)
    bs_full = jnp.repeat(jnp.swapaxes(b_s, 1, 2), BRANCH_WIDTH // A_GROUPS, axis=2)
    gmh_rep = jnp.broadcast_to(g_mh.reshape(depth, C_HEADS, C_HEAD_DIM, 1),
                               (depth, C_HEADS, C_HEAD_DIM, LANES))
    w_branch, w_out, w_mq, w_mkv, w_mo, w_up, w_down = (
        a.astype(BF16) for a in (w_branch, w_out, w_mq, w_mkv, w_mo, w_up, w_down))

    for i in range(depth):
        p2d, gs = _proj_call(x2d, rows(g_mix), w_main, w_small, w_conv_c, i, seq, tm_proj)
        p3d = p2d.reshape(bsz, seq, PROJ_COLS)
        gc, gr, u, gsum = _gates_call(gs.reshape(bsz, seq, LANES), rows(gate_bias), i)
        yb = _fox_call(p3d, gc, t_fox).reshape(m, BRANCH_WIDTH)
        yc = _mlstm_call(p3d, u, gr, gsum, gmh_rep, i, ts_mlstm).reshape(m, BRANCH_WIDTH)
        kv = _memkv_call(mem2d, rows(g_mem_kv), w_mkv, i)
        x2d = _merge_call(p2d, yb, yc, x2d, rows(g_sgu), ws_masked, bs_full, w_branch, w_out,
                          rows(g_mem_q), w_mq, kv, w_mo, i, seq, tm)
        x2d = _ffn_call(x2d, rows(g_ffn), w_up, w_ffn_conv, w_down, g_final.reshape(1, -1), i, seq, tm_ffn,
                        256, i == depth - 1)
    return x2d.reshape(bsz, seq, D_MODEL)
```

```python
import functools

import jax
import jax.numpy as jnp
from jax import lax
from jax.experimental import pallas as pl
from jax.experimental.pallas import tpu as pltpu

F32 = jnp.float32
BF16 = jnp.bfloat16

D_MODEL = 1024
EPS = 1e-6
LANES = 128
CARRY_ROWS = 8

BRANCH_WIDTH = 512
A_BLOCK = 128
A_GROUPS = 4
CHUNK = 64
B_HEADS = 8
B_HEAD_DIM = 64
C_HEADS = 4
C_HEAD_DIM = 128
C_CONV = 4
MLSTM_CHUNK = 128
N_MEM = 256
MEM_HEADS = 4
MEM_HEAD_DIM = 256
D_FF = 2816
FFN_CONV = 3

PROJ_TN = 512
COL_G = 0
COL_A = 3072
COL_BQ = 4096
COL_BK = 4608
COL_BV = 5120
COL_CQ = 5632
COL_CK = 6144
COL_CV = 6656
COL_CO = 7168
PROJ_COLS = 7680
LANE_BF = 0
LANE_CI = 8
LANE_CF = 12
GATE_ROWS = 16

LOG2E = 1.4426950408889634
FOX_Q_SCALE = B_HEAD_DIM ** -0.5 * LOG2E
NEG = -1e30
VMEM_LIMIT = 56 * 1024 * 1024


def _cparams(sem):
    return pltpu.CompilerParams(dimension_semantics=sem, vmem_limit_bytes=VMEM_LIMIT)


def _rms(xf, g):
    return xf * lax.rsqrt(jnp.mean(xf * xf, axis=-1, keepdims=True) + EPS) * g


def _sigmoid(x):
    return 0.5 * jnp.tanh(0.5 * x) + 0.5


def _gelu_tanh(x):
    return 0.5 * x * (1.0 + jnp.tanh(0.7978845608028654 * (x + 0.044715 * (x * x * x))))


def _dot(a, b):
    return jnp.dot(a, b, preferred_element_type=F32)


def _dot_nt(a, b):
    return lax.dot_general(a, b, (((1,), (1,)), ((), ())), preferred_element_type=F32)


def _const_spec(shape, layer=None):
    nd = len(shape)
    if layer is None:
        return pl.BlockSpec(shape, lambda *_: (0,) * nd, pipeline_mode=pl.Buffered(1))
    return pl.BlockSpec((None,) + tuple(shape), lambda *_: (layer,) + (0,) * nd,
                        pipeline_mode=pl.Buffered(1))


def _proj_kernel(x_ref, g_ref, w_ref, ws_ref, wc_ref, p_ref, gs_ref, ext_ref, carry_ref, *, tm, seq):
    i = pl.program_id(0)
    h = _rms(x_ref[...], g_ref[...]).astype(BF16)
    gs_ref[...] = _dot(h, ws_ref[...])
    seq_start = (i * tm) % seq == 0

    def conv_silu(acc, slot):
        ext_ref[slot, pl.ds(CARRY_ROWS, tm), :] = acc
        ext_ref[slot, pl.ds(0, CARRY_ROWS), :] = jnp.where(seq_start, 0.0, carry_ref[slot])
        carry_ref[slot] = ext_ref[slot, pl.ds(tm, CARRY_ROWS), :]
        wc = 0.5 * wc_ref[:, slot * PROJ_TN:(slot + 1) * PROJ_TN]
        half = wc[C_CONV - 1:C_CONV, :] * ext_ref[slot, pl.ds(CARRY_ROWS, tm), :]
        for d in range(1, C_CONV):
            half = half + wc[C_CONV - 1 - d:C_CONV - d, :] * ext_ref[slot, pl.ds(CARRY_ROWS - d, tm), :]
        return half * (jnp.tanh(half) + 1.0)

    order = [COL_CQ, COL_CK] + [c for c in range(0, PROJ_COLS, PROJ_TN) if c not in (COL_CQ, COL_CK)]
    for c0 in order:
        cs = slice(c0, c0 + PROJ_TN)
        acc = _dot(h, w_ref[:, cs])
        if c0 < COL_A or c0 == COL_CO:
            out = _sigmoid(acc)
        elif c0 < COL_BQ:
            out = _gelu_tanh(acc)
        elif c0 == COL_BQ:
            out = acc * FOX_Q_SCALE
        elif c0 == COL_CQ:
            out = conv_silu(acc, 0)
        elif c0 == COL_CK:
            out = conv_silu(acc, 1)
        else:
            out = acc
        p_ref[:, cs] = out.astype(BF16)


def _proj_call(x2d, g, w_main, w_small, w_conv, layer, seq, tm):
    m = x2d.shape[0]
    return pl.pallas_call(
        functools.partial(_proj_kernel, tm=tm, seq=seq),
        grid=(m // tm,),
        in_specs=[
            pl.BlockSpec((tm, D_MODEL), lambda i: (i, 0)),
            _const_spec((1, D_MODEL), layer),
            _const_spec((D_MODEL, PROJ_COLS), layer),
            _const_spec((D_MODEL, LANES), layer),
            _const_spec((C_CONV, 2 * PROJ_TN), layer),
        ],
        out_specs=[
            pl.BlockSpec((tm, PROJ_COLS), lambda i: (i, 0)),
            pl.BlockSpec((tm, LANES), lambda i: (i, 0)),
        ],
        out_shape=[
            jax.ShapeDtypeStruct((m, PROJ_COLS), BF16),
            jax.ShapeDtypeStruct((m, LANES), F32),
        ],
        scratch_shapes=[
            pltpu.VMEM((2, tm + CARRY_ROWS, PROJ_TN), F32),
            pltpu.VMEM((2, CARRY_ROWS, PROJ_TN), F32),
        ],
        compiler_params=_cparams(("arbitrary",)),
        name="proj",
    )(x2d, g, w_main, w_small, w_conv)


def _gates_kernel(gs_ref, bias_ref, gc_ref, gr_ref, u_ref, gsum_ref, *, seq):
    blk = MLSTM_CHUNK
    row = lax.broadcasted_iota(jnp.int32, (blk, blk), 0)
    col = lax.broadcasted_iota(jnp.int32, (blk, blk), 1)
    tri = (col <= row).astype(F32)
    lane = lax.broadcasted_iota(jnp.int32, (1, LANES), 1)

    def body(r, carry):
        r0 = pl.multiple_of(r * blk, blk)
        raw = gs_ref[pl.ds(r0, blk), :] + bias_ref[...]
        logsig = jnp.minimum(raw, 0.0) - jnp.log1p(jnp.exp(-jnp.abs(raw)))
        local = jnp.dot(tri, logsig, precision=lax.Precision.HIGHEST, preferred_element_type=F32)
        glob = local + carry
        out = jnp.where(lane < LANE_CI, glob, jnp.where(lane < LANE_CF, raw, local))
        gc_ref[pl.ds(r0, blk), :] = out
        gr_ref[:, pl.ds(r0, blk)] = out.T[0:GATE_ROWS, :]
        g_rows, mloc_rows = [], []
        for h in range(C_HEADS):
            d = out[:, LANE_CI + h:LANE_CI + h + 1] - out[:, LANE_CF + h:LANE_CF + h + 1]
            u = jnp.broadcast_to(d, (blk, LANES))
            u_ref[h, pl.ds(r0, blk), :] = u
            g = jnp.broadcast_to(out[blk - 1:blk, LANE_CF + h:LANE_CF + h + 1], (1, LANES))
            g_rows.append(g)
            mloc_rows.append(g + jnp.max(u, axis=0, keepdims=True))
        gsum_ref[r] = jnp.concatenate(g_rows + mloc_rows, axis=0)
        return glob[blk - 1:blk, :]

    lax.fori_loop(0, seq // blk, body, jnp.zeros((1, LANES), F32), unroll=4)


def _gates_call(gs3d, bias, layer):
    b, seq, _ = gs3d.shape
    nchunk = seq // MLSTM_CHUNK
    return pl.pallas_call(
        functools.partial(_gates_kernel, seq=seq),
        grid=(b,),
        in_specs=[
            pl.BlockSpec((None, seq, LANES), lambda bi: (bi, 0, 0)),
            _const_spec((1, LANES), layer),
        ],
        out_specs=[
            pl.BlockSpec((None, seq, LANES), lambda bi: (bi, 0, 0)),
            pl.BlockSpec((None, GATE_ROWS, seq), lambda bi: (bi, 0, 0)),
            pl.BlockSpec((None, C_HEADS, seq, LANES), lambda bi: (bi, 0, 0, 0)),
            pl.BlockSpec((None, nchunk, 2 * C_HEADS, LANES), lambda bi: (bi, 0, 0, 0)),
        ],
        out_shape=[
            jax.ShapeDtypeStruct((b, seq, LANES), F32),
            jax.ShapeDtypeStruct((b, GATE_ROWS, seq), F32),
            jax.ShapeDtypeStruct((b, C_HEADS, seq, LANES), F32),
            jax.ShapeDtypeStruct((b, nchunk, 2 * C_HEADS, LANES), F32),
        ],
        compiler_params=_cparams(("arbitrary",)),
        name="gates",
    )(gs3d, bias)


FOX_VROWS = 80
FOX_KMULT = 2
FOX_HEADS = 8


def _fox_kernel(q_ref, k_ref, v_ref, gc_ref, place_ref, o_ref, kaug_ref, vt_ref, *, t, seq):
    qi = pl.program_id(2)
    hd = B_HEAD_DIM
    nh = FOX_HEADS
    lane = lax.broadcasted_iota(jnp.int32, (1, LANES), 1)
    own = (lane < hd, lane >= hd)
    aug0 = (hd, 0)

    @pl.when(qi == 0)
    def _():
        sub = lax.broadcasted_iota(jnp.int32, (FOX_VROWS - hd, seq), 0)
        tail = jnp.where(sub == 0, 1.0, 0.0).astype(BF16)
        for hh in range(nh):
            vt_ref[hh, hd:FOX_VROWS, :] = tail
        eye = (lax.broadcasted_iota(jnp.int32, (LANES, LANES), 0)
               == lax.broadcasted_iota(jnp.int32, (LANES, LANES), 1)).astype(F32).astype(BF16)

        def body(r, _):
            r0 = pl.multiple_of(r * LANES, LANES)
            neg = -LOG2E * gc_ref[pl.ds(r0, LANES), :]
            hi = neg.astype(BF16).astype(F32)
            mid = (neg - hi).astype(BF16).astype(F32)
            lo = (neg - hi) - mid
            pieces = jnp.where(lane < B_HEADS, hi,
                               jnp.where(lane < 2 * B_HEADS, pltpu.roll(mid, B_HEADS, 1),
                                         pltpu.roll(lo, 2 * B_HEADS, 1))).astype(BF16)
            extra = _dot(pieces, place_ref[...]).astype(BF16)
            for pp in range(nh // 2):
                ls = slice(pp * LANES, (pp + 1) * LANES)
                kb = k_ref[pl.ds(r0, LANES), ls]
                vt = _dot_nt(eye, v_ref[pl.ds(r0, LANES), ls])
                for h in range(2):
                    hh = 2 * pp + h
                    kaug_ref[hh, pl.ds(r0, LANES), :] = jnp.where(
                        own[h], kb, extra[:, hh * LANES:(hh + 1) * LANES])
                    vt_ref[hh, 0:hd, pl.ds(r0, LANES)] = vt[h * hd:(h + 1) * hd, :].astype(BF16)
            return 0

        lax.fori_loop(0, seq // LANES, body, 0, unroll=4)

    qa = []
    for pp in range(nh // 2):
        q = q_ref[:, pp * LANES:(pp + 1) * LANES].astype(F32)
        for h in range(2):
            a = aug0[h]
            ones3 = jnp.where((lane >= a) & (lane < a + 3), 1.0, 0.0)
            qa.append(jnp.where(own[h], q, ones3).astype(BF16))
    row = lax.broadcasted_iota(jnp.int32, (t, t), 0)
    col = lax.broadcasted_iota(jnp.int32, (t, t), 1)
    causal = row <= col

    def qk(hh, k0, tk):
        return _dot_nt(kaug_ref[hh, pl.ds(k0, tk), :], qa[hh])

    def update(k0, tk, state, masked):
        k0 = pl.multiple_of(k0, t)
        sts = [qk(hh, k0, tk) for hh in range(nh)]
        ms, ps, alphas = [], [], []
        for hh in range(nh):
            m = state[hh][0]
            s = jnp.where(causal, sts[hh], NEG) if masked else sts[hh]
            m_new = jnp.maximum(m, jnp.max(s, axis=0, keepdims=True))
            ps.append(jnp.exp2((s - m_new).astype(BF16)))
            alphas.append(jnp.exp2(m - m_new))
            ms.append(m_new)
        pvs = [_dot(vt_ref[hh, :, pl.ds(k0, tk)], ps[hh]) for hh in range(nh)]
        return tuple((ms[hh], alphas[hh] * state[hh][1] + pvs[hh]) for hh in range(nh))

    km = FOX_KMULT
    init = tuple((jnp.full((1, t), NEG, F32), jnp.zeros((FOX_VROWS, t), F32)) for _ in range(nh))
    def wide_pair(j, state):
        state = update(j * 2 * km * t, km * t, state, False)
        return update((2 * j + 1) * km * t, km * t, state, False)

    assert km == 2
    nw = qi // km
    state = lax.fori_loop(0, nw // 2, wide_pair, init)

    def tail(has_wide, has_single):
        def run(state):
            if has_wide:
                state = update((nw - 1) * km * t, km * t, state, False)
            if has_single:
                state = update((qi - 1) * t, t, state, False)
            return update(qi * t, t, state, True)
        return run

    state = lax.switch((nw % 2) * 2 + qi % 2,
                       [tail(False, False), tail(False, True), tail(True, False), tail(True, True)], state)
    ot = jnp.concatenate([acc[0:hd] / acc[hd:hd + 1] for _, acc in state], axis=0)
    o_ref[...] = ot.T.astype(BF16)


def _fox_place_matrix():
    assert FOX_HEADS == B_HEADS and LANE_BF == 0
    r = jnp.arange(LANES)[:, None]
    c = jnp.arange(B_HEADS * LANES)[None, :]
    head = c // LANES
    aug0 = jnp.where(head % 2 == 0, B_HEAD_DIM, 0)
    piece = c % LANES - aug0
    return ((piece >= 0) & (piece < 3) & (r == piece * B_HEADS + head)).astype(BF16)


def _fox_call(p3d, gc, t):
    b, seq, _ = p3d.shape
    w = FOX_HEADS * B_HEAD_DIM
    return pl.pallas_call(
        functools.partial(_fox_kernel, t=t, seq=seq),
        grid=(b, B_HEADS // FOX_HEADS, seq // t),
        in_specs=[
            pl.BlockSpec((None, t, w), lambda bi, hg, qi: (bi, qi, COL_BQ // w + hg)),
            pl.BlockSpec((None, seq, w), lambda bi, hg, qi: (bi, 0, COL_BK // w + hg)),
            pl.BlockSpec((None, seq, w), lambda bi, hg, qi: (bi, 0, COL_BV // w + hg)),
            pl.BlockSpec((None, seq, LANES), lambda bi, hg, qi: (bi, 0, 0)),
            _const_spec((LANES, FOX_HEADS * LANES)),
        ],
        out_specs=pl.BlockSpec((None, t, w), lambda bi, hg, qi: (bi, qi, hg)),
        out_shape=jax.ShapeDtypeStruct((b, seq, BRANCH_WIDTH), BF16),
        scratch_shapes=[
            pltpu.VMEM((FOX_HEADS, seq, LANES), BF16),
            pltpu.VMEM((FOX_HEADS, FOX_VROWS, seq), BF16),
        ],
        compiler_params=_cparams(("arbitrary", "arbitrary", "arbitrary")),
        name="fox",
    )(p3d, p3d, p3d, gc, _fox_place_matrix())


MLSTM_ROWS = 144


def _mlstm_kernel(q_ref, k_ref, v_ref, o_ref, u_ref, gr_ref, gsum_ref, gmh_ref, y_ref, c_ref, m_ref,
                  *, ts):
    L = MLSTM_CHUNK
    dh = C_HEAD_DIM
    scale = dh ** -0.5
    si = pl.program_id(1)

    @pl.when(si == 0)
    def _():
        c_ref[...] = jnp.zeros_like(c_ref)
        m_ref[...] = jnp.zeros_like(m_ref)

    row = lax.broadcasted_iota(jnp.int32, (L, L), 0)
    col = lax.broadcasted_iota(jnp.int32, (L, L), 1)
    causal = row <= col
    sub = lax.broadcasted_iota(jnp.int32, (MLSTM_ROWS - dh, L), 0)
    tail = jnp.where(sub == 0, 1.0, 0.0)

    def chunk(c, _):
        r0 = pl.multiple_of(c * L, L)
        gsum = gsum_ref[c]
        for h in range(C_HEADS):
            hs = slice(h * dh, (h + 1) * dh)
            q = q_ref[pl.ds(r0, L), hs]
            k = k_ref[pl.ds(r0, L), hs]
            vt = jnp.concatenate([v_ref[pl.ds(r0, L), hs].astype(F32).T, tail], axis=0)
            ig = gr_ref[pl.ds(LANE_CI + h, 1), pl.ds(r0, L)]
            b = gr_ref[pl.ds(LANE_CF + h, 1), pl.ds(r0, L)]
            g = gsum[h:h + 1, :]
            m_loc = gsum[C_HEADS + h:C_HEADS + h + 1, :]
            m_in = m_ref[h]
            c_in = c_ref[h]

            dlog = jnp.where(causal, u_ref[h, pl.ds(r0, L), :] + b, NEG)
            inter = b + m_in
            m_t = jnp.maximum(jnp.max(dlog, axis=0, keepdims=True), inter)
            sm = (_dot_nt(k, q) * scale) * jnp.exp(dlog - m_t)
            w_int = jnp.exp(inter - m_t)
            ext = _dot(vt.astype(BF16), sm.astype(BF16)) + w_int * _dot_nt(c_in.astype(BF16), q)
            hh = ext[0:dh] / jnp.maximum(jnp.abs(ext[dh:dh + 1]), jnp.exp(-m_t))
            hn = hh * lax.rsqrt(jnp.mean(hh * hh, axis=0, keepdims=True) + EPS) * gmh_ref[h]
            y_ref[pl.ds(r0, L), hs] = (hn.T * o_ref[pl.ds(r0, L), hs].astype(F32)).astype(BF16)

            m_new = jnp.maximum(g + m_in, m_loc)
            w = jnp.exp(g + (ig - b) - m_new) * scale
            c_ref[h] = jnp.exp(g + m_in - m_new) * c_in + _dot((vt * w).astype(BF16), k)
            m_ref[h] = m_new
        return 0

    lax.fori_loop(0, ts // L, chunk, 0, unroll=8)


def _mlstm_call(p3d, u, gr, gsum, gmh_rep, layer, ts):
    b, seq, _ = p3d.shape
    w = BRANCH_WIDTH
    nc = ts // MLSTM_CHUNK

    def pspec(col):
        return pl.BlockSpec((None, ts, w), lambda bi, si: (bi, si, col // w))

    return pl.pallas_call(
        functools.partial(_mlstm_kernel, ts=ts),
        grid=(b, seq // ts),
        in_specs=[
            pspec(COL_CQ), pspec(COL_CK), pspec(COL_CV), pspec(COL_CO),
            pl.BlockSpec((None, C_HEADS, ts, LANES), lambda bi, si: (bi, 0, si, 0)),
            pl.BlockSpec((None, GATE_ROWS, ts), lambda bi, si: (bi, 0, si)),
            pl.BlockSpec((None, nc, 2 * C_HEADS, LANES), lambda bi, si: (bi, si, 0, 0)),
            _const_spec((C_HEADS, C_HEAD_DIM, LANES), layer),
        ],
        out_specs=pl.BlockSpec((None, ts, w), lambda bi, si: (bi, si, 0)),
        out_shape=jax.ShapeDtypeStruct((b, seq, w), BF16),
        scratch_shapes=[
            pltpu.VMEM((C_HEADS, MLSTM_ROWS, C_HEAD_DIM), F32),
            pltpu.VMEM((C_HEADS, 1, LANES), F32),
        ],
        compiler_params=_cparams(("arbitrary", "arbitrary")),
        name="mlstm",
    )(p3d, p3d, p3d, p3d, u, gr, gsum, gmh_rep)


def _merge_kernel(gates_ref, uv_ref, yb_ref, yc_ref, x_ref, gsgu_ref, ws_ref, bs_ref, wb_ref, wo_ref,
                  gq_ref, wq_ref, kv_ref, wmo_ref, out_ref, ya_ref, o_ref, *, tm):
    w = BRANCH_WIDTH
    u = uv_ref[:, :w].astype(F32)
    v = uv_ref[:, w:].astype(F32)
    vn = _rms(v, gsgu_ref[...]).astype(BF16)
    gd = w // A_GROUPS
    for nb in range(tm // A_BLOCK):
        rs = slice(nb * A_BLOCK, (nb + 1) * A_BLOCK)
        mixed = jnp.concatenate(
            [_dot(ws_ref[g], vn[rs, g * gd:(g + 1) * gd]) for g in range(A_GROUPS)], axis=1)
        ya_ref[rs, :] = (u[rs, :] * (mixed + bs_ref[...])).astype(BF16)
    merged = gates_ref[:, 0:D_MODEL].astype(F32) * _dot(ya_ref[...], wb_ref[0])
    merged += gates_ref[:, D_MODEL:2 * D_MODEL].astype(F32) * _dot(yb_ref[...], wb_ref[1])
    merged += gates_ref[:, 2 * D_MODEL:3 * D_MODEL].astype(F32) * _dot(yc_ref[...], wb_ref[2])
    x = x_ref[...] + _dot(merged.astype(BF16), wo_ref[...])

    h = _rms(x, gq_ref[...]).astype(BF16)
    q = (_dot(h, wq_ref[...]) * (MEM_HEAD_DIM ** -0.5)).astype(BF16)
    dh = MEM_HEAD_DIM
    for hd in range(MEM_HEADS):
        hs = slice(hd * dh, (hd + 1) * dh)
        s = _dot_nt(q[:, hs], kv_ref[:, hs])
        p = jnp.exp(s - jnp.max(s, axis=-1, keepdims=True))
        o = _dot(p.astype(BF16), kv_ref[:, D_MODEL + hd * dh:D_MODEL + (hd + 1) * dh])
        o_ref[:, hs] = (o / jnp.sum(p, axis=-1, keepdims=True)).astype(BF16)
    out_ref[...] = x + _dot(o_ref[...], wmo_ref[...])


def _merge_call(p2d, yb, yc, x2d, g_sgu, ws_masked, bs_full, w_branch, w_out, g_mq, w_mq, kv, w_mo,
                layer, seq, tm):
    m = x2d.shape[0]
    w = BRANCH_WIDTH
    return pl.pallas_call(
        functools.partial(_merge_kernel, tm=tm),
        grid=(m // tm,),
        in_specs=[
            pl.BlockSpec((tm, 3 * D_MODEL), lambda i: (i, COL_G // (3 * D_MODEL))),
            pl.BlockSpec((tm, 2 * w), lambda i: (i, COL_A // (2 * w))),
            pl.BlockSpec((tm, w), lambda i: (i, 0)),
            pl.BlockSpec((tm, w), lambda i: (i, 0)),
            pl.BlockSpec((tm, D_MODEL), lambda i: (i, 0)),
            _const_spec((1, w), layer),
            _const_spec((A_GROUPS, A_BLOCK, A_BLOCK), layer),
            _const_spec((A_BLOCK, w), layer),
            _const_spec((3, w, D_MODEL), layer),
            _const_spec((D_MODEL, D_MODEL), layer),
            _const_spec((1, D_MODEL), layer),
            _const_spec((D_MODEL, D_MODEL), layer),
            pl.BlockSpec((N_MEM, 2 * D_MODEL), lambda i: ((i * tm) // seq, 0)),
            _const_spec((D_MODEL, D_MODEL), layer),
        ],
        out_specs=pl.BlockSpec((tm, D_MODEL), lambda i: (i, 0)),
        out_shape=jax.ShapeDtypeStruct((m, D_MODEL), F32),
        scratch_shapes=[pltpu.VMEM((tm, w), BF16), pltpu.VMEM((tm, D_MODEL), BF16)],
        compiler_params=_cparams(("arbitrary",)),
        name="merge",
    )(p2d, p2d, yb, yc, x2d, g_sgu, ws_masked, bs_full, w_branch, w_out, g_mq, w_mq, kv, w_mo)


def _memkv_kernel(mem_ref, g_ref, w_ref, kv_ref):
    kv_ref[...] = _dot(_rms(mem_ref[...], g_ref[...]).astype(BF16), w_ref[...]).astype(BF16)


def _memkv_call(mem2d, g, w_mkv, layer):
    m = mem2d.shape[0]
    return pl.pallas_call(
        _memkv_kernel,
        grid=(m // N_MEM,),
        in_specs=[
            pl.BlockSpec((N_MEM, D_MODEL), lambda i: (i, 0)),
            _const_spec((1, D_MODEL), layer),
            _const_spec((D_MODEL, 2 * D_MODEL), layer),
        ],
        out_specs=pl.BlockSpec((N_MEM, 2 * D_MODEL), lambda i: (i, 0)),
        out_shape=jax.ShapeDtypeStruct((m, 2 * D_MODEL), BF16),
        compiler_params=_cparams(("arbitrary",)),
        name="memkv",
    )(mem2d, g, w_mkv)


def _ffn_kernel(x_ref, g_ref, wup_ref, wconv_ref, wdown_ref, gfin_ref, out_ref,
                act_ref, ext_ref, carry_ref, *, tm, tf, seq, final_norm):
    i = pl.program_id(0)
    x = x_ref[...]
    h = _rms(x, g_ref[...]).astype(BF16)
    seq_start = (i * tm) % seq == 0

    def conv(slot, half, cs):
        up = _dot(h, wup_ref[:, cs])
        ext_ref[half, pl.ds(CARRY_ROWS, tm), :] = up
        ext_ref[half, pl.ds(0, CARRY_ROWS), :] = jnp.where(seq_start, 0.0, carry_ref[slot])
        carry_ref[slot] = up[tm - CARRY_ROWS:, :]
        wc = wconv_ref[:, cs]
        y = wc[FFN_CONV - 1:FFN_CONV, :] * up
        for d in range(1, FFN_CONV):
            y = y + wc[FFN_CONV - 1 - d:FFN_CONV - d, :] * ext_ref[half, pl.ds(CARRY_ROWS - d, tm), :]
        return y

    nchunk = D_FF // tf
    for c in range(nchunk):
        a = conv(c, 0, slice(c * tf, (c + 1) * tf))
        b = conv(nchunk + c, 1, slice(D_FF + c * tf, D_FF + (c + 1) * tf))
        act_ref[:, c * tf:(c + 1) * tf] = (a * _sigmoid(a) * b).astype(BF16)
    y = x + _dot(act_ref[...], wdown_ref[...])
    if final_norm:
        y = _rms(y, gfin_ref[...])
    out_ref[...] = y


def _ffn_call(x2d, g, w_up, w_conv, w_down, g_final, layer, seq, tm, tf, final_norm):
    m = x2d.shape[0]
    return pl.pallas_call(
        functools.partial(_ffn_kernel, tm=tm, tf=tf, seq=seq, final_norm=final_norm),
        grid=(m // tm,),
        in_specs=[
            pl.BlockSpec((tm, D_MODEL), lambda i: (i, 0)),
            _const_spec((1, D_MODEL), layer),
            _const_spec((D_MODEL, 2 * D_FF), layer),
            _const_spec((FFN_CONV, 2 * D_FF), layer),
            _const_spec((D_FF, D_MODEL), layer),
            _const_spec((1, D_MODEL)),
        ],
        out_specs=pl.BlockSpec((tm, D_MODEL), lambda i: (i, 0)),
        out_shape=jax.ShapeDtypeStruct((m, D_MODEL), F32),
        scratch_shapes=[
            pltpu.VMEM((tm, D_FF), BF16),
            pltpu.VMEM((2, tm + CARRY_ROWS, tf), F32),
            pltpu.VMEM((2 * (D_FF // tf), CARRY_ROWS, tf), F32),
        ],
        compiler_params=_cparams(("arbitrary",)),
        name="ffn",
    )(x2d, g, w_up, w_conv, w_down, g_final)


def _rearrange_w_in(w):
    bw = BRANCH_WIDTH
    a0 = 0
    b0 = 2 * bw
    c0 = b0 + 3 * bw + B_HEADS
    g0 = c0 + 3 * bw + 2 * C_HEADS + bw
    co = c0 + 3 * bw + 2 * C_HEADS
    w = w.astype(BF16)
    main = jnp.concatenate([
        w[..., g0:g0 + 3 * D_MODEL],
        w[..., a0:a0 + 2 * bw],
        w[..., b0:b0 + 3 * bw],
        w[..., c0:c0 + 3 * bw],
        w[..., co:co + bw],
    ], axis=-1)
    small = jnp.concatenate([
        w[..., b0 + 3 * bw:b0 + 3 * bw + B_HEADS],
        w[..., c0 + 3 * bw:c0 + 3 * bw + 2 * C_HEADS],
        jnp.zeros(w.shape[:-1] + (LANES - GATE_ROWS,), w.dtype),
    ], axis=-1)
    return main, small


def kernel(x, mem, g_mix, w_in, g_sgu, w_s, b_s, b_fox_f, w_conv_c, b_mlstm_i, b_mlstm_f, g_mh,
           w_branch, w_out, g_mem_q, g_mem_kv, w_mq, w_mkv, w_mo, g_ffn, w_up, w_ffn_conv, w_down,
           g_final):
    bsz, seq, _ = x.shape
    depth = w_in.shape[0]
    m = bsz * seq
    tm = min(512, seq)
    tm_proj = tm
    tm_ffn = tm
    t_fox = min(256, seq)
    ts_mlstm = min(1024, seq)

    idx = jnp.arange(A_BLOCK)
    chunk_causal = (idx[None, :] // CHUNK) <= (idx[:, None] // CHUNK)

    x2d = x.reshape(m, D_MODEL)
    mem2d = mem.reshape(bsz * N_MEM, D_MODEL)
    rows = lambda a: a.reshape(depth, 1, -1)

    w_main, w_small = _rearrange_w_in(w_in)
    gate_bias = jnp.concatenate(
        [b_fox_f, b_mlstm_i, b_mlstm_f, jnp.zeros((depth, LANES - GATE_ROWS), F32)], axis=-1)
    ws_masked = jnp.where(chunk_causal, w_s, 0).astype(BF16)
    bs_full = jnp.repeat(jnp.swapaxes(b_s, 1, 2), BRANCH_WIDTH // A_GROUPS, axis=2)
    gmh_rep = jnp.broadcast_to(g_mh.reshape(depth, C_HEADS, C_HEAD_DIM, 1),
                               (depth, C_HEADS, C_HEAD_DIM, LANES))
    w_branch, w_out, w_mq, w_mkv, w_mo, w_up, w_down = (
        a.astype(BF16) for a in (w_branch, w_out, w_mq, w_mkv, w_mo, w_up, w_down))

    for i in range(depth):
        p2d, gs = _proj_call(x2d, rows(g_mix), w_main, w_small, w_conv_c, i, seq, tm_proj)
        p3d = p2d.reshape(bsz, seq, PROJ_COLS)
        gc, gr, u, gsum = _gates_call(gs.reshape(bsz, seq, LANES), rows(gate_bias), i)
        yb = _fox_call(p3d, gc, t_fox).reshape(m, BRANCH_WIDTH)
        yc = _mlstm_call(p3d, u, gr, gsum, gmh_rep, i, ts_mlstm).reshape(m, BRANCH_WIDTH)
        kv = _memkv_call(mem2d, rows(g_mem_kv), w_mkv, i)
        x2d = _merge_call(p2d, yb, yc, x2d, rows(g_sgu), ws_masked, bs_full, w_branch, w_out,
                          rows(g_mem_q), w_mq, kv, w_mo, i, seq, tm)
        x2d = _ffn_call(x2d, rows(g_ffn), w_up, w_ffn_conv, w_down, g_final.reshape(1, -1), i, seq, tm_ffn,
                        256, i == depth - 1)
    return x2d.reshape(bsz, seq, D_MODEL)
```

```python
import functools

import jax
import jax.numpy as jnp
from jax import lax
from jax.experimental import pallas as pl
from jax.experimental.pallas import tpu as pltpu

F32 = jnp.float32
BF16 = jnp.bfloat16

D_MODEL = 1024
EPS = 1e-6
LANES = 128
CARRY_ROWS = 8

BRANCH_WIDTH = 512
A_BLOCK = 128
A_GROUPS = 4
CHUNK = 64
B_HEADS = 8
B_HEAD_DIM = 64
C_HEADS = 4
C_HEAD_DIM = 128
C_CONV = 4
MLSTM_CHUNK = 128
N_MEM = 256
MEM_HEADS = 4
MEM_HEAD_DIM = 256
D_FF = 2816
FFN_CONV = 3

PROJ_TN = 512
COL_G = 0
COL_A = 3072
COL_BQ = 4096
COL_BK = 4608
COL_BV = 5120
COL_CQ = 5632
COL_CK = 6144
COL_CV = 6656
COL_CO = 7168
PROJ_COLS = 7680
LANE_BF = 0
LANE_CI = 8
LANE_CF = 12
GATE_ROWS = 16

LOG2E = 1.4426950408889634
FOX_Q_SCALE = B_HEAD_DIM ** -0.5 * LOG2E
NEG = -1e30
VMEM_LIMIT = 56 * 1024 * 1024


def _cparams(sem):
    return pltpu.CompilerParams(dimension_semantics=sem, vmem_limit_bytes=VMEM_LIMIT)


def _rms(xf, g):
    return xf * lax.rsqrt(jnp.mean(xf * xf, axis=-1, keepdims=True) + EPS) * g


def _sigmoid(x):
    return 0.5 * jnp.tanh(0.5 * x) + 0.5


def _gelu_tanh(x):
    return 0.5 * x * (1.0 + jnp.tanh(0.7978845608028654 * (x + 0.044715 * (x * x * x))))


def _dot(a, b):
    return jnp.dot(a, b, preferred_element_type=F32)


def _dot_nt(a, b):
    return lax.dot_general(a, b, (((1,), (1,)), ((), ())), preferred_element_type=F32)


def _const_spec(shape, layer=None):
    nd = len(shape)
    if layer is None:
        return pl.BlockSpec(shape, lambda *_: (0,) * nd, pipeline_mode=pl.Buffered(1))
    return pl.BlockSpec((None,) + tuple(shape), lambda *_: (layer,) + (0,) * nd,
                        pipeline_mode=pl.Buffered(1))


def _proj_kernel(x_ref, g_ref, w_ref, ws_ref, wc_ref, p_ref, gs_ref, ext_ref, carry_ref, *, tm, seq):
    i = pl.program_id(0)
    h = _rms(x_ref[...], g_ref[...]).astype(BF16)
    gs_ref[...] = _dot(h, ws_ref[...])
    seq_start = (i * tm) % seq == 0

    def conv_silu(acc, slot):
        ext_ref[slot, pl.ds(CARRY_ROWS, tm), :] = acc
        ext_ref[slot, pl.ds(0, CARRY_ROWS), :] = jnp.where(seq_start, 0.0, carry_ref[slot])
        carry_ref[slot] = ext_ref[slot, pl.ds(tm, CARRY_ROWS), :]
        wc = 0.5 * wc_ref[:, slot * PROJ_TN:(slot + 1) * PROJ_TN]
        half = wc[C_CONV - 1:C_CONV, :] * ext_ref[slot, pl.ds(CARRY_ROWS, tm), :]
        for d in range(1, C_CONV):
            half = half + wc[C_CONV - 1 - d:C_CONV - d, :] * ext_ref[slot, pl.ds(CARRY_ROWS - d, tm), :]
        return half * (jnp.tanh(half) + 1.0)

    order = [COL_CQ, COL_CK] + [c for c in range(0, PROJ_COLS, PROJ_TN) if c not in (COL_CQ, COL_CK)]
    for c0 in order:
        cs = slice(c0, c0 + PROJ_TN)
        acc = _dot(h, w_ref[:, cs])
        if c0 < COL_A or c0 == COL_CO:
            out = _sigmoid(acc)
        elif c0 < COL_BQ:
            out = _gelu_tanh(acc)
        elif c0 == COL_BQ:
            out = acc * FOX_Q_SCALE
        elif c0 == COL_CQ:
            out = conv_silu(acc, 0)
        elif c0 == COL_CK:
            out = conv_silu(acc, 1)
        else:
            out = acc
        p_ref[:, cs] = out.astype(BF16)


def _proj_call(x2d, g, w_main, w_small, w_conv, layer, seq, tm):
    m = x2d.shape[0]
    return pl.pallas_call(
        functools.partial(_proj_kernel, tm=tm, seq=seq),
        grid=(m // tm,),
        in_specs=[
            pl.BlockSpec((tm, D_MODEL), lambda i: (i, 0)),
            _const_spec((1, D_MODEL), layer),
            _const_spec((D_MODEL, PROJ_COLS), layer),
            _const_spec((D_MODEL, LANES), layer),
            _const_spec((C_CONV, 2 * PROJ_TN), layer),
        ],
        out_specs=[
            pl.BlockSpec((tm, PROJ_COLS), lambda i: (i, 0)),
            pl.BlockSpec((tm, LANES), lambda i: (i, 0)),
        ],
        out_shape=[
            jax.ShapeDtypeStruct((m, PROJ_COLS), BF16),
            jax.ShapeDtypeStruct((m, LANES), F32),
        ],
        scratch_shapes=[
            pltpu.VMEM((2, tm + CARRY_ROWS, PROJ_TN), F32),
            pltpu.VMEM((2, CARRY_ROWS, PROJ_TN), F32),
        ],
        compiler_params=_cparams(("arbitrary",)),
        name="proj",
    )(x2d, g, w_main, w_small, w_conv)


def _gates_kernel(gs_ref, bias_ref, gc_ref, gr_ref, u_ref, gsum_ref, *, seq):
    blk = MLSTM_CHUNK
    row = lax.broadcasted_iota(jnp.int32, (blk, blk), 0)
    col = lax.broadcasted_iota(jnp.int32, (blk, blk), 1)
    tri = (col <= row).astype(F32)
    lane = lax.broadcasted_iota(jnp.int32, (1, LANES), 1)

    def body(r, carry):
        r0 = pl.multiple_of(r * blk, blk)
        raw = gs_ref[pl.ds(r0, blk), :] + bias_ref[...]
        logsig = jnp.minimum(raw, 0.0) - jnp.log1p(jnp.exp(-jnp.abs(raw)))
        local = jnp.dot(tri, logsig, precision=lax.Precision.HIGHEST, preferred_element_type=F32)
        glob = local + carry
        out = jnp.where(lane < LANE_CI, glob, jnp.where(lane < LANE_CF, raw, local))
        gc_ref[pl.ds(r0, blk), :] = out
        gr_ref[:, pl.ds(r0, blk)] = out.T[0:GATE_ROWS, :]
        g_rows, mloc_rows = [], []
        for h in range(C_HEADS):
            d = out[:, LANE_CI + h:LANE_CI + h + 1] - out[:, LANE_CF + h:LANE_CF + h + 1]
            u = jnp.broadcast_to(d, (blk, LANES))
            u_ref[h, pl.ds(r0, blk), :] = u
            g = jnp.broadcast_to(out[blk - 1:blk, LANE_CF + h:LANE_CF + h + 1], (1, LANES))
            g_rows.append(g)
            mloc_rows.append(g + jnp.max(u, axis=0, keepdims=True))
        gsum_ref[r] = jnp.concatenate(g_rows + mloc_rows, axis=0)
        return glob[blk - 1:blk, :]

    lax.fori_loop(0, seq // blk, body, jnp.zeros((1, LANES), F32), unroll=4)


def _gates_call(gs3d, bias, layer):
    b, seq, _ = gs3d.shape
    nchunk = seq // MLSTM_CHUNK
    return pl.pallas_call(
        functools.partial(_gates_kernel, seq=seq),
        grid=(b,),
        in_specs=[
            pl.BlockSpec((None, seq, LANES), lambda bi: (bi, 0, 0)),
            _const_spec((1, LANES), layer),
        ],
        out_specs=[
            pl.BlockSpec((None, seq, LANES), lambda bi: (bi, 0, 0)),
            pl.BlockSpec((None, GATE_ROWS, seq), lambda bi: (bi, 0, 0)),
            pl.BlockSpec((None, C_HEADS, seq, LANES), lambda bi: (bi, 0, 0, 0)),
            pl.BlockSpec((None, nchunk, 2 * C_HEADS, LANES), lambda bi: (bi, 0, 0, 0)),
        ],
        out_shape=[
            jax.ShapeDtypeStruct((b, seq, LANES), F32),
            jax.ShapeDtypeStruct((b, GATE_ROWS, seq), F32),
            jax.ShapeDtypeStruct((b, C_HEADS, seq, LANES), F32),
            jax.ShapeDtypeStruct((b, nchunk, 2 * C_HEADS, LANES), F32),
        ],
        compiler_params=_cparams(("arbitrary",)),
        name="gates",
    )(gs3d, bias)


FOX_VROWS = 80
FOX_KMULT = 2
FOX_BODY = 4
FOX_HEADS = 8


def _fox_kernel(q_ref, k_ref, v_ref, gc_ref, place_ref, o_ref, kaug_ref, vt_ref, *, t, seq):
    qi = pl.program_id(2)
    hd = B_HEAD_DIM
    nh = FOX_HEADS
    lane = lax.broadcasted_iota(jnp.int32, (1, LANES), 1)
    own = (lane < hd, lane >= hd)
    aug0 = (hd, 0)

    @pl.when(qi == 0)
    def _():
        sub = lax.broadcasted_iota(jnp.int32, (FOX_VROWS - hd, seq), 0)
        tail = jnp.where(sub == 0, 1.0, 0.0).astype(BF16)
        for hh in range(nh):
            vt_ref[hh, hd:FOX_VROWS, :] = tail
        eye = (lax.broadcasted_iota(jnp.int32, (LANES, LANES), 0)
               == lax.broadcasted_iota(jnp.int32, (LANES, LANES), 1)).astype(F32).astype(BF16)

        def body(r, _):
            r0 = pl.multiple_of(r * LANES, LANES)
            neg = -LOG2E * gc_ref[pl.ds(r0, LANES), :]
            hi = neg.astype(BF16).astype(F32)
            mid = (neg - hi).astype(BF16).astype(F32)
            lo = (neg - hi) - mid
            pieces = jnp.where(lane < B_HEADS, hi,
                               jnp.where(lane < 2 * B_HEADS, pltpu.roll(mid, B_HEADS, 1),
                                         pltpu.roll(lo, 2 * B_HEADS, 1))).astype(BF16)
            extra = _dot(pieces, place_ref[...]).astype(BF16)
            for pp in range(nh // 2):
                ls = slice(pp * LANES, (pp + 1) * LANES)
                kb = k_ref[pl.ds(r0, LANES), ls]
                vt = _dot_nt(eye, v_ref[pl.ds(r0, LANES), ls])
                for h in range(2):
                    hh = 2 * pp + h
                    kaug_ref[hh, pl.ds(r0, LANES), :] = jnp.where(
                        own[h], kb, extra[:, hh * LANES:(hh + 1) * LANES])
                    vt_ref[hh, 0:hd, pl.ds(r0, LANES)] = vt[h * hd:(h + 1) * hd, :].astype(BF16)
            return 0

        lax.fori_loop(0, seq // LANES, body, 0, unroll=4)

    qa = []
    for pp in range(nh // 2):
        q = q_ref[:, pp * LANES:(pp + 1) * LANES].astype(F32)
        for h in range(2):
            a = aug0[h]
            ones3 = jnp.where((lane >= a) & (lane < a + 3), 1.0, 0.0)
            qa.append(jnp.where(own[h], q, ones3).astype(BF16))
    row = lax.broadcasted_iota(jnp.int32, (t, t), 0)
    col = lax.broadcasted_iota(jnp.int32, (t, t), 1)
    causal = row <= col

    def qk(hh, k0, tk):
        return _dot_nt(kaug_ref[hh, pl.ds(k0, tk), :], qa[hh])

    def update(k0, tk, state, masked):
        k0 = pl.multiple_of(k0, t)
        sts = [qk(hh, k0, tk) for hh in range(nh)]
        ms, ps, alphas = [], [], []
        for hh in range(nh):
            m = state[hh][0]
            s = jnp.where(causal, sts[hh], NEG) if masked else sts[hh]
            m_new = jnp.maximum(m, jnp.max(s, axis=0, keepdims=True))
            ps.append(jnp.exp2((s - m_new).astype(BF16)))
            alphas.append(jnp.exp2(m - m_new))
            ms.append(m_new)
        pvs = [_dot(vt_ref[hh, :, pl.ds(k0, tk)], ps[hh]) for hh in range(nh)]
        return tuple((ms[hh], alphas[hh] * state[hh][1] + pvs[hh]) for hh in range(nh))

    km = FOX_KMULT
    nb = FOX_BODY
    init = tuple((jnp.full((1, t), NEG, F32), jnp.zeros((FOX_VROWS, t), F32)) for _ in range(nh))

    def body(j, state):
        for i in range(nb):
            state = update((j * nb + i) * km * t, km * t, state, False)
        return state

    nw = qi // km
    state = lax.fori_loop(0, nw // nb, body, init)

    def tail(n_wide, n_single):
        def run(state):
            for i in range(n_wide):
                state = update((nw - n_wide + i) * km * t, km * t, state, False)
            for i in range(n_single):
                state = update((qi - n_single + i) * t, t, state, False)
            return update(qi * t, t, state, True)
        return run

    state = lax.switch((nw % nb) * km + qi % km,
                       [tail(w, s) for w in range(nb) for s in range(km)], state)
    ot = jnp.concatenate([acc[0:hd] / acc[hd:hd + 1] for _, acc in state], axis=0)
    o_ref[...] = ot.T.astype(BF16)


def _fox_place_matrix():
    assert FOX_HEADS == B_HEADS and LANE_BF == 0
    r = jnp.arange(LANES)[:, None]
    c = jnp.arange(B_HEADS * LANES)[None, :]
    head = c // LANES
    aug0 = jnp.where(head % 2 == 0, B_HEAD_DIM, 0)
    piece = c % LANES - aug0
    return ((piece >= 0) & (piece < 3) & (r == piece * B_HEADS + head)).astype(BF16)


def _fox_call(p3d, gc, t):
    b, seq, _ = p3d.shape
    w = FOX_HEADS * B_HEAD_DIM
    return pl.pallas_call(
        functools.partial(_fox_kernel, t=t, seq=seq),
        grid=(b, B_HEADS // FOX_HEADS, seq // t),
        in_specs=[
            pl.BlockSpec((None, t, w), lambda bi, hg, qi: (bi, qi, COL_BQ // w + hg)),
            pl.BlockSpec((None, seq, w), lambda bi, hg, qi: (bi, 0, COL_BK // w + hg)),
            pl.BlockSpec((None, seq, w), lambda bi, hg, qi: (bi, 0, COL_BV // w + hg)),
            pl.BlockSpec((None, seq, LANES), lambda bi, hg, qi: (bi, 0, 0)),
            _const_spec((LANES, FOX_HEADS * LANES)),
        ],
        out_specs=pl.BlockSpec((None, t, w), lambda bi, hg, qi: (bi, qi, hg)),
        out_shape=jax.ShapeDtypeStruct((b, seq, BRANCH_WIDTH), BF16),
        scratch_shapes=[
            pltpu.VMEM((FOX_HEADS, seq, LANES), BF16),
            pltpu.VMEM((FOX_HEADS, FOX_VROWS, seq), BF16),
        ],
        compiler_params=_cparams(("arbitrary", "arbitrary", "arbitrary")),
        name="fox",
    )(p3d, p3d, p3d, gc, _fox_place_matrix())


MLSTM_ROWS = 144


def _mlstm_kernel(q_ref, k_ref, v_ref, o_ref, u_ref, gr_ref, gsum_ref, gmh_ref, y_ref, c_ref, m_ref,
                  *, ts):
    L = MLSTM_CHUNK
    dh = C_HEAD_DIM
    scale = dh ** -0.5
    si = pl.program_id(1)

    @pl.when(si == 0)
    def _():
        c_ref[...] = jnp.zeros_like(c_ref)
        m_ref[...] = jnp.zeros_like(m_ref)

    row = lax.broadcasted_iota(jnp.int32, (L, L), 0)
    col = lax.broadcasted_iota(jnp.int32, (L, L), 1)
    causal = row <= col
    sub = lax.broadcasted_iota(jnp.int32, (MLSTM_ROWS - dh, L), 0)
    tail = jnp.where(sub == 0, 1.0, 0.0)

    def chunk(c, _):
        r0 = pl.multiple_of(c * L, L)
        gsum = gsum_ref[c]
        for h in range(C_HEADS):
            hs = slice(h * dh, (h + 1) * dh)
            q = q_ref[pl.ds(r0, L), hs]
            k = k_ref[pl.ds(r0, L), hs]
            vt = jnp.concatenate([v_ref[pl.ds(r0, L), hs].astype(F32).T, tail], axis=0)
            ig = gr_ref[pl.ds(LANE_CI + h, 1), pl.ds(r0, L)]
            b = gr_ref[pl.ds(LANE_CF + h, 1), pl.ds(r0, L)]
            g = gsum[h:h + 1, :]
            m_loc = gsum[C_HEADS + h:C_HEADS + h + 1, :]
            m_in = m_ref[h]
            c_in = c_ref[h]

            dlog = jnp.where(causal, u_ref[h, pl.ds(r0, L), :] + b, NEG)
            inter = b + m_in
            m_t = jnp.maximum(jnp.max(dlog, axis=0, keepdims=True), inter)
            sm = (_dot_nt(k, q) * scale) * jnp.exp(dlog - m_t)
            w_int = jnp.exp(inter - m_t)
            ext = _dot(vt.astype(BF16), sm.astype(BF16)) + w_int * _dot_nt(c_in.astype(BF16), q)
            hh = ext[0:dh] / jnp.maximum(jnp.abs(ext[dh:dh + 1]), jnp.exp(-m_t))
            hn = hh * lax.rsqrt(jnp.mean(hh * hh, axis=0, keepdims=True) + EPS) * gmh_ref[h]
            y_ref[pl.ds(r0, L), hs] = (hn.T * o_ref[pl.ds(r0, L), hs].astype(F32)).astype(BF16)

            m_new = jnp.maximum(g + m_in, m_loc)
            w = jnp.exp(g + (ig - b) - m_new) * scale
            c_ref[h] = jnp.exp(g + m_in - m_new) * c_in + _dot((vt * w).astype(BF16), k)
            m_ref[h] = m_new
        return 0

    lax.fori_loop(0, ts // L, chunk, 0, unroll=8)


def _mlstm_call(p3d, u, gr, gsum, gmh_rep, layer, ts):
    b, seq, _ = p3d.shape
    w = BRANCH_WIDTH
    nc = ts // MLSTM_CHUNK

    def pspec(col):
        return pl.BlockSpec((None, ts, w), lambda bi, si: (bi, si, col // w))

    return pl.pallas_call(
        functools.partial(_mlstm_kernel, ts=ts),
        grid=(b, seq // ts),
        in_specs=[
            pspec(COL_CQ), pspec(COL_CK), pspec(COL_CV), pspec(COL_CO),
            pl.BlockSpec((None, C_HEADS, ts, LANES), lambda bi, si: (bi, 0, si, 0)),
            pl.BlockSpec((None, GATE_ROWS, ts), lambda bi, si: (bi, 0, si)),
            pl.BlockSpec((None, nc, 2 * C_HEADS, LANES), lambda bi, si: (bi, si, 0, 0)),
            _const_spec((C_HEADS, C_HEAD_DIM, LANES), layer),
        ],
        out_specs=pl.BlockSpec((None, ts, w), lambda bi, si: (bi, si, 0)),
        out_shape=jax.ShapeDtypeStruct((b, seq, w), BF16),
        scratch_shapes=[
            pltpu.VMEM((C_HEADS, MLSTM_ROWS, C_HEAD_DIM), F32),
            pltpu.VMEM((C_HEADS, 1, LANES), F32),
        ],
        compiler_params=_cparams(("arbitrary", "arbitrary")),
        name="mlstm",
    )(p3d, p3d, p3d, p3d, u, gr, gsum, gmh_rep)


def _merge_kernel(gates_ref, uv_ref, yb_ref, yc_ref, x_ref, gsgu_ref, ws_ref, bs_ref, wb_ref, wo_ref,
                  gq_ref, wq_ref, kv_ref, wmo_ref, out_ref, ya_ref, o_ref, *, tm):
    w = BRANCH_WIDTH
    u = uv_ref[:, :w].astype(F32)
    v = uv_ref[:, w:].astype(F32)
    vn = _rms(v, gsgu_ref[...]).astype(BF16)
    gd = w // A_GROUPS
    for nb in range(tm // A_BLOCK):
        rs = slice(nb * A_BLOCK, (nb + 1) * A_BLOCK)
        mixed = jnp.concatenate(
            [_dot(ws_ref[g], vn[rs, g * gd:(g + 1) * gd]) for g in range(A_GROUPS)], axis=1)
        ya_ref[rs, :] = (u[rs, :] * (mixed + bs_ref[...])).astype(BF16)
    merged = gates_ref[:, 0:D_MODEL].astype(F32) * _dot(ya_ref[...], wb_ref[0])
    merged += gates_ref[:, D_MODEL:2 * D_MODEL].astype(F32) * _dot(yb_ref[...], wb_ref[1])
    merged += gates_ref[:, 2 * D_MODEL:3 * D_MODEL].astype(F32) * _dot(yc_ref[...], wb_ref[2])
    x = x_ref[...] + _dot(merged.astype(BF16), wo_ref[...])

    h = _rms(x, gq_ref[...]).astype(BF16)
    q = (_dot(h, wq_ref[...]) * (MEM_HEAD_DIM ** -0.5)).astype(BF16)
    dh = MEM_HEAD_DIM
    for hd in range(MEM_HEADS):
        hs = slice(hd * dh, (hd + 1) * dh)
        s = _dot_nt(q[:, hs], kv_ref[:, hs])
        p = jnp.exp(s - jnp.max(s, axis=-1, keepdims=True))
        o = _dot(p.astype(BF16), kv_ref[:, D_MODEL + hd * dh:D_MODEL + (hd + 1) * dh])
        o_ref[:, hs] = (o / jnp.sum(p, axis=-1, keepdims=True)).astype(BF16)
    out_ref[...] = x + _dot(o_ref[...], wmo_ref[...])


def _merge_call(p2d, yb, yc, x2d, g_sgu, ws_masked, bs_full, w_branch, w_out, g_mq, w_mq, kv, w_mo,
                layer, seq, tm):
    m = x2d.shape[0]
    w = BRANCH_WIDTH
    return pl.pallas_call(
        functools.partial(_merge_kernel, tm=tm),
        grid=(m // tm,),
        in_specs=[
            pl.BlockSpec((tm, 3 * D_MODEL), lambda i: (i, COL_G // (3 * D_MODEL))),
            pl.BlockSpec((tm, 2 * w), lambda i: (i, COL_A // (2 * w))),
            pl.BlockSpec((tm, w), lambda i: (i, 0)),
            pl.BlockSpec((tm, w), lambda i: (i, 0)),
            pl.BlockSpec((tm, D_MODEL), lambda i: (i, 0)),
            _const_spec((1, w), layer),
            _const_spec((A_GROUPS, A_BLOCK, A_BLOCK), layer),
            _const_spec((A_BLOCK, w), layer),
            _const_spec((3, w, D_MODEL), layer),
            _const_spec((D_MODEL, D_MODEL), layer),
            _const_spec((1, D_MODEL), layer),
            _const_spec((D_MODEL, D_MODEL), layer),
            pl.BlockSpec((N_MEM, 2 * D_MODEL), lambda i: ((i * tm) // seq, 0)),
            _const_spec((D_MODEL, D_MODEL), layer),
        ],
        out_specs=pl.BlockSpec((tm, D_MODEL), lambda i: (i, 0)),
        out_shape=jax.ShapeDtypeStruct((m, D_MODEL), F32),
        scratch_shapes=[pltpu.VMEM((tm, w), BF16), pltpu.VMEM((tm, D_MODEL), BF16)],
        compiler_params=_cparams(("arbitrary",)),
        name="merge",
    )(p2d, p2d, yb, yc, x2d, g_sgu, ws_masked, bs_full, w_branch, w_out, g_mq, w_mq, kv, w_mo)


def _memkv_kernel(mem_ref, g_ref, w_ref, kv_ref):
    kv_ref[...] = _dot(_rms(mem_ref[...], g_ref[...]).astype(BF16), w_ref[...]).astype(BF16)


def _memkv_call(mem2d, g, w_mkv, layer):
    m = mem2d.shape[0]
    return pl.pallas_call(
        _memkv_kernel,
        grid=(m // N_MEM,),
        in_specs=[
            pl.BlockSpec((N_MEM, D_MODEL), lambda i: (i, 0)),
            _const_spec((1, D_MODEL), layer),
            _const_spec((D_MODEL, 2 * D_MODEL), layer),
        ],
        out_specs=pl.BlockSpec((N_MEM, 2 * D_MODEL), lambda i: (i, 0)),
        out_shape=jax.ShapeDtypeStruct((m, 2 * D_MODEL), BF16),
        compiler_params=_cparams(("arbitrary",)),
        name="memkv",
    )(mem2d, g, w_mkv)


def _ffn_kernel(x_ref, g_ref, wup_ref, wconv_ref, wdown_ref, gfin_ref, out_ref,
                act_ref, ext_ref, carry_ref, *, tm, tf, seq, final_norm):
    i = pl.program_id(0)
    x = x_ref[...]
    h = _rms(x, g_ref[...]).astype(BF16)
    seq_start = (i * tm) % seq == 0

    def conv(slot, half, cs):
        up = _dot(h, wup_ref[:, cs])
        ext_ref[half, pl.ds(CARRY_ROWS, tm), :] = up
        ext_ref[half, pl.ds(0, CARRY_ROWS), :] = jnp.where(seq_start, 0.0, carry_ref[slot])
        carry_ref[slot] = up[tm - CARRY_ROWS:, :]
        wc = wconv_ref[:, cs]
        y = wc[FFN_CONV - 1:FFN_CONV, :] * up
        for d in range(1, FFN_CONV):
            y = y + wc[FFN_CONV - 1 - d:FFN_CONV - d, :] * ext_ref[half, pl.ds(CARRY_ROWS - d, tm), :]
        return y

    nchunk = D_FF // tf
    for c in range(nchunk):
        a = conv(c, 0, slice(c * tf, (c + 1) * tf))
        b = conv(nchunk + c, 1, slice(D_FF + c * tf, D_FF + (c + 1) * tf))
        act_ref[:, c * tf:(c + 1) * tf] = (a * _sigmoid(a) * b).astype(BF16)
    y = x + _dot(act_ref[...], wdown_ref[...])
    if final_norm:
        y = _rms(y, gfin_ref[...])
    out_ref[...] = y


def _ffn_call(x2d, g, w_up, w_conv, w_down, g_final, layer, seq, tm, tf, final_norm):
    m = x2d.shape[0]
    return pl.pallas_call(
        functools.partial(_ffn_kernel, tm=tm, tf=tf, seq=seq, final_norm=final_norm),
        grid=(m // tm,),
        in_specs=[
            pl.BlockSpec((tm, D_MODEL), lambda i: (i, 0)),
            _const_spec((1, D_MODEL), layer),
            _const_spec((D_MODEL, 2 * D_FF), layer),
            _const_spec((FFN_CONV, 2 * D_FF), layer),
            _const_spec((D_FF, D_MODEL), layer),
            _const_spec((1, D_MODEL)),
        ],
        out_specs=pl.BlockSpec((tm, D_MODEL), lambda i: (i, 0)),
        out_shape=jax.ShapeDtypeStruct((m, D_MODEL), F32),
        scratch_shapes=[
            pltpu.VMEM((tm, D_FF), BF16),
            pltpu.VMEM((2, tm + CARRY_ROWS, tf), F32),
            pltpu.VMEM((2 * (D_FF // tf), CARRY_ROWS, tf), F32),
        ],
        compiler_params=_cparams(("arbitrary",)),
        name="ffn",
    )(x2d, g, w_up, w_conv, w_down, g_final)


def _rearrange_w_in(w):
    bw = BRANCH_WIDTH
    a0 = 0
    b0 = 2 * bw
    c0 = b0 + 3 * bw + B_HEADS
    g0 = c0 + 3 * bw + 2 * C_HEADS + bw
    co = c0 + 3 * bw + 2 * C_HEADS
    w = w.astype(BF16)
    main = jnp.concatenate([
        w[..., g0:g0 + 3 * D_MODEL],
        w[..., a0:a0 + 2 * bw],
        w[..., b0:b0 + 3 * bw],
        w[..., c0:c0 + 3 * bw],
        w[..., co:co + bw],
    ], axis=-1)
    small = jnp.concatenate([
        w[..., b0 + 3 * bw:b0 + 3 * bw + B_HEADS],
        w[..., c0 + 3 * bw:c0 + 3 * bw + 2 * C_HEADS],
        jnp.zeros(w.shape[:-1] + (LANES - GATE_ROWS,), w.dtype),
    ], axis=-1)
    return main, small


def kernel(x, mem, g_mix, w_in, g_sgu, w_s, b_s, b_fox_f, w_conv_c, b_mlstm_i, b_mlstm_f, g_mh,
           w_branch, w_out, g_mem_q, g_mem_kv, w_mq, w_mkv, w_mo, g_ffn, w_up, w_ffn_conv, w_down,
           g_final):
    bsz, seq, _ = x.shape
    depth = w_in.shape[0]
    m = bsz * seq
    tm = min(512, seq)
    tm_proj = tm
    tm_ffn = tm
    t_fox = min(256, seq)
    ts_mlstm = min(1024, seq)

    idx = jnp.arange(A_BLOCK)
    chunk_causal = (idx[None, :] // CHUNK) <= (idx[:, None] // CHUNK)

    x2d = x.reshape(m, D_MODEL)
    mem2d = mem.reshape(bsz * N_MEM, D_MODEL)
    rows = lambda a: a.reshape(depth, 1, -1)

    w_main, w_small = _rearrange_w_in(w_in)
    gate_bias = jnp.concatenate(
        [b_fox_f, b_mlstm_i, b_mlstm_f, jnp.zeros((depth, LANES - GATE_ROWS), F32)], axis=-1)
    ws_masked = jnp.where(chunk_causal, w_s, 0).astype(BF16)
    bs_full = jnp.repeat(jnp.swapaxes(b_s, 1, 2), BRANCH_WIDTH // A_GROUPS, axis=2)
    gmh_rep = jnp.broadcast_to(g_mh.reshape(depth, C_HEADS, C_HEAD_DIM, 1),
                               (depth, C_HEADS, C_HEAD_DIM, LANES))
    w_branch, w_out, w_mq, w_mkv, w_mo, w_up, w_down = (
        a.astype(BF16) for a in (w_branch, w_out, w_mq, w_mkv, w_mo, w_up, w_down))

    for i in range(depth):
        p2d, gs = _proj_call(x2d, rows(g_mix), w_main, w_small, w_conv_c, i, seq, tm_proj)
        p3d = p2d.reshape(bsz, seq, PROJ_COLS)
        gc, gr, u, gsum = _gates_call(gs.reshape(bsz, seq, LANES), rows(gate_bias), i)
        yb = _fox_call(p3d, gc, t_fox).reshape(m, BRANCH_WIDTH)
        yc = _mlstm_call(p3d, u, gr, gsum, gmh_rep, i, ts_mlstm).reshape(m, BRANCH_WIDTH)
        kv = _memkv_call(mem2d, rows(g_mem_kv), w_mkv, i)
        x2d = _merge_call(p2d, yb, yc, x2d, rows(g_sgu), ws_masked, bs_full, w_branch, w_out,
                          rows(g_mem_q), w_mq, kv, w_mo, i, seq, tm)
        x2d = _ffn_call(x2d, rows(g_ffn), w_up, w_ffn_conv, w_down, g_final.reshape(1, -1), i, seq, tm_ffn,
                        256, i == depth - 1)
    return x2d.reshape(bsz, seq, D_MODEL)
```

```python
import functools

import jax
import jax.numpy as jnp
from jax import lax
from jax.experimental import pallas as pl
from jax.experimental.pallas import tpu as pltpu

F32 = jnp.float32
BF16 = jnp.bfloat16

D_MODEL = 1024
EPS = 1e-6
LANES = 128
CARRY_ROWS = 8

BRANCH_WIDTH = 512
A_BLOCK = 128
A_GROUPS = 4
CHUNK = 64
B_HEADS = 8
B_HEAD_DIM = 64
C_HEADS = 4
C_HEAD_DIM = 128
C_CONV = 4
MLSTM_CHUNK = 128
N_MEM = 256
MEM_HEADS = 4
MEM_HEAD_DIM = 256
D_FF = 2816
FFN_CONV = 3

PROJ_TN = 512
COL_G = 0
COL_A = 3072
COL_BQ = 4096
COL_BK = 4608
COL_BV = 5120
COL_CQ = 5632
COL_CK = 6144
COL_CV = 6656
COL_CO = 7168
PROJ_COLS = 7680
LANE_BF = 0
LANE_CI = 8
LANE_CF = 12
GATE_ROWS = 16

LOG2E = 1.4426950408889634
FOX_Q_SCALE = B_HEAD_DIM ** -0.5 * LOG2E
NEG = -1e30
VMEM_LIMIT = 56 * 1024 * 1024


def _cparams(sem):
    return pltpu.CompilerParams(dimension_semantics=sem, vmem_limit_bytes=VMEM_LIMIT)


def _rms(xf, g):
    return xf * lax.rsqrt(jnp.mean(xf * xf, axis=-1, keepdims=True) + EPS) * g


def _sigmoid(x):
    return 0.5 * jnp.tanh(0.5 * x) + 0.5


def _gelu_tanh(x):
    return 0.5 * x * (1.0 + jnp.tanh(0.7978845608028654 * (x + 0.044715 * (x * x * x))))


def _dot(a, b):
    return jnp.dot(a, b, preferred_element_type=F32)


def _dot_nt(a, b):
    return lax.dot_general(a, b, (((1,), (1,)), ((), ())), preferred_element_type=F32)


def _const_spec(shape, layer=None):
    nd = len(shape)
    if layer is None:
        return pl.BlockSpec(shape, lambda *_: (0,) * nd, pipeline_mode=pl.Buffered(1))
    return pl.BlockSpec((None,) + tuple(shape), lambda *_: (layer,) + (0,) * nd,
                        pipeline_mode=pl.Buffered(1))


def _proj_kernel(x_ref, g_ref, w_ref, ws_ref, wc_ref, p_ref, gs_ref, ext_ref, carry_ref, *, tm, seq):
    i = pl.program_id(0)
    h = _rms(x_ref[...], g_ref[...]).astype(BF16)
    gs_ref[...] = _dot(h, ws_ref[...])
    seq_start = (i * tm) % seq == 0

    def conv_silu(acc, slot):
        ext_ref[slot, pl.ds(CARRY_ROWS, tm), :] = acc
        ext_ref[slot, pl.ds(0, CARRY_ROWS), :] = jnp.where(seq_start, 0.0, carry_ref[slot])
        carry_ref[slot] = ext_ref[slot, pl.ds(tm, CARRY_ROWS), :]
        wc = 0.5 * wc_ref[:, slot * PROJ_TN:(slot + 1) * PROJ_TN]
        half = wc[C_CONV - 1:C_CONV, :] * ext_ref[slot, pl.ds(CARRY_ROWS, tm), :]
        for d in range(1, C_CONV):
            half = half + wc[C_CONV - 1 - d:C_CONV - d, :] * ext_ref[slot, pl.ds(CARRY_ROWS - d, tm), :]
        return half * (jnp.tanh(half) + 1.0)

    order = [COL_CQ, COL_CK] + [c for c in range(0, PROJ_COLS, PROJ_TN) if c not in (COL_CQ, COL_CK)]
    for c0 in order:
        cs = slice(c0, c0 + PROJ_TN)
        acc = _dot(h, w_ref[:, cs])
        if c0 < COL_A or c0 == COL_CO:
            out = _sigmoid(acc)
        elif c0 < COL_BQ:
            out = _gelu_tanh(acc)
        elif c0 == COL_BQ:
            out = acc * FOX_Q_SCALE
        elif c0 == COL_CQ:
            out = conv_silu(acc, 0)
        elif c0 == COL_CK:
            out = conv_silu(acc, 1)
        else:
            out = acc
        p_ref[:, cs] = out.astype(BF16)


def _proj_call(x2d, g, w_main, w_small, w_conv, layer, seq, tm):
    m = x2d.shape[0]
    return pl.pallas_call(
        functools.partial(_proj_kernel, tm=tm, seq=seq),
        grid=(m // tm,),
        in_specs=[
            pl.BlockSpec((tm, D_MODEL), lambda i: (i, 0)),
            _const_spec((1, D_MODEL), layer),
            _const_spec((D_MODEL, PROJ_COLS), layer),
            _const_spec((D_MODEL, LANES), layer),
            _const_spec((C_CONV, 2 * PROJ_TN), layer),
        ],
        out_specs=[
            pl.BlockSpec((tm, PROJ_COLS), lambda i: (i, 0)),
            pl.BlockSpec((tm, LANES), lambda i: (i, 0)),
        ],
        out_shape=[
            jax.ShapeDtypeStruct((m, PROJ_COLS), BF16),
            jax.ShapeDtypeStruct((m, LANES), F32),
        ],
        scratch_shapes=[
            pltpu.VMEM((2, tm + CARRY_ROWS, PROJ_TN), F32),
            pltpu.VMEM((2, CARRY_ROWS, PROJ_TN), F32),
        ],
        compiler_params=_cparams(("arbitrary",)),
        name="proj",
    )(x2d, g, w_main, w_small, w_conv)


def _gates_kernel(gs_ref, bias_ref, gc_ref, gr_ref, u_ref, gsum_ref, *, seq):
    blk = MLSTM_CHUNK
    row = lax.broadcasted_iota(jnp.int32, (blk, blk), 0)
    col = lax.broadcasted_iota(jnp.int32, (blk, blk), 1)
    tri = (col <= row).astype(F32)
    lane = lax.broadcasted_iota(jnp.int32, (1, LANES), 1)

    def body(r, carry):
        r0 = pl.multiple_of(r * blk, blk)
        raw = gs_ref[pl.ds(r0, blk), :] + bias_ref[...]
        logsig = jnp.minimum(raw, 0.0) - jnp.log1p(jnp.exp(-jnp.abs(raw)))
        local = jnp.dot(tri, logsig, precision=lax.Precision.HIGHEST, preferred_element_type=F32)
        glob = local + carry
        out = jnp.where(lane < LANE_CI, glob, jnp.where(lane < LANE_CF, raw, local))
        gc_ref[pl.ds(r0, blk), :] = out
        gr_ref[:, pl.ds(r0, blk)] = out.T[0:GATE_ROWS, :]
        g_rows, mloc_rows = [], []
        for h in range(C_HEADS):
            d = out[:, LANE_CI + h:LANE_CI + h + 1] - out[:, LANE_CF + h:LANE_CF + h + 1]
            u = jnp.broadcast_to(d, (blk, LANES))
            u_ref[h, pl.ds(r0, blk), :] = u
            g = jnp.broadcast_to(out[blk - 1:blk, LANE_CF + h:LANE_CF + h + 1], (1, LANES))
            g_rows.append(g)
            mloc_rows.append(g + jnp.max(u, axis=0, keepdims=True))
        gsum_ref[r] = jnp.concatenate(g_rows + mloc_rows, axis=0)
        return glob[blk - 1:blk, :]

    lax.fori_loop(0, seq // blk, body, jnp.zeros((1, LANES), F32), unroll=4)


def _gates_call(gs3d, bias, layer):
    b, seq, _ = gs3d.shape
    nchunk = seq // MLSTM_CHUNK
    return pl.pallas_call(
        functools.partial(_gates_kernel, seq=seq),
        grid=(b,),
        in_specs=[
            pl.BlockSpec((None, seq, LANES), lambda bi: (bi, 0, 0)),
            _const_spec((1, LANES), layer),
        ],
        out_specs=[
            pl.BlockSpec((None, seq, LANES), lambda bi: (bi, 0, 0)),
            pl.BlockSpec((None, GATE_ROWS, seq), lambda bi: (bi, 0, 0)),
            pl.BlockSpec((None, C_HEADS, seq, LANES), lambda bi: (bi, 0, 0, 0)),
            pl.BlockSpec((None, nchunk, 2 * C_HEADS, LANES), lambda bi: (bi, 0, 0, 0)),
        ],
        out_shape=[
            jax.ShapeDtypeStruct((b, seq, LANES), F32),
            jax.ShapeDtypeStruct((b, GATE_ROWS, seq), F32),
            jax.ShapeDtypeStruct((b, C_HEADS, seq, LANES), F32),
            jax.ShapeDtypeStruct((b, nchunk, 2 * C_HEADS, LANES), F32),
        ],
        compiler_params=_cparams(("arbitrary",)),
        name="gates",
    )(gs3d, bias)


FOX_VROWS = 80
FOX_KMULT = 2
FOX_BODY = 2
FOX_HEADS = 8


def _fox_kernel(q_ref, k_ref, v_ref, gc_ref, place_ref, o_ref, kaug_ref, vt_ref, s_ref, *, t, seq):
    qi = pl.program_id(2)
    hd = B_HEAD_DIM
    nh = FOX_HEADS
    lane = lax.broadcasted_iota(jnp.int32, (1, LANES), 1)
    own = (lane < hd, lane >= hd)
    aug0 = (hd, 0)

    @pl.when(qi == 0)
    def _():
        sub = lax.broadcasted_iota(jnp.int32, (FOX_VROWS - hd, seq), 0)
        tail = jnp.where(sub == 0, 1.0, 0.0).astype(BF16)
        for hh in range(nh):
            vt_ref[hh, hd:FOX_VROWS, :] = tail
        eye = (lax.broadcasted_iota(jnp.int32, (LANES, LANES), 0)
               == lax.broadcasted_iota(jnp.int32, (LANES, LANES), 1)).astype(F32).astype(BF16)

        def body(r, _):
            r0 = pl.multiple_of(r * LANES, LANES)
            neg = -LOG2E * gc_ref[pl.ds(r0, LANES), :]
            hi = neg.astype(BF16).astype(F32)
            mid = (neg - hi).astype(BF16).astype(F32)
            lo = (neg - hi) - mid
            pieces = jnp.where(lane < B_HEADS, hi,
                               jnp.where(lane < 2 * B_HEADS, pltpu.roll(mid, B_HEADS, 1),
                                         pltpu.roll(lo, 2 * B_HEADS, 1))).astype(BF16)
            extra = _dot(pieces, place_ref[...]).astype(BF16)
            for pp in range(nh // 2):
                ls = slice(pp * LANES, (pp + 1) * LANES)
                kb = k_ref[pl.ds(r0, LANES), ls]
                vt = _dot_nt(eye, v_ref[pl.ds(r0, LANES), ls])
                for h in range(2):
                    hh = 2 * pp + h
                    kaug_ref[hh, pl.ds(r0, LANES), :] = jnp.where(
                        own[h], kb, extra[:, hh * LANES:(hh + 1) * LANES])
                    vt_ref[hh, 0:hd, pl.ds(r0, LANES)] = vt[h * hd:(h + 1) * hd, :].astype(BF16)
            return 0

        lax.fori_loop(0, seq // LANES, body, 0, unroll=4)

    qa = []
    for pp in range(nh // 2):
        q = q_ref[:, pp * LANES:(pp + 1) * LANES].astype(F32)
        for h in range(2):
            a = aug0[h]
            ones3 = jnp.where((lane >= a) & (lane < a + 3), 1.0, 0.0)
            qa.append(jnp.where(own[h], q, ones3).astype(BF16))
    row = lax.broadcasted_iota(jnp.int32, (t, t), 0)
    col = lax.broadcasted_iota(jnp.int32, (t, t), 1)
    causal = row <= col

    def qk(hh, k0, tk):
        return _dot_nt(kaug_ref[hh, pl.ds(k0, tk), :], qa[hh])

    def update(k0, tk, state, masked, slot):
        k0 = pl.multiple_of(k0, t)
        for hh in range(nh):
            s_ref[slot, hh, 0:tk, :] = qk(hh, k0, tk)
        ms, ps, alphas = [], [], []
        for hh in range(nh):
            m = state[hh][0]
            load = lambda: (jnp.where(causal, s_ref[slot, hh, 0:tk, :], NEG) if masked
                            else s_ref[slot, hh, 0:tk, :])
            m_new = jnp.maximum(m, jnp.max(load(), axis=0, keepdims=True))
            ps.append(jnp.exp2((load() - m_new).astype(BF16)))
            alphas.append(jnp.exp2(m - m_new))
            ms.append(m_new)
        pvs = [_dot(vt_ref[hh, :, pl.ds(k0, tk)], ps[hh]) for hh in range(nh)]
        return tuple((ms[hh], alphas[hh] * state[hh][1] + pvs[hh]) for hh in range(nh))

    km = FOX_KMULT
    nb = FOX_BODY
    init = tuple((jnp.full((1, t), NEG, F32), jnp.zeros((FOX_VROWS, t), F32)) for _ in range(nh))

    def body(j, state):
        for i in range(nb):
            state = update((j * nb + i) * km * t, km * t, state, False, i % 2)
        return state

    nw = qi // km
    state = lax.fori_loop(0, nw // nb, body, init)

    def tail(n_wide, n_single):
        def run(state):
            for i in range(n_wide):
                state = update((nw - n_wide + i) * km * t, km * t, state, False, i % 2)
            for i in range(n_single):
                state = update((qi - n_single + i) * t, t, state, False, (n_wide + i) % 2)
            return update(qi * t, t, state, True, (n_wide + n_single) % 2)
        return run

    state = lax.switch((nw % nb) * km + qi % km,
                       [tail(w, s) for w in range(nb) for s in range(km)], state)
    ot = jnp.concatenate([acc[0:hd] / acc[hd:hd + 1] for _, acc in state], axis=0)
    o_ref[...] = ot.T.astype(BF16)


def _fox_place_matrix():
    assert FOX_HEADS == B_HEADS and LANE_BF == 0
    r = jnp.arange(LANES)[:, None]
    c = jnp.arange(B_HEADS * LANES)[None, :]
    head = c // LANES
    aug0 = jnp.where(head % 2 == 0, B_HEAD_DIM, 0)
    piece = c % LANES - aug0
    return ((piece >= 0) & (piece < 3) & (r == piece * B_HEADS + head)).astype(BF16)


def _fox_call(p3d, gc, t):
    b, seq, _ = p3d.shape
    w = FOX_HEADS * B_HEAD_DIM
    return pl.pallas_call(
        functools.partial(_fox_kernel, t=t, seq=seq),
        grid=(b, B_HEADS // FOX_HEADS, seq // t),
        in_specs=[
            pl.BlockSpec((None, t, w), lambda bi, hg, qi: (bi, qi, COL_BQ // w + hg)),
            pl.BlockSpec((None, seq, w), lambda bi, hg, qi: (bi, 0, COL_BK // w + hg)),
            pl.BlockSpec((None, seq, w), lambda bi, hg, qi: (bi, 0, COL_BV // w + hg)),
            pl.BlockSpec((None, seq, LANES), lambda bi, hg, qi: (bi, 0, 0)),
            _const_spec((LANES, FOX_HEADS * LANES)),
        ],
        out_specs=pl.BlockSpec((None, t, w), lambda bi, hg, qi: (bi, qi, hg)),
        out_shape=jax.ShapeDtypeStruct((b, seq, BRANCH_WIDTH), BF16),
        scratch_shapes=[
            pltpu.VMEM((FOX_HEADS, seq, LANES), BF16),
            pltpu.VMEM((FOX_HEADS, FOX_VROWS, seq), BF16),
            pltpu.VMEM((2, FOX_HEADS, FOX_KMULT * t, t), F32),
        ],
        compiler_params=_cparams(("arbitrary", "arbitrary", "arbitrary")),
        name="fox",
    )(p3d, p3d, p3d, gc, _fox_place_matrix())


MLSTM_ROWS = 144


def _mlstm_kernel(q_ref, k_ref, v_ref, o_ref, u_ref, gr_ref, gsum_ref, gmh_ref, y_ref, c_ref, m_ref,
                  *, ts):
    L = MLSTM_CHUNK
    dh = C_HEAD_DIM
    scale = dh ** -0.5
    si = pl.program_id(1)

    @pl.when(si == 0)
    def _():
        c_ref[...] = jnp.zeros_like(c_ref)
        m_ref[...] = jnp.zeros_like(m_ref)

    row = lax.broadcasted_iota(jnp.int32, (L, L), 0)
    col = lax.broadcasted_iota(jnp.int32, (L, L), 1)
    causal = row <= col
    sub = lax.broadcasted_iota(jnp.int32, (MLSTM_ROWS - dh, L), 0)
    tail = jnp.where(sub == 0, 1.0, 0.0)

    def chunk(c, _):
        r0 = pl.multiple_of(c * L, L)
        gsum = gsum_ref[c]
        for h in range(C_HEADS):
            hs = slice(h * dh, (h + 1) * dh)
            q = q_ref[pl.ds(r0, L), hs]
            k = k_ref[pl.ds(r0, L), hs]
            vt = jnp.concatenate([v_ref[pl.ds(r0, L), hs].astype(F32).T, tail], axis=0)
            ig = gr_ref[pl.ds(LANE_CI + h, 1), pl.ds(r0, L)]
            b = gr_ref[pl.ds(LANE_CF + h, 1), pl.ds(r0, L)]
            g = gsum[h:h + 1, :]
            m_loc = gsum[C_HEADS + h:C_HEADS + h + 1, :]
            m_in = m_ref[h]
            c_in = c_ref[h]

            dlog = jnp.where(causal, u_ref[h, pl.ds(r0, L), :] + b, NEG)
            inter = b + m_in
            m_t = jnp.maximum(jnp.max(dlog, axis=0, keepdims=True), inter)
            sm = (_dot_nt(k, q) * scale) * jnp.exp(dlog - m_t)
            w_int = jnp.exp(inter - m_t)
            ext = _dot(vt.astype(BF16), sm.astype(BF16)) + w_int * _dot_nt(c_in.astype(BF16), q)
            hh = ext[0:dh] / jnp.maximum(jnp.abs(ext[dh:dh + 1]), jnp.exp(-m_t))
            hn = hh * lax.rsqrt(jnp.mean(hh * hh, axis=0, keepdims=True) + EPS) * gmh_ref[h]
            y_ref[pl.ds(r0, L), hs] = (hn.T * o_ref[pl.ds(r0, L), hs].astype(F32)).astype(BF16)

            m_new = jnp.maximum(g + m_in, m_loc)
            w = jnp.exp(g + (ig - b) - m_new) * scale
            c_ref[h] = jnp.exp(g + m_in - m_new) * c_in + _dot((vt * w).astype(BF16), k)
            m_ref[h] = m_new
        return 0

    lax.fori_loop(0, ts // L, chunk, 0, unroll=8)


def _mlstm_call(p3d, u, gr, gsum, gmh_rep, layer, ts):
    b, seq, _ = p3d.shape
    w = BRANCH_WIDTH
    nc = ts // MLSTM_CHUNK

    def pspec(col):
        return pl.BlockSpec((None, ts, w), lambda bi, si: (bi, si, col // w))

    return pl.pallas_call(
        functools.partial(_mlstm_kernel, ts=ts),
        grid=(b, seq // ts),
        in_specs=[
            pspec(COL_CQ), pspec(COL_CK), pspec(COL_CV), pspec(COL_CO),
            pl.BlockSpec((None, C_HEADS, ts, LANES), lambda bi, si: (bi, 0, si, 0)),
            pl.BlockSpec((None, GATE_ROWS, ts), lambda bi, si: (bi, 0, si)),
            pl.BlockSpec((None, nc, 2 * C_HEADS, LANES), lambda bi, si: (bi, si, 0, 0)),
            _const_spec((C_HEADS, C_HEAD_DIM, LANES), layer),
        ],
        out_specs=pl.BlockSpec((None, ts, w), lambda bi, si: (bi, si, 0)),
        out_shape=jax.ShapeDtypeStruct((b, seq, w), BF16),
        scratch_shapes=[
            pltpu.VMEM((C_HEADS, MLSTM_ROWS, C_HEAD_DIM), F32),
            pltpu.VMEM((C_HEADS, 1, LANES), F32),
        ],
        compiler_params=_cparams(("arbitrary", "arbitrary")),
        name="mlstm",
    )(p3d, p3d, p3d, p3d, u, gr, gsum, gmh_rep)


def _merge_kernel(gates_ref, uv_ref, yb_ref, yc_ref, x_ref, gsgu_ref, ws_ref, bs_ref, wb_ref, wo_ref,
                  gq_ref, wq_ref, kv_ref, wmo_ref, out_ref, ya_ref, o_ref, *, tm):
    w = BRANCH_WIDTH
    u = uv_ref[:, :w].astype(F32)
    v = uv_ref[:, w:].astype(F32)
    vn = _rms(v, gsgu_ref[...]).astype(BF16)
    gd = w // A_GROUPS
    for nb in range(tm // A_BLOCK):
        rs = slice(nb * A_BLOCK, (nb + 1) * A_BLOCK)
        mixed = jnp.concatenate(
            [_dot(ws_ref[g], vn[rs, g * gd:(g + 1) * gd]) for g in range(A_GROUPS)], axis=1)
        ya_ref[rs, :] = (u[rs, :] * (mixed + bs_ref[...])).astype(BF16)
    merged = gates_ref[:, 0:D_MODEL].astype(F32) * _dot(ya_ref[...], wb_ref[0])
    merged += gates_ref[:, D_MODEL:2 * D_MODEL].astype(F32) * _dot(yb_ref[...], wb_ref[1])
    merged += gates_ref[:, 2 * D_MODEL:3 * D_MODEL].astype(F32) * _dot(yc_ref[...], wb_ref[2])
    x = x_ref[...] + _dot(merged.astype(BF16), wo_ref[...])

    h = _rms(x, gq_ref[...]).astype(BF16)
    q = (_dot(h, wq_ref[...]) * (MEM_HEAD_DIM ** -0.5)).astype(BF16)
    dh = MEM_HEAD_DIM
    for hd in range(MEM_HEADS):
        hs = slice(hd * dh, (hd + 1) * dh)
        s = _dot_nt(q[:, hs], kv_ref[:, hs])
        p = jnp.exp(s - jnp.max(s, axis=-1, keepdims=True))
        o = _dot(p.astype(BF16), kv_ref[:, D_MODEL + hd * dh:D_MODEL + (hd + 1) * dh])
        o_ref[:, hs] = (o / jnp.sum(p, axis=-1, keepdims=True)).astype(BF16)
    out_ref[...] = x + _dot(o_ref[...], wmo_ref[...])


def _merge_call(p2d, yb, yc, x2d, g_sgu, ws_masked, bs_full, w_branch, w_out, g_mq, w_mq, kv, w_mo,
                layer, seq, tm):
    m = x2d.shape[0]
    w = BRANCH_WIDTH
    return pl.pallas_call(
        functools.partial(_merge_kernel, tm=tm),
        grid=(m // tm,),
        in_specs=[
            pl.BlockSpec((tm, 3 * D_MODEL), lambda i: (i, COL_G // (3 * D_MODEL))),
            pl.BlockSpec((tm, 2 * w), lambda i: (i, COL_A // (2 * w))),
            pl.BlockSpec((tm, w), lambda i: (i, 0)),
            pl.BlockSpec((tm, w), lambda i: (i, 0)),
            pl.BlockSpec((tm, D_MODEL), lambda i: (i, 0)),
            _const_spec((1, w), layer),
            _const_spec((A_GROUPS, A_BLOCK, A_BLOCK), layer),
            _const_spec((A_BLOCK, w), layer),
            _const_spec((3, w, D_MODEL), layer),
            _const_spec((D_MODEL, D_MODEL), layer),
            _const_spec((1, D_MODEL), layer),
            _const_spec((D_MODEL, D_MODEL), layer),
            pl.BlockSpec((N_MEM, 2 * D_MODEL), lambda i: ((i * tm) // seq, 0)),
            _const_spec((D_MODEL, D_MODEL), layer),
        ],
        out_specs=pl.BlockSpec((tm, D_MODEL), lambda i: (i, 0)),
        out_shape=jax.ShapeDtypeStruct((m, D_MODEL), F32),
        scratch_shapes=[pltpu.VMEM((tm, w), BF16), pltpu.VMEM((tm, D_MODEL), BF16)],
        compiler_params=_cparams(("arbitrary",)),
        name="merge",
    )(p2d, p2d, yb, yc, x2d, g_sgu, ws_masked, bs_full, w_branch, w_out, g_mq, w_mq, kv, w_mo)


def _memkv_kernel(mem_ref, g_ref, w_ref, kv_ref):
    kv_ref[...] = _dot(_rms(mem_ref[...], g_ref[...]).astype(BF16), w_ref[...]).astype(BF16)


def _memkv_call(mem2d, g, w_mkv, layer):
    m = mem2d.shape[0]
    return pl.pallas_call(
        _memkv_kernel,
        grid=(m // N_MEM,),
        in_specs=[
            pl.BlockSpec((N_MEM, D_MODEL), lambda i: (i, 0)),
            _const_spec((1, D_MODEL), layer),
            _const_spec((D_MODEL, 2 * D_MODEL), layer),
        ],
        out_specs=pl.BlockSpec((N_MEM, 2 * D_MODEL), lambda i: (i, 0)),
        out_shape=jax.ShapeDtypeStruct((m, 2 * D_MODEL), BF16),
        compiler_params=_cparams(("arbitrary",)),
        name="memkv",
    )(mem2d, g, w_mkv)


def _ffn_kernel(x_ref, g_ref, wup_ref, wconv_ref, wdown_ref, gfin_ref, out_ref,
                act_ref, ext_ref, carry_ref, *, tm, tf, seq, final_norm):
    i = pl.program_id(0)
    x = x_ref[...]
    h = _rms(x, g_ref[...]).astype(BF16)
    seq_start = (i * tm) % seq == 0

    def conv(slot, half, cs):
        up = _dot(h, wup_ref[:, cs])
        ext_ref[half, pl.ds(CARRY_ROWS, tm), :] = up
        ext_ref[half, pl.ds(0, CARRY_ROWS), :] = jnp.where(seq_start, 0.0, carry_ref[slot])
        carry_ref[slot] = up[tm - CARRY_ROWS:, :]
        wc = wconv_ref[:, cs]
        y = wc[FFN_CONV - 1:FFN_CONV, :] * up
        for d in range(1, FFN_CONV):
            y = y + wc[FFN_CONV - 1 - d:FFN_CONV - d, :] * ext_ref[half, pl.ds(CARRY_ROWS - d, tm), :]
        return y

    nchunk = D_FF // tf
    for c in range(nchunk):
        a = conv(c, 0, slice(c * tf, (c + 1) * tf))
        b = conv(nchunk + c, 1, slice(D_FF + c * tf, D_FF + (c + 1) * tf))
        act_ref[:, c * tf:(c + 1) * tf] = (a * _sigmoid(a) * b).astype(BF16)
    y = x + _dot(act_ref[...], wdown_ref[...])
    if final_norm:
        y = _rms(y, gfin_ref[...])
    out_ref[...] = y


def _ffn_call(x2d, g, w_up, w_conv, w_down, g_final, layer, seq, tm, tf, final_norm):
    m = x2d.shape[0]
    return pl.pallas_call(
        functools.partial(_ffn_kernel, tm=tm, tf=tf, seq=seq, final_norm=final_norm),
        grid=(m // tm,),
        in_specs=[
            pl.BlockSpec((tm, D_MODEL), lambda i: (i, 0)),
            _const_spec((1, D_MODEL), layer),
            _const_spec((D_MODEL, 2 * D_FF), layer),
            _const_spec((FFN_CONV, 2 * D_FF), layer),
            _const_spec((D_FF, D_MODEL), layer),
            _const_spec((1, D_MODEL)),
        ],
        out_specs=pl.BlockSpec((tm, D_MODEL), lambda i: (i, 0)),
        out_shape=jax.ShapeDtypeStruct((m, D_MODEL), F32),
        scratch_shapes=[
            pltpu.VMEM((tm, D_FF), BF16),
            pltpu.VMEM((2, tm + CARRY_ROWS, tf), F32),
            pltpu.VMEM((2 * (D_FF // tf), CARRY_ROWS, tf), F32),
        ],
        compiler_params=_cparams(("arbitrary",)),
        name="ffn",
    )(x2d, g, w_up, w_conv, w_down, g_final)


def _rearrange_w_in(w):
    bw = BRANCH_WIDTH
    a0 = 0
    b0 = 2 * bw
    c0 = b0 + 3 * bw + B_HEADS
    g0 = c0 + 3 * bw + 2 * C_HEADS + bw
    co = c0 + 3 * bw + 2 * C_HEADS
    w = w.astype(BF16)
    main = jnp.concatenate([
        w[..., g0:g0 + 3 * D_MODEL],
        w[..., a0:a0 + 2 * bw],
        w[..., b0:b0 + 3 * bw],
        w[..., c0:c0 + 3 * bw],
        w[..., co:co + bw],
    ], axis=-1)
    small = jnp.concatenate([
        w[..., b0 + 3 * bw:b0 + 3 * bw + B_HEADS],
        w[..., c0 + 3 * bw:c0 + 3 * bw + 2 * C_HEADS],
        jnp.zeros(w.shape[:-1] + (LANES - GATE_ROWS,), w.dtype),
    ], axis=-1)
    return main, small


def kernel(x, mem, g_mix, w_in, g_sgu, w_s, b_s, b_fox_f, w_conv_c, b_mlstm_i, b_mlstm_f, g_mh,
           w_branch, w_out, g_mem_q, g_mem_kv, w_mq, w_mkv, w_mo, g_ffn, w_up, w_ffn_conv, w_down,
           g_final):
    bsz, seq, _ = x.shape
    depth = w_in.shape[0]
    m = bsz * seq
    tm = min(512, seq)
    tm_proj = tm
    tm_ffn = tm
    t_fox = min(256, seq)
    ts_mlstm = min(1024, seq)

    idx = jnp.arange(A_BLOCK)
    chunk_causal = (idx[None, :] // CHUNK) <= (idx[:, None] // CHUNK)

    x2d = x.reshape(m, D_MODEL)
    mem2d = mem.reshape(bsz * N_MEM, D_MODEL)
    rows = lambda a: a.reshape(depth, 1, -1)

    w_main, w_small = _rearrange_w_in(w_in)
    gate_bias = jnp.concatenate(
        [b_fox_f, b_mlstm_i, b_mlstm_f, jnp.zeros((depth, LANES - GATE_ROWS), F32)], axis=-1)
    ws_masked = jnp.where(chunk_causal, w_s, 0).astype(BF16)
    bs_full = jnp.repeat(jnp.swapaxes(b_s, 1, 2), BRANCH_WIDTH // A_GROUPS, axis=2)
    gmh_rep = jnp.broadcast_to(g_mh.reshape(depth, C_HEADS, C_HEAD_DIM, 1),
                               (depth, C_HEADS, C_HEAD_DIM, LANES))
    w_branch, w_out, w_mq, w_mkv, w_mo, w_up, w_down = (
        a.astype(BF16) for a in (w_branch, w_out, w_mq, w_mkv, w_mo, w_up, w_down))

    for i in range(depth):
        p2d, gs = _proj_call(x2d, rows(g_mix), w_main, w_small, w_conv_c, i, seq, tm_proj)
        p3d = p2d.reshape(bsz, seq, PROJ_COLS)
        gc, gr, u, gsum = _gates_call(gs.reshape(bsz, seq, LANES), rows(gate_bias), i)
        yb = _fox_call(p3d, gc, t_fox).reshape(m, BRANCH_WIDTH)
        yc = _mlstm_call(p3d, u, gr, gsum, gmh_rep, i, ts_mlstm).reshape(m, BRANCH_WIDTH)
        kv = _memkv_call(mem2d, rows(g_mem_kv), w_mkv, i)
        x2d = _merge_call(p2d, yb, yc, x2d, rows(g_sgu), ws_masked, bs_full, w_branch, w_out,
                          rows(g_mem_q), w_mq, kv, w_mo, i, seq, tm)
        x2d = _ffn_call(x2d, rows(g_ffn), w_up, w_ffn_conv, w_down, g_final.reshape(1, -1), i, seq, tm_ffn,
                        256, i == depth - 1)
    return x2d.reshape(bsz, seq, D_MODEL)
```

```python
import functools

import jax
import jax.numpy as jnp
from jax import lax
from jax.experimental import pallas as pl
from jax.experimental.pallas import tpu as pltpu

F32 = jnp.float32
BF16 = jnp.bfloat16

D_MODEL = 1024
EPS = 1e-6
LANES = 128
CARRY_ROWS = 8

BRANCH_WIDTH = 512
A_BLOCK = 128
A_GROUPS = 4
CHUNK = 64
B_HEADS = 8
B_HEAD_DIM = 64
C_HEADS = 4
C_HEAD_DIM = 128
C_CONV = 4
MLSTM_CHUNK = 128
N_MEM = 256
MEM_HEADS = 4
MEM_HEAD_DIM = 256
D_FF = 2816
FFN_CONV = 3

PROJ_TN = 512
COL_G = 0
COL_A = 3072
COL_BQ = 4096
COL_BK = 4608
COL_BV = 5120
COL_CQ = 5632
COL_CK = 6144
COL_CV = 6656
COL_CO = 7168
PROJ_COLS = 7680
LANE_BF = 0
LANE_CI = 8
LANE_CF = 12
GATE_ROWS = 16

LOG2E = 1.4426950408889634
FOX_Q_SCALE = B_HEAD_DIM ** -0.5 * LOG2E
NEG = -1e30
VMEM_LIMIT = 56 * 1024 * 1024


def _cparams(sem):
    return pltpu.CompilerParams(dimension_semantics=sem, vmem_limit_bytes=VMEM_LIMIT)


def _rms(xf, g):
    return xf * lax.rsqrt(jnp.mean(xf * xf, axis=-1, keepdims=True) + EPS) * g


def _sigmoid(x):
    return 0.5 * jnp.tanh(0.5 * x) + 0.5


def _gelu_tanh(x):
    return 0.5 * x * (1.0 + jnp.tanh(0.7978845608028654 * (x + 0.044715 * (x * x * x))))


def _dot(a, b):
    return jnp.dot(a, b, preferred_element_type=F32)


def _dot_nt(a, b):
    return lax.dot_general(a, b, (((1,), (1,)), ((), ())), preferred_element_type=F32)


def _const_spec(shape, layer=None):
    nd = len(shape)
    if layer is None:
        return pl.BlockSpec(shape, lambda *_: (0,) * nd, pipeline_mode=pl.Buffered(1))
    return pl.BlockSpec((None,) + tuple(shape), lambda *_: (layer,) + (0,) * nd,
                        pipeline_mode=pl.Buffered(1))


def _proj_kernel(x_ref, g_ref, w_ref, ws_ref, wc_ref, p_ref, gs_ref, ext_ref, carry_ref, *, tm, seq):
    i = pl.program_id(0)
    h = _rms(x_ref[...], g_ref[...]).astype(BF16)
    gs_ref[...] = _dot(h, ws_ref[...])
    seq_start = (i * tm) % seq == 0

    def conv_silu(acc, slot):
        ext_ref[slot, pl.ds(CARRY_ROWS, tm), :] = acc
        ext_ref[slot, pl.ds(0, CARRY_ROWS), :] = jnp.where(seq_start, 0.0, carry_ref[slot])
        carry_ref[slot] = ext_ref[slot, pl.ds(tm, CARRY_ROWS), :]
        wc = 0.5 * wc_ref[:, slot * PROJ_TN:(slot + 1) * PROJ_TN]
        half = wc[C_CONV - 1:C_CONV, :] * ext_ref[slot, pl.ds(CARRY_ROWS, tm), :]
        for d in range(1, C_CONV):
            half = half + wc[C_CONV - 1 - d:C_CONV - d, :] * ext_ref[slot, pl.ds(CARRY_ROWS - d, tm), :]
        return half * (jnp.tanh(half) + 1.0)

    order = [COL_CQ, COL_CK] + [c for c in range(0, PROJ_COLS, PROJ_TN) if c not in (COL_CQ, COL_CK)]
    for c0 in order:
        cs = slice(c0, c0 + PROJ_TN)
        acc = _dot(h, w_ref[:, cs])
        if c0 < COL_A or c0 == COL_CO:
            out = _sigmoid(acc)
        elif c0 < COL_BQ:
            out = _gelu_tanh(acc)
        elif c0 == COL_BQ:
            out = acc * FOX_Q_SCALE
        elif c0 == COL_CQ:
            out = conv_silu(acc, 0)
        elif c0 == COL_CK:
            out = conv_silu(acc, 1)
        else:
            out = acc
        p_ref[:, cs] = out.astype(BF16)


def _proj_call(x2d, g, w_main, w_small, w_conv, layer, seq, tm):
    m = x2d.shape[0]
    return pl.pallas_call(
        functools.partial(_proj_kernel, tm=tm, seq=seq),
        grid=(m // tm,),
        in_specs=[
            pl.BlockSpec((tm, D_MODEL), lambda i: (i, 0)),
            _const_spec((1, D_MODEL), layer),
            _const_spec((D_MODEL, PROJ_COLS), layer),
            _const_spec((D_MODEL, LANES), layer),
            _const_spec((C_CONV, 2 * PROJ_TN), layer),
        ],
        out_specs=[
            pl.BlockSpec((tm, PROJ_COLS), lambda i: (i, 0)),
            pl.BlockSpec((tm, LANES), lambda i: (i, 0)),
        ],
        out_shape=[
            jax.ShapeDtypeStruct((m, PROJ_COLS), BF16),
            jax.ShapeDtypeStruct((m, LANES), F32),
        ],
        scratch_shapes=[
            pltpu.VMEM((2, tm + CARRY_ROWS, PROJ_TN), F32),
            pltpu.VMEM((2, CARRY_ROWS, PROJ_TN), F32),
        ],
        compiler_params=_cparams(("arbitrary",)),
        name="proj",
    )(x2d, g, w_main, w_small, w_conv)


def _gates_kernel(gs_ref, bias_ref, gc_ref, gr_ref, u_ref, gsum_ref, *, seq):
    blk = MLSTM_CHUNK
    row = lax.broadcasted_iota(jnp.int32, (blk, blk), 0)
    col = lax.broadcasted_iota(jnp.int32, (blk, blk), 1)
    tri = (col <= row).astype(F32)
    lane = lax.broadcasted_iota(jnp.int32, (1, LANES), 1)

    def body(r, carry):
        r0 = pl.multiple_of(r * blk, blk)
        raw = gs_ref[pl.ds(r0, blk), :] + bias_ref[...]
        logsig = jnp.minimum(raw, 0.0) - jnp.log1p(jnp.exp(-jnp.abs(raw)))
        local = jnp.dot(tri, logsig, precision=lax.Precision.HIGHEST, preferred_element_type=F32)
        glob = local + carry
        out = jnp.where(lane < LANE_CI, glob, jnp.where(lane < LANE_CF, raw, local))
        gc_ref[pl.ds(r0, blk), :] = out
        gr_ref[:, pl.ds(r0, blk)] = out.T[0:GATE_ROWS, :]
        g_rows, mloc_rows = [], []
        for h in range(C_HEADS):
            d = out[:, LANE_CI + h:LANE_CI + h + 1] - out[:, LANE_CF + h:LANE_CF + h + 1]
            u = jnp.broadcast_to(d, (blk, LANES))
            u_ref[h, pl.ds(r0, blk), :] = u
            g = jnp.broadcast_to(out[blk - 1:blk, LANE_CF + h:LANE_CF + h + 1], (1, LANES))
            g_rows.append(g)
            mloc_rows.append(g + jnp.max(u, axis=0, keepdims=True))
        gsum_ref[r] = jnp.concatenate(g_rows + mloc_rows, axis=0)
        return glob[blk - 1:blk, :]

    lax.fori_loop(0, seq // blk, body, jnp.zeros((1, LANES), F32), unroll=4)


def _gates_call(gs3d, bias, layer):
    b, seq, _ = gs3d.shape
    nchunk = seq // MLSTM_CHUNK
    return pl.pallas_call(
        functools.partial(_gates_kernel, seq=seq),
        grid=(b,),
        in_specs=[
            pl.BlockSpec((None, seq, LANES), lambda bi: (bi, 0, 0)),
            _const_spec((1, LANES), layer),
        ],
        out_specs=[
            pl.BlockSpec((None, seq, LANES), lambda bi: (bi, 0, 0)),
            pl.BlockSpec((None, GATE_ROWS, seq), lambda bi: (bi, 0, 0)),
            pl.BlockSpec((None, C_HEADS, seq, LANES), lambda bi: (bi, 0, 0, 0)),
            pl.BlockSpec((None, nchunk, 2 * C_HEADS, LANES), lambda bi: (bi, 0, 0, 0)),
        ],
        out_shape=[
            jax.ShapeDtypeStruct((b, seq, LANES), F32),
            jax.ShapeDtypeStruct((b, GATE_ROWS, seq), F32),
            jax.ShapeDtypeStruct((b, C_HEADS, seq, LANES), F32),
            jax.ShapeDtypeStruct((b, nchunk, 2 * C_HEADS, LANES), F32),
        ],
        compiler_params=_cparams(("arbitrary",)),
        name="gates",
    )(gs3d, bias)


FOX_VROWS = 80
FOX_KMULT = 2
FOX_BODY = 2
FOX_HEADS = 8


def _fox_kernel(q_ref, k_ref, v_ref, gc_ref, place_ref, o_ref, kaug_ref, vt_ref, s_ref, m_ref, acc_ref,
                *, t, seq):
    qi = pl.program_id(2)
    hd = B_HEAD_DIM
    nh = FOX_HEADS
    lane = lax.broadcasted_iota(jnp.int32, (1, LANES), 1)
    own = (lane < hd, lane >= hd)
    aug0 = (hd, 0)

    @pl.when(qi == 0)
    def _():
        sub = lax.broadcasted_iota(jnp.int32, (FOX_VROWS - hd, seq), 0)
        tail = jnp.where(sub == 0, 1.0, 0.0).astype(BF16)
        for hh in range(nh):
            vt_ref[hh, hd:FOX_VROWS, :] = tail
        eye = (lax.broadcasted_iota(jnp.int32, (LANES, LANES), 0)
               == lax.broadcasted_iota(jnp.int32, (LANES, LANES), 1)).astype(F32).astype(BF16)

        def body(r, _):
            r0 = pl.multiple_of(r * LANES, LANES)
            neg = -LOG2E * gc_ref[pl.ds(r0, LANES), :]
            hi = neg.astype(BF16).astype(F32)
            mid = (neg - hi).astype(BF16).astype(F32)
            lo = (neg - hi) - mid
            pieces = jnp.where(lane < B_HEADS, hi,
                               jnp.where(lane < 2 * B_HEADS, pltpu.roll(mid, B_HEADS, 1),
                                         pltpu.roll(lo, 2 * B_HEADS, 1))).astype(BF16)
            extra = _dot(pieces, place_ref[...]).astype(BF16)
            for pp in range(nh // 2):
                ls = slice(pp * LANES, (pp + 1) * LANES)
                kb = k_ref[pl.ds(r0, LANES), ls]
                vt = _dot_nt(eye, v_ref[pl.ds(r0, LANES), ls])
                for h in range(2):
                    hh = 2 * pp + h
                    kaug_ref[hh, pl.ds(r0, LANES), :] = jnp.where(
                        own[h], kb, extra[:, hh * LANES:(hh + 1) * LANES])
                    vt_ref[hh, 0:hd, pl.ds(r0, LANES)] = vt[h * hd:(h + 1) * hd, :].astype(BF16)
            return 0

        lax.fori_loop(0, seq // LANES, body, 0, unroll=4)

    qa = []
    for pp in range(nh // 2):
        q = q_ref[:, pp * LANES:(pp + 1) * LANES].astype(F32)
        for h in range(2):
            a = aug0[h]
            ones3 = jnp.where((lane >= a) & (lane < a + 3), 1.0, 0.0)
            qa.append(jnp.where(own[h], q, ones3).astype(BF16))
    row = lax.broadcasted_iota(jnp.int32, (t, t), 0)
    col = lax.broadcasted_iota(jnp.int32, (t, t), 1)
    causal = row <= col

    def qk(hh, k0, tk):
        return _dot_nt(kaug_ref[hh, pl.ds(k0, tk), :], qa[hh])

    def update(k0, tk, masked, slot):
        k0 = pl.multiple_of(k0, t)
        for hh in range(nh):
            s_ref[slot, hh, 0:tk, :] = qk(hh, k0, tk)
        ps, alphas = [], []
        for hh in range(nh):
            m = m_ref[hh]
            load = lambda: (jnp.where(causal, s_ref[slot, hh, 0:tk, :], NEG) if masked
                            else s_ref[slot, hh, 0:tk, :])
            m_new = jnp.maximum(m, jnp.max(load(), axis=0, keepdims=True))
            ps.append(jnp.exp2((load() - m_new).astype(BF16)))
            alphas.append(jnp.exp2(m - m_new))
            m_ref[hh] = m_new
        for hh in range(nh):
            acc_ref[hh] = alphas[hh] * acc_ref[hh] + _dot(vt_ref[hh, :, pl.ds(k0, tk)], ps[hh])

    m_ref[...] = jnp.full(m_ref.shape, NEG, F32)
    acc_ref[...] = jnp.zeros(acc_ref.shape, F32)

    km = FOX_KMULT
    nb = FOX_BODY

    def body(j, carry):
        for i in range(nb):
            update((j * nb + i) * km * t, km * t, False, i % 2)
        return carry

    nw = qi // km
    lax.fori_loop(0, nw // nb, body, 0)

    def tail(n_wide, n_single):
        def run():
            for i in range(n_wide):
                update((nw - n_wide + i) * km * t, km * t, False, i % 2)
            for i in range(n_single):
                update((qi - n_single + i) * t, t, False, (n_wide + i) % 2)
            update(qi * t, t, True, (n_wide + n_single) % 2)
        return run

    lax.switch((nw % nb) * km + qi % km, [tail(w, s) for w in range(nb) for s in range(km)])
    ot = jnp.concatenate([acc_ref[hh, 0:hd, :] / acc_ref[hh, hd:hd + 1, :] for hh in range(nh)], axis=0)
    o_ref[...] = ot.T.astype(BF16)


def _fox_place_matrix():
    assert FOX_HEADS == B_HEADS and LANE_BF == 0
    r = jnp.arange(LANES)[:, None]
    c = jnp.arange(B_HEADS * LANES)[None, :]
    head = c // LANES
    aug0 = jnp.where(head % 2 == 0, B_HEAD_DIM, 0)
    piece = c % LANES - aug0
    return ((piece >= 0) & (piece < 3) & (r == piece * B_HEADS + head)).astype(BF16)


def _fox_call(p3d, gc, t):
    b, seq, _ = p3d.shape
    w = FOX_HEADS * B_HEAD_DIM
    return pl.pallas_call(
        functools.partial(_fox_kernel, t=t, seq=seq),
        grid=(b, B_HEADS // FOX_HEADS, seq // t),
        in_specs=[
            pl.BlockSpec((None, t, w), lambda bi, hg, qi: (bi, qi, COL_BQ // w + hg)),
            pl.BlockSpec((None, seq, w), lambda bi, hg, qi: (bi, 0, COL_BK // w + hg)),
            pl.BlockSpec((None, seq, w), lambda bi, hg, qi: (bi, 0, COL_BV // w + hg)),
            pl.BlockSpec((None, seq, LANES), lambda bi, hg, qi: (bi, 0, 0)),
            _const_spec((LANES, FOX_HEADS * LANES)),
        ],
        out_specs=pl.BlockSpec((None, t, w), lambda bi, hg, qi: (bi, qi, hg)),
        out_shape=jax.ShapeDtypeStruct((b, seq, BRANCH_WIDTH), BF16),
        scratch_shapes=[
            pltpu.VMEM((FOX_HEADS, seq, LANES), BF16),
            pltpu.VMEM((FOX_HEADS, FOX_VROWS, seq), BF16),
            pltpu.VMEM((2, FOX_HEADS, FOX_KMULT * t, t), F32),
            pltpu.VMEM((FOX_HEADS, 1, t), F32),
            pltpu.VMEM((FOX_HEADS, FOX_VROWS, t), F32),
        ],
        compiler_params=_cparams(("arbitrary", "arbitrary", "arbitrary")),
        name="fox",
    )(p3d, p3d, p3d, gc, _fox_place_matrix())


MLSTM_ROWS = 144


def _mlstm_kernel(q_ref, k_ref, v_ref, o_ref, u_ref, gr_ref, gsum_ref, gmh_ref, y_ref, c_ref, m_ref,
                  *, ts):
    L = MLSTM_CHUNK
    dh = C_HEAD_DIM
    scale = dh ** -0.5
    si = pl.program_id(1)

    @pl.when(si == 0)
    def _():
        c_ref[...] = jnp.zeros_like(c_ref)
        m_ref[...] = jnp.zeros_like(m_ref)

    row = lax.broadcasted_iota(jnp.int32, (L, L), 0)
    col = lax.broadcasted_iota(jnp.int32, (L, L), 1)
    causal = row <= col
    sub = lax.broadcasted_iota(jnp.int32, (MLSTM_ROWS - dh, L), 0)
    tail = jnp.where(sub == 0, 1.0, 0.0)

    def chunk(c, _):
        r0 = pl.multiple_of(c * L, L)
        gsum = gsum_ref[c]
        for h in range(C_HEADS):
            hs = slice(h * dh, (h + 1) * dh)
            q = q_ref[pl.ds(r0, L), hs]
            k = k_ref[pl.ds(r0, L), hs]
            vt = jnp.concatenate([v_ref[pl.ds(r0, L), hs].astype(F32).T, tail], axis=0)
            ig = gr_ref[pl.ds(LANE_CI + h, 1), pl.ds(r0, L)]
            b = gr_ref[pl.ds(LANE_CF + h, 1), pl.ds(r0, L)]
            g = gsum[h:h + 1, :]
            m_loc = gsum[C_HEADS + h:C_HEADS + h + 1, :]
            m_in = m_ref[h]
            c_in = c_ref[h]

            dlog = jnp.where(causal, u_ref[h, pl.ds(r0, L), :] + b, NEG)
            inter = b + m_in
            m_t = jnp.maximum(jnp.max(dlog, axis=0, keepdims=True), inter)
            sm = (_dot_nt(k, q) * scale) * jnp.exp(dlog - m_t)
            w_int = jnp.exp(inter - m_t)
            ext = _dot(vt.astype(BF16), sm.astype(BF16)) + w_int * _dot_nt(c_in.astype(BF16), q)
            hh = ext[0:dh] / jnp.maximum(jnp.abs(ext[dh:dh + 1]), jnp.exp(-m_t))
            hn = hh * lax.rsqrt(jnp.mean(hh * hh, axis=0, keepdims=True) + EPS) * gmh_ref[h]
            y_ref[pl.ds(r0, L), hs] = (hn.T * o_ref[pl.ds(r0, L), hs].astype(F32)).astype(BF16)

            m_new = jnp.maximum(g + m_in, m_loc)
            w = jnp.exp(g + (ig - b) - m_new) * scale
            c_ref[h] = jnp.exp(g + m_in - m_new) * c_in + _dot((vt * w).astype(BF16), k)
            m_ref[h] = m_new
        return 0

    lax.fori_loop(0, ts // L, chunk, 0, unroll=8)


def _mlstm_call(p3d, u, gr, gsum, gmh_rep, layer, ts):
    b, seq, _ = p3d.shape
    w = BRANCH_WIDTH
    nc = ts // MLSTM_CHUNK

    def pspec(col):
        return pl.BlockSpec((None, ts, w), lambda bi, si: (bi, si, col // w))

    return pl.pallas_call(
        functools.partial(_mlstm_kernel, ts=ts),
        grid=(b, seq // ts),
        in_specs=[
            pspec(COL_CQ), pspec(COL_CK), pspec(COL_CV), pspec(COL_CO),
            pl.BlockSpec((None, C_HEADS, ts, LANES), lambda bi, si: (bi, 0, si, 0)),
            pl.BlockSpec((None, GATE_ROWS, ts), lambda bi, si: (bi, 0, si)),
            pl.BlockSpec((None, nc, 2 * C_HEADS, LANES), lambda bi, si: (bi, si, 0, 0)),
            _const_spec((C_HEADS, C_HEAD_DIM, LANES), layer),
        ],
        out_specs=pl.BlockSpec((None, ts, w), lambda bi, si: (bi, si, 0)),
        out_shape=jax.ShapeDtypeStruct((b, seq, w), BF16),
        scratch_shapes=[
            pltpu.VMEM((C_HEADS, MLSTM_ROWS, C_HEAD_DIM), F32),
            pltpu.VMEM((C_HEADS, 1, LANES), F32),
        ],
        compiler_params=_cparams(("arbitrary", "arbitrary")),
        name="mlstm",
    )(p3d, p3d, p3d, p3d, u, gr, gsum, gmh_rep)


def _merge_kernel(gates_ref, uv_ref, yb_ref, yc_ref, x_ref, gsgu_ref, ws_ref, bs_ref, wb_ref, wo_ref,
                  gq_ref, wq_ref, kv_ref, wmo_ref, out_ref, ya_ref, o_ref, *, tm):
    w = BRANCH_WIDTH
    u = uv_ref[:, :w].astype(F32)
    v = uv_ref[:, w:].astype(F32)
    vn = _rms(v, gsgu_ref[...]).astype(BF16)
    gd = w // A_GROUPS
    for nb in range(tm // A_BLOCK):
        rs = slice(nb * A_BLOCK, (nb + 1) * A_BLOCK)
        mixed = jnp.concatenate(
            [_dot(ws_ref[g], vn[rs, g * gd:(g + 1) * gd]) for g in range(A_GROUPS)], axis=1)
        ya_ref[rs, :] = (u[rs, :] * (mixed + bs_ref[...])).astype(BF16)
    merged = gates_ref[:, 0:D_MODEL].astype(F32) * _dot(ya_ref[...], wb_ref[0])
    merged += gates_ref[:, D_MODEL:2 * D_MODEL].astype(F32) * _dot(yb_ref[...], wb_ref[1])
    merged += gates_ref[:, 2 * D_MODEL:3 * D_MODEL].astype(F32) * _dot(yc_ref[...], wb_ref[2])
    x = x_ref[...] + _dot(merged.astype(BF16), wo_ref[...])

    h = _rms(x, gq_ref[...]).astype(BF16)
    q = (_dot(h, wq_ref[...]) * (MEM_HEAD_DIM ** -0.5)).astype(BF16)
    dh = MEM_HEAD_DIM
    for hd in range(MEM_HEADS):
        hs = slice(hd * dh, (hd + 1) * dh)
        s = _dot_nt(q[:, hs], kv_ref[:, hs])
        p = jnp.exp(s - jnp.max(s, axis=-1, keepdims=True))
        o = _dot(p.astype(BF16), kv_ref[:, D_MODEL + hd * dh:D_MODEL + (hd + 1) * dh])
        o_ref[:, hs] = (o / jnp.sum(p, axis=-1, keepdims=True)).astype(BF16)
    out_ref[...] = x + _dot(o_ref[...], wmo_ref[...])


def _merge_call(p2d, yb, yc, x2d, g_sgu, ws_masked, bs_full, w_branch, w_out, g_mq, w_mq, kv, w_mo,
                layer, seq, tm):
    m = x2d.shape[0]
    w = BRANCH_WIDTH
    return pl.pallas_call(
        functools.partial(_merge_kernel, tm=tm),
        grid=(m // tm,),
        in_specs=[
            pl.BlockSpec((tm, 3 * D_MODEL), lambda i: (i, COL_G // (3 * D_MODEL))),
            pl.BlockSpec((tm, 2 * w), lambda i: (i, COL_A // (2 * w))),
            pl.BlockSpec((tm, w), lambda i: (i, 0)),
            pl.BlockSpec((tm, w), lambda i: (i, 0)),
            pl.BlockSpec((tm, D_MODEL), lambda i: (i, 0)),
            _const_spec((1, w), layer),
            _const_spec((A_GROUPS, A_BLOCK, A_BLOCK), layer),
            _const_spec((A_BLOCK, w), layer),
            _const_spec((3, w, D_MODEL), layer),
            _const_spec((D_MODEL, D_MODEL), layer),
            _const_spec((1, D_MODEL), layer),
            _const_spec((D_MODEL, D_MODEL), layer),
            pl.BlockSpec((N_MEM, 2 * D_MODEL), lambda i: ((i * tm) // seq, 0)),
            _const_spec((D_MODEL, D_MODEL), layer),
        ],
        out_specs=pl.BlockSpec((tm, D_MODEL), lambda i: (i, 0)),
        out_shape=jax.ShapeDtypeStruct((m, D_MODEL), F32),
        scratch_shapes=[pltpu.VMEM((tm, w), BF16), pltpu.VMEM((tm, D_MODEL), BF16)],
        compiler_params=_cparams(("arbitrary",)),
        name="merge",
    )(p2d, p2d, yb, yc, x2d, g_sgu, ws_masked, bs_full, w_branch, w_out, g_mq, w_mq, kv, w_mo)


def _memkv_kernel(mem_ref, g_ref, w_ref, kv_ref):
    kv_ref[...] = _dot(_rms(mem_ref[...], g_ref[...]).astype(BF16), w_ref[...]).astype(BF16)


def _memkv_call(mem2d, g, w_mkv, layer):
    m = mem2d.shape[0]
    return pl.pallas_call(
        _memkv_kernel,
        grid=(m // N_MEM,),
        in_specs=[
            pl.BlockSpec((N_MEM, D_MODEL), lambda i: (i, 0)),
            _const_spec((1, D_MODEL), layer),
            _const_spec((D_MODEL, 2 * D_MODEL), layer),
        ],
        out_specs=pl.BlockSpec((N_MEM, 2 * D_MODEL), lambda i: (i, 0)),
        out_shape=jax.ShapeDtypeStruct((m, 2 * D_MODEL), BF16),
        compiler_params=_cparams(("arbitrary",)),
        name="memkv",
    )(mem2d, g, w_mkv)


def _ffn_kernel(x_ref, g_ref, wup_ref, wconv_ref, wdown_ref, gfin_ref, out_ref,
                act_ref, ext_ref, carry_ref, *, tm, tf, seq, final_norm):
    i = pl.program_id(0)
    x = x_ref[...]
    h = _rms(x, g_ref[...]).astype(BF16)
    seq_start = (i * tm) % seq == 0

    def conv(slot, half, cs):
        up = _dot(h, wup_ref[:, cs])
        ext_ref[half, pl.ds(CARRY_ROWS, tm), :] = up
        ext_ref[half, pl.ds(0, CARRY_ROWS), :] = jnp.where(seq_start, 0.0, carry_ref[slot])
        carry_ref[slot] = up[tm - CARRY_ROWS:, :]
        wc = wconv_ref[:, cs]
        y = wc[FFN_CONV - 1:FFN_CONV, :] * up
        for d in range(1, FFN_CONV):
            y = y + wc[FFN_CONV - 1 - d:FFN_CONV - d, :] * ext_ref[half, pl.ds(CARRY_ROWS - d, tm), :]
        return y

    nchunk = D_FF // tf
    for c in range(nchunk):
        a = conv(c, 0, slice(c * tf, (c + 1) * tf))
        b = conv(nchunk + c, 1, slice(D_FF + c * tf, D_FF + (c + 1) * tf))
        act_ref[:, c * tf:(c + 1) * tf] = (a * _sigmoid(a) * b).astype(BF16)
    y = x + _dot(act_ref[...], wdown_ref[...])
    if final_norm:
        y = _rms(y, gfin_ref[...])
    out_ref[...] = y


def _ffn_call(x2d, g, w_up, w_conv, w_down, g_final, layer, seq, tm, tf, final_norm):
    m = x2d.shape[0]
    return pl.pallas_call(
        functools.partial(_ffn_kernel, tm=tm, tf=tf, seq=seq, final_norm=final_norm),
        grid=(m // tm,),
        in_specs=[
            pl.BlockSpec((tm, D_MODEL), lambda i: (i, 0)),
            _const_spec((1, D_MODEL), layer),
            _const_spec((D_MODEL, 2 * D_FF), layer),
            _const_spec((FFN_CONV, 2 * D_FF), layer),
            _const_spec((D_FF, D_MODEL), layer),
            _const_spec((1, D_MODEL)),
        ],
        out_specs=pl.BlockSpec((tm, D_MODEL), lambda i: (i, 0)),
        out_shape=jax.ShapeDtypeStruct((m, D_MODEL), F32),
        scratch_shapes=[
            pltpu.VMEM((tm, D_FF), BF16),
            pltpu.VMEM((2, tm + CARRY_ROWS, tf), F32),
            pltpu.VMEM((2 * (D_FF // tf), CARRY_ROWS, tf), F32),
        ],
        compiler_params=_cparams(("arbitrary",)),
        name="ffn",
    )(x2d, g, w_up, w_conv, w_down, g_final)


def _rearrange_w_in(w):
    bw = BRANCH_WIDTH
    a0 = 0
    b0 = 2 * bw
    c0 = b0 + 3 * bw + B_HEADS
    g0 = c0 + 3 * bw + 2 * C_HEADS + bw
    co = c0 + 3 * bw + 2 * C_HEADS
    w = w.astype(BF16)
    main = jnp.concatenate([
        w[..., g0:g0 + 3 * D_MODEL],
        w[..., a0:a0 + 2 * bw],
        w[..., b0:b0 + 3 * bw],
        w[..., c0:c0 + 3 * bw],
        w[..., co:co + bw],
    ], axis=-1)
    small = jnp.concatenate([
        w[..., b0 + 3 * bw:b0 + 3 * bw + B_HEADS],
        w[..., c0 + 3 * bw:c0 + 3 * bw + 2 * C_HEADS],
        jnp.zeros(w.shape[:-1] + (LANES - GATE_ROWS,), w.dtype),
    ], axis=-1)
    return main, small


def kernel(x, mem, g_mix, w_in, g_sgu, w_s, b_s, b_fox_f, w_conv_c, b_mlstm_i, b_mlstm_f, g_mh,
           w_branch, w_out, g_mem_q, g_mem_kv, w_mq, w_mkv, w_mo, g_ffn, w_up, w_ffn_conv, w_down,
           g_final):
    bsz, seq, _ = x.shape
    depth = w_in.shape[0]
    m = bsz * seq
    tm = min(512, seq)
    tm_proj = tm
    tm_ffn = tm
    t_fox = min(256, seq)
    ts_mlstm = min(1024, seq)

    idx = jnp.arange(A_BLOCK)
    chunk_causal = (idx[None, :] // CHUNK) <= (idx[:, None] // CHUNK)

    x2d = x.reshape(m, D_MODEL)
    mem2d = mem.reshape(bsz * N_MEM, D_MODEL)
    rows = lambda a: a.reshape(depth, 1, -1)

    w_main, w_small = _rearrange_w_in(w_in)
    gate_bias = jnp.concatenate(
        [b_fox_f, b_mlstm_i, b_mlstm_f, jnp.zeros((depth, LANES - GATE_ROWS), F32)], axis=-1)
    ws_masked = jnp.where(chunk_causal, w_s, 0).astype(BF16)
    bs_full = jnp.repeat(jnp.swapaxes(b_s, 1, 2), BRANCH_WIDTH // A_GROUPS, axis=2)
    gmh_rep = jnp.broadcast_to(g_mh.reshape(depth, C_HEADS, C_HEAD_DIM, 1),
                               (depth, C_HEADS, C_HEAD_DIM, LANES))
    w_branch, w_out, w_mq, w_mkv, w_mo, w_up, w_down = (
        a.astype(BF16) for a in (w_branch, w_out, w_mq, w_mkv, w_mo, w_up, w_down))

    for i in range(depth):
        p2d, gs = _proj_call(x2d, rows(g_mix), w_main, w_small, w_conv_c, i, seq, tm_proj)
        p3d = p2d.reshape(bsz, seq, PROJ_COLS)
        gc, gr, u, gsum = _gates_call(gs.reshape(bsz, seq, LANES), rows(gate_bias), i)
        yb = _fox_call(p3d, gc, t_fox).reshape(m, BRANCH_WIDTH)
        yc = _mlstm_call(p3d, u, gr, gsum, gmh_rep, i, ts_mlstm).reshape(m, BRANCH_WIDTH)
        kv = _memkv_call(mem2d, rows(g_mem_kv), w_mkv, i)
        x2d = _merge_call(p2d, yb, yc, x2d, rows(g_sgu), ws_masked, bs_full, w_branch, w_out,
                          rows(g_mem_q), w_mq, kv, w_mo, i, seq, tm)
        x2d = _ffn_call(x2d, rows(g_ffn), w_up, w_ffn_conv, w_down, g_final.reshape(1, -1), i, seq, tm_ffn,
                        256, i == depth - 1)
    return x2d.reshape(bsz, seq, D_MODEL)
```

```python
import functools

import jax
import jax.numpy as jnp
from jax import lax
from jax.experimental import pallas as pl
from jax.experimental.pallas import tpu as pltpu

F32 = jnp.float32
BF16 = jnp.bfloat16

D_MODEL = 1024
EPS = 1e-6
LANES = 128
CARRY_ROWS = 8

BRANCH_WIDTH = 512
A_BLOCK = 128
A_GROUPS = 4
CHUNK = 64
B_HEADS = 8
B_HEAD_DIM = 64
C_HEADS = 4
C_HEAD_DIM = 128
C_CONV = 4
MLSTM_CHUNK = 128
N_MEM = 256
MEM_HEADS = 4
MEM_HEAD_DIM = 256
D_FF = 2816
FFN_CONV = 3

PROJ_TN = 512
COL_G = 0
COL_A = 3072
COL_BQ = 4096
COL_BK = 4608
COL_BV = 5120
COL_CQ = 5632
COL_CK = 6144
COL_CV = 6656
COL_CO = 7168
PROJ_COLS = 7680
LANE_BF = 0
LANE_CI = 8
LANE_CF = 12
GATE_ROWS = 16

LOG2E = 1.4426950408889634
FOX_Q_SCALE = B_HEAD_DIM ** -0.5 * LOG2E
NEG = -1e30
VMEM_LIMIT = 56 * 1024 * 1024


def _cparams(sem):
    return pltpu.CompilerParams(dimension_semantics=sem, vmem_limit_bytes=VMEM_LIMIT)


def _rms(xf, g):
    return xf * lax.rsqrt(jnp.mean(xf * xf, axis=-1, keepdims=True) + EPS) * g


def _sigmoid(x):
    return 0.5 * jnp.tanh(0.5 * x) + 0.5


def _gelu_tanh(x):
    return 0.5 * x * (1.0 + jnp.tanh(0.7978845608028654 * (x + 0.044715 * (x * x * x))))


def _dot(a, b):
    return jnp.dot(a, b, preferred_element_type=F32)


def _dot_nt(a, b):
    return lax.dot_general(a, b, (((1,), (1,)), ((), ())), preferred_element_type=F32)


def _const_spec(shape, layer=None):
    nd = len(shape)
    if layer is None:
        return pl.BlockSpec(shape, lambda *_: (0,) * nd, pipeline_mode=pl.Buffered(1))
    return pl.BlockSpec((None,) + tuple(shape), lambda *_: (layer,) + (0,) * nd,
                        pipeline_mode=pl.Buffered(1))


def _proj_kernel(x_ref, g_ref, w_ref, ws_ref, wc_ref, p_ref, gs_ref, ext_ref, carry_ref, *, tm, seq):
    i = pl.program_id(0)
    h = _rms(x_ref[...], g_ref[...]).astype(BF16)
    gs_ref[...] = _dot(h, ws_ref[...])
    seq_start = (i * tm) % seq == 0

    def conv_silu(acc, slot):
        ext_ref[slot, pl.ds(CARRY_ROWS, tm), :] = acc
        ext_ref[slot, pl.ds(0, CARRY_ROWS), :] = jnp.where(seq_start, 0.0, carry_ref[slot])
        carry_ref[slot] = ext_ref[slot, pl.ds(tm, CARRY_ROWS), :]
        wc = 0.5 * wc_ref[:, slot * PROJ_TN:(slot + 1) * PROJ_TN]
        half = wc[C_CONV - 1:C_CONV, :] * ext_ref[slot, pl.ds(CARRY_ROWS, tm), :]
        for d in range(1, C_CONV):
            half = half + wc[C_CONV - 1 - d:C_CONV - d, :] * ext_ref[slot, pl.ds(CARRY_ROWS - d, tm), :]
        return half * (jnp.tanh(half) + 1.0)

    order = [COL_CQ, COL_CK] + [c for c in range(0, PROJ_COLS, PROJ_TN) if c not in (COL_CQ, COL_CK)]
    for c0 in order:
        cs = slice(c0, c0 + PROJ_TN)
        acc = _dot(h, w_ref[:, cs])
        if c0 < COL_A or c0 == COL_CO:
            out = _sigmoid(acc)
        elif c0 < COL_BQ:
            out = _gelu_tanh(acc)
        elif c0 == COL_BQ:
            out = acc * FOX_Q_SCALE
        elif c0 == COL_CQ:
            out = conv_silu(acc, 0)
        elif c0 == COL_CK:
            out = conv_silu(acc, 1)
        else:
            out = acc
        p_ref[:, cs] = out.astype(BF16)


def _proj_call(x2d, g, w_main, w_small, w_conv, layer, seq, tm):
    m = x2d.shape[0]
    return pl.pallas_call(
        functools.partial(_proj_kernel, tm=tm, seq=seq),
        grid=(m // tm,),
        in_specs=[
            pl.BlockSpec((tm, D_MODEL), lambda i: (i, 0)),
            _const_spec((1, D_MODEL), layer),
            _const_spec((D_MODEL, PROJ_COLS), layer),
            _const_spec((D_MODEL, LANES), layer),
            _const_spec((C_CONV, 2 * PROJ_TN), layer),
        ],
        out_specs=[
            pl.BlockSpec((tm, PROJ_COLS), lambda i: (i, 0)),
            pl.BlockSpec((tm, LANES), lambda i: (i, 0)),
        ],
        out_shape=[
            jax.ShapeDtypeStruct((m, PROJ_COLS), BF16),
            jax.ShapeDtypeStruct((m, LANES), F32),
        ],
        scratch_shapes=[
            pltpu.VMEM((2, tm + CARRY_ROWS, PROJ_TN), F32),
            pltpu.VMEM((2, CARRY_ROWS, PROJ_TN), F32),
        ],
        compiler_params=_cparams(("arbitrary",)),
        name="proj",
    )(x2d, g, w_main, w_small, w_conv)


def _gates_kernel(gs_ref, bias_ref, gc_ref, gr_ref, u_ref, gsum_ref, *, seq):
    blk = MLSTM_CHUNK
    row = lax.broadcasted_iota(jnp.int32, (blk, blk), 0)
    col = lax.broadcasted_iota(jnp.int32, (blk, blk), 1)
    tri = (col <= row).astype(F32)
    lane = lax.broadcasted_iota(jnp.int32, (1, LANES), 1)

    def body(r, carry):
        r0 = pl.multiple_of(r * blk, blk)
        raw = gs_ref[pl.ds(r0, blk), :] + bias_ref[...]
        logsig = jnp.minimum(raw, 0.0) - jnp.log1p(jnp.exp(-jnp.abs(raw)))
        local = jnp.dot(tri, logsig, precision=lax.Precision.HIGHEST, preferred_element_type=F32)
        glob = local + carry
        out = jnp.where(lane < LANE_CI, glob, jnp.where(lane < LANE_CF, raw, local))
        gc_ref[pl.ds(r0, blk), :] = out
        gr_ref[:, pl.ds(r0, blk)] = out.T[0:GATE_ROWS, :]
        g_rows, mloc_rows = [], []
        for h in range(C_HEADS):
            d = out[:, LANE_CI + h:LANE_CI + h + 1] - out[:, LANE_CF + h:LANE_CF + h + 1]
            u = jnp.broadcast_to(d, (blk, LANES))
            u_ref[h, pl.ds(r0, blk), :] = u
            g = jnp.broadcast_to(out[blk - 1:blk, LANE_CF + h:LANE_CF + h + 1], (1, LANES))
            g_rows.append(g)
            mloc_rows.append(g + jnp.max(u, axis=0, keepdims=True))
        gsum_ref[r] = jnp.concatenate(g_rows + mloc_rows, axis=0)
        return glob[blk - 1:blk, :]

    lax.fori_loop(0, seq // blk, body, jnp.zeros((1, LANES), F32), unroll=4)


def _gates_call(gs3d, bias, layer):
    b, seq, _ = gs3d.shape
    nchunk = seq // MLSTM_CHUNK
    return pl.pallas_call(
        functools.partial(_gates_kernel, seq=seq),
        grid=(b,),
        in_specs=[
            pl.BlockSpec((None, seq, LANES), lambda bi: (bi, 0, 0)),
            _const_spec((1, LANES), layer),
        ],
        out_specs=[
            pl.BlockSpec((None, seq, LANES), lambda bi: (bi, 0, 0)),
            pl.BlockSpec((None, GATE_ROWS, seq), lambda bi: (bi, 0, 0)),
            pl.BlockSpec((None, C_HEADS, seq, LANES), lambda bi: (bi, 0, 0, 0)),
            pl.BlockSpec((None, nchunk, 2 * C_HEADS, LANES), lambda bi: (bi, 0, 0, 0)),
        ],
        out_shape=[
            jax.ShapeDtypeStruct((b, seq, LANES), F32),
            jax.ShapeDtypeStruct((b, GATE_ROWS, seq), F32),
            jax.ShapeDtypeStruct((b, C_HEADS, seq, LANES), F32),
            jax.ShapeDtypeStruct((b, nchunk, 2 * C_HEADS, LANES), F32),
        ],
        compiler_params=_cparams(("arbitrary",)),
        name="gates",
    )(gs3d, bias)


FOX_VROWS = 80
FOX_KMULT = 2
FOX_BODY = 2
FOX_HEADS = 8


def _fox_kernel(q_ref, k_ref, v_ref, gc_ref, place_ref, o_ref, kaug_ref, vt_ref, s_ref, m_ref, acc_ref,
                *, t, seq):
    qi = pl.program_id(2)
    hd = B_HEAD_DIM
    nh = FOX_HEADS
    lane = lax.broadcasted_iota(jnp.int32, (1, LANES), 1)
    own = (lane < hd, lane >= hd)
    aug0 = (hd, 0)

    @pl.when(qi == 0)
    def _():
        sub = lax.broadcasted_iota(jnp.int32, (FOX_VROWS - hd, seq), 0)
        tail = jnp.where(sub == 0, 1.0, 0.0).astype(BF16)
        for hh in range(nh):
            vt_ref[hh, hd:FOX_VROWS, :] = tail
        eye = (lax.broadcasted_iota(jnp.int32, (LANES, LANES), 0)
               == lax.broadcasted_iota(jnp.int32, (LANES, LANES), 1)).astype(F32).astype(BF16)

        def body(r, _):
            r0 = pl.multiple_of(r * LANES, LANES)
            neg = -LOG2E * gc_ref[pl.ds(r0, LANES), :]
            hi = neg.astype(BF16).astype(F32)
            mid = (neg - hi).astype(BF16).astype(F32)
            lo = (neg - hi) - mid
            pieces = jnp.where(lane < B_HEADS, hi,
                               jnp.where(lane < 2 * B_HEADS, pltpu.roll(mid, B_HEADS, 1),
                                         pltpu.roll(lo, 2 * B_HEADS, 1))).astype(BF16)
            extra = _dot(pieces, place_ref[...]).astype(BF16)
            for pp in range(nh // 2):
                ls = slice(pp * LANES, (pp + 1) * LANES)
                kb = k_ref[pl.ds(r0, LANES), ls]
                vt = _dot_nt(eye, v_ref[pl.ds(r0, LANES), ls])
                for h in range(2):
                    hh = 2 * pp + h
                    kaug_ref[hh, pl.ds(r0, LANES), :] = jnp.where(
                        own[h], kb, extra[:, hh * LANES:(hh + 1) * LANES])
                    vt_ref[hh, 0:hd, pl.ds(r0, LANES)] = vt[h * hd:(h + 1) * hd, :].astype(BF16)
            return 0

        lax.fori_loop(0, seq // LANES, body, 0, unroll=4)

    qa = []
    for pp in range(nh // 2):
        q = q_ref[:, pp * LANES:(pp + 1) * LANES].astype(F32)
        for h in range(2):
            a = aug0[h]
            ones3 = jnp.where((lane >= a) & (lane < a + 3), 1.0, 0.0)
            qa.append(jnp.where(own[h], q, ones3).astype(BF16))
    row = lax.broadcasted_iota(jnp.int32, (t, t), 0)
    col = lax.broadcasted_iota(jnp.int32, (t, t), 1)
    causal = row <= col

    def qk(hh, k0, tk):
        return _dot_nt(kaug_ref[hh, pl.ds(k0, tk), :], qa[hh])

    def update(k0, tk, masked, slot):
        k0 = pl.multiple_of(k0, t)
        block_max = []
        for hh in range(nh):
            st = qk(hh, k0, tk)
            if masked:
                st = jnp.where(causal, st, NEG)
            s_ref[slot, hh, 0:tk, :] = st
            block_max.append(jnp.max(st, axis=0, keepdims=True))
        ps, alphas = [], []
        for hh in range(nh):
            m = m_ref[hh]
            m_new = jnp.maximum(m, block_max[hh])
            ps.append(jnp.exp2((s_ref[slot, hh, 0:tk, :] - m_new).astype(BF16)))
            alphas.append(jnp.exp2(m - m_new))
            m_ref[hh] = m_new
        for hh in range(nh):
            acc_ref[hh] = alphas[hh] * acc_ref[hh] + _dot(vt_ref[hh, :, pl.ds(k0, tk)], ps[hh])

    m_ref[...] = jnp.full(m_ref.shape, NEG, F32)
    acc_ref[...] = jnp.zeros(acc_ref.shape, F32)

    km = FOX_KMULT
    nb = FOX_BODY

    def body(j, carry):
        for i in range(nb):
            update((j * nb + i) * km * t, km * t, False, i % 2)
        return carry

    nw = qi // km
    lax.fori_loop(0, nw // nb, body, 0)

    def tail(n_wide, n_single):
        def run():
            for i in range(n_wide):
                update((nw - n_wide + i) * km * t, km * t, False, i % 2)
            for i in range(n_single):
                update((qi - n_single + i) * t, t, False, (n_wide + i) % 2)
            update(qi * t, t, True, (n_wide + n_single) % 2)
        return run

    lax.switch((nw % nb) * km + qi % km, [tail(w, s) for w in range(nb) for s in range(km)])
    ot = jnp.concatenate([acc_ref[hh, 0:hd, :] / acc_ref[hh, hd:hd + 1, :] for hh in range(nh)], axis=0)
    o_ref[...] = ot.T.astype(BF16)


def _fox_place_matrix():
    assert FOX_HEADS == B_HEADS and LANE_BF == 0
    r = jnp.arange(LANES)[:, None]
    c = jnp.arange(B_HEADS * LANES)[None, :]
    head = c // LANES
    aug0 = jnp.where(head % 2 == 0, B_HEAD_DIM, 0)
    piece = c % LANES - aug0
    return ((piece >= 0) & (piece < 3) & (r == piece * B_HEADS + head)).astype(BF16)


def _fox_call(p3d, gc, t):
    b, seq, _ = p3d.shape
    w = FOX_HEADS * B_HEAD_DIM
    return pl.pallas_call(
        functools.partial(_fox_kernel, t=t, seq=seq),
        grid=(b, B_HEADS // FOX_HEADS, seq // t),
        in_specs=[
            pl.BlockSpec((None, t, w), lambda bi, hg, qi: (bi, qi, COL_BQ // w + hg)),
            pl.BlockSpec((None, seq, w), lambda bi, hg, qi: (bi, 0, COL_BK // w + hg)),
            pl.BlockSpec((None, seq, w), lambda bi, hg, qi: (bi, 0, COL_BV // w + hg)),
            pl.BlockSpec((None, seq, LANES), lambda bi, hg, qi: (bi, 0, 0)),
            _const_spec((LANES, FOX_HEADS * LANES)),
        ],
        out_specs=pl.BlockSpec((None, t, w), lambda bi, hg, qi: (bi, qi, hg)),
        out_shape=jax.ShapeDtypeStruct((b, seq, BRANCH_WIDTH), BF16),
        scratch_shapes=[
            pltpu.VMEM((FOX_HEADS, seq, LANES), BF16),
            pltpu.VMEM((FOX_HEADS, FOX_VROWS, seq), BF16),
            pltpu.VMEM((2, FOX_HEADS, FOX_KMULT * t, t), F32),
            pltpu.VMEM((FOX_HEADS, 1, t), F32),
            pltpu.VMEM((FOX_HEADS, FOX_VROWS, t), F32),
        ],
        compiler_params=_cparams(("arbitrary", "arbitrary", "arbitrary")),
        name="fox",
    )(p3d, p3d, p3d, gc, _fox_place_matrix())


MLSTM_ROWS = 144


def _mlstm_kernel(q_ref, k_ref, v_ref, o_ref, u_ref, gr_ref, gsum_ref, gmh_ref, y_ref, c_ref, m_ref,
                  *, ts):
    L = MLSTM_CHUNK
    dh = C_HEAD_DIM
    scale = dh ** -0.5
    si = pl.program_id(1)

    @pl.when(si == 0)
    def _():
        c_ref[...] = jnp.zeros_like(c_ref)
        m_ref[...] = jnp.zeros_like(m_ref)

    row = lax.broadcasted_iota(jnp.int32, (L, L), 0)
    col = lax.broadcasted_iota(jnp.int32, (L, L), 1)
    causal = row <= col
    sub = lax.broadcasted_iota(jnp.int32, (MLSTM_ROWS - dh, L), 0)
    tail = jnp.where(sub == 0, 1.0, 0.0)

    def chunk(c, _):
        r0 = pl.multiple_of(c * L, L)
        gsum = gsum_ref[c]
        for h in range(C_HEADS):
            hs = slice(h * dh, (h + 1) * dh)
            q = q_ref[pl.ds(r0, L), hs]
            k = k_ref[pl.ds(r0, L), hs]
            vt = jnp.concatenate([v_ref[pl.ds(r0, L), hs].astype(F32).T, tail], axis=0)
            ig = gr_ref[pl.ds(LANE_CI + h, 1), pl.ds(r0, L)]
            b = gr_ref[pl.ds(LANE_CF + h, 1), pl.ds(r0, L)]
            g = gsum[h:h + 1, :]
            m_loc = gsum[C_HEADS + h:C_HEADS + h + 1, :]
            m_in = m_ref[h]
            c_in = c_ref[h]

            dlog = jnp.where(causal, u_ref[h, pl.ds(r0, L), :] + b, NEG)
            inter = b + m_in
            m_t = jnp.maximum(jnp.max(dlog, axis=0, keepdims=True), inter)
            sm = (_dot_nt(k, q) * scale) * jnp.exp(dlog - m_t)
            w_int = jnp.exp(inter - m_t)
            ext = _dot(vt.astype(BF16), sm.astype(BF16)) + w_int * _dot_nt(c_in.astype(BF16), q)
            hh = ext[0:dh] / jnp.maximum(jnp.abs(ext[dh:dh + 1]), jnp.exp(-m_t))
            hn = hh * lax.rsqrt(jnp.mean(hh * hh, axis=0, keepdims=True) + EPS) * gmh_ref[h]
            y_ref[pl.ds(r0, L), hs] = (hn.T * o_ref[pl.ds(r0, L), hs].astype(F32)).astype(BF16)

            m_new = jnp.maximum(g + m_in, m_loc)
            w = jnp.exp(g + (ig - b) - m_new) * scale
            c_ref[h] = jnp.exp(g + m_in - m_new) * c_in + _dot((vt * w).astype(BF16), k)
            m_ref[h] = m_new
        return 0

    lax.fori_loop(0, ts // L, chunk, 0, unroll=8)


def _mlstm_call(p3d, u, gr, gsum, gmh_rep, layer, ts):
    b, seq, _ = p3d.shape
    w = BRANCH_WIDTH
    nc = ts // MLSTM_CHUNK

    def pspec(col):
        return pl.BlockSpec((None, ts, w), lambda bi, si: (bi, si, col // w))

    return pl.pallas_call(
        functools.partial(_mlstm_kernel, ts=ts),
        grid=(b, seq // ts),
        in_specs=[
            pspec(COL_CQ), pspec(COL_CK), pspec(COL_CV), pspec(COL_CO),
            pl.BlockSpec((None, C_HEADS, ts, LANES), lambda bi, si: (bi, 0, si, 0)),
            pl.BlockSpec((None, GATE_ROWS, ts), lambda bi, si: (bi, 0, si)),
            pl.BlockSpec((None, nc, 2 * C_HEADS, LANES), lambda bi, si: (bi, si, 0, 0)),
            _const_spec((C_HEADS, C_HEAD_DIM, LANES), layer),
        ],
        out_specs=pl.BlockSpec((None, ts, w), lambda bi, si: (bi, si, 0)),
        out_shape=jax.ShapeDtypeStruct((b, seq, w), BF16),
        scratch_shapes=[
            pltpu.VMEM((C_HEADS, MLSTM_ROWS, C_HEAD_DIM), F32),
            pltpu.VMEM((C_HEADS, 1, LANES), F32),
        ],
        compiler_params=_cparams(("arbitrary", "arbitrary")),
        name="mlstm",
    )(p3d, p3d, p3d, p3d, u, gr, gsum, gmh_rep)


def _merge_kernel(gates_ref, uv_ref, yb_ref, yc_ref, x_ref, gsgu_ref, ws_ref, bs_ref, wb_ref, wo_ref,
                  gq_ref, wq_ref, kv_ref, wmo_ref, out_ref, ya_ref, o_ref, *, tm):
    w = BRANCH_WIDTH
    u = uv_ref[:, :w].astype(F32)
    v = uv_ref[:, w:].astype(F32)
    vn = _rms(v, gsgu_ref[...]).astype(BF16)
    gd = w // A_GROUPS
    for nb in range(tm // A_BLOCK):
        rs = slice(nb * A_BLOCK, (nb + 1) * A_BLOCK)
        mixed = jnp.concatenate(
            [_dot(ws_ref[g], vn[rs, g * gd:(g + 1) * gd]) for g in range(A_GROUPS)], axis=1)
        ya_ref[rs, :] = (u[rs, :] * (mixed + bs_ref[...])).astype(BF16)
    merged = gates_ref[:, 0:D_MODEL].astype(F32) * _dot(ya_ref[...], wb_ref[0])
    merged += gates_ref[:, D_MODEL:2 * D_MODEL].astype(F32) * _dot(yb_ref[...], wb_ref[1])
    merged += gates_ref[:, 2 * D_MODEL:3 * D_MODEL].astype(F32) * _dot(yc_ref[...], wb_ref[2])
    x = x_ref[...] + _dot(merged.astype(BF16), wo_ref[...])

    h = _rms(x, gq_ref[...]).astype(BF16)
    q = (_dot(h, wq_ref[...]) * (MEM_HEAD_DIM ** -0.5)).astype(BF16)
    dh = MEM_HEAD_DIM
    for hd in range(MEM_HEADS):
        hs = slice(hd * dh, (hd + 1) * dh)
        s = _dot_nt(q[:, hs], kv_ref[:, hs])
        p = jnp.exp(s - jnp.max(s, axis=-1, keepdims=True))
        o = _dot(p.astype(BF16), kv_ref[:, D_MODEL + hd * dh:D_MODEL + (hd + 1) * dh])
        o_ref[:, hs] = (o / jnp.sum(p, axis=-1, keepdims=True)).astype(BF16)
    out_ref[...] = x + _dot(o_ref[...], wmo_ref[...])


def _merge_call(p2d, yb, yc, x2d, g_sgu, ws_masked, bs_full, w_branch, w_out, g_mq, w_mq, kv, w_mo,
                layer, seq, tm):
    m = x2d.shape[0]
    w = BRANCH_WIDTH
    return pl.pallas_call(
        functools.partial(_merge_kernel, tm=tm),
        grid=(m // tm,),
        in_specs=[
            pl.BlockSpec((tm, 3 * D_MODEL), lambda i: (i, COL_G // (3 * D_MODEL))),
            pl.BlockSpec((tm, 2 * w), lambda i: (i, COL_A // (2 * w))),
            pl.BlockSpec((tm, w), lambda i: (i, 0)),
            pl.BlockSpec((tm, w), lambda i: (i, 0)),
            pl.BlockSpec((tm, D_MODEL), lambda i: (i, 0)),
            _const_spec((1, w), layer),
            _const_spec((A_GROUPS, A_BLOCK, A_BLOCK), layer),
            _const_spec((A_BLOCK, w), layer),
            _const_spec((3, w, D_MODEL), layer),
            _const_spec((D_MODEL, D_MODEL), layer),
            _const_spec((1, D_MODEL), layer),
            _const_spec((D_MODEL, D_MODEL), layer),
            pl.BlockSpec((N_MEM, 2 * D_MODEL), lambda i: ((i * tm) // seq, 0)),
            _const_spec((D_MODEL, D_MODEL), layer),
        ],
        out_specs=pl.BlockSpec((tm, D_MODEL), lambda i: (i, 0)),
        out_shape=jax.ShapeDtypeStruct((m, D_MODEL), F32),
        scratch_shapes=[pltpu.VMEM((tm, w), BF16), pltpu.VMEM((tm, D_MODEL), BF16)],
        compiler_params=_cparams(("arbitrary",)),
        name="merge",
    )(p2d, p2d, yb, yc, x2d, g_sgu, ws_masked, bs_full, w_branch, w_out, g_mq, w_mq, kv, w_mo)


def _memkv_kernel(mem_ref, g_ref, w_ref, kv_ref):
    kv_ref[...] = _dot(_rms(mem_ref[...], g_ref[...]).astype(BF16), w_ref[...]).astype(BF16)


def _memkv_call(mem2d, g, w_mkv, layer):
    m = mem2d.shape[0]
    return pl.pallas_call(
        _memkv_kernel,
        grid=(m // N_MEM,),
        in_specs=[
            pl.BlockSpec((N_MEM, D_MODEL), lambda i: (i, 0)),
            _const_spec((1, D_MODEL), layer),
            _const_spec((D_MODEL, 2 * D_MODEL), layer),
        ],
        out_specs=pl.BlockSpec((N_MEM, 2 * D_MODEL), lambda i: (i, 0)),
        out_shape=jax.ShapeDtypeStruct((m, 2 * D_MODEL), BF16),
        compiler_params=_cparams(("arbitrary",)),
        name="memkv",
    )(mem2d, g, w_mkv)


def _ffn_kernel(x_ref, g_ref, wup_ref, wconv_ref, wdown_ref, gfin_ref, out_ref,
                act_ref, ext_ref, carry_ref, *, tm, tf, seq, final_norm):
    i = pl.program_id(0)
    x = x_ref[...]
    h = _rms(x, g_ref[...]).astype(BF16)
    seq_start = (i * tm) % seq == 0

    def conv(slot, half, cs):
        up = _dot(h, wup_ref[:, cs])
        ext_ref[half, pl.ds(CARRY_ROWS, tm), :] = up
        ext_ref[half, pl.ds(0, CARRY_ROWS), :] = jnp.where(seq_start, 0.0, carry_ref[slot])
        carry_ref[slot] = up[tm - CARRY_ROWS:, :]
        wc = wconv_ref[:, cs]
        y = wc[FFN_CONV - 1:FFN_CONV, :] * up
        for d in range(1, FFN_CONV):
            y = y + wc[FFN_CONV - 1 - d:FFN_CONV - d, :] * ext_ref[half, pl.ds(CARRY_ROWS - d, tm), :]
        return y

    nchunk = D_FF // tf
    for c in range(nchunk):
        a = conv(c, 0, slice(c * tf, (c + 1) * tf))
        b = conv(nchunk + c, 1, slice(D_FF + c * tf, D_FF + (c + 1) * tf))
        act_ref[:, c * tf:(c + 1) * tf] = (a * _sigmoid(a) * b).astype(BF16)
    y = x + _dot(act_ref[...], wdown_ref[...])
    if final_norm:
        y = _rms(y, gfin_ref[...])
    out_ref[...] = y


def _ffn_call(x2d, g, w_up, w_conv, w_down, g_final, layer, seq, tm, tf, final_norm):
    m = x2d.shape[0]
    return pl.pallas_call(
        functools.partial(_ffn_kernel, tm=tm, tf=tf, seq=seq, final_norm=final_norm),
        grid=(m // tm,),
        in_specs=[
            pl.BlockSpec((tm, D_MODEL), lambda i: (i, 0)),
            _const_spec((1, D_MODEL), layer),
            _const_spec((D_MODEL, 2 * D_FF), layer),
            _const_spec((FFN_CONV, 2 * D_FF), layer),
            _const_spec((D_FF, D_MODEL), layer),
            _const_spec((1, D_MODEL)),
        ],
        out_specs=pl.BlockSpec((tm, D_MODEL), lambda i: (i, 0)),
        out_shape=jax.ShapeDtypeStruct((m, D_MODEL), F32),
        scratch_shapes=[
            pltpu.VMEM((tm, D_FF), BF16),
            pltpu.VMEM((2, tm + CARRY_ROWS, tf), F32),
            pltpu.VMEM((2 * (D_FF // tf), CARRY_ROWS, tf), F32),
        ],
        compiler_params=_cparams(("arbitrary",)),
        name="ffn",
    )(x2d, g, w_up, w_conv, w_down, g_final)


def _rearrange_w_in(w):
    bw = BRANCH_WIDTH
    a0 = 0
    b0 = 2 * bw
    c0 = b0 + 3 * bw + B_HEADS
    g0 = c0 + 3 * bw + 2 * C_HEADS + bw
    co = c0 + 3 * bw + 2 * C_HEADS
    w = w.astype(BF16)
    main = jnp.concatenate([
        w[..., g0:g0 + 3 * D_MODEL],
        w[..., a0:a0 + 2 * bw],
        w[..., b0:b0 + 3 * bw],
        w[..., c0:c0 + 3 * bw],
        w[..., co:co + bw],
    ], axis=-1)
    small = jnp.concatenate([
        w[..., b0 + 3 * bw:b0 + 3 * bw + B_HEADS],
        w[..., c0 + 3 * bw:c0 + 3 * bw + 2 * C_HEADS],
        jnp.zeros(w.shape[:-1] + (LANES - GATE_ROWS,), w.dtype),
    ], axis=-1)
    return main, small


def kernel(x, mem, g_mix, w_in, g_sgu, w_s, b_s, b_fox_f, w_conv_c, b_mlstm_i, b_mlstm_f, g_mh,
           w_branch, w_out, g_mem_q, g_mem_kv, w_mq, w_mkv, w_mo, g_ffn, w_up, w_ffn_conv, w_down,
           g_final):
    bsz, seq, _ = x.shape
    depth = w_in.shape[0]
    m = bsz * seq
    tm = min(512, seq)
    tm_proj = tm
    tm_ffn = tm
    t_fox = min(256, seq)
    ts_mlstm = min(1024, seq)

    idx = jnp.arange(A_BLOCK)
    chunk_causal = (idx[None, :] // CHUNK) <= (idx[:, None] // CHUNK)

    x2d = x.reshape(m, D_MODEL)
    mem2d = mem.reshape(bsz * N_MEM, D_MODEL)
    rows = lambda a: a.reshape(depth, 1, -1)

    w_main, w_small = _rearrange_w_in(w_in)
    gate_bias = jnp.concatenate(
        [b_fox_f, b_mlstm_i, b_mlstm_f, jnp.zeros((depth, LANES - GATE_ROWS), F32)], axis=-1)
    ws_masked = jnp.where(chunk_causal, w_s, 0).astype(BF16)
    bs_full = jnp.repeat(jnp.swapaxes(b_s, 1, 2), BRANCH_WIDTH // A_GROUPS, axis=2)
    gmh_rep = jnp.broadcast_to(g_mh.reshape(depth, C_HEADS, C_HEAD_DIM, 1),
                               (depth, C_HEADS, C_HEAD_DIM, LANES))
    w_branch, w_out, w_mq, w_mkv, w_mo, w_up, w_down = (
        a.astype(BF16) for a in (w_branch, w_out, w_mq, w_mkv, w_mo, w_up, w_down))

    for i in range(depth):
        p2d, gs = _proj_call(x2d, rows(g_mix), w_main, w_small, w_conv_c, i, seq, tm_proj)
        p3d = p2d.reshape(bsz, seq, PROJ_COLS)
        gc, gr, u, gsum = _gates_call(gs.reshape(bsz, seq, LANES), rows(gate_bias), i)
        yb = _fox_call(p3d, gc, t_fox).reshape(m, BRANCH_WIDTH)
        yc = _mlstm_call(p3d, u, gr, gsum, gmh_rep, i, ts_mlstm).reshape(m, BRANCH_WIDTH)
        kv = _memkv_call(mem2d, rows(g_mem_kv), w_mkv, i)
        x2d = _merge_call(p2d, yb, yc, x2d, rows(g_sgu), ws_masked, bs_full, w_branch, w_out,
                          rows(g_mem_q), w_mq, kv, w_mo, i, seq, tm)
        x2d = _ffn_call(x2d, rows(g_ffn), w_up, w_ffn_conv, w_down, g_final.reshape(1, -1), i, seq, tm_ffn,
                        256, i == depth - 1)
    return x2d.reshape(bsz, seq, D_MODEL)
```

```python
import functools

import jax
import jax.numpy as jnp
from jax import lax
from jax.experimental import pallas as pl
from jax.experimental.pallas import tpu as pltpu

F32 = jnp.float32
BF16 = jnp.bfloat16

D_MODEL = 1024
EPS = 1e-6
LANES = 128
CARRY_ROWS = 8

BRANCH_WIDTH = 512
A_BLOCK = 128
A_GROUPS = 4
CHUNK = 64
B_HEADS = 8
B_HEAD_DIM = 64
C_HEADS = 4
C_HEAD_DIM = 128
C_CONV = 4
MLSTM_CHUNK = 128
N_MEM = 256
MEM_HEADS = 4
MEM_HEAD_DIM = 256
D_FF = 2816
FFN_CONV = 3

PROJ_TN = 512
COL_G = 0
COL_A = 3072
COL_BQ = 4096
COL_BK = 4608
COL_BV = 5120
COL_CQ = 5632
COL_CK = 6144
COL_CV = 6656
COL_CO = 7168
PROJ_COLS = 7680
LANE_BF = 0
LANE_CI = 8
LANE_CF = 12
GATE_ROWS = 16

LOG2E = 1.4426950408889634
FOX_Q_SCALE = B_HEAD_DIM ** -0.5 * LOG2E
NEG = -1e30
VMEM_LIMIT = 56 * 1024 * 1024


def _cparams(sem):
    return pltpu.CompilerParams(dimension_semantics=sem, vmem_limit_bytes=VMEM_LIMIT)


def _rms(xf, g):
    return xf * lax.rsqrt(jnp.mean(xf * xf, axis=-1, keepdims=True) + EPS) * g


def _sigmoid(x):
    return 0.5 * jnp.tanh(0.5 * x) + 0.5


def _gelu_tanh(x):
    return 0.5 * x * (1.0 + jnp.tanh(0.7978845608028654 * (x + 0.044715 * (x * x * x))))


def _dot(a, b):
    return jnp.dot(a, b, preferred_element_type=F32)


def _dot_nt(a, b):
    return lax.dot_general(a, b, (((1,), (1,)), ((), ())), preferred_element_type=F32)


def _const_spec(shape, layer=None):
    nd = len(shape)
    if layer is None:
        return pl.BlockSpec(shape, lambda *_: (0,) * nd, pipeline_mode=pl.Buffered(1))
    return pl.BlockSpec((None,) + tuple(shape), lambda *_: (layer,) + (0,) * nd,
                        pipeline_mode=pl.Buffered(1))


def _proj_kernel(x_ref, g_ref, wg_ref, wab_ref, wcq_ref, wco_ref, ws_ref, wc_ref, p_ref, gs_ref,
                 ext_ref, carry_ref, *, tm, seq):
    i = pl.program_id(0)

    def w_tile(c0):
        for ref, base, end in ((wg_ref, COL_G, COL_A), (wab_ref, COL_A, COL_CQ),
                               (wcq_ref, COL_CQ, COL_CO), (wco_ref, COL_CO, PROJ_COLS)):
            if base <= c0 < end:
                return ref[:, c0 - base:c0 - base + PROJ_TN]

    h = _rms(x_ref[...], g_ref[...]).astype(BF16)
    gs_ref[...] = _dot(h, ws_ref[...])
    seq_start = (i * tm) % seq == 0

    def conv_silu(acc, slot):
        ext_ref[slot, pl.ds(CARRY_ROWS, tm), :] = acc
        ext_ref[slot, pl.ds(0, CARRY_ROWS), :] = jnp.where(seq_start, 0.0, carry_ref[slot])
        carry_ref[slot] = ext_ref[slot, pl.ds(tm, CARRY_ROWS), :]
        wc = 0.5 * wc_ref[:, slot * PROJ_TN:(slot + 1) * PROJ_TN]
        half = wc[C_CONV - 1:C_CONV, :] * ext_ref[slot, pl.ds(CARRY_ROWS, tm), :]
        for d in range(1, C_CONV):
            half = half + wc[C_CONV - 1 - d:C_CONV - d, :] * ext_ref[slot, pl.ds(CARRY_ROWS - d, tm), :]
        return half * (jnp.tanh(half) + 1.0)

    order = [COL_CQ, COL_CK] + [c for c in range(0, PROJ_COLS, PROJ_TN) if c not in (COL_CQ, COL_CK)]
    for c0 in order:
        cs = slice(c0, c0 + PROJ_TN)
        acc = _dot(h, w_tile(c0))
        if c0 < COL_A or c0 == COL_CO:
            out = _sigmoid(acc)
        elif c0 < COL_BQ:
            out = _gelu_tanh(acc)
        elif c0 == COL_BQ:
            out = acc * FOX_Q_SCALE
        elif c0 == COL_CQ:
            out = conv_silu(acc, 0)
        elif c0 == COL_CK:
            out = conv_silu(acc, 1)
        else:
            out = acc
        p_ref[:, cs] = out.astype(BF16)


def _proj_call(x2d, g, w_groups, w_small, w_conv, layer, seq, tm):
    m = x2d.shape[0]
    return pl.pallas_call(
        functools.partial(_proj_kernel, tm=tm, seq=seq),
        grid=(m // tm,),
        in_specs=[
            pl.BlockSpec((tm, D_MODEL), lambda i: (i, 0)),
            _const_spec((1, D_MODEL), layer),
            _const_spec((D_MODEL, COL_A - COL_G), layer),
            _const_spec((D_MODEL, COL_CQ - COL_A), layer),
            _const_spec((D_MODEL, COL_CO - COL_CQ), layer),
            _const_spec((D_MODEL, PROJ_COLS - COL_CO), layer),
            _const_spec((D_MODEL, LANES), layer),
            _const_spec((C_CONV, 2 * PROJ_TN), layer),
        ],
        out_specs=[
            pl.BlockSpec((tm, PROJ_COLS), lambda i: (i, 0)),
            pl.BlockSpec((tm, LANES), lambda i: (i, 0)),
        ],
        out_shape=[
            jax.ShapeDtypeStruct((m, PROJ_COLS), BF16),
            jax.ShapeDtypeStruct((m, LANES), F32),
        ],
        scratch_shapes=[
            pltpu.VMEM((2, tm + CARRY_ROWS, PROJ_TN), F32),
            pltpu.VMEM((2, CARRY_ROWS, PROJ_TN), F32),
        ],
        compiler_params=_cparams(("arbitrary",)),
        name="proj",
    )(x2d, g, *w_groups, w_small, w_conv)


def _gates_kernel(gs_ref, bias_ref, gc_ref, gr_ref, u_ref, gsum_ref, *, seq):
    blk = MLSTM_CHUNK
    row = lax.broadcasted_iota(jnp.int32, (blk, blk), 0)
    col = lax.broadcasted_iota(jnp.int32, (blk, blk), 1)
    tri = (col <= row).astype(F32)
    lane = lax.broadcasted_iota(jnp.int32, (1, LANES), 1)

    def body(r, carry):
        r0 = pl.multiple_of(r * blk, blk)
        raw = gs_ref[pl.ds(r0, blk), :] + bias_ref[...]
        logsig = jnp.minimum(raw, 0.0) - jnp.log1p(jnp.exp(-jnp.abs(raw)))
        local = jnp.dot(tri, logsig, precision=lax.Precision.HIGHEST, preferred_element_type=F32)
        glob = local + carry
        out = jnp.where(lane < LANE_CI, glob, jnp.where(lane < LANE_CF, raw, local))
        gc_ref[pl.ds(r0, blk), :] = out
        gr_ref[:, pl.ds(r0, blk)] = out.T[0:GATE_ROWS, :]
        g_rows, mloc_rows = [], []
        for h in range(C_HEADS):
            d = out[:, LANE_CI + h:LANE_CI + h + 1] - out[:, LANE_CF + h:LANE_CF + h + 1]
            u = jnp.broadcast_to(d, (blk, LANES))
            u_ref[h, pl.ds(r0, blk), :] = u
            g = jnp.broadcast_to(out[blk - 1:blk, LANE_CF + h:LANE_CF + h + 1], (1, LANES))
            g_rows.append(g)
            mloc_rows.append(g + jnp.max(u, axis=0, keepdims=True))
        gsum_ref[r] = jnp.concatenate(g_rows + mloc_rows, axis=0)
        return glob[blk - 1:blk, :]

    lax.fori_loop(0, seq // blk, body, jnp.zeros((1, LANES), F32), unroll=4)


def _gates_call(gs3d, bias, layer):
    b, seq, _ = gs3d.shape
    nchunk = seq // MLSTM_CHUNK
    return pl.pallas_call(
        functools.partial(_gates_kernel, seq=seq),
        grid=(b,),
        in_specs=[
            pl.BlockSpec((None, seq, LANES), lambda bi: (bi, 0, 0)),
            _const_spec((1, LANES), layer),
        ],
        out_specs=[
            pl.BlockSpec((None, seq, LANES), lambda bi: (bi, 0, 0)),
            pl.BlockSpec((None, GATE_ROWS, seq), lambda bi: (bi, 0, 0)),
            pl.BlockSpec((None, C_HEADS, seq, LANES), lambda bi: (bi, 0, 0, 0)),
            pl.BlockSpec((None, nchunk, 2 * C_HEADS, LANES), lambda bi: (bi, 0, 0, 0)),
        ],
        out_shape=[
            jax.ShapeDtypeStruct((b, seq, LANES), F32),
            jax.ShapeDtypeStruct((b, GATE_ROWS, seq), F32),
            jax.ShapeDtypeStruct((b, C_HEADS, seq, LANES), F32),
            jax.ShapeDtypeStruct((b, nchunk, 2 * C_HEADS, LANES), F32),
        ],
        compiler_params=_cparams(("arbitrary",)),
        name="gates",
    )(gs3d, bias)


FOX_VROWS = 80
FOX_KMULT = 2
FOX_BODY = 2
FOX_HEADS = 8


def _fox_kernel(q_ref, k_ref, v_ref, gc_ref, place_ref, o_ref, kaug_ref, vt_ref, s_ref, m_ref, acc_ref,
                *, t, seq):
    qi = pl.program_id(2)
    hd = B_HEAD_DIM
    nh = FOX_HEADS
    lane = lax.broadcasted_iota(jnp.int32, (1, LANES), 1)
    own = (lane < hd, lane >= hd)
    aug0 = (hd, 0)

    @pl.when(qi == 0)
    def _():
        sub = lax.broadcasted_iota(jnp.int32, (FOX_VROWS - hd, seq), 0)
        tail = jnp.where(sub == 0, 1.0, 0.0).astype(BF16)
        for hh in range(nh):
            vt_ref[hh, hd:FOX_VROWS, :] = tail
        eye = (lax.broadcasted_iota(jnp.int32, (LANES, LANES), 0)
               == lax.broadcasted_iota(jnp.int32, (LANES, LANES), 1)).astype(F32).astype(BF16)

        def body(r, _):
            r0 = pl.multiple_of(r * LANES, LANES)
            neg = -LOG2E * gc_ref[pl.ds(r0, LANES), :]
            hi = neg.astype(BF16).astype(F32)
            mid = (neg - hi).astype(BF16).astype(F32)
            lo = (neg - hi) - mid
            pieces = jnp.where(lane < B_HEADS, hi,
                               jnp.where(lane < 2 * B_HEADS, pltpu.roll(mid, B_HEADS, 1),
                                         pltpu.roll(lo, 2 * B_HEADS, 1))).astype(BF16)
            extra = _dot(pieces, place_ref[...]).astype(BF16)
            for pp in range(nh // 2):
                ls = slice(pp * LANES, (pp + 1) * LANES)
                kb = k_ref[pl.ds(r0, LANES), ls]
                vt = _dot_nt(eye, v_ref[pl.ds(r0, LANES), ls])
                for h in range(2):
                    hh = 2 * pp + h
                    kaug_ref[hh, pl.ds(r0, LANES), :] = jnp.where(
                        own[h], kb, extra[:, hh * LANES:(hh + 1) * LANES])
                    vt_ref[hh, 0:hd, pl.ds(r0, LANES)] = vt[h * hd:(h + 1) * hd, :].astype(BF16)
            return 0

        lax.fori_loop(0, seq // LANES, body, 0, unroll=4)

    qa = []
    for pp in range(nh // 2):
        q = q_ref[:, pp * LANES:(pp + 1) * LANES].astype(F32)
        for h in range(2):
            a = aug0[h]
            ones3 = jnp.where((lane >= a) & (lane < a + 3), 1.0, 0.0)
            qa.append(jnp.where(own[h], q, ones3).astype(BF16))
    row = lax.broadcasted_iota(jnp.int32, (t, t), 0)
    col = lax.broadcasted_iota(jnp.int32, (t, t), 1)
    causal = row <= col

    def qk(hh, k0, tk):
        return _dot_nt(kaug_ref[hh, pl.ds(k0, tk), :], qa[hh])

    def update(k0, tk, masked, slot):
        k0 = pl.multiple_of(k0, t)
        for hh in range(nh):
            s_ref[slot, hh, 0:tk, :] = qk(hh, k0, tk)
        ps, alphas = [], []
        for hh in range(nh):
            m = m_ref[hh]
            load = lambda: (jnp.where(causal, s_ref[slot, hh, 0:tk, :], NEG) if masked
                            else s_ref[slot, hh, 0:tk, :])
            m_new = jnp.maximum(m, jnp.max(load(), axis=0, keepdims=True))
            ps.append(jnp.exp2((load() - m_new).astype(BF16)))
            alphas.append(jnp.exp2(m - m_new))
            m_ref[hh] = m_new
        for hh in range(nh):
            acc_ref[hh] = alphas[hh] * acc_ref[hh] + _dot(vt_ref[hh, :, pl.ds(k0, tk)], ps[hh])

    m_ref[...] = jnp.full(m_ref.shape, NEG, F32)
    acc_ref[...] = jnp.zeros(acc_ref.shape, F32)

    km = FOX_KMULT
    nb = FOX_BODY

    def body(j, carry):
        for i in range(nb):
            update((j * nb + i) * km * t, km * t, False, i % 2)
        return carry

    nw = qi // km
    lax.fori_loop(0, nw // nb, body, 0)

    def tail(n_wide, n_single):
        def run():
            for i in range(n_wide):
                update((nw - n_wide + i) * km * t, km * t, False, i % 2)
            for i in range(n_single):
                update((qi - n_single + i) * t, t, False, (n_wide + i) % 2)
            update(qi * t, t, True, (n_wide + n_single) % 2)
        return run

    lax.switch((nw % nb) * km + qi % km, [tail(w, s) for w in range(nb) for s in range(km)])
    ot = jnp.concatenate([acc_ref[hh, 0:hd, :] / acc_ref[hh, hd:hd + 1, :] for hh in range(nh)], axis=0)
    o_ref[...] = ot.T.astype(BF16)


def _fox_place_matrix():
    assert FOX_HEADS == B_HEADS and LANE_BF == 0
    r = jnp.arange(LANES)[:, None]
    c = jnp.arange(B_HEADS * LANES)[None, :]
    head = c // LANES
    aug0 = jnp.where(head % 2 == 0, B_HEAD_DIM, 0)
    piece = c % LANES - aug0
    return ((piece >= 0) & (piece < 3) & (r == piece * B_HEADS + head)).astype(BF16)


def _fox_call(p3d, gc, t):
    b, seq, _ = p3d.shape
    w = FOX_HEADS * B_HEAD_DIM
    return pl.pallas_call(
        functools.partial(_fox_kernel, t=t, seq=seq),
        grid=(b, B_HEADS // FOX_HEADS, seq // t),
        in_specs=[
            pl.BlockSpec((None, t, w), lambda bi, hg, qi: (bi, qi, COL_BQ // w + hg)),
            pl.BlockSpec((None, seq, w), lambda bi, hg, qi: (bi, 0, COL_BK // w + hg)),
            pl.BlockSpec((None, seq, w), lambda bi, hg, qi: (bi, 0, COL_BV // w + hg)),
            pl.BlockSpec((None, seq, LANES), lambda bi, hg, qi: (bi, 0, 0)),
            _const_spec((LANES, FOX_HEADS * LANES)),
        ],
        out_specs=pl.BlockSpec((None, t, w), lambda bi, hg, qi: (bi, qi, hg)),
        out_shape=jax.ShapeDtypeStruct((b, seq, BRANCH_WIDTH), BF16),
        scratch_shapes=[
            pltpu.VMEM((FOX_HEADS, seq, LANES), BF16),
            pltpu.VMEM((FOX_HEADS, FOX_VROWS, seq), BF16),
            pltpu.VMEM((2, FOX_HEADS, FOX_KMULT * t, t), F32),
            pltpu.VMEM((FOX_HEADS, 1, t), F32),
            pltpu.VMEM((FOX_HEADS, FOX_VROWS, t), F32),
        ],
        compiler_params=_cparams(("arbitrary", "arbitrary", "arbitrary")),
        name="fox",
    )(p3d, p3d, p3d, gc, _fox_place_matrix())


MLSTM_ROWS = 144


def _mlstm_kernel(q_ref, k_ref, v_ref, o_ref, u_ref, gr_ref, gsum_ref, gmh_ref, y_ref, c_ref, m_ref,
                  *, ts):
    L = MLSTM_CHUNK
    dh = C_HEAD_DIM
    scale = dh ** -0.5
    si = pl.program_id(1)

    @pl.when(si == 0)
    def _():
        c_ref[...] = jnp.zeros_like(c_ref)
        m_ref[...] = jnp.zeros_like(m_ref)

    row = lax.broadcasted_iota(jnp.int32, (L, L), 0)
    col = lax.broadcasted_iota(jnp.int32, (L, L), 1)
    causal = row <= col
    sub = lax.broadcasted_iota(jnp.int32, (MLSTM_ROWS - dh, L), 0)
    tail = jnp.where(sub == 0, 1.0, 0.0)

    def chunk(c, _):
        r0 = pl.multiple_of(c * L, L)
        gsum = gsum_ref[c]
        for h in range(C_HEADS):
            hs = slice(h * dh, (h + 1) * dh)
            q = q_ref[pl.ds(r0, L), hs]
            k = k_ref[pl.ds(r0, L), hs]
            vt = jnp.concatenate([v_ref[pl.ds(r0, L), hs].astype(F32).T, tail], axis=0)
            ig = gr_ref[pl.ds(LANE_CI + h, 1), pl.ds(r0, L)]
            b = gr_ref[pl.ds(LANE_CF + h, 1), pl.ds(r0, L)]
            g = gsum[h:h + 1, :]
            m_loc = gsum[C_HEADS + h:C_HEADS + h + 1, :]
            m_in = m_ref[h]
            c_in = c_ref[h]

            dlog = jnp.where(causal, u_ref[h, pl.ds(r0, L), :] + b, NEG)
            inter = b + m_in
            m_t = jnp.maximum(jnp.max(dlog, axis=0, keepdims=True), inter)
            sm = (_dot_nt(k, q) * scale) * jnp.exp(dlog - m_t)
            w_int = jnp.exp(inter - m_t)
            ext = _dot(vt.astype(BF16), sm.astype(BF16)) + w_int * _dot_nt(c_in.astype(BF16), q)
            hh = ext[0:dh] / jnp.maximum(jnp.abs(ext[dh:dh + 1]), jnp.exp(-m_t))
            hn = hh * lax.rsqrt(jnp.mean(hh * hh, axis=0, keepdims=True) + EPS) * gmh_ref[h]
            y_ref[pl.ds(r0, L), hs] = (hn.T * o_ref[pl.ds(r0, L), hs].astype(F32)).astype(BF16)

            m_new = jnp.maximum(g + m_in, m_loc)
            w = jnp.exp(g + (ig - b) - m_new) * scale
            c_ref[h] = jnp.exp(g + m_in - m_new) * c_in + _dot((vt * w).astype(BF16), k)
            m_ref[h] = m_new
        return 0

    lax.fori_loop(0, ts // L, chunk, 0, unroll=8)


def _mlstm_call(p3d, u, gr, gsum, gmh_rep, layer, ts):
    b, seq, _ = p3d.shape
    w = BRANCH_WIDTH
    nc = ts // MLSTM_CHUNK

    def pspec(col):
        return pl.BlockSpec((None, ts, w), lambda bi, si: (bi, si, col // w))

    return pl.pallas_call(
        functools.partial(_mlstm_kernel, ts=ts),
        grid=(b, seq // ts),
        in_specs=[
            pspec(COL_CQ), pspec(COL_CK), pspec(COL_CV), pspec(COL_CO),
            pl.BlockSpec((None, C_HEADS, ts, LANES), lambda bi, si: (bi, 0, si, 0)),
            pl.BlockSpec((None, GATE_ROWS, ts), lambda bi, si: (bi, 0, si)),
            pl.BlockSpec((None, nc, 2 * C_HEADS, LANES), lambda bi, si: (bi, si, 0, 0)),
            _const_spec((C_HEADS, C_HEAD_DIM, LANES), layer),
        ],
        out_specs=pl.BlockSpec((None, ts, w), lambda bi, si: (bi, si, 0)),
        out_shape=jax.ShapeDtypeStruct((b, seq, w), BF16),
        scratch_shapes=[
            pltpu.VMEM((C_HEADS, MLSTM_ROWS, C_HEAD_DIM), F32),
            pltpu.VMEM((C_HEADS, 1, LANES), F32),
        ],
        compiler_params=_cparams(("arbitrary", "arbitrary")),
        name="mlstm",
    )(p3d, p3d, p3d, p3d, u, gr, gsum, gmh_rep)


def _merge_kernel(gates_ref, uv_ref, yb_ref, yc_ref, x_ref, gsgu_ref, ws_ref, bs_ref, wb_ref, wo_ref,
                  gq_ref, wq_ref, kv_ref, wmo_ref, out_ref, ya_ref, o_ref, *, tm):
    w = BRANCH_WIDTH
    u = uv_ref[:, :w].astype(F32)
    v = uv_ref[:, w:].astype(F32)
    vn = _rms(v, gsgu_ref[...]).astype(BF16)
    gd = w // A_GROUPS
    for nb in range(tm // A_BLOCK):
        rs = slice(nb * A_BLOCK, (nb + 1) * A_BLOCK)
        mixed = jnp.concatenate(
            [_dot(ws_ref[g], vn[rs, g * gd:(g + 1) * gd]) for g in range(A_GROUPS)], axis=1)
        ya_ref[rs, :] = (u[rs, :] * (mixed + bs_ref[...])).astype(BF16)
    merged = gates_ref[:, 0:D_MODEL].astype(F32) * _dot(ya_ref[...], wb_ref[0])
    merged += gates_ref[:, D_MODEL:2 * D_MODEL].astype(F32) * _dot(yb_ref[...], wb_ref[1])
    merged += gates_ref[:, 2 * D_MODEL:3 * D_MODEL].astype(F32) * _dot(yc_ref[...], wb_ref[2])
    x = x_ref[...] + _dot(merged.astype(BF16), wo_ref[...])

    h = _rms(x, gq_ref[...]).astype(BF16)
    q = (_dot(h, wq_ref[...]) * (MEM_HEAD_DIM ** -0.5)).astype(BF16)
    dh = MEM_HEAD_DIM
    for hd in range(MEM_HEADS):
        hs = slice(hd * dh, (hd + 1) * dh)
        s = _dot_nt(q[:, hs], kv_ref[:, hs])
        p = jnp.exp(s - jnp.max(s, axis=-1, keepdims=True))
        o = _dot(p.astype(BF16), kv_ref[:, D_MODEL + hd * dh:D_MODEL + (hd + 1) * dh])
        o_ref[:, hs] = (o / jnp.sum(p, axis=-1, keepdims=True)).astype(BF16)
    out_ref[...] = x + _dot(o_ref[...], wmo_ref[...])


def _merge_call(p2d, yb, yc, x2d, g_sgu, ws_masked, bs_full, w_branch, w_out, g_mq, w_mq, kv, w_mo,
                layer, seq, tm):
    m = x2d.shape[0]
    w = BRANCH_WIDTH
    return pl.pallas_call(
        functools.partial(_merge_kernel, tm=tm),
        grid=(m // tm,),
        in_specs=[
            pl.BlockSpec((tm, 3 * D_MODEL), lambda i: (i, COL_G // (3 * D_MODEL))),
            pl.BlockSpec((tm, 2 * w), lambda i: (i, COL_A // (2 * w))),
            pl.BlockSpec((tm, w), lambda i: (i, 0)),
            pl.BlockSpec((tm, w), lambda i: (i, 0)),
            pl.BlockSpec((tm, D_MODEL), lambda i: (i, 0)),
            _const_spec((1, w), layer),
            _const_spec((A_GROUPS, A_BLOCK, A_BLOCK), layer),
            _const_spec((A_BLOCK, w), layer),
            _const_spec((3, w, D_MODEL), layer),
            _const_spec((D_MODEL, D_MODEL), layer),
            _const_spec((1, D_MODEL), layer),
            _const_spec((D_MODEL, D_MODEL), layer),
            pl.BlockSpec((N_MEM, 2 * D_MODEL), lambda i: ((i * tm) // seq, 0)),
            _const_spec((D_MODEL, D_MODEL), layer),
        ],
        out_specs=pl.BlockSpec((tm, D_MODEL), lambda i: (i, 0)),
        out_shape=jax.ShapeDtypeStruct((m, D_MODEL), F32),
        scratch_shapes=[pltpu.VMEM((tm, w), BF16), pltpu.VMEM((tm, D_MODEL), BF16)],
        compiler_params=_cparams(("arbitrary",)),
        name="merge",
    )(p2d, p2d, yb, yc, x2d, g_sgu, ws_masked, bs_full, w_branch, w_out, g_mq, w_mq, kv, w_mo)


def _memkv_kernel(mem_ref, g_ref, w_ref, kv_ref):
    kv_ref[...] = _dot(_rms(mem_ref[...], g_ref[...]).astype(BF16), w_ref[...]).astype(BF16)


def _memkv_call(mem2d, g, w_mkv, layer):
    m = mem2d.shape[0]
    return pl.pallas_call(
        _memkv_kernel,
        grid=(m // N_MEM,),
        in_specs=[
            pl.BlockSpec((N_MEM, D_MODEL), lambda i: (i, 0)),
            _const_spec((1, D_MODEL), layer),
            _const_spec((D_MODEL, 2 * D_MODEL), layer),
        ],
        out_specs=pl.BlockSpec((N_MEM, 2 * D_MODEL), lambda i: (i, 0)),
        out_shape=jax.ShapeDtypeStruct((m, 2 * D_MODEL), BF16),
        compiler_params=_cparams(("arbitrary",)),
        name="memkv",
    )(mem2d, g, w_mkv)


def _ffn_kernel(x_ref, g_ref, wup_ref, wconv_ref, wdown_ref, gfin_ref, out_ref,
                act_ref, ext_ref, carry_ref, *, tm, tf, seq, final_norm):
    i = pl.program_id(0)
    x = x_ref[...]
    h = _rms(x, g_ref[...]).astype(BF16)
    seq_start = (i * tm) % seq == 0

    def conv(slot, half, cs):
        up = _dot(h, wup_ref[:, cs])
        ext_ref[half, pl.ds(CARRY_ROWS, tm), :] = up
        ext_ref[half, pl.ds(0, CARRY_ROWS), :] = jnp.where(seq_start, 0.0, carry_ref[slot])
        carry_ref[slot] = up[tm - CARRY_ROWS:, :]
        wc = wconv_ref[:, cs]
        y = wc[FFN_CONV - 1:FFN_CONV, :] * up
        for d in range(1, FFN_CONV):
            y = y + wc[FFN_CONV - 1 - d:FFN_CONV - d, :] * ext_ref[half, pl.ds(CARRY_ROWS - d, tm), :]
        return y

    nchunk = D_FF // tf
    for c in range(nchunk):
        a = conv(c, 0, slice(c * tf, (c + 1) * tf))
        b = conv(nchunk + c, 1, slice(D_FF + c * tf, D_FF + (c + 1) * tf))
        act_ref[:, c * tf:(c + 1) * tf] = (a * _sigmoid(a) * b).astype(BF16)
    y = x + _dot(act_ref[...], wdown_ref[...])
    if final_norm:
        y = _rms(y, gfin_ref[...])
    out_ref[...] = y


def _ffn_call(x2d, g, w_up, w_conv, w_down, g_final, layer, seq, tm, tf, final_norm):
    m = x2d.shape[0]
    return pl.pallas_call(
        functools.partial(_ffn_kernel, tm=tm, tf=tf, seq=seq, final_norm=final_norm),
        grid=(m // tm,),
        in_specs=[
            pl.BlockSpec((tm, D_MODEL), lambda i: (i, 0)),
            _const_spec((1, D_MODEL), layer),
            _const_spec((D_MODEL, 2 * D_FF), layer),
            _const_spec((FFN_CONV, 2 * D_FF), layer),
            _const_spec((D_FF, D_MODEL), layer),
            _const_spec((1, D_MODEL)),
        ],
        out_specs=pl.BlockSpec((tm, D_MODEL), lambda i: (i, 0)),
        out_shape=jax.ShapeDtypeStruct((m, D_MODEL), F32),
        scratch_shapes=[
            pltpu.VMEM((tm, D_FF), BF16),
            pltpu.VMEM((2, tm + CARRY_ROWS, tf), F32),
            pltpu.VMEM((2 * (D_FF // tf), CARRY_ROWS, tf), F32),
        ],
        compiler_params=_cparams(("arbitrary",)),
        name="ffn",
    )(x2d, g, w_up, w_conv, w_down, g_final)


def _rearrange_w_in(w):
    bw = BRANCH_WIDTH
    a0 = 0
    b0 = 2 * bw
    c0 = b0 + 3 * bw + B_HEADS
    g0 = c0 + 3 * bw + 2 * C_HEADS + bw
    co = c0 + 3 * bw + 2 * C_HEADS
    groups = tuple(w[..., lo:hi].astype(BF16) for lo, hi in (
        (g0, g0 + 3 * D_MODEL), (a0, b0 + 3 * bw), (c0, c0 + 3 * bw), (co, co + bw)))
    small = jnp.concatenate([
        w[..., b0 + 3 * bw:b0 + 3 * bw + B_HEADS],
        w[..., c0 + 3 * bw:c0 + 3 * bw + 2 * C_HEADS],
        jnp.zeros(w.shape[:-1] + (LANES - GATE_ROWS,), w.dtype),
    ], axis=-1).astype(BF16)
    return groups, small


def kernel(x, mem, g_mix, w_in, g_sgu, w_s, b_s, b_fox_f, w_conv_c, b_mlstm_i, b_mlstm_f, g_mh,
           w_branch, w_out, g_mem_q, g_mem_kv, w_mq, w_mkv, w_mo, g_ffn, w_up, w_ffn_conv, w_down,
           g_final):
    bsz, seq, _ = x.shape
    depth = w_in.shape[0]
    m = bsz * seq
    tm = min(512, seq)
    tm_proj = tm
    tm_ffn = tm
    t_fox = min(256, seq)
    ts_mlstm = min(1024, seq)

    idx = jnp.arange(A_BLOCK)
    chunk_causal = (idx[None, :] // CHUNK) <= (idx[:, None] // CHUNK)

    x2d = x.reshape(m, D_MODEL)
    mem2d = mem.reshape(bsz * N_MEM, D_MODEL)
    rows = lambda a: a.reshape(depth, 1, -1)

    w_groups, w_small = _rearrange_w_in(w_in)
    gate_bias = jnp.concatenate(
        [b_fox_f, b_mlstm_i, b_mlstm_f, jnp.zeros((depth, LANES - GATE_ROWS), F32)], axis=-1)
    ws_masked = jnp.where(chunk_causal, w_s, 0).astype(BF16)
    bs_full = jnp.repeat(jnp.swapaxes(b_s, 1, 2), BRANCH_WIDTH // A_GROUPS, axis=2)
    gmh_rep = jnp.broadcast_to(g_mh.reshape(depth, C_HEADS, C_HEAD_DIM, 1),
                               (depth, C_HEADS, C_HEAD_DIM, LANES))
    w_branch, w_out, w_mq, w_mkv, w_mo, w_up, w_down = (
        a.astype(BF16) for a in (w_branch, w_out, w_mq, w_mkv, w_mo, w_up, w_down))

    for i in range(depth):
        p2d, gs = _proj_call(x2d, rows(g_mix), w_groups, w_small, w_conv_c, i, seq, tm_proj)
        p3d = p2d.reshape(bsz, seq, PROJ_COLS)
        gc, gr, u, gsum = _gates_call(gs.reshape(bsz, seq, LANES), rows(gate_bias), i)
        yb = _fox_call(p3d, gc, t_fox).reshape(m, BRANCH_WIDTH)
        yc = _mlstm_call(p3d, u, gr, gsum, gmh_rep, i, ts_mlstm).reshape(m, BRANCH_WIDTH)
        kv = _memkv_call(mem2d, rows(g_mem_kv), w_mkv, i)
        x2d = _merge_call(p2d, yb, yc, x2d, rows(g_sgu), ws_masked, bs_full, w_branch, w_out,
                          rows(g_mem_q), w_mq, kv, w_mo, i, seq, tm)
        x2d = _ffn_call(x2d, rows(g_ffn), w_up, w_ffn_conv, w_down, g_final.reshape(1, -1), i, seq, tm_ffn,
                        256, i == depth - 1)
    return x2d.reshape(bsz, seq, D_MODEL)
```

```python
import functools

import jax
import jax.numpy as jnp
from jax import lax
from jax.experimental import pallas as pl
from jax.experimental.pallas import tpu as pltpu

F32 = jnp.float32
BF16 = jnp.bfloat16

D_MODEL = 1024
EPS = 1e-6
LANES = 128
CARRY_ROWS = 8

BRANCH_WIDTH = 512
A_BLOCK = 128
A_GROUPS = 4
CHUNK = 64
B_HEADS = 8
B_HEAD_DIM = 64
C_HEADS = 4
C_HEAD_DIM = 128
C_CONV = 4
MLSTM_CHUNK = 128
N_MEM = 256
MEM_HEADS = 4
MEM_HEAD_DIM = 256
D_FF = 2816
FFN_CONV = 3

PROJ_TN = 512
COL_G = 0
COL_A = 3072
COL_BQ = 4096
COL_BK = 4608
COL_BV = 5120
COL_CQ = 5632
COL_CK = 6144
COL_CV = 6656
COL_CO = 7168
PROJ_COLS = 7680
LANE_BF = 0
LANE_CI = 8
LANE_CF = 12
GATE_ROWS = 16

LOG2E = 1.4426950408889634
FOX_Q_SCALE = B_HEAD_DIM ** -0.5 * LOG2E
NEG = -1e30
VMEM_LIMIT = 56 * 1024 * 1024


def _cparams(sem):
    return pltpu.CompilerParams(dimension_semantics=sem, vmem_limit_bytes=VMEM_LIMIT)


def _rms(xf, g):
    return xf * lax.rsqrt(jnp.mean(xf * xf, axis=-1, keepdims=True) + EPS) * g


def _sigmoid(x):
    return 0.5 * jnp.tanh(0.5 * x) + 0.5


def _gelu_tanh(x):
    return 0.5 * x * (1.0 + jnp.tanh(0.7978845608028654 * (x + 0.044715 * (x * x * x))))


def _dot(a, b):
    return jnp.dot(a, b, preferred_element_type=F32)


def _dot_nt(a, b):
    return lax.dot_general(a, b, (((1,), (1,)), ((), ())), preferred_element_type=F32)


def _const_spec(shape, layer=None):
    nd = len(shape)
    if layer is None:
        return pl.BlockSpec(shape, lambda *_: (0,) * nd, pipeline_mode=pl.Buffered(1))
    return pl.BlockSpec((None,) + tuple(shape), lambda *_: (layer,) + (0,) * nd,
                        pipeline_mode=pl.Buffered(1))


def _proj_kernel(x_ref, g_ref, wg_ref, wab_ref, wcq_ref, wco_ref, ws_ref, wc_ref, p_ref, gs_ref,
                 ext_ref, carry_ref, *, tm, seq):
    i = pl.program_id(0)

    def w_tile(c0):
        for ref, base, end in ((wg_ref, COL_G, COL_A), (wab_ref, COL_A, COL_CQ),
                               (wcq_ref, COL_CQ, COL_CO), (wco_ref, COL_CO, PROJ_COLS)):
            if base <= c0 < end:
                return ref[:, c0 - base:c0 - base + PROJ_TN]

    h = _rms(x_ref[...], g_ref[...]).astype(BF16)
    gs_ref[...] = _dot(h, ws_ref[...])
    seq_start = (i * tm) % seq == 0

    def conv_silu(acc, slot):
        ext_ref[slot, pl.ds(CARRY_ROWS, tm), :] = acc
        ext_ref[slot, pl.ds(0, CARRY_ROWS), :] = jnp.where(seq_start, 0.0, carry_ref[slot])
        carry_ref[slot] = ext_ref[slot, pl.ds(tm, CARRY_ROWS), :]
        wc = 0.5 * wc_ref[:, slot * PROJ_TN:(slot + 1) * PROJ_TN]
        half = wc[C_CONV - 1:C_CONV, :] * ext_ref[slot, pl.ds(CARRY_ROWS, tm), :]
        for d in range(1, C_CONV):
            half = half + wc[C_CONV - 1 - d:C_CONV - d, :] * ext_ref[slot, pl.ds(CARRY_ROWS - d, tm), :]
        return half * (jnp.tanh(half) + 1.0)

    order = [COL_CQ, COL_CK] + [c for c in range(0, PROJ_COLS, PROJ_TN) if c not in (COL_CQ, COL_CK)]
    for c0 in order:
        cs = slice(c0, c0 + PROJ_TN)
        acc = _dot(h, w_tile(c0))
        if c0 < COL_A or c0 == COL_CO:
            out = _sigmoid(acc)
        elif c0 < COL_BQ:
            out = _gelu_tanh(acc)
        elif c0 == COL_BQ:
            out = acc * FOX_Q_SCALE
        elif c0 == COL_CQ:
            out = conv_silu(acc, 0)
        elif c0 == COL_CK:
            out = conv_silu(acc, 1)
        else:
            out = acc
        p_ref[:, cs] = out.astype(BF16)


def _proj_call(x2d, g, w_groups, w_small, w_conv, layer, seq, tm):
    m = x2d.shape[0]
    return pl.pallas_call(
        functools.partial(_proj_kernel, tm=tm, seq=seq),
        grid=(m // tm,),
        in_specs=[
            pl.BlockSpec((tm, D_MODEL), lambda i: (i, 0)),
            _const_spec((1, D_MODEL), layer),
            _const_spec((D_MODEL, COL_A - COL_G), layer),
            _const_spec((D_MODEL, COL_CQ - COL_A), layer),
            _const_spec((D_MODEL, COL_CO - COL_CQ), layer),
            _const_spec((D_MODEL, PROJ_COLS - COL_CO), layer),
            _const_spec((D_MODEL, LANES), layer),
            _const_spec((C_CONV, 2 * PROJ_TN), layer),
        ],
        out_specs=[
            pl.BlockSpec((tm, PROJ_COLS), lambda i: (i, 0)),
            pl.BlockSpec((tm, LANES), lambda i: (i, 0)),
        ],
        out_shape=[
            jax.ShapeDtypeStruct((m, PROJ_COLS), BF16),
            jax.ShapeDtypeStruct((m, LANES), F32),
        ],
        scratch_shapes=[
            pltpu.VMEM((2, tm + CARRY_ROWS, PROJ_TN), F32),
            pltpu.VMEM((2, CARRY_ROWS, PROJ_TN), F32),
        ],
        compiler_params=_cparams(("arbitrary",)),
        name="proj",
    )(x2d, g, *w_groups, w_small, w_conv)


def _gates_kernel(gs_ref, bias_ref, gc_ref, gr_ref, u_ref, gsum_ref, *, seq):
    blk = MLSTM_CHUNK
    row = lax.broadcasted_iota(jnp.int32, (blk, blk), 0)
    col = lax.broadcasted_iota(jnp.int32, (blk, blk), 1)
    tri = (col <= row).astype(F32)
    lane = lax.broadcasted_iota(jnp.int32, (1, LANES), 1)

    def body(r, carry):
        r0 = pl.multiple_of(r * blk, blk)
        raw = gs_ref[pl.ds(r0, blk), :] + bias_ref[...]
        logsig = jnp.minimum(raw, 0.0) - jnp.log1p(jnp.exp(-jnp.abs(raw)))
        local = jnp.dot(tri, logsig, precision=lax.Precision.HIGHEST, preferred_element_type=F32)
        glob = local + carry
        out = jnp.where(lane < LANE_CI, glob, jnp.where(lane < LANE_CF, raw, local))
        gc_ref[pl.ds(r0, blk), :] = out
        gr_ref[:, pl.ds(r0, blk)] = out.T[0:GATE_ROWS, :]
        g_rows, mloc_rows = [], []
        for h in range(C_HEADS):
            d = out[:, LANE_CI + h:LANE_CI + h + 1] - out[:, LANE_CF + h:LANE_CF + h + 1]
            u = jnp.broadcast_to(d, (blk, LANES))
            u_ref[h, pl.ds(r0, blk), :] = u
            g = jnp.broadcast_to(out[blk - 1:blk, LANE_CF + h:LANE_CF + h + 1], (1, LANES))
            g_rows.append(g)
            mloc_rows.append(g + jnp.max(u, axis=0, keepdims=True))
        gsum_ref[r] = jnp.concatenate(g_rows + mloc_rows, axis=0)
        return glob[blk - 1:blk, :]

    lax.fori_loop(0, seq // blk, body, jnp.zeros((1, LANES), F32), unroll=4)


def _gates_call(gs3d, bias, layer):
    b, seq, _ = gs3d.shape
    nchunk = seq // MLSTM_CHUNK
    return pl.pallas_call(
        functools.partial(_gates_kernel, seq=seq),
        grid=(b,),
        in_specs=[
            pl.BlockSpec((None, seq, LANES), lambda bi: (bi, 0, 0)),
            _const_spec((1, LANES), layer),
        ],
        out_specs=[
            pl.BlockSpec((None, seq, LANES), lambda bi: (bi, 0, 0)),
            pl.BlockSpec((None, GATE_ROWS, seq), lambda bi: (bi, 0, 0)),
            pl.BlockSpec((None, C_HEADS, seq, LANES), lambda bi: (bi, 0, 0, 0)),
            pl.BlockSpec((None, nchunk, 2 * C_HEADS, LANES), lambda bi: (bi, 0, 0, 0)),
        ],
        out_shape=[
            jax.ShapeDtypeStruct((b, seq, LANES), F32),
            jax.ShapeDtypeStruct((b, GATE_ROWS, seq), F32),
            jax.ShapeDtypeStruct((b, C_HEADS, seq, LANES), F32),
            jax.ShapeDtypeStruct((b, nchunk, 2 * C_HEADS, LANES), F32),
        ],
        compiler_params=_cparams(("arbitrary",)),
        name="gates",
    )(gs3d, bias)


FOX_VROWS = 80
FOX_KMULT = 2
FOX_BODY = 2
FOX_HEADS = 8


def _fox_kernel(q_ref, k_ref, v_ref, gc_ref, place_ref, o_ref, kaug_ref, vt_ref, s_ref, m_ref, acc_ref,
                qa_ref, *, t, seq):
    qi = pl.program_id(2)
    hd = B_HEAD_DIM
    nh = FOX_HEADS
    lane = lax.broadcasted_iota(jnp.int32, (1, LANES), 1)
    own = (lane < hd, lane >= hd)
    aug0 = (hd, 0)

    @pl.when(qi == 0)
    def _():
        sub = lax.broadcasted_iota(jnp.int32, (FOX_VROWS - hd, seq), 0)
        tail = jnp.where(sub == 0, 1.0, 0.0).astype(BF16)
        for hh in range(nh):
            vt_ref[hh, hd:FOX_VROWS, :] = tail
        eye = (lax.broadcasted_iota(jnp.int32, (LANES, LANES), 0)
               == lax.broadcasted_iota(jnp.int32, (LANES, LANES), 1)).astype(F32).astype(BF16)

        def body(r, _):
            r0 = pl.multiple_of(r * LANES, LANES)
            neg = -LOG2E * gc_ref[pl.ds(r0, LANES), :]
            hi = neg.astype(BF16).astype(F32)
            mid = (neg - hi).astype(BF16).astype(F32)
            lo = (neg - hi) - mid
            pieces = jnp.where(lane < B_HEADS, hi,
                               jnp.where(lane < 2 * B_HEADS, pltpu.roll(mid, B_HEADS, 1),
                                         pltpu.roll(lo, 2 * B_HEADS, 1))).astype(BF16)
            extra = _dot(pieces, place_ref[...]).astype(BF16)
            for pp in range(nh // 2):
                ls = slice(pp * LANES, (pp + 1) * LANES)
                kb = k_ref[pl.ds(r0, LANES), ls]
                vt = _dot_nt(eye, v_ref[pl.ds(r0, LANES), ls])
                for h in range(2):
                    hh = 2 * pp + h
                    kaug_ref[hh, pl.ds(r0, LANES), :] = jnp.where(
                        own[h], kb, extra[:, hh * LANES:(hh + 1) * LANES])
                    vt_ref[hh, 0:hd, pl.ds(r0, LANES)] = vt[h * hd:(h + 1) * hd, :].astype(BF16)
            return 0

        lax.fori_loop(0, seq // LANES, body, 0, unroll=4)

    for pp in range(nh // 2):
        q = q_ref[:, pp * LANES:(pp + 1) * LANES].astype(F32)
        for h in range(2):
            a = aug0[h]
            ones3 = jnp.where((lane >= a) & (lane < a + 3), 1.0, 0.0)
            qa_ref[2 * pp + h] = jnp.where(own[h], q, ones3).astype(BF16)

    def qk(hh, k0, tk):
        return _dot_nt(kaug_ref[hh, pl.ds(k0, tk), :], qa_ref[hh])

    def update(k0, tk, masked, slot):
        k0 = pl.multiple_of(k0, t)
        if masked:
            causal = (lax.broadcasted_iota(jnp.int32, (t, t), 0)
                      <= lax.broadcasted_iota(jnp.int32, (t, t), 1))
        for hh in range(nh):
            s_ref[slot, hh, 0:tk, :] = qk(hh, k0, tk)
        ps, alphas = [], []
        for hh in range(nh):
            m = m_ref[hh]
            load = lambda: (jnp.where(causal, s_ref[slot, hh, 0:tk, :], NEG) if masked
                            else s_ref[slot, hh, 0:tk, :])
            m_new = jnp.maximum(m, jnp.max(load(), axis=0, keepdims=True))
            ps.append(jnp.exp2((load() - m_new).astype(BF16)))
            alphas.append(jnp.exp2(m - m_new))
            m_ref[hh] = m_new
        for hh in range(nh):
            acc_ref[hh] = alphas[hh] * acc_ref[hh] + _dot(vt_ref[hh, :, pl.ds(k0, tk)], ps[hh])

    m_ref[...] = jnp.full(m_ref.shape, NEG, F32)
    acc_ref[...] = jnp.zeros(acc_ref.shape, F32)

    km = FOX_KMULT
    nb = FOX_BODY

    def body(j, carry):
        for i in range(nb):
            update((j * nb + i) * km * t, km * t, False, i % 2)
        return carry

    nw = qi // km
    lax.fori_loop(0, nw // nb, body, 0)

    def tail(n_wide, n_single):
        def run():
            for i in range(n_wide):
                update((nw - n_wide + i) * km * t, km * t, False, i % 2)
            for i in range(n_single):
                update((qi - n_single + i) * t, t, False, (n_wide + i) % 2)
            update(qi * t, t, True, (n_wide + n_single) % 2)
        return run

    lax.switch((nw % nb) * km + qi % km, [tail(w, s) for w in range(nb) for s in range(km)])
    ot = jnp.concatenate([acc_ref[hh, 0:hd, :] / acc_ref[hh, hd:hd + 1, :] for hh in range(nh)], axis=0)
    o_ref[...] = ot.T.astype(BF16)


def _fox_place_matrix():
    assert FOX_HEADS == B_HEADS and LANE_BF == 0
    r = jnp.arange(LANES)[:, None]
    c = jnp.arange(B_HEADS * LANES)[None, :]
    head = c // LANES
    aug0 = jnp.where(head % 2 == 0, B_HEAD_DIM, 0)
    piece = c % LANES - aug0
    return ((piece >= 0) & (piece < 3) & (r == piece * B_HEADS + head)).astype(BF16)


def _fox_call(p3d, gc, t):
    b, seq, _ = p3d.shape
    w = FOX_HEADS * B_HEAD_DIM
    return pl.pallas_call(
        functools.partial(_fox_kernel, t=t, seq=seq),
        grid=(b, B_HEADS // FOX_HEADS, seq // t),
        in_specs=[
            pl.BlockSpec((None, t, w), lambda bi, hg, qi: (bi, qi, COL_BQ // w + hg)),
            pl.BlockSpec((None, seq, w), lambda bi, hg, qi: (bi, 0, COL_BK // w + hg)),
            pl.BlockSpec((None, seq, w), lambda bi, hg, qi: (bi, 0, COL_BV // w + hg)),
            pl.BlockSpec((None, seq, LANES), lambda bi, hg, qi: (bi, 0, 0)),
            _const_spec((LANES, FOX_HEADS * LANES)),
        ],
        out_specs=pl.BlockSpec((None, t, w), lambda bi, hg, qi: (bi, qi, hg)),
        out_shape=jax.ShapeDtypeStruct((b, seq, BRANCH_WIDTH), BF16),
        scratch_shapes=[
            pltpu.VMEM((FOX_HEADS, seq, LANES), BF16),
            pltpu.VMEM((FOX_HEADS, FOX_VROWS, seq), BF16),
            pltpu.VMEM((2, FOX_HEADS, FOX_KMULT * t, t), F32),
            pltpu.VMEM((FOX_HEADS, 1, t), F32),
            pltpu.VMEM((FOX_HEADS, FOX_VROWS, t), F32),
            pltpu.VMEM((FOX_HEADS, t, LANES), BF16),
        ],
        compiler_params=_cparams(("arbitrary", "arbitrary", "arbitrary")),
        name="fox",
    )(p3d, p3d, p3d, gc, _fox_place_matrix())


MLSTM_ROWS = 144


def _mlstm_kernel(q_ref, k_ref, v_ref, o_ref, u_ref, gr_ref, gsum_ref, gmh_ref, y_ref, c_ref, m_ref,
                  *, ts):
    L = MLSTM_CHUNK
    dh = C_HEAD_DIM
    scale = dh ** -0.5
    si = pl.program_id(1)

    @pl.when(si == 0)
    def _():
        c_ref[...] = jnp.zeros_like(c_ref)
        m_ref[...] = jnp.zeros_like(m_ref)

    row = lax.broadcasted_iota(jnp.int32, (L, L), 0)
    col = lax.broadcasted_iota(jnp.int32, (L, L), 1)
    causal = row <= col
    sub = lax.broadcasted_iota(jnp.int32, (MLSTM_ROWS - dh, L), 0)
    tail = jnp.where(sub == 0, 1.0, 0.0)

    def chunk(c, _):
        r0 = pl.multiple_of(c * L, L)
        gsum = gsum_ref[c]
        for h in range(C_HEADS):
            hs = slice(h * dh, (h + 1) * dh)
            q = q_ref[pl.ds(r0, L), hs]
            k = k_ref[pl.ds(r0, L), hs]
            vt = jnp.concatenate([v_ref[pl.ds(r0, L), hs].astype(F32).T, tail], axis=0)
            ig = gr_ref[pl.ds(LANE_CI + h, 1), pl.ds(r0, L)]
            b = gr_ref[pl.ds(LANE_CF + h, 1), pl.ds(r0, L)]
            g = gsum[h:h + 1, :]
            m_loc = gsum[C_HEADS + h:C_HEADS + h + 1, :]
            m_in = m_ref[h]
            c_in = c_ref[h]

            dlog = jnp.where(causal, u_ref[h, pl.ds(r0, L), :] + b, NEG)
            inter = b + m_in
            m_t = jnp.maximum(jnp.max(dlog, axis=0, keepdims=True), inter)
            sm = (_dot_nt(k, q) * scale) * jnp.exp(dlog - m_t)
            w_int = jnp.exp(inter - m_t)
            ext = _dot(vt.astype(BF16), sm.astype(BF16)) + w_int * _dot_nt(c_in.astype(BF16), q)
            hh = ext[0:dh] / jnp.maximum(jnp.abs(ext[dh:dh + 1]), jnp.exp(-m_t))
            hn = hh * lax.rsqrt(jnp.mean(hh * hh, axis=0, keepdims=True) + EPS) * gmh_ref[h]
            y_ref[pl.ds(r0, L), hs] = (hn.T * o_ref[pl.ds(r0, L), hs].astype(F32)).astype(BF16)

            m_new = jnp.maximum(g + m_in, m_loc)
            w = jnp.exp(g + (ig - b) - m_new) * scale
            c_ref[h] = jnp.exp(g + m_in - m_new) * c_in + _dot((vt * w).astype(BF16), k)
            m_ref[h] = m_new
        return 0

    lax.fori_loop(0, ts // L, chunk, 0, unroll=8)


def _mlstm_call(p3d, u, gr, gsum, gmh_rep, layer, ts):
    b, seq, _ = p3d.shape
    w = BRANCH_WIDTH
    nc = ts // MLSTM_CHUNK

    def pspec(col):
        return pl.BlockSpec((None, ts, w), lambda bi, si: (bi, si, col // w))

    return pl.pallas_call(
        functools.partial(_mlstm_kernel, ts=ts),
        grid=(b, seq // ts),
        in_specs=[
            pspec(COL_CQ), pspec(COL_CK), pspec(COL_CV), pspec(COL_CO),
            pl.BlockSpec((None, C_HEADS, ts, LANES), lambda bi, si: (bi, 0, si, 0)),
            pl.BlockSpec((None, GATE_ROWS, ts), lambda bi, si: (bi, 0, si)),
            pl.BlockSpec((None, nc, 2 * C_HEADS, LANES), lambda bi, si: (bi, si, 0, 0)),
            _const_spec((C_HEADS, C_HEAD_DIM, LANES), layer),
        ],
        out_specs=pl.BlockSpec((None, ts, w), lambda bi, si: (bi, si, 0)),
        out_shape=jax.ShapeDtypeStruct((b, seq, w), BF16),
        scratch_shapes=[
            pltpu.VMEM((C_HEADS, MLSTM_ROWS, C_HEAD_DIM), F32),
            pltpu.VMEM((C_HEADS, 1, LANES), F32),
        ],
        compiler_params=_cparams(("arbitrary", "arbitrary")),
        name="mlstm",
    )(p3d, p3d, p3d, p3d, u, gr, gsum, gmh_rep)


def _merge_kernel(gates_ref, uv_ref, yb_ref, yc_ref, x_ref, gsgu_ref, ws_ref, bs_ref, wb_ref, wo_ref,
                  gq_ref, wq_ref, kv_ref, wmo_ref, out_ref, ya_ref, o_ref, *, tm):
    w = BRANCH_WIDTH
    u = uv_ref[:, :w].astype(F32)
    v = uv_ref[:, w:].astype(F32)
    vn = _rms(v, gsgu_ref[...]).astype(BF16)
    gd = w // A_GROUPS
    for nb in range(tm // A_BLOCK):
        rs = slice(nb * A_BLOCK, (nb + 1) * A_BLOCK)
        mixed = jnp.concatenate(
            [_dot(ws_ref[g], vn[rs, g * gd:(g + 1) * gd]) for g in range(A_GROUPS)], axis=1)
        ya_ref[rs, :] = (u[rs, :] * (mixed + bs_ref[...])).astype(BF16)
    merged = gates_ref[:, 0:D_MODEL].astype(F32) * _dot(ya_ref[...], wb_ref[0])
    merged += gates_ref[:, D_MODEL:2 * D_MODEL].astype(F32) * _dot(yb_ref[...], wb_ref[1])
    merged += gates_ref[:, 2 * D_MODEL:3 * D_MODEL].astype(F32) * _dot(yc_ref[...], wb_ref[2])
    x = x_ref[...] + _dot(merged.astype(BF16), wo_ref[...])

    h = _rms(x, gq_ref[...]).astype(BF16)
    q = (_dot(h, wq_ref[...]) * (MEM_HEAD_DIM ** -0.5)).astype(BF16)
    dh = MEM_HEAD_DIM
    for hd in range(MEM_HEADS):
        hs = slice(hd * dh, (hd + 1) * dh)
        s = _dot_nt(q[:, hs], kv_ref[:, hs])
        p = jnp.exp(s - jnp.max(s, axis=-1, keepdims=True))
        o = _dot(p.astype(BF16), kv_ref[:, D_MODEL + hd * dh:D_MODEL + (hd + 1) * dh])
        o_ref[:, hs] = (o / jnp.sum(p, axis=-1, keepdims=True)).astype(BF16)
    out_ref[...] = x + _dot(o_ref[...], wmo_ref[...])


def _merge_call(p2d, yb, yc, x2d, g_sgu, ws_masked, bs_full, w_branch, w_out, g_mq, w_mq, kv, w_mo,
                layer, seq, tm):
    m = x2d.shape[0]
    w = BRANCH_WIDTH
    return pl.pallas_call(
        functools.partial(_merge_kernel, tm=tm),
        grid=(m // tm,),
        in_specs=[
            pl.BlockSpec((tm, 3 * D_MODEL), lambda i: (i, COL_G // (3 * D_MODEL))),
            pl.BlockSpec((tm, 2 * w), lambda i: (i, COL_A // (2 * w))),
            pl.BlockSpec((tm, w), lambda i: (i, 0)),
            pl.BlockSpec((tm, w), lambda i: (i, 0)),
            pl.BlockSpec((tm, D_MODEL), lambda i: (i, 0)),
            _const_spec((1, w), layer),
            _const_spec((A_GROUPS, A_BLOCK, A_BLOCK), layer),
            _const_spec((A_BLOCK, w), layer),
            _const_spec((3, w, D_MODEL), layer),
            _const_spec((D_MODEL, D_MODEL), layer),
            _const_spec((1, D_MODEL), layer),
            _const_spec((D_MODEL, D_MODEL), layer),
            pl.BlockSpec((N_MEM, 2 * D_MODEL), lambda i: ((i * tm) // seq, 0)),
            _const_spec((D_MODEL, D_MODEL), layer),
        ],
        out_specs=pl.BlockSpec((tm, D_MODEL), lambda i: (i, 0)),
        out_shape=jax.ShapeDtypeStruct((m, D_MODEL), F32),
        scratch_shapes=[pltpu.VMEM((tm, w), BF16), pltpu.VMEM((tm, D_MODEL), BF16)],
        compiler_params=_cparams(("arbitrary",)),
        name="merge",
    )(p2d, p2d, yb, yc, x2d, g_sgu, ws_masked, bs_full, w_branch, w_out, g_mq, w_mq, kv, w_mo)


def _memkv_kernel(mem_ref, g_ref, w_ref, kv_ref):
    kv_ref[...] = _dot(_rms(mem_ref[...], g_ref[...]).astype(BF16), w_ref[...]).astype(BF16)


def _memkv_call(mem2d, g, w_mkv, layer):
    m = mem2d.shape[0]
    return pl.pallas_call(
        _memkv_kernel,
        grid=(m // N_MEM,),
        in_specs=[
            pl.BlockSpec((N_MEM, D_MODEL), lambda i: (i, 0)),
            _const_spec((1, D_MODEL), layer),
            _const_spec((D_MODEL, 2 * D_MODEL), layer),
        ],
        out_specs=pl.BlockSpec((N_MEM, 2 * D_MODEL), lambda i: (i, 0)),
        out_shape=jax.ShapeDtypeStruct((m, 2 * D_MODEL), BF16),
        compiler_params=_cparams(("arbitrary",)),
        name="memkv",
    )(mem2d, g, w_mkv)


def _ffn_kernel(x_ref, g_ref, wup_ref, wconv_ref, wdown_ref, gfin_ref, out_ref,
                act_ref, ext_ref, carry_ref, *, tm, tf, seq, final_norm):
    i = pl.program_id(0)
    x = x_ref[...]
    h = _rms(x, g_ref[...]).astype(BF16)
    seq_start = (i * tm) % seq == 0

    def conv(slot, half, cs):
        up = _dot(h, wup_ref[:, cs])
        ext_ref[half, pl.ds(CARRY_ROWS, tm), :] = up
        ext_ref[half, pl.ds(0, CARRY_ROWS), :] = jnp.where(seq_start, 0.0, carry_ref[slot])
        carry_ref[slot] = up[tm - CARRY_ROWS:, :]
        wc = wconv_ref[:, cs]
        y = wc[FFN_CONV - 1:FFN_CONV, :] * up
        for d in range(1, FFN_CONV):
            y = y + wc[FFN_CONV - 1 - d:FFN_CONV - d, :] * ext_ref[half, pl.ds(CARRY_ROWS - d, tm), :]
        return y

    nchunk = D_FF // tf
    for c in range(nchunk):
        a = conv(c, 0, slice(c * tf, (c + 1) * tf))
        b = conv(nchunk + c, 1, slice(D_FF + c * tf, D_FF + (c + 1) * tf))
        act_ref[:, c * tf:(c + 1) * tf] = (a * _sigmoid(a) * b).astype(BF16)
    y = x + _dot(act_ref[...], wdown_ref[...])
    if final_norm:
        y = _rms(y, gfin_ref[...])
    out_ref[...] = y


def _ffn_call(x2d, g, w_up, w_conv, w_down, g_final, layer, seq, tm, tf, final_norm):
    m = x2d.shape[0]
    return pl.pallas_call(
        functools.partial(_ffn_kernel, tm=tm, tf=tf, seq=seq, final_norm=final_norm),
        grid=(m // tm,),
        in_specs=[
            pl.BlockSpec((tm, D_MODEL), lambda i: (i, 0)),
            _const_spec((1, D_MODEL), layer),
            _const_spec((D_MODEL, 2 * D_FF), layer),
            _const_spec((FFN_CONV, 2 * D_FF), layer),
            _const_spec((D_FF, D_MODEL), layer),
            _const_spec((1, D_MODEL)),
        ],
        out_specs=pl.BlockSpec((tm, D_MODEL), lambda i: (i, 0)),
        out_shape=jax.ShapeDtypeStruct((m, D_MODEL), F32),
        scratch_shapes=[
            pltpu.VMEM((tm, D_FF), BF16),
            pltpu.VMEM((2, tm + CARRY_ROWS, tf), F32),
            pltpu.VMEM((2 * (D_FF // tf), CARRY_ROWS, tf), F32),
        ],
        compiler_params=_cparams(("arbitrary",)),
        name="ffn",
    )(x2d, g, w_up, w_conv, w_down, g_final)


def _rearrange_w_in(w):
    bw = BRANCH_WIDTH
    a0 = 0
    b0 = 2 * bw
    c0 = b0 + 3 * bw + B_HEADS
    g0 = c0 + 3 * bw + 2 * C_HEADS + bw
    co = c0 + 3 * bw + 2 * C_HEADS
    groups = tuple(w[..., lo:hi].astype(BF16) for lo, hi in (
        (g0, g0 + 3 * D_MODEL), (a0, b0 + 3 * bw), (c0, c0 + 3 * bw), (co, co + bw)))
    small = jnp.concatenate([
        w[..., b0 + 3 * bw:b0 + 3 * bw + B_HEADS],
        w[..., c0 + 3 * bw:c0 + 3 * bw + 2 * C_HEADS],
        jnp.zeros(w.shape[:-1] + (LANES - GATE_ROWS,), w.dtype),
    ], axis=-1).astype(BF16)
    return groups, small


def kernel(x, mem, g_mix, w_in, g_sgu, w_s, b_s, b_fox_f, w_conv_c, b_mlstm_i, b_mlstm_f, g_mh,
           w_branch, w_out, g_mem_q, g_mem_kv, w_mq, w_mkv, w_mo, g_ffn, w_up, w_ffn_conv, w_down,
           g_final):
    bsz, seq, _ = x.shape
    depth = w_in.shape[0]
    m = bsz * seq
    tm = min(512, seq)
    tm_proj = tm
    tm_ffn = tm
    t_fox = min(256, seq)
    ts_mlstm = min(1024, seq)

    idx = jnp.arange(A_BLOCK)
    chunk_causal = (idx[None, :] // CHUNK) <= (idx[:, None] // CHUNK)

    x2d = x.reshape(m, D_MODEL)
    mem2d = mem.reshape(bsz * N_MEM, D_MODEL)
    rows = lambda a: a.reshape(depth, 1, -1)

    w_groups, w_small = _rearrange_w_in(w_in)
    gate_bias = jnp.concatenate(
        [b_fox_f, b_mlstm_i, b_mlstm_f, jnp.zeros((depth, LANES - GATE_ROWS), F32)], axis=-1)
    ws_masked = jnp.where(chunk_causal, w_s, 0).astype(BF16)
    bs_full = jnp.repeat(jnp.swapaxes(b_s, 1, 2), BRANCH_WIDTH // A_GROUPS, axis=2)
    gmh_rep = jnp.broadcast_to(g_mh.reshape(depth, C_HEADS, C_HEAD_DIM, 1),
                               (depth, C_HEADS, C_HEAD_DIM, LANES))
    w_branch, w_out, w_mq, w_mkv, w_mo, w_up, w_down = (
        a.astype(BF16) for a in (w_branch, w_out, w_mq, w_mkv, w_mo, w_up, w_down))

    for i in range(depth):
        p2d, gs = _proj_call(x2d, rows(g_mix), w_groups, w_small, w_conv_c, i, seq, tm_proj)
        p3d = p2d.reshape(bsz, seq, PROJ_COLS)
        gc, gr, u, gsum = _gates_call(gs.reshape(bsz, seq, LANES), rows(gate_bias), i)
        yb = _fox_call(p3d, gc, t_fox).reshape(m, BRANCH_WIDTH)
        yc = _mlstm_call(p3d, u, gr, gsum, gmh_rep, i, ts_mlstm).reshape(m, BRANCH_WIDTH)
        kv = _memkv_call(mem2d, rows(g_mem_kv), w_mkv, i)
        x2d = _merge_call(p2d, yb, yc, x2d, rows(g_sgu), ws_masked, bs_full, w_branch, w_out,
                          rows(g_mem_q), w_mq, kv, w_mo, i, seq, tm)
        x2d = _ffn_call(x2d, rows(g_ffn), w_up, w_ffn_conv, w_down, g_final.reshape(1, -1), i, seq, tm_ffn,
                        256, i == depth - 1)
    return x2d.reshape(bsz, seq, D_MODEL)
```

```python
import functools

import jax
import jax.numpy as jnp
from jax import lax
from jax.experimental import pallas as pl
from jax.experimental.pallas import tpu as pltpu

F32 = jnp.float32
BF16 = jnp.bfloat16

D_MODEL = 1024
EPS = 1e-6
LANES = 128
CARRY_ROWS = 8

BRANCH_WIDTH = 512
A_BLOCK = 128
A_GROUPS = 4
CHUNK = 64
B_HEADS = 8
B_HEAD_DIM = 64
C_HEADS = 4
C_HEAD_DIM = 128
C_CONV = 4
MLSTM_CHUNK = 128
N_MEM = 256
MEM_HEADS = 4
MEM_HEAD_DIM = 256
D_FF = 2816
FFN_CONV = 3

PROJ_TN = 512
COL_G = 0
COL_A = 3072
COL_BQ = 4096
COL_BK = 4608
COL_BV = 5120
COL_CQ = 5632
COL_CK = 6144
COL_CV = 6656
COL_CO = 7168
PROJ_COLS = 7680
LANE_BF = 0
LANE_CI = 8
LANE_CF = 12
GATE_ROWS = 16

LOG2E = 1.4426950408889634
FOX_Q_SCALE = B_HEAD_DIM ** -0.5 * LOG2E
NEG = -1e30
VMEM_LIMIT = 56 * 1024 * 1024


def _cparams(sem):
    return pltpu.CompilerParams(dimension_semantics=sem, vmem_limit_bytes=VMEM_LIMIT)


def _rms(xf, g):
    return xf * lax.rsqrt(jnp.mean(xf * xf, axis=-1, keepdims=True) + EPS) * g


def _sigmoid(x):
    return 0.5 * jnp.tanh(0.5 * x) + 0.5


def _gelu_tanh(x):
    return 0.5 * x * (1.0 + jnp.tanh(0.7978845608028654 * (x + 0.044715 * (x * x * x))))


def _dot(a, b):
    return jnp.dot(a, b, preferred_element_type=F32)


def _dot_nt(a, b):
    return lax.dot_general(a, b, (((1,), (1,)), ((), ())), preferred_element_type=F32)


def _const_spec(shape, layer=None):
    nd = len(shape)
    if layer is None:
        return pl.BlockSpec(shape, lambda *_: (0,) * nd, pipeline_mode=pl.Buffered(1))
    return pl.BlockSpec((None,) + tuple(shape), lambda *_: (layer,) + (0,) * nd,
                        pipeline_mode=pl.Buffered(1))


def _proj_kernel(x_ref, g_ref, wg_ref, wab_ref, wcq_ref, wco_ref, ws_ref, wc_ref, p_ref, gs_ref,
                 ext_ref, carry_ref, *, tm, seq):
    i = pl.program_id(0)

    def w_tile(c0):
        for ref, base, end in ((wg_ref, COL_G, COL_A), (wab_ref, COL_A, COL_CQ),
                               (wcq_ref, COL_CQ, COL_CO), (wco_ref, COL_CO, PROJ_COLS)):
            if base <= c0 < end:
                return ref[:, c0 - base:c0 - base + PROJ_TN]

    h = _rms(x_ref[...], g_ref[...]).astype(BF16)
    gs_ref[...] = _dot(h, ws_ref[...])
    seq_start = (i * tm) % seq == 0

    def conv_silu(acc, slot):
        ext_ref[slot, pl.ds(CARRY_ROWS, tm), :] = acc
        ext_ref[slot, pl.ds(0, CARRY_ROWS), :] = jnp.where(seq_start, 0.0, carry_ref[slot])
        carry_ref[slot] = ext_ref[slot, pl.ds(tm, CARRY_ROWS), :]
        wc = 0.5 * wc_ref[:, slot * PROJ_TN:(slot + 1) * PROJ_TN]
        half = wc[C_CONV - 1:C_CONV, :] * ext_ref[slot, pl.ds(CARRY_ROWS, tm), :]
        for d in range(1, C_CONV):
            half = half + wc[C_CONV - 1 - d:C_CONV - d, :] * ext_ref[slot, pl.ds(CARRY_ROWS - d, tm), :]
        return half * (jnp.tanh(half) + 1.0)

    order = [COL_CQ, COL_CK] + [c for c in range(0, PROJ_COLS, PROJ_TN) if c not in (COL_CQ, COL_CK)]
    for c0 in order:
        cs = slice(c0, c0 + PROJ_TN)
        acc = _dot(h, w_tile(c0))
        if c0 < COL_A or c0 == COL_CO:
            out = _sigmoid(acc)
        elif c0 < COL_BQ:
            out = _gelu_tanh(acc)
        elif c0 == COL_BQ:
            out = acc * FOX_Q_SCALE
        elif c0 == COL_CQ:
            out = conv_silu(acc, 0)
        elif c0 == COL_CK:
            out = conv_silu(acc, 1)
        else:
            out = acc
        p_ref[:, cs] = out.astype(BF16)


def _proj_call(x2d, g, w_groups, w_small, w_conv, layer, seq, tm):
    m = x2d.shape[0]
    return pl.pallas_call(
        functools.partial(_proj_kernel, tm=tm, seq=seq),
        grid=(m // tm,),
        in_specs=[
            pl.BlockSpec((tm, D_MODEL), lambda i: (i, 0)),
            _const_spec((1, D_MODEL), layer),
            _const_spec((D_MODEL, COL_A - COL_G), layer),
            _const_spec((D_MODEL, COL_CQ - COL_A), layer),
            _const_spec((D_MODEL, COL_CO - COL_CQ), layer),
            _const_spec((D_MODEL, PROJ_COLS - COL_CO), layer),
            _const_spec((D_MODEL, LANES), layer),
            _const_spec((C_CONV, 2 * PROJ_TN), layer),
        ],
        out_specs=[
            pl.BlockSpec((tm, PROJ_COLS), lambda i: (i, 0)),
            pl.BlockSpec((tm, LANES), lambda i: (i, 0)),
        ],
        out_shape=[
            jax.ShapeDtypeStruct((m, PROJ_COLS), BF16),
            jax.ShapeDtypeStruct((m, LANES), F32),
        ],
        scratch_shapes=[
            pltpu.VMEM((2, tm + CARRY_ROWS, PROJ_TN), F32),
            pltpu.VMEM((2, CARRY_ROWS, PROJ_TN), F32),
        ],
        compiler_params=_cparams(("arbitrary",)),
        name="proj",
    )(x2d, g, *w_groups, w_small, w_conv)


def _gates_kernel(gs_ref, bias_ref, gc_ref, gr_ref, u_ref, gsum_ref, *, seq):
    blk = MLSTM_CHUNK
    row = lax.broadcasted_iota(jnp.int32, (blk, blk), 0)
    col = lax.broadcasted_iota(jnp.int32, (blk, blk), 1)
    tri = (col <= row).astype(F32)
    lane = lax.broadcasted_iota(jnp.int32, (1, LANES), 1)

    def body(r, carry):
        r0 = pl.multiple_of(r * blk, blk)
        raw = gs_ref[pl.ds(r0, blk), :] + bias_ref[...]
        logsig = jnp.minimum(raw, 0.0) - jnp.log1p(jnp.exp(-jnp.abs(raw)))
        local = jnp.dot(tri, logsig, precision=lax.Precision.HIGHEST, preferred_element_type=F32)
        glob = local + carry
        out = jnp.where(lane < LANE_CI, glob, jnp.where(lane < LANE_CF, raw, local))
        gc_ref[pl.ds(r0, blk), :] = out
        gr_ref[:, pl.ds(r0, blk)] = out.T[0:GATE_ROWS, :]
        g_rows, mloc_rows = [], []
        for h in range(C_HEADS):
            d = out[:, LANE_CI + h:LANE_CI + h + 1] - out[:, LANE_CF + h:LANE_CF + h + 1]
            u = jnp.broadcast_to(d, (blk, LANES))
            u_ref[h, pl.ds(r0, blk), :] = u
            g = jnp.broadcast_to(out[blk - 1:blk, LANE_CF + h:LANE_CF + h + 1], (1, LANES))
            g_rows.append(g)
            mloc_rows.append(g + jnp.max(u, axis=0, keepdims=True))
        gsum_ref[r] = jnp.concatenate(g_rows + mloc_rows, axis=0)
        return glob[blk - 1:blk, :]

    lax.fori_loop(0, seq // blk, body, jnp.zeros((1, LANES), F32), unroll=4)


def _gates_call(gs3d, bias, layer):
    b, seq, _ = gs3d.shape
    nchunk = seq // MLSTM_CHUNK
    return pl.pallas_call(
        functools.partial(_gates_kernel, seq=seq),
        grid=(b,),
        in_specs=[
            pl.BlockSpec((None, seq, LANES), lambda bi: (bi, 0, 0)),
            _const_spec((1, LANES), layer),
        ],
        out_specs=[
            pl.BlockSpec((None, seq, LANES), lambda bi: (bi, 0, 0)),
            pl.BlockSpec((None, GATE_ROWS, seq), lambda bi: (bi, 0, 0)),
            pl.BlockSpec((None, C_HEADS, seq, LANES), lambda bi: (bi, 0, 0, 0)),
            pl.BlockSpec((None, nchunk, 2 * C_HEADS, LANES), lambda bi: (bi, 0, 0, 0)),
        ],
        out_shape=[
            jax.ShapeDtypeStruct((b, seq, LANES), F32),
            jax.ShapeDtypeStruct((b, GATE_ROWS, seq), F32),
            jax.ShapeDtypeStruct((b, C_HEADS, seq, LANES), F32),
            jax.ShapeDtypeStruct((b, nchunk, 2 * C_HEADS, LANES), F32),
        ],
        compiler_params=_cparams(("arbitrary",)),
        name="gates",
    )(gs3d, bias)


FOX_VROWS = 80
FOX_KMULT = 2
FOX_BODY = 2
FOX_HEADS = 8


def _fox_kernel(q_ref, k_ref, v_ref, gc_ref, place_ref, o_ref, kaug_ref, vt_ref, s_ref, m_ref, acc_ref,
                qa_ref, *, t, seq):
    qi = pl.program_id(2)
    hd = B_HEAD_DIM
    nh = FOX_HEADS
    lane = lax.broadcasted_iota(jnp.int32, (1, LANES), 1)
    own = (lane < hd, lane >= hd)
    aug0 = (hd, 0)

    def split_pieces(hi, mid, lo, one):
        return jnp.where(lane < B_HEADS, hi,
                         jnp.where(lane < 2 * B_HEADS, pltpu.roll(mid, B_HEADS, 1),
                                   jnp.where(lane < 3 * B_HEADS, pltpu.roll(lo, 2 * B_HEADS, 1),
                                             jnp.where(lane == 3 * B_HEADS, one, 0.0)))).astype(BF16)

    @pl.when(qi == 0)
    def _():
        sub = lax.broadcasted_iota(jnp.int32, (FOX_VROWS - hd, seq), 0)
        tail = jnp.where(sub == 0, 1.0, 0.0).astype(BF16)
        for hh in range(nh):
            vt_ref[hh, hd:FOX_VROWS, :] = tail
        eye = (lax.broadcasted_iota(jnp.int32, (LANES, LANES), 0)
               == lax.broadcasted_iota(jnp.int32, (LANES, LANES), 1)).astype(F32).astype(BF16)

        def body(r, _):
            r0 = pl.multiple_of(r * LANES, LANES)
            neg = -LOG2E * gc_ref[pl.ds(r0, LANES), :]
            hi = neg.astype(BF16).astype(F32)
            mid = (neg - hi).astype(BF16).astype(F32)
            lo = (neg - hi) - mid
            pieces = split_pieces(hi, mid, lo, 1.0)
            extra = _dot(pieces, place_ref[0]).astype(BF16)
            for pp in range(nh // 2):
                ls = slice(pp * LANES, (pp + 1) * LANES)
                kb = k_ref[pl.ds(r0, LANES), ls]
                vt = _dot_nt(eye, v_ref[pl.ds(r0, LANES), ls])
                for h in range(2):
                    hh = 2 * pp + h
                    kaug_ref[hh, pl.ds(r0, LANES), :] = jnp.where(
                        own[h], kb, extra[:, hh * LANES:(hh + 1) * LANES])
                    vt_ref[hh, 0:hd, pl.ds(r0, LANES)] = vt[h * hd:(h + 1) * hd, :].astype(BF16)
            return 0

        lax.fori_loop(0, seq // LANES, body, 0, unroll=4)

    pos = LOG2E * gc_ref[pl.ds(pl.multiple_of(qi * t, t), t), :]
    hi = pos.astype(BF16).astype(F32)
    mid = (pos - hi).astype(BF16).astype(F32)
    extra_q = _dot(split_pieces(hi, mid, (pos - hi) - mid, 0.0), place_ref[1])
    for pp in range(nh // 2):
        q = q_ref[:, pp * LANES:(pp + 1) * LANES].astype(F32)
        for h in range(2):
            hh = 2 * pp + h
            a = aug0[h]
            rest = jnp.where((lane >= a) & (lane < a + 3), 1.0, extra_q[:, hh * LANES:(hh + 1) * LANES])
            qa_ref[hh] = jnp.where(own[h], q, rest).astype(BF16)

    def qk(hh, k0, tk):
        return _dot_nt(kaug_ref[hh, pl.ds(k0, tk), :], qa_ref[hh])

    def update(k0, tk, masked, slot):
        k0 = pl.multiple_of(k0, t)
        if masked:
            causal = (lax.broadcasted_iota(jnp.int32, (t, t), 0)
                      <= lax.broadcasted_iota(jnp.int32, (t, t), 1))
        for hh in range(nh):
            s_ref[slot, hh, 0:tk, :] = qk(hh, k0, tk)
        ps, alphas = [], []
        for hh in range(nh):
            m = m_ref[hh]
            load = lambda: (jnp.where(causal, s_ref[slot, hh, 0:tk, :], NEG) if masked
                            else s_ref[slot, hh, 0:tk, :])
            m_new = jnp.maximum(m, jnp.max(load(), axis=0, keepdims=True))
            ps.append(jnp.exp2((load() - m_new).astype(BF16)))
            alphas.append(jnp.exp2(m - m_new))
            m_ref[hh] = m_new
        for hh in range(nh):
            acc_ref[hh] = alphas[hh] * acc_ref[hh] + _dot(vt_ref[hh, :, pl.ds(k0, tk)], ps[hh])

    m_ref[...] = jnp.full(m_ref.shape, NEG, F32)
    acc_ref[...] = jnp.zeros(acc_ref.shape, F32)

    km = FOX_KMULT
    nb = FOX_BODY

    def body(j, carry):
        for i in range(nb):
            update((j * nb + i) * km * t, km * t, False, i % 2)
        return carry

    nw = qi // km
    lax.fori_loop(0, nw // nb, body, 0)

    def tail(n_wide, n_single):
        def run():
            for i in range(n_wide):
                update((nw - n_wide + i) * km * t, km * t, False, i % 2)
            for i in range(n_single):
                update((qi - n_single + i) * t, t, False, (n_wide + i) % 2)
            update(qi * t, t, True, (n_wide + n_single) % 2)
        return run

    lax.switch((nw % nb) * km + qi % km, [tail(w, s) for w in range(nb) for s in range(km)])
    ot = jnp.concatenate([acc_ref[hh, 0:hd, :] / acc_ref[hh, hd:hd + 1, :] for hh in range(nh)], axis=0)
    o_ref[...] = ot.T.astype(BF16)


def _fox_place_matrix():
    assert FOX_HEADS == B_HEADS and LANE_BF == 0
    r = jnp.arange(LANES)[:, None]
    c = jnp.arange(B_HEADS * LANES)[None, :]
    head = c // LANES
    aug0 = jnp.where(head % 2 == 0, B_HEAD_DIM, 0)
    piece = c % LANES - aug0
    from_piece = lambda p: (p >= 0) & (p < 3) & (r == p * B_HEADS + head)
    key_side = from_piece(piece) | ((piece >= 3) & (piece < 6) & (r == 3 * B_HEADS))
    query_side = from_piece(piece - 3)
    return jnp.stack([key_side, query_side]).astype(BF16)


def _fox_call(p3d, gc, t):
    b, seq, _ = p3d.shape
    w = FOX_HEADS * B_HEAD_DIM
    return pl.pallas_call(
        functools.partial(_fox_kernel, t=t, seq=seq),
        grid=(b, B_HEADS // FOX_HEADS, seq // t),
        in_specs=[
            pl.BlockSpec((None, t, w), lambda bi, hg, qi: (bi, qi, COL_BQ // w + hg)),
            pl.BlockSpec((None, seq, w), lambda bi, hg, qi: (bi, 0, COL_BK // w + hg)),
            pl.BlockSpec((None, seq, w), lambda bi, hg, qi: (bi, 0, COL_BV // w + hg)),
            pl.BlockSpec((None, seq, LANES), lambda bi, hg, qi: (bi, 0, 0)),
            _const_spec((2, LANES, FOX_HEADS * LANES)),
        ],
        out_specs=pl.BlockSpec((None, t, w), lambda bi, hg, qi: (bi, qi, hg)),
        out_shape=jax.ShapeDtypeStruct((b, seq, BRANCH_WIDTH), BF16),
        scratch_shapes=[
            pltpu.VMEM((FOX_HEADS, seq, LANES), BF16),
            pltpu.VMEM((FOX_HEADS, FOX_VROWS, seq), BF16),
            pltpu.VMEM((2, FOX_HEADS, FOX_KMULT * t, t), F32),
            pltpu.VMEM((FOX_HEADS, 1, t), F32),
            pltpu.VMEM((FOX_HEADS, FOX_VROWS, t), F32),
            pltpu.VMEM((FOX_HEADS, t, LANES), BF16),
        ],
        compiler_params=_cparams(("arbitrary", "arbitrary", "arbitrary")),
        name="fox",
    )(p3d, p3d, p3d, gc, _fox_place_matrix())


MLSTM_ROWS = 144


def _mlstm_kernel(q_ref, k_ref, v_ref, o_ref, u_ref, gr_ref, gsum_ref, gmh_ref, y_ref, c_ref, m_ref,
                  *, ts):
    L = MLSTM_CHUNK
    dh = C_HEAD_DIM
    scale = dh ** -0.5
    si = pl.program_id(1)

    @pl.when(si == 0)
    def _():
        c_ref[...] = jnp.zeros_like(c_ref)
        m_ref[...] = jnp.zeros_like(m_ref)

    row = lax.broadcasted_iota(jnp.int32, (L, L), 0)
    col = lax.broadcasted_iota(jnp.int32, (L, L), 1)
    causal = row <= col
    sub = lax.broadcasted_iota(jnp.int32, (MLSTM_ROWS - dh, L), 0)
    tail = jnp.where(sub == 0, 1.0, 0.0)

    def chunk(c, _):
        r0 = pl.multiple_of(c * L, L)
        gsum = gsum_ref[c]
        for h in range(C_HEADS):
            hs = slice(h * dh, (h + 1) * dh)
            q = q_ref[pl.ds(r0, L), hs]
            k = k_ref[pl.ds(r0, L), hs]
            vt = jnp.concatenate([v_ref[pl.ds(r0, L), hs].astype(F32).T, tail], axis=0)
            ig = gr_ref[pl.ds(LANE_CI + h, 1), pl.ds(r0, L)]
            b = gr_ref[pl.ds(LANE_CF + h, 1), pl.ds(r0, L)]
            g = gsum[h:h + 1, :]
            m_loc = gsum[C_HEADS + h:C_HEADS + h + 1, :]
            m_in = m_ref[h]
            c_in = c_ref[h]

            dlog = jnp.where(causal, u_ref[h, pl.ds(r0, L), :] + b, NEG)
            inter = b + m_in
            m_t = jnp.maximum(jnp.max(dlog, axis=0, keepdims=True), inter)
            sm = (_dot_nt(k, q) * scale) * jnp.exp(dlog - m_t)
            w_int = jnp.exp(inter - m_t)
            ext = _dot(vt.astype(BF16), sm.astype(BF16)) + w_int * _dot_nt(c_in.astype(BF16), q)
            hh = ext[0:dh] / jnp.maximum(jnp.abs(ext[dh:dh + 1]), jnp.exp(-m_t))
            hn = hh * lax.rsqrt(jnp.mean(hh * hh, axis=0, keepdims=True) + EPS) * gmh_ref[h]
            y_ref[pl.ds(r0, L), hs] = (hn.T * o_ref[pl.ds(r0, L), hs].astype(F32)).astype(BF16)

            m_new = jnp.maximum(g + m_in, m_loc)
            w = jnp.exp(g + (ig - b) - m_new) * scale
            c_ref[h] = jnp.exp(g + m_in - m_new) * c_in + _dot((vt * w).astype(BF16), k)
            m_ref[h] = m_new
        return 0

    lax.fori_loop(0, ts // L, chunk, 0, unroll=8)


def _mlstm_call(p3d, u, gr, gsum, gmh_rep, layer, ts):
    b, seq, _ = p3d.shape
    w = BRANCH_WIDTH
    nc = ts // MLSTM_CHUNK

    def pspec(col):
        return pl.BlockSpec((None, ts, w), lambda bi, si: (bi, si, col // w))

    return pl.pallas_call(
        functools.partial(_mlstm_kernel, ts=ts),
        grid=(b, seq // ts),
        in_specs=[
            pspec(COL_CQ), pspec(COL_CK), pspec(COL_CV), pspec(COL_CO),
            pl.BlockSpec((None, C_HEADS, ts, LANES), lambda bi, si: (bi, 0, si, 0)),
            pl.BlockSpec((None, GATE_ROWS, ts), lambda bi, si: (bi, 0, si)),
            pl.BlockSpec((None, nc, 2 * C_HEADS, LANES), lambda bi, si: (bi, si, 0, 0)),
            _const_spec((C_HEADS, C_HEAD_DIM, LANES), layer),
        ],
        out_specs=pl.BlockSpec((None, ts, w), lambda bi, si: (bi, si, 0)),
        out_shape=jax.ShapeDtypeStruct((b, seq, w), BF16),
        scratch_shapes=[
            pltpu.VMEM((C_HEADS, MLSTM_ROWS, C_HEAD_DIM), F32),
            pltpu.VMEM((C_HEADS, 1, LANES), F32),
        ],
        compiler_params=_cparams(("arbitrary", "arbitrary")),
        name="mlstm",
    )(p3d, p3d, p3d, p3d, u, gr, gsum, gmh_rep)


def _merge_kernel(gates_ref, uv_ref, yb_ref, yc_ref, x_ref, gsgu_ref, ws_ref, bs_ref, wb_ref, wo_ref,
                  gq_ref, wq_ref, kv_ref, wmo_ref, out_ref, ya_ref, o_ref, *, tm):
    w = BRANCH_WIDTH
    u = uv_ref[:, :w].astype(F32)
    v = uv_ref[:, w:].astype(F32)
    vn = _rms(v, gsgu_ref[...]).astype(BF16)
    gd = w // A_GROUPS
    for nb in range(tm // A_BLOCK):
        rs = slice(nb * A_BLOCK, (nb + 1) * A_BLOCK)
        mixed = jnp.concatenate(
            [_dot(ws_ref[g], vn[rs, g * gd:(g + 1) * gd]) for g in range(A_GROUPS)], axis=1)
        ya_ref[rs, :] = (u[rs, :] * (mixed + bs_ref[...])).astype(BF16)
    merged = gates_ref[:, 0:D_MODEL].astype(F32) * _dot(ya_ref[...], wb_ref[0])
    merged += gates_ref[:, D_MODEL:2 * D_MODEL].astype(F32) * _dot(yb_ref[...], wb_ref[1])
    merged += gates_ref[:, 2 * D_MODEL:3 * D_MODEL].astype(F32) * _dot(yc_ref[...], wb_ref[2])
    x = x_ref[...] + _dot(merged.astype(BF16), wo_ref[...])

    h = _rms(x, gq_ref[...]).astype(BF16)
    q = (_dot(h, wq_ref[...]) * (MEM_HEAD_DIM ** -0.5)).astype(BF16)
    dh = MEM_HEAD_DIM
    for hd in range(MEM_HEADS):
        hs = slice(hd * dh, (hd + 1) * dh)
        s = _dot_nt(q[:, hs], kv_ref[:, hs])
        p = jnp.exp(s - jnp.max(s, axis=-1, keepdims=True))
        o = _dot(p.astype(BF16), kv_ref[:, D_MODEL + hd * dh:D_MODEL + (hd + 1) * dh])
        o_ref[:, hs] = (o / jnp.sum(p, axis=-1, keepdims=True)).astype(BF16)
    out_ref[...] = x + _dot(o_ref[...], wmo_ref[...])


def _merge_call(p2d, yb, yc, x2d, g_sgu, ws_masked, bs_full, w_branch, w_out, g_mq, w_mq, kv, w_mo,
                layer, seq, tm):
    m = x2d.shape[0]
    w = BRANCH_WIDTH
    return pl.pallas_call(
        functools.partial(_merge_kernel, tm=tm),
        grid=(m // tm,),
        in_specs=[
            pl.BlockSpec((tm, 3 * D_MODEL), lambda i: (i, COL_G // (3 * D_MODEL))),
            pl.BlockSpec((tm, 2 * w), lambda i: (i, COL_A // (2 * w))),
            pl.BlockSpec((tm, w), lambda i: (i, 0)),
            pl.BlockSpec((tm, w), lambda i: (i, 0)),
            pl.BlockSpec((tm, D_MODEL), lambda i: (i, 0)),
            _const_spec((1, w), layer),
            _const_spec((A_GROUPS, A_BLOCK, A_BLOCK), layer),
            _const_spec((A_BLOCK, w), layer),
            _const_spec((3, w, D_MODEL), layer),
            _const_spec((D_MODEL, D_MODEL), layer),
            _const_spec((1, D_MODEL), layer),
            _const_spec((D_MODEL, D_MODEL), layer),
            pl.BlockSpec((N_MEM, 2 * D_MODEL), lambda i: ((i * tm) // seq, 0)),
            _const_spec((D_MODEL, D_MODEL), layer),
        ],
        out_specs=pl.BlockSpec((tm, D_MODEL), lambda i: (i, 0)),
        out_shape=jax.ShapeDtypeStruct((m, D_MODEL), F32),
        scratch_shapes=[pltpu.VMEM((tm, w), BF16), pltpu.VMEM((tm, D_MODEL), BF16)],
        compiler_params=_cparams(("arbitrary",)),
        name="merge",
    )(p2d, p2d, yb, yc, x2d, g_sgu, ws_masked, bs_full, w_branch, w_out, g_mq, w_mq, kv, w_mo)


def _memkv_kernel(mem_ref, g_ref, w_ref, kv_ref):
    kv_ref[...] = _dot(_rms(mem_ref[...], g_ref[...]).astype(BF16), w_ref[...]).astype(BF16)


def _memkv_call(mem2d, g, w_mkv, layer):
    m = mem2d.shape[0]
    return pl.pallas_call(
        _memkv_kernel,
        grid=(m // N_MEM,),
        in_specs=[
            pl.BlockSpec((N_MEM, D_MODEL), lambda i: (i, 0)),
            _const_spec((1, D_MODEL), layer),
            _const_spec((D_MODEL, 2 * D_MODEL), layer),
        ],
        out_specs=pl.BlockSpec((N_MEM, 2 * D_MODEL), lambda i: (i, 0)),
        out_shape=jax.ShapeDtypeStruct((m, 2 * D_MODEL), BF16),
        compiler_params=_cparams(("arbitrary",)),
        name="memkv",
    )(mem2d, g, w_mkv)


def _ffn_kernel(x_ref, g_ref, wup_ref, wconv_ref, wdown_ref, gfin_ref, out_ref,
                act_ref, ext_ref, carry_ref, *, tm, tf, seq, final_norm):
    i = pl.program_id(0)
    x = x_ref[...]
    h = _rms(x, g_ref[...]).astype(BF16)
    seq_start = (i * tm) % seq == 0

    def conv(slot, half, cs):
        up = _dot(h, wup_ref[:, cs])
        ext_ref[half, pl.ds(CARRY_ROWS, tm), :] = up
        ext_ref[half, pl.ds(0, CARRY_ROWS), :] = jnp.where(seq_start, 0.0, carry_ref[slot])
        carry_ref[slot] = up[tm - CARRY_ROWS:, :]
        wc = wconv_ref[:, cs]
        y = wc[FFN_CONV - 1:FFN_CONV, :] * up
        for d in range(1, FFN_CONV):
            y = y + wc[FFN_CONV - 1 - d:FFN_CONV - d, :] * ext_ref[half, pl.ds(CARRY_ROWS - d, tm), :]
        return y

    nchunk = D_FF // tf
    for c in range(nchunk):
        a = conv(c, 0, slice(c * tf, (c + 1) * tf))
        b = conv(nchunk + c, 1, slice(D_FF + c * tf, D_FF + (c + 1) * tf))
        act_ref[:, c * tf:(c + 1) * tf] = (a * _sigmoid(a) * b).astype(BF16)
    y = x + _dot(act_ref[...], wdown_ref[...])
    if final_norm:
        y = _rms(y, gfin_ref[...])
    out_ref[...] = y


def _ffn_call(x2d, g, w_up, w_conv, w_down, g_final, layer, seq, tm, tf, final_norm):
    m = x2d.shape[0]
    return pl.pallas_call(
        functools.partial(_ffn_kernel, tm=tm, tf=tf, seq=seq, final_norm=final_norm),
        grid=(m // tm,),
        in_specs=[
            pl.BlockSpec((tm, D_MODEL), lambda i: (i, 0)),
            _const_spec((1, D_MODEL), layer),
            _const_spec((D_MODEL, 2 * D_FF), layer),
            _const_spec((FFN_CONV, 2 * D_FF), layer),
            _const_spec((D_FF, D_MODEL), layer),
            _const_spec((1, D_MODEL)),
        ],
        out_specs=pl.BlockSpec((tm, D_MODEL), lambda i: (i, 0)),
        out_shape=jax.ShapeDtypeStruct((m, D_MODEL), F32),
        scratch_shapes=[
            pltpu.VMEM((tm, D_FF), BF16),
            pltpu.VMEM((2, tm + CARRY_ROWS, tf), F32),
            pltpu.VMEM((2 * (D_FF // tf), CARRY_ROWS, tf), F32),
        ],
        compiler_params=_cparams(("arbitrary",)),
        name="ffn",
    )(x2d, g, w_up, w_conv, w_down, g_final)


def _rearrange_w_in(w):
    bw = BRANCH_WIDTH
    a0 = 0
    b0 = 2 * bw
    c0 = b0 + 3 * bw + B_HEADS
    g0 = c0 + 3 * bw + 2 * C_HEADS + bw
    co = c0 + 3 * bw + 2 * C_HEADS
    groups = tuple(w[..., lo:hi].astype(BF16) for lo, hi in (
        (g0, g0 + 3 * D_MODEL), (a0, b0 + 3 * bw), (c0, c0 + 3 * bw), (co, co + bw)))
    small = jnp.concatenate([
        w[..., b0 + 3 * bw:b0 + 3 * bw + B_HEADS],
        w[..., c0 + 3 * bw:c0 + 3 * bw + 2 * C_HEADS],
        jnp.zeros(w.shape[:-1] + (LANES - GATE_ROWS,), w.dtype),
    ], axis=-1).astype(BF16)
    return groups, small


def kernel(x, mem, g_mix, w_in, g_sgu, w_s, b_s, b_fox_f, w_conv_c, b_mlstm_i, b_mlstm_f, g_mh,
           w_branch, w_out, g_mem_q, g_mem_kv, w_mq, w_mkv, w_mo, g_ffn, w_up, w_ffn_conv, w_down,
           g_final):
    bsz, seq, _ = x.shape
    depth = w_in.shape[0]
    m = bsz * seq
    tm = min(512, seq)
    tm_proj = tm
    tm_ffn = tm
    t_fox = min(256, seq)
    ts_mlstm = min(1024, seq)

    idx = jnp.arange(A_BLOCK)
    chunk_causal = (idx[None, :] // CHUNK) <= (idx[:, None] // CHUNK)

    x2d = x.reshape(m, D_MODEL)
    mem2d = mem.reshape(bsz * N_MEM, D_MODEL)
    rows = lambda a: a.reshape(depth, 1, -1)

    w_groups, w_small = _rearrange_w_in(w_in)
    gate_bias = jnp.concatenate(
        [b_fox_f, b_mlstm_i, b_mlstm_f, jnp.zeros((depth, LANES - GATE_ROWS), F32)], axis=-1)
    ws_masked = jnp.where(chunk_causal, w_s, 0).astype(BF16)
    bs_full = jnp.repeat(jnp.swapaxes(b_s, 1, 2), BRANCH_WIDTH // A_GROUPS, axis=2)
    gmh_rep = jnp.broadcast_to(g_mh.reshape(depth, C_HEADS, C_HEAD_DIM, 1),
                               (depth, C_HEADS, C_HEAD_DIM, LANES))
    w_branch, w_out, w_mq, w_mkv, w_mo, w_up, w_down = (
        a.astype(BF16) for a in (w_branch, w_out, w_mq, w_mkv, w_mo, w_up, w_down))

    for i in range(depth):
        p2d, gs = _proj_call(x2d, rows(g_mix), w_groups, w_small, w_conv_c, i, seq, tm_proj)
        p3d = p2d.reshape(bsz, seq, PROJ_COLS)
        gc, gr, u, gsum = _gates_call(gs.reshape(bsz, seq, LANES), rows(gate_bias), i)
        yb = _fox_call(p3d, gc, t_fox).reshape(m, BRANCH_WIDTH)
        yc = _mlstm_call(p3d, u, gr, gsum, gmh_rep, i, ts_mlstm).reshape(m, BRANCH_WIDTH)
        kv = _memkv_call(mem2d, rows(g_mem_kv), w_mkv, i)
        x2d = _merge_call(p2d, yb, yc, x2d, rows(g_sgu), ws_masked, bs_full, w_branch, w_out,
                          rows(g_mem_q), w_mq, kv, w_mo, i, seq, tm)
        x2d = _ffn_call(x2d, rows(g_ffn), w_up, w_ffn_conv, w_down, g_final.reshape(1, -1), i, seq, tm_ffn,
                        256, i == depth - 1)
    return x2d.reshape(bsz, seq, D_MODEL)
```
